```python
import jax, jax.numpy as jnp
from jax import lax
import numpy as np

D_MODEL = 1024
BATCH = 16
SEQ = 4096
DEPTH = 4

N_HEADS = 8
QK_NOPE_DIM = 64
QK_ROPE_DIM = 32
QK_DIM = QK_NOPE_DIM + QK_ROPE_DIM
V_HEAD_DIM = 64
Q_LORA_RANK = 384
KV_LORA_RANK = 256
ROPE_THETA = 10000.0
Q_BLOCK = 128
CONV_CHANNELS = 512
CONV_WIDTH = 31
FFN_HIDDEN = -(-8 * D_MODEL // (3 * 256)) * 256
N_MOD = 6
EPS = 1e-6
NEG_INF = -1e30

IN_WIDTHS = (Q_LORA_RANK, KV_LORA_RANK, QK_ROPE_DIM, 2 * CONV_CHANNELS, D_MODEL, D_MODEL)
SPLIT_IDX = (Q_LORA_RANK,
             Q_LORA_RANK + KV_LORA_RANK,
             Q_LORA_RANK + KV_LORA_RANK + QK_ROPE_DIM,
             Q_LORA_RANK + KV_LORA_RANK + QK_ROPE_DIM + 2 * CONV_CHANNELS,
             Q_LORA_RANK + KV_LORA_RANK + QK_ROPE_DIM + 2 * CONV_CHANNELS + D_MODEL)
IN_WIDTH = sum(IN_WIDTHS)

kernel_name = "hybrid_mla_conformer_adaln_block"


def rms_norm(x, gain):
    xf = x.astype(jnp.float32)
    y = xf * lax.rsqrt(jnp.mean(xf * xf, axis=-1, keepdims=True) + EPS)
    return (y * gain.astype(jnp.float32)).astype(x.dtype)


def layer_norm(x, gain, bias):
    xf = x.astype(jnp.float32)
    mu = jnp.mean(xf, axis=-1, keepdims=True)
    xc = xf - mu
    y = xc * lax.rsqrt(jnp.mean(xc * xc, axis=-1, keepdims=True) + EPS)
    return (y * gain.astype(jnp.float32) + bias.astype(jnp.float32)).astype(x.dtype)


def rope_tables(positions, dtype):
    inv_freq = ROPE_THETA ** (-jnp.arange(0, QK_ROPE_DIM, 2, dtype=jnp.float32) / QK_ROPE_DIM)
    ang = positions.astype(jnp.float32)[..., None] * inv_freq
    return jnp.cos(ang).astype(dtype), jnp.sin(ang).astype(dtype)


def apply_rope(t, cos, sin):
    t1, t2 = jnp.split(t, 2, axis=-1)
    return jnp.concatenate([t1 * cos - t2 * sin, t2 * cos + t1 * sin], axis=-1)


def mla_attention(q_nope, q_rope, k_nope, k_rope, v):
    seq = q_nope.shape[1]
    scale = QK_DIM ** -0.5
    outs = []
    for i in range(seq // Q_BLOCK):
        s0, s1 = i * Q_BLOCK, (i + 1) * Q_BLOCK
        logits = (jnp.einsum('bqhd,bkhd->bhqk', q_nope[:, s0:s1], k_nope[:, :s1])
                  + jnp.einsum('bqhr,bkr->bhqk', q_rope[:, s0:s1], k_rope[:, :s1]))
        logits = logits.astype(jnp.float32) * scale
        causal = (s0 + jnp.arange(Q_BLOCK))[:, None] >= jnp.arange(s1)[None, :]
        probs = jax.nn.softmax(jnp.where(causal, logits, NEG_INF), axis=-1).astype(v.dtype)
        outs.append(jnp.einsum('bhqk,bkhd->bqhd', probs, v[:, :s1]))
    return jnp.concatenate(outs, axis=1)


def causal_depthwise_conv(u, w, b):
    rhs = w.astype(u.dtype)[:, None, :]
    y = lax.conv_general_dilated(u, rhs, window_strides=(1,), padding=[(CONV_WIDTH - 1, 0)],
                                 dimension_numbers=('NWC', 'WIO', 'NWC'),
                                 feature_group_count=u.shape[-1])
    return y + b


def hybrid_layer(x, c_act, cos, sin, w_ada, b_ada, g_mix, w_in, g_q, w_uq, g_kv, w_ukv, w_o_attn,
                 w_dw, b_dw, g_cn, b_cn, w_pw2, w_out, g_ffn, w_gu, w_down):
    bsz, seq, _ = x.shape
    mod = (c_act @ w_ada + b_ada)[:, None, :]
    sh1, sc1, gt1, sh2, sc2, gt2 = jnp.split(mod, N_MOD, axis=-1)

    h = rms_norm(x, g_mix) * (1 + sc1) + sh1
    proj = h @ w_in
    q_lat, kv_lat, k_rope, glu_in, gate_a, gate_b = jnp.split(proj, SPLIT_IDX, axis=-1)

    q = (rms_norm(q_lat, g_q) @ w_uq).reshape(bsz, seq, N_HEADS, QK_DIM)
    q_nope = q[..., :QK_NOPE_DIM]
    q_rope = apply_rope(q[..., QK_NOPE_DIM:], cos[:, :, None, :], sin[:, :, None, :])
    k_rope = apply_rope(k_rope, cos, sin)
    kv = (rms_norm(kv_lat, g_kv) @ w_ukv).reshape(bsz, seq, N_HEADS, QK_NOPE_DIM + V_HEAD_DIM)
    k_nope, v = kv[..., :QK_NOPE_DIM], kv[..., QK_NOPE_DIM:]
    attn = mla_attention(q_nope, q_rope, k_nope, k_rope, v).reshape(bsz, seq, N_HEADS * V_HEAD_DIM)
    y_attn = attn @ w_o_attn

    glu_a, glu_b = jnp.split(glu_in, 2, axis=-1)
    u = causal_depthwise_conv(glu_a * jax.nn.sigmoid(glu_b), w_dw, b_dw)
    y_conv = jax.nn.silu(layer_norm(u, g_cn, b_cn)) @ w_pw2

    y = jax.nn.sigmoid(gate_a) * y_attn + jax.nn.sigmoid(gate_b) * y_conv
    x = x + gt1 * (y @ w_out)

    h = rms_norm(x, g_ffn) * (1 + sc2) + sh2
    g, up = jnp.split(h @ w_gu, 2, axis=-1)
    x = x + gt2 * ((jax.nn.silu(g) * up) @ w_down)
    return x


def _fwd_setup_inputs(seed: int = 0) -> dict:
    key = jax.random.key(seed)
    ks = jax.random.split(key, 24)
    f32 = jnp.float32
    L, D = DEPTH, D_MODEL

    def w(k, shape, fan_in, mult=1.0):
        return jax.random.normal(k, shape, f32) * (mult * fan_in ** -0.5)

    def gain(k, shape):
        return 1.0 + 0.02 * jax.random.normal(k, shape, f32)

    def bias(k, shape):
        return 0.01 * jax.random.normal(k, shape, f32)

    x = jax.random.normal(ks[0], (BATCH, SEQ, D), f32)
    c = jax.random.normal(ks[1], (BATCH, D), f32)
    offset = jax.random.randint(ks[2], (BATCH, 1), 0, 1024, dtype=jnp.int32)
    positions = offset + jnp.arange(SEQ, dtype=jnp.int32)[None, :]
    return {
        "x": x,
        "c": c,
        "positions": positions,
        "w_ada": w(ks[3], (L, D, N_MOD * D), D, 0.5),
        "b_ada": bias(ks[4], (L, N_MOD * D)),
        "g_mix": gain(ks[5], (L, D)),
        "w_in": w(ks[6], (L, D, IN_WIDTH), D),
        "g_q": gain(ks[7], (L, Q_LORA_RANK)),
        "w_uq": w(ks[8], (L, Q_LORA_RANK, N_HEADS * QK_DIM), Q_LORA_RANK),
        "g_kv": gain(ks[9], (L, KV_LORA_RANK)),
        "w_ukv": w(ks[10], (L, KV_LORA_RANK, N_HEADS * (QK_NOPE_DIM + V_HEAD_DIM)), KV_LORA_RANK),
        "w_o_attn": w(ks[11], (L, N_HEADS * V_HEAD_DIM, D), N_HEADS * V_HEAD_DIM),
        "w_dw": w(ks[12], (L, CONV_WIDTH, CONV_CHANNELS), CONV_WIDTH),
        "b_dw": bias(ks[13], (L, CONV_CHANNELS)),
        "g_cn": gain(ks[14], (L, CONV_CHANNELS)),
        "b_cn": bias(ks[15], (L, CONV_CHANNELS)),
        "w_pw2": w(ks[16], (L, CONV_CHANNELS, D), CONV_CHANNELS),
        "w_out": w(ks[17], (L, D, D), D),
        "g_ffn": gain(ks[18], (L, D)),
        "w_gu": w(ks[19], (L, D, 2 * FFN_HIDDEN), D),
        "w_down": w(ks[20], (L, FFN_HIDDEN, D), FFN_HIDDEN),
        "g_final": gain(ks[21], (D,)),
    }


def _fwd_reference(x, c, positions, w_ada, b_ada, g_mix, w_in, g_q, w_uq, g_kv, w_ukv, w_o_attn,
              w_dw, b_dw, g_cn, b_cn, w_pw2, w_out, g_ffn, w_gu, w_down, g_final):
    cos, sin = rope_tables(positions, x.dtype)
    c_act = jax.nn.silu(c)
    for l in range(DEPTH):
        x = hybrid_layer(x, c_act, cos, sin, w_ada[l], b_ada[l], g_mix[l], w_in[l], g_q[l], w_uq[l],
                         g_kv[l], w_ukv[l], w_o_attn[l], w_dw[l], b_dw[l], g_cn[l], b_cn[l],
                         w_pw2[l], w_out[l], g_ffn[l], w_gu[l], w_down[l])
    return rms_norm(x, g_final)


import jax as _jax
import jax.numpy as _jnp

TWIN_FORMAT = 'train_step'
FWD_PARAMS = ['x', 'c', 'positions', 'w_ada', 'b_ada', 'g_mix', 'w_in', 'g_q', 'w_uq', 'g_kv', 'w_ukv', 'w_o_attn', 'w_dw', 'b_dw', 'g_cn', 'b_cn', 'w_pw2', 'w_out', 'g_ffn', 'w_gu', 'w_down', 'g_final']
TWIN_WEIGHTS = ['w_ada', 'b_ada', 'g_mix', 'w_in', 'g_q', 'w_uq', 'g_kv', 'w_ukv', 'w_o_attn', 'w_dw', 'b_dw', 'g_cn', 'b_cn', 'w_pw2', 'w_out', 'g_ffn', 'w_gu', 'w_down', 'g_final']
TWIN_DIFF_INPUT = 'x'
TWIN_INPUTS = ['x', 'c', 'positions', 'w_ada', 'b_ada', 'g_mix', 'w_in', 'g_q', 'w_uq', 'g_kv', 'w_ukv', 'w_o_attn', 'w_dw', 'b_dw', 'g_cn', 'b_cn', 'w_pw2', 'w_out', 'g_ffn', 'w_gu', 'w_down', 'g_final', 'loss_target', 'm_w_ada', 'm_b_ada', 'm_g_mix', 'm_w_in', 'm_g_q', 'm_w_uq', 'm_g_kv', 'm_w_ukv', 'm_w_o_attn', 'm_w_dw', 'm_b_dw', 'm_g_cn', 'm_b_cn', 'm_w_pw2', 'm_w_out', 'm_g_ffn', 'm_w_gu', 'm_w_down', 'm_g_final', 'v_w_ada', 'v_b_ada', 'v_g_mix', 'v_w_in', 'v_g_q', 'v_w_uq', 'v_g_kv', 'v_w_ukv', 'v_w_o_attn', 'v_w_dw', 'v_b_dw', 'v_g_cn', 'v_b_cn', 'v_w_pw2', 'v_w_out', 'v_g_ffn', 'v_w_gu', 'v_w_down', 'v_g_final']
TWIN_OUTPUTS = ['loss', 'grad_x', 'grad_w_ada', 'grad_b_ada', 'grad_g_mix', 'grad_w_in', 'grad_g_q', 'grad_w_uq', 'grad_g_kv', 'grad_w_ukv', 'grad_w_o_attn', 'grad_w_dw', 'grad_b_dw', 'grad_g_cn', 'grad_b_cn', 'grad_w_pw2', 'grad_w_out', 'grad_g_ffn', 'grad_w_gu', 'grad_w_down', 'grad_g_final', 'delta_w_ada', 'delta_b_ada', 'delta_g_mix', 'delta_w_in', 'delta_g_q', 'delta_w_uq', 'delta_g_kv', 'delta_w_ukv', 'delta_w_o_attn', 'delta_w_dw', 'delta_b_dw', 'delta_g_cn', 'delta_b_cn', 'delta_w_pw2', 'delta_w_out', 'delta_g_ffn', 'delta_w_gu', 'delta_w_down', 'delta_g_final', 'new_m_w_ada', 'new_m_b_ada', 'new_m_g_mix', 'new_m_w_in', 'new_m_g_q', 'new_m_w_uq', 'new_m_g_kv', 'new_m_w_ukv', 'new_m_w_o_attn', 'new_m_w_dw', 'new_m_b_dw', 'new_m_g_cn', 'new_m_b_cn', 'new_m_w_pw2', 'new_m_w_out', 'new_m_g_ffn', 'new_m_w_gu', 'new_m_w_down', 'new_m_g_final', 'new_v_w_ada', 'new_v_b_ada', 'new_v_g_mix', 'new_v_w_in', 'new_v_g_q', 'new_v_w_uq', 'new_v_g_kv', 'new_v_w_ukv', 'new_v_w_o_attn', 'new_v_w_dw', 'new_v_b_dw', 'new_v_g_cn', 'new_v_b_cn', 'new_v_w_pw2', 'new_v_w_out', 'new_v_g_ffn', 'new_v_w_gu', 'new_v_w_down', 'new_v_g_final']
TWIN_LEAF_KINDS = {'loss': 'loss', 'grad_x': 'grad_x', 'grad_w_ada': 'grad_w', 'grad_b_ada': 'grad_w', 'grad_g_mix': 'grad_w', 'grad_w_in': 'grad_w', 'grad_g_q': 'grad_w', 'grad_w_uq': 'grad_w', 'grad_g_kv': 'grad_w', 'grad_w_ukv': 'grad_w', 'grad_w_o_attn': 'grad_w', 'grad_w_dw': 'grad_w', 'grad_b_dw': 'grad_w', 'grad_g_cn': 'grad_w', 'grad_b_cn': 'grad_w', 'grad_w_pw2': 'grad_w', 'grad_w_out': 'grad_w', 'grad_g_ffn': 'grad_w', 'grad_w_gu': 'grad_w', 'grad_w_down': 'grad_w', 'grad_g_final': 'grad_w', 'delta_w_ada': 'delta_w', 'delta_b_ada': 'delta_w', 'delta_g_mix': 'delta_w', 'delta_w_in': 'delta_w', 'delta_g_q': 'delta_w', 'delta_w_uq': 'delta_w', 'delta_g_kv': 'delta_w', 'delta_w_ukv': 'delta_w', 'delta_w_o_attn': 'delta_w', 'delta_w_dw': 'delta_w', 'delta_b_dw': 'delta_w', 'delta_g_cn': 'delta_w', 'delta_b_cn': 'delta_w', 'delta_w_pw2': 'delta_w', 'delta_w_out': 'delta_w', 'delta_g_ffn': 'delta_w', 'delta_w_gu': 'delta_w', 'delta_w_down': 'delta_w', 'delta_g_final': 'delta_w', 'new_m_w_ada': 'new_m', 'new_m_b_ada': 'new_m', 'new_m_g_mix': 'new_m', 'new_m_w_in': 'new_m', 'new_m_g_q': 'new_m', 'new_m_w_uq': 'new_m', 'new_m_g_kv': 'new_m', 'new_m_w_ukv': 'new_m', 'new_m_w_o_attn': 'new_m', 'new_m_w_dw': 'new_m', 'new_m_b_dw': 'new_m', 'new_m_g_cn': 'new_m', 'new_m_b_cn': 'new_m', 'new_m_w_pw2': 'new_m', 'new_m_w_out': 'new_m', 'new_m_g_ffn': 'new_m', 'new_m_w_gu': 'new_m', 'new_m_w_down': 'new_m', 'new_m_g_final': 'new_m', 'new_v_w_ada': 'new_v', 'new_v_b_ada': 'new_v', 'new_v_g_mix': 'new_v', 'new_v_w_in': 'new_v', 'new_v_g_q': 'new_v', 'new_v_w_uq': 'new_v', 'new_v_g_kv': 'new_v', 'new_v_w_ukv': 'new_v', 'new_v_w_o_attn': 'new_v', 'new_v_w_dw': 'new_v', 'new_v_b_dw': 'new_v', 'new_v_g_cn': 'new_v', 'new_v_b_cn': 'new_v', 'new_v_w_pw2': 'new_v', 'new_v_w_out': 'new_v', 'new_v_g_ffn': 'new_v', 'new_v_w_gu': 'new_v', 'new_v_w_down': 'new_v', 'new_v_g_final': 'new_v'}


def _forward(args):
    return _fwd_reference(*[args[k] for k in FWD_PARAMS])


def _output_shape():
    out = _jax.eval_shape(lambda: _forward(_fwd_setup_inputs(0)))
    return out.shape, out.dtype

N_MICROBATCH = 1
ADAM_LR = 0.001
ADAM_B1 = 0.9
ADAM_B2 = 0.999
ADAM_EPS = 1e-08
ADAM_WD = 0.01
ADAM_STEP = 10
PER_EXAMPLE_BATCH_AXIS = {'x': 0, 'c': 0, 'positions': 0, 'loss_target': 0}
SHARED_INPUTS = []
_WEIGHT_DTYPES = {'w_ada': _jnp.float32, 'b_ada': _jnp.float32, 'g_mix': _jnp.float32, 'w_in': _jnp.float32, 'g_q': _jnp.float32, 'w_uq': _jnp.float32, 'g_kv': _jnp.float32, 'w_ukv': _jnp.float32, 'w_o_attn': _jnp.float32, 'w_dw': _jnp.float32, 'b_dw': _jnp.float32, 'g_cn': _jnp.float32, 'b_cn': _jnp.float32, 'w_pw2': _jnp.float32, 'w_out': _jnp.float32, 'g_ffn': _jnp.float32, 'w_gu': _jnp.float32, 'w_down': _jnp.float32, 'g_final': _jnp.float32}
MOMENT_SCALE = {'w_ada': 6.000249e-02, 'b_ada': 9.614871e-02, 'g_mix': 3.284454e-02, 'w_in': 1.894894e-02, 'g_q': 1.186313e-02, 'w_uq': 8.465189e-03, 'g_kv': 3.404784e-02, 'w_ukv': 1.673795e-02, 'w_o_attn': 1.551192e-02, 'w_dw': 3.747703e-02, 'b_dw': 7.133089e-02, 'g_cn': 4.370471e-02, 'b_cn': 3.624943e-02, 'w_pw2': 2.563025e-02, 'w_out': 2.973574e-02, 'g_ffn': 7.597451e-02, 'w_gu': 3.348466e-02, 'w_down': 5.469840e-02, 'g_final': 6.396417e+01}


def _to_microbatches(a, axis):
    t = _jnp.moveaxis(a, axis, 0)
    t = t.reshape((N_MICROBATCH, t.shape[0] // N_MICROBATCH) + t.shape[1:])
    return _jnp.moveaxis(t, 1, axis + 1)


def setup_inputs(seed: int = 0) -> dict:
    inp = _fwd_setup_inputs(seed)
    key = _jax.random.fold_in(_jax.random.key(seed), 7919)
    shape, _ = _output_shape()
    out = dict(inp)
    out["loss_target"] = _jax.random.normal(_jax.random.fold_in(key, 0), shape, _jnp.float32)
    for i, name in enumerate(TWIN_WEIGHTS):
        w = inp[name].astype(_jnp.float32)
        if MOMENT_SCALE is None:
            s = _jnp.sqrt(_jnp.mean(_jnp.square(w)) + 1e-30)
        else:
            s = MOMENT_SCALE[name]
        km, kv = _jax.random.split(_jax.random.fold_in(key, i + 1))
        out[name] = w
        out["m_" + name] = s * _jax.random.normal(km, w.shape, _jnp.float32)
        out["v_" + name] = (s * s) * _jax.random.uniform(kv, w.shape, _jnp.float32, 0.5, 1.5)
    if N_MICROBATCH > 1:
        for name, axis in PER_EXAMPLE_BATCH_AXIS.items():
            out[name] = _to_microbatches(out[name], axis)
    return {'x': out['x'], 'c': out['c'], 'positions': out['positions'], 'w_ada': out['w_ada'], 'b_ada': out['b_ada'], 'g_mix': out['g_mix'], 'w_in': out['w_in'], 'g_q': out['g_q'], 'w_uq': out['w_uq'], 'g_kv': out['g_kv'], 'w_ukv': out['w_ukv'], 'w_o_attn': out['w_o_attn'], 'w_dw': out['w_dw'], 'b_dw': out['b_dw'], 'g_cn': out['g_cn'], 'b_cn': out['b_cn'], 'w_pw2': out['w_pw2'], 'w_out': out['w_out'], 'g_ffn': out['g_ffn'], 'w_gu': out['w_gu'], 'w_down': out['w_down'], 'g_final': out['g_final'], 'loss_target': out['loss_target'], 'm_w_ada': out['m_w_ada'], 'm_b_ada': out['m_b_ada'], 'm_g_mix': out['m_g_mix'], 'm_w_in': out['m_w_in'], 'm_g_q': out['m_g_q'], 'm_w_uq': out['m_w_uq'], 'm_g_kv': out['m_g_kv'], 'm_w_ukv': out['m_w_ukv'], 'm_w_o_attn': out['m_w_o_attn'], 'm_w_dw': out['m_w_dw'], 'm_b_dw': out['m_b_dw'], 'm_g_cn': out['m_g_cn'], 'm_b_cn': out['m_b_cn'], 'm_w_pw2': out['m_w_pw2'], 'm_w_out': out['m_w_out'], 'm_g_ffn': out['m_g_ffn'], 'm_w_gu': out['m_w_gu'], 'm_w_down': out['m_w_down'], 'm_g_final': out['m_g_final'], 'v_w_ada': out['v_w_ada'], 'v_b_ada': out['v_b_ada'], 'v_g_mix': out['v_g_mix'], 'v_w_in': out['v_w_in'], 'v_g_q': out['v_g_q'], 'v_w_uq': out['v_w_uq'], 'v_g_kv': out['v_g_kv'], 'v_w_ukv': out['v_w_ukv'], 'v_w_o_attn': out['v_w_o_attn'], 'v_w_dw': out['v_w_dw'], 'v_b_dw': out['v_b_dw'], 'v_g_cn': out['v_g_cn'], 'v_b_cn': out['v_b_cn'], 'v_w_pw2': out['v_w_pw2'], 'v_w_out': out['v_w_out'], 'v_g_ffn': out['v_g_ffn'], 'v_w_gu': out['v_w_gu'], 'v_w_down': out['v_w_down'], 'v_g_final': out['v_g_final']}


def _loss(weights, diff, rest, loss_target):
    with _jax.named_scope("forward"):
        args = {**rest, TWIN_DIFF_INPUT: diff, **{k: w.astype(_WEIGHT_DTYPES[k]) for k, w in weights.items()}}
        y = _forward(args)
    with _jax.named_scope("loss_head"):
        err = _jnp.square(y.astype(_jnp.float32) - loss_target)
        return 0.5 * _jnp.sum(_jnp.mean(err, axis=-1)) if err.ndim else 0.5 * err


def _adamw(w, g, m, v):
    m = ADAM_B1 * m + (1.0 - ADAM_B1) * g
    v = ADAM_B2 * v + (1.0 - ADAM_B2) * _jnp.square(g)
    m_hat = m / (1.0 - ADAM_B1 ** ADAM_STEP)
    v_hat = v / (1.0 - ADAM_B2 ** ADAM_STEP)
    delta = -ADAM_LR * (m_hat / (_jnp.sqrt(v_hat) + ADAM_EPS) + ADAM_WD * w)
    return delta, m, v


def reference(x, c, positions, w_ada, b_ada, g_mix, w_in, g_q, w_uq, g_kv, w_ukv, w_o_attn, w_dw, b_dw, g_cn, b_cn, w_pw2, w_out, g_ffn, w_gu, w_down, g_final, loss_target, m_w_ada, m_b_ada, m_g_mix, m_w_in, m_g_q, m_w_uq, m_g_kv, m_w_ukv, m_w_o_attn, m_w_dw, m_b_dw, m_g_cn, m_b_cn, m_w_pw2, m_w_out, m_g_ffn, m_w_gu, m_w_down, m_g_final, v_w_ada, v_b_ada, v_g_mix, v_w_in, v_g_q, v_w_uq, v_g_kv, v_w_ukv, v_w_o_attn, v_w_dw, v_b_dw, v_g_cn, v_b_cn, v_w_pw2, v_w_out, v_g_ffn, v_w_gu, v_w_down, v_g_final):
    given = dict(x=x, c=c, positions=positions, w_ada=w_ada, b_ada=b_ada, g_mix=g_mix, w_in=w_in, g_q=g_q, w_uq=w_uq, g_kv=g_kv, w_ukv=w_ukv, w_o_attn=w_o_attn, w_dw=w_dw, b_dw=b_dw, g_cn=g_cn, b_cn=b_cn, w_pw2=w_pw2, w_out=w_out, g_ffn=g_ffn, w_gu=w_gu, w_down=w_down, g_final=g_final, loss_target=loss_target, m_w_ada=m_w_ada, m_b_ada=m_b_ada, m_g_mix=m_g_mix, m_w_in=m_w_in, m_g_q=m_g_q, m_w_uq=m_w_uq, m_g_kv=m_g_kv, m_w_ukv=m_w_ukv, m_w_o_attn=m_w_o_attn, m_w_dw=m_w_dw, m_b_dw=m_b_dw, m_g_cn=m_g_cn, m_b_cn=m_b_cn, m_w_pw2=m_w_pw2, m_w_out=m_w_out, m_g_ffn=m_g_ffn, m_w_gu=m_w_gu, m_w_down=m_w_down, m_g_final=m_g_final, v_w_ada=v_w_ada, v_b_ada=v_b_ada, v_g_mix=v_g_mix, v_w_in=v_w_in, v_g_q=v_g_q, v_w_uq=v_w_uq, v_g_kv=v_g_kv, v_w_ukv=v_w_ukv, v_w_o_attn=v_w_o_attn, v_w_dw=v_w_dw, v_b_dw=v_b_dw, v_g_cn=v_g_cn, v_b_cn=v_b_cn, v_w_pw2=v_w_pw2, v_w_out=v_w_out, v_g_ffn=v_g_ffn, v_w_gu=v_w_gu, v_w_down=v_w_down, v_g_final=v_g_final)
    weights = {n: given[n] for n in TWIN_WEIGHTS}
    shared = {n: given[n] for n in SHARED_INPUTS}
    per_example = {n: given[n] for n in ['x', 'c', 'positions']}
    grad_fn = _jax.value_and_grad(_loss, argnums=(0, 1))

    def one_microbatch(ex, loss_target):
        ex = dict(ex)
        diff = ex.pop(TWIN_DIFF_INPUT)
        return grad_fn(weights, diff, {**shared, **ex}, loss_target)

    if N_MICROBATCH == 1:
        loss, (grad_w, grad_x) = one_microbatch(per_example, given["loss_target"])
    else:
        def body(carry, xs):
            loss_sum, grad_sum = carry
            l_k, (gw_k, gx_k) = one_microbatch(xs[0], xs[1])
            with _jax.named_scope("update"):
                return (loss_sum + l_k, _jax.tree.map(_jnp.add, grad_sum, gw_k)), gx_k

        init = (_jnp.zeros((), _jnp.float32), _jax.tree.map(_jnp.zeros_like, weights))
        (loss, grad_w), grad_x = _jax.lax.scan(body, init, (per_example, given["loss_target"]))
    with _jax.named_scope("update"):
        delta_w, new_m, new_v = {}, {}, {}
        for n in TWIN_WEIGHTS:
            delta_w[n], new_m[n], new_v[n] = _adamw(weights[n], grad_w[n], given["m_" + n], given["v_" + n])
    return (loss, grad_x, *[grad_w[n] for n in TWIN_WEIGHTS], *[delta_w[n] for n in TWIN_WEIGHTS],
            *[new_m[n] for n in TWIN_WEIGHTS], *[new_v[n] for n in TWIN_WEIGHTS])
```

```python
import functools

import jax
import jax.numpy as jnp
from jax import lax
from jax.experimental import pallas as pl
from jax.experimental.pallas import tpu as pltpu

F32, BF16 = jnp.float32, jnp.bfloat16
MESH = pl.DeviceIdType.MESH
AXES = ("x", "y", "c")
N_DEV = 8

N_HEADS = 8
QK_NOPE = 64
QK_ROPE = 32
V_HEAD = 64
HEAD_PAD = 128
CONV_W = 31
HALO = 32
EPS = 1e-6
ROPE_THETA = 10000.0
NEG_INF = -1e30
ATTN_SCALE = (QK_NOPE + QK_ROPE) ** -0.5

ADAM_LR, ADAM_B1, ADAM_B2, ADAM_EPS, ADAM_WD, ADAM_STEP = 0.001, 0.9, 0.999, 1e-08, 0.01, 10

LANES = 128
VMEM_LIMIT = 60 * 1024 * 1024
PACK_COLS = 1024


def _params(sem=None):
    return pltpu.CompilerParams(dimension_semantics=sem, vmem_limit_bytes=VMEM_LIMIT)


def _pick(n, cap):
    if n <= cap:
        return n
    best = None
    for d in range(LANES, cap + 1, LANES):
        if n % d == 0:
            best = d
    assert best is not None, (n, cap)
    return best


def _div_tile(n, cap, mult):
    best = None
    for d in range(mult, min(n, cap) + 1, mult):
        if n % d == 0:
            best = d
    assert best is not None, (n, cap, mult)
    return best


def _row_tile(s, cap=512):
    return cap if s % cap == 0 and s >= 2 * cap else s // 2


def _sig(v):
    return 1.0 / (1.0 + jnp.exp(-v))


def _rstd(v):
    return lax.rsqrt(jnp.mean(v * v, axis=-1, keepdims=True) + EPS)


def _dot(a, b):
    return jnp.dot(a, b, preferred_element_type=F32)


def _dot_nt(a, b):
    return lax.dot_general(a, b, (((1,), (1,)), ((), ())), preferred_element_type=F32)


def _dot_tn(a, b):
    return lax.dot_general(a, b, (((0,), (0,)), ((), ())), preferred_element_type=F32)


def _rope(v, cos_t, sin_t):
    return v * cos_t + pltpu.roll(v, HEAD_PAD - QK_ROPE, 1) * sin_t


def _rope_bwd(dv, cos_t, sin_t):
    return dv * cos_t + pltpu.roll(dv * sin_t, QK_ROPE, 1)


def _const(shape):
    n = len(shape)
    return pl.BlockSpec(shape, lambda *_: (0,) * n, pipeline_mode=pl.Buffered(1))


def _rows(tm, w):
    return pl.BlockSpec((1, tm, w), lambda b, s: (b, s, 0))


def _heads(tm):
    return pl.BlockSpec((1, N_HEADS, tm, HEAD_PAD), lambda b, s: (b, 0, s, 0))


def _per_b(r, w):
    return pl.BlockSpec((1, r, 1, w), lambda b, s: (b, 0, 0, 0))


def _all_gather(x, vmem):
    m, n = x.shape
    space = pltpu.VMEM if vmem else pl.ANY

    def body(x_ref, out_ref, send_sems, recv_sems, local_sem):
        x_, y_, c_ = lax.axis_index("x"), lax.axis_index("y"), lax.axis_index("c")
        me, sibling = (x_, y_, c_), (x_, y_, 1 - c_)
        chips = [(1 - x_, y_), (x_, 1 - y_), (1 - x_, 1 - y_)]

        def slot(px, py, pc):
            return out_ref.at[4 * px + 2 * py + pc]

        def copy(k, block, to, src=None):
            return pltpu.make_async_remote_copy(
                src_ref=slot(*block) if src is None else src, dst_ref=slot(*block),
                send_sem=send_sems.at[k], recv_sem=recv_sems.at[k], device_id=to, device_id_type=MESH)

        mine = pltpu.make_async_copy(x_ref, slot(*me), local_sem)
        mine.start()
        first = [copy(0, me, sibling, src=x_ref)]
        first += [copy(1 + j, me, (*chip, c_), src=x_ref) for j, chip in enumerate(chips)]
        for cp in first:
            cp.start()
        passed = [copy(4 + j, (*chip, c_), sibling) for j, chip in enumerate(chips)]
        for j, chip in enumerate(chips):
            copy(1 + j, (*chip, c_), me).wait_recv()
            passed[j].start()
        copy(0, sibling, me).wait_recv()
        for j, chip in enumerate(chips):
            copy(4 + j, (*chip, 1 - c_), me).wait_recv()
        for cp in first + passed:
            cp.wait_send()
        mine.wait()

    return pl.pallas_call(
        body, name="all_gather_vmem" if vmem else "all_gather_hbm",
        out_shape=jax.ShapeDtypeStruct((N_DEV, m, n), x.dtype),
        in_specs=[pl.BlockSpec(memory_space=space)], out_specs=pl.BlockSpec(memory_space=space),
        scratch_shapes=[pltpu.SemaphoreType.DMA((7,)), pltpu.SemaphoreType.DMA((7,)), pltpu.SemaphoreType.DMA],
        compiler_params=pltpu.CompilerParams(vmem_limit_bytes=VMEM_LIMIT),
    )(x)


def _sibling_exchange(g):
    _, r, cc = g.shape

    def body(g_ref, out_ref, send_sem, recv_sem):
        x_, y_, c_ = lax.axis_index("x"), lax.axis_index("y"), lax.axis_index("c")
        cp = pltpu.make_async_remote_copy(
            src_ref=g_ref.at[pl.ds(4 * (1 - c_), 4)], dst_ref=out_ref, send_sem=send_sem, recv_sem=recv_sem,
            device_id=(x_, y_, 1 - c_), device_id_type=MESH)
        cp.start()
        cp.wait()

    return pl.pallas_call(
        body, name="grad_sibling_exchange", out_shape=jax.ShapeDtypeStruct((4, r, cc), g.dtype),
        in_specs=[pl.BlockSpec(memory_space=pl.ANY)], out_specs=pl.BlockSpec(memory_space=pl.ANY),
        scratch_shapes=[pltpu.SemaphoreType.DMA, pltpu.SemaphoreType.DMA],
    )(g)


def _chip_exchange(p):
    _, r, cc = p.shape

    def body(p_ref, out_ref, send_sems, recv_sems):
        x_, y_, c_ = lax.axis_index("x"), lax.axis_index("y"), lax.axis_index("c")
        chips = [(1 - x_, y_), (x_, 1 - y_), (1 - x_, 1 - y_)]
        cps = [pltpu.make_async_remote_copy(
            src_ref=p_ref.at[2 * px + py], dst_ref=out_ref.at[k], send_sem=send_sems.at[k], recv_sem=recv_sems.at[k],
            device_id=(px, py, c_), device_id_type=MESH) for k, (px, py) in enumerate(chips)]
        for cp in cps:
            cp.start()
        for cp in cps:
            cp.wait()

    return pl.pallas_call(
        body, name="grad_chip_exchange", out_shape=jax.ShapeDtypeStruct((3, r, cc), p.dtype),
        in_specs=[pl.BlockSpec(memory_space=pl.ANY)], out_specs=pl.BlockSpec(memory_space=pl.ANY),
        scratch_shapes=[pltpu.SemaphoreType.DMA((3,)), pltpu.SemaphoreType.DMA((3,))],
    )(p)


def _add_pair(a, b):
    n, r, cc = a.shape
    tr = _div_tile(r, 512, 16)

    def body(a_ref, b_ref, o32, o16):
        s = a_ref[...].astype(F32) + b_ref[...].astype(F32)
        o32[...] = s
        o16[...] = s.astype(BF16)

    spec = pl.BlockSpec((1, tr, cc), lambda i, j: (i, j, 0))
    return pl.pallas_call(
        body, name="grad_add_pair", grid=(n, r // tr),
        out_shape=(jax.ShapeDtypeStruct(a.shape, F32), jax.ShapeDtypeStruct(a.shape, BF16)),
        in_specs=[spec, spec], out_specs=(spec, spec), compiler_params=_params(("arbitrary", "arbitrary")),
    )(a, b)


def _add_four(a, rest):
    r, cc = a.shape
    tr = _div_tile(r, 512, 16)

    def body(a_ref, r_ref, o_ref):
        o_ref[...] = ((a_ref[...] + r_ref[0].astype(F32)) + r_ref[1].astype(F32)) + r_ref[2].astype(F32)

    return pl.pallas_call(
        body, name="grad_add_four", grid=(r // tr,), out_shape=jax.ShapeDtypeStruct((r, cc), F32),
        in_specs=[pl.BlockSpec((tr, cc), lambda i: (i, 0)), pl.BlockSpec((3, tr, cc), lambda i: (0, i, 0))],
        out_specs=pl.BlockSpec((tr, cc), lambda i: (i, 0)), compiler_params=_params(("arbitrary",)),
    )(a, rest)


def _sum_devices(g):
    _, m, n = g.shape

    def body(g_ref, o_ref):
        s = g_ref[0]
        for j in range(1, N_DEV):
            s = s + g_ref[j]
        o_ref[...] = s

    return pl.pallas_call(body, name="small_grad_sum", out_shape=jax.ShapeDtypeStruct((m, n), F32),
                          compiler_params=pltpu.CompilerParams(vmem_limit_bytes=VMEM_LIMIT))(g)


def _pack(arrs, cols, row_mult, dtype):
    flat = jnp.concatenate([a.reshape(-1).astype(dtype) for a in arrs])
    unit = cols * row_mult
    total = -(-flat.shape[0] // unit) * unit
    return jnp.pad(flat, (0, total - flat.shape[0])).reshape(total // cols, cols)


def _unpack(flat2d, shapes, lead=()):
    flat = flat2d.reshape(lead + (-1,))
    out, off = [], 0
    for shp in shapes:
        n = 1
        for d in shp:
            n *= d
        out.append(flat[..., off:off + n].reshape(lead + tuple(shp)))
        off += n
    return out


def _gather_small(arrs):
    packed = _pack(arrs, LANES, 8, F32)
    got = _all_gather(packed, vmem=True)
    return _unpack(got, [a.shape for a in arrs], lead=(N_DEV,))


def _ada_fwd(c_full, w_ada, b_cols):
    nl, d, n = w_ada.shape
    nb = c_full.shape[0]

    def body(c_ref, w_ref, b_ref, o_ref):
        cv = c_ref[...]
        act = cv * _sig(cv)
        o_ref[0] = jnp.dot(act, w_ref[0], preferred_element_type=F32, precision=lax.Precision.HIGHEST) + b_ref[0]

    return pl.pallas_call(
        body, name="ada_fwd", grid=(nl,), out_shape=jax.ShapeDtypeStruct((nl, nb, n), F32),
        in_specs=[pl.BlockSpec((nb, d), lambda l: (0, 0)), pl.BlockSpec((1, d, n), lambda l: (l, 0, 0)),
                  pl.BlockSpec((1, 1, n), lambda l: (l, 0, 0))],
        out_specs=pl.BlockSpec((1, nb, n), lambda l: (l, 0, 0)), compiler_params=_params(("arbitrary",)),
    )(c_full, w_ada, b_cols)


def _ada_bwd(c_full, dmod_cols, dmod_full):
    nl, nb, n = dmod_cols.shape
    d = c_full.shape[1]
    nfull = dmod_full.shape[2]

    def body(c_ref, dc_ref, df_ref, gw_ref, gb_ref):
        cv = c_ref[...]
        act = cv * _sig(cv)
        gw_ref[0] = lax.dot_general(act, dc_ref[0], (((0,), (0,)), ((), ())), preferred_element_type=F32,
                                    precision=lax.Precision.HIGHEST)
        gb_ref[0] = jnp.sum(df_ref[0], axis=0, keepdims=True)

    return pl.pallas_call(
        body, name="ada_bwd", grid=(nl,),
        out_shape=(jax.ShapeDtypeStruct((nl, d, n), F32), jax.ShapeDtypeStruct((nl, 1, nfull), F32)),
        in_specs=[pl.BlockSpec((nb, d), lambda l: (0, 0)), pl.BlockSpec((1, nb, n), lambda l: (l, 0, 0)),
                  pl.BlockSpec((1, nb, nfull), lambda l: (l, 0, 0))],
        out_specs=(pl.BlockSpec((1, d, n), lambda l: (l, 0, 0)), pl.BlockSpec((1, 1, nfull), lambda l: (l, 0, 0))),
        compiler_params=_params(("arbitrary",)),
    )(c_full, dmod_cols, dmod_full)


def _rope_tables(pos, freq_row, sign_row):
    b, s, _ = pos.shape
    tm = _row_tile(s)

    def body(p_ref, f_ref, g_ref, c_ref, s_ref):
        ang = p_ref[0] * f_ref[...]
        lane = lax.broadcasted_iota(jnp.int32, ang.shape, 1)
        c_ref[0] = jnp.where(lane < QK_NOPE, 1.0, jnp.where(lane < QK_NOPE + QK_ROPE, jnp.cos(ang), 0.0))
        s_ref[0] = g_ref[...] * jnp.sin(ang)

    return pl.pallas_call(
        body, name="rope_tables", grid=(b, s // tm),
        out_shape=(jax.ShapeDtypeStruct((b, s, HEAD_PAD), F32),) * 2,
        in_specs=[_rows(tm, 1), pl.BlockSpec((1, HEAD_PAD), lambda i, j: (0, 0)),
                  pl.BlockSpec((1, HEAD_PAD), lambda i, j: (0, 0))],
        out_specs=(_rows(tm, HEAD_PAD),) * 2, compiler_params=_params(("arbitrary", "arbitrary")),
    )(pos, freq_row, sign_row)


def _mix_pre_fwd(x, mod, cos_t, sin_t, g_mix, g_q, g_kv, w_lat, w_glu, w_gate, w_q, w_k, w_v):
    b, s, d = x.shape
    ql, kl = g_q.shape[1], g_kv.shape[1]
    wl, wg, wt = w_lat.shape[1], w_glu.shape[1], w_gate.shape[1]
    tm = _row_tile(s)

    def body(x_ref, mod_ref, cos_ref, sin_ref, gm_ref, gq_ref, gkv_ref, wlat_ref, wglu_ref, wgate_ref, wq_ref, wk_ref,
             wv_ref, lat_ref, glu_ref, gate_ref, q_ref, k_ref, v_ref):
        xf = x_ref[0]
        sh, sc = mod_ref[0, 0], mod_ref[0, 1]
        hb = ((xf * _rstd(xf) * gm_ref[...]) * (1.0 + sc) + sh).astype(BF16)
        glu_ref[0] = _dot(hb, wglu_ref[...]).astype(BF16)
        gate_ref[0] = _dot(hb, wgate_ref[...]).astype(BF16)
        lat = _dot(hb, wlat_ref[...]).astype(BF16)
        lat_ref[0] = lat
        latf = lat.astype(F32)
        q_lat, kv_lat, kr_sec = latf[:, :ql], latf[:, ql:ql + kl], latf[:, ql + kl:]
        qn = (q_lat * _rstd(q_lat) * gq_ref[...]).astype(BF16)
        kvn = (kv_lat * _rstd(kv_lat) * gkv_ref[...]).astype(BF16)
        cos_v, sin_v = cos_ref[0], sin_ref[0]
        lane = lax.broadcasted_iota(jnp.int32, kr_sec.shape, 1)
        kr = jnp.where(lane >= QK_NOPE, _rope(kr_sec, cos_v, sin_v), 0.0)
        for h in range(N_HEADS):
            q_ref[0, h] = _rope(_dot(qn, wq_ref[h]), cos_v, sin_v).astype(BF16)
            k_ref[0, h] = (_dot(kvn, wk_ref[h]) + kr).astype(BF16)
            v_ref[0, h] = _dot(kvn, wv_ref[h]).astype(BF16)

    hshape = jax.ShapeDtypeStruct((b, N_HEADS, s, HEAD_PAD), BF16)
    return pl.pallas_call(
        body, name="mix_pre_fwd", grid=(b, s // tm),
        out_shape=(jax.ShapeDtypeStruct((b, s, wl), BF16), jax.ShapeDtypeStruct((b, s, wg), BF16),
                   jax.ShapeDtypeStruct((b, s, wt), BF16), hshape, hshape, hshape),
        in_specs=[_rows(tm, d), _per_b(6, d), _rows(tm, HEAD_PAD), _rows(tm, HEAD_PAD), _const((1, d)), _const((1, ql)),
                  _const((1, kl)), _const(w_lat.shape), _const(w_glu.shape), _const(w_gate.shape), _const(w_q.shape),
                  _const(w_k.shape), _const(w_v.shape)],
        out_specs=(_rows(tm, wl), _rows(tm, wg), _rows(tm, wt), _heads(tm), _heads(tm), _heads(tm)),
        compiler_params=_params(("arbitrary", "arbitrary")),
    )(x, mod, cos_t, sin_t, g_mix, g_q, g_kv, w_lat, w_glu, w_gate, w_q, w_k, w_v)


def _causal_mask(tq, tk):
    return lax.broadcasted_iota(jnp.int32, (tq, tk), 0) >= lax.broadcasted_iota(jnp.int32, (tq, tk), 1)


def _flash_fwd(q, k, v):
    b, nh, s, hp = q.shape
    t = _row_tile(s)

    def body(q_ref, k_ref, v_ref, o_ref, lse_ref):
        i = pl.program_id(2)
        qv = q_ref[0, 0]

        def step(j, carry, masked):
            m, l, acc = carry
            rows = pl.ds(pl.multiple_of(j * t, t), t)
            sc = _dot_nt(qv, k_ref[0, 0, rows, :]) * ATTN_SCALE
            if masked:
                sc = jnp.where(_causal_mask(t, t), sc, NEG_INF)
            m_new = jnp.maximum(m, jnp.max(sc, axis=-1, keepdims=True))
            p = jnp.exp(sc - m_new)
            alpha = jnp.exp(m - m_new)
            l = alpha * l + jnp.sum(p, axis=-1, keepdims=True)
            acc = alpha * acc + _dot(p.astype(BF16), v_ref[0, 0, rows, :])
            return m_new, l, acc

        init = (jnp.full((t, 1), NEG_INF, F32), jnp.zeros((t, 1), F32), jnp.zeros((t, hp), F32))
        carry = lax.fori_loop(0, i, lambda j, cr: step(j, cr, False), init)
        m, l, acc = step(i, carry, True)
        o_ref[0, 0] = (acc / l).astype(BF16)
        lse_ref[0, 0] = m + jnp.log(l)

    tile = pl.BlockSpec((1, 1, t, hp), lambda bb, hh, ii: (bb, hh, ii, 0))
    full = pl.BlockSpec((1, 1, s, hp), lambda bb, hh, ii: (bb, hh, 0, 0))
    return pl.pallas_call(
        body, name="flash_fwd", grid=(b, nh, s // t),
        out_shape=(jax.ShapeDtypeStruct((b, nh, s, hp), BF16), jax.ShapeDtypeStruct((b, nh, s, 1), F32)),
        in_specs=[tile, full, full],
        out_specs=(tile, pl.BlockSpec((1, 1, t, 1), lambda bb, hh, ii: (bb, hh, ii, 0))),
        compiler_params=_params(("arbitrary", "arbitrary", "arbitrary")),
    )(q, k, v)


def _halo_prev(tm, w):
    r = tm // HALO
    return pl.BlockSpec((1, HALO, w), lambda b, s: (b, jnp.maximum(s * r - 1, 0), 0))


def _halo_next(tm, w, n_tiles):
    r = tm // HALO
    return pl.BlockSpec((1, HALO, w), lambda b, s: (b, jnp.minimum((s + 1) * r, n_tiles * r - 1), 0))


def _glu(v, cc):
    a, g = v[:, :cc].astype(F32), v[:, cc:].astype(F32)
    return a * _sig(g)


def _conv_fwd(glu, w_dw, b_dw, g_cn, b_cn, w_pw2):
    b, s, w2 = glu.shape
    cc = w2 // 2
    d = w_pw2.shape[1]
    tm = _row_tile(s)

    def body(cur_ref, prev_ref, w_ref, bdw_ref, g_ref, bcn_ref, wp_ref, u_ref, y_ref, ext):
        first = pl.program_id(1) == 0
        ext[pl.ds(0, HALO), :] = jnp.where(first, 0.0, _glu(prev_ref[0], cc))
        ext[pl.ds(HALO, tm), :] = _glu(cur_ref[0], cc)
        u = jnp.zeros((tm, cc), F32) + bdw_ref[...]
        for kk in range(CONV_W):
            u = u + w_ref[pl.ds(kk, 1), :] * ext[pl.ds(HALO - CONV_W + 1 + kk, tm), :]
        ub = u.astype(BF16)
        u_ref[0] = ub
        uf = ub.astype(F32)
        mu = jnp.mean(uf, axis=-1, keepdims=True)
        uc = uf - mu
        ln = uc * lax.rsqrt(jnp.mean(uc * uc, axis=-1, keepdims=True) + EPS) * g_ref[...] + bcn_ref[...]
        y_ref[0] = _dot((ln * _sig(ln)).astype(BF16), wp_ref[...]).astype(BF16)

    return pl.pallas_call(
        body, name="conv_fwd", grid=(b, s // tm),
        out_shape=(jax.ShapeDtypeStruct((b, s, cc), BF16), jax.ShapeDtypeStruct((b, s, d), BF16)),
        in_specs=[_rows(tm, w2), _halo_prev(tm, w2), _const(w_dw.shape), _const((1, cc)), _const((1, cc)), _const((1, cc)),
                  _const(w_pw2.shape)],
        out_specs=(_rows(tm, cc), _rows(tm, d)),
        scratch_shapes=[pltpu.VMEM((tm + HALO, cc), F32)],
        compiler_params=_params(("arbitrary", "arbitrary")),
    )(glu, glu, w_dw, b_dw, g_cn, b_cn, w_pw2)


def _mix_post_fwd(x, mod, o, y_conv, gate, w_o, w_out):
    b, s, d = x.shape
    tm = _row_tile(s)

    def body(x_ref, mod_ref, o_ref, yc_ref, gate_ref, wo_ref, wout_ref, x2_ref, ya_ref, o1_ref):
        ya = _dot(o_ref[0, 0], wo_ref[0])
        for h in range(1, N_HEADS):
            ya = ya + _dot(o_ref[0, h], wo_ref[h])
        yab = ya.astype(BF16)
        ya_ref[0] = yab
        gv = gate_ref[0]
        y = _sig(gv[:, :d].astype(F32)) * yab.astype(F32) + _sig(gv[:, d:].astype(F32)) * yc_ref[0].astype(F32)
        o1 = _dot(y.astype(BF16), wout_ref[...])
        o1_ref[0] = o1.astype(BF16)
        x2_ref[0] = x_ref[0] + mod_ref[0, 2] * o1

    return pl.pallas_call(
        body, name="mix_post_fwd", grid=(b, s // tm),
        out_shape=(jax.ShapeDtypeStruct((b, s, d), F32), jax.ShapeDtypeStruct((b, s, d), BF16),
                   jax.ShapeDtypeStruct((b, s, d), BF16)),
        in_specs=[_rows(tm, d), _per_b(6, d), _heads(tm), _rows(tm, d), _rows(tm, 2 * d), _const(w_o.shape),
                  _const(w_out.shape)],
        out_specs=(_rows(tm, d), _rows(tm, d), _rows(tm, d)),
        compiler_params=_params(("arbitrary", "arbitrary")),
    )(x, mod, o, y_conv, gate, w_o, w_out)


def _ffn_fwd(x2, mod, g_ffn, w_gu, w_down):
    b, s, d = x2.shape
    f = w_down.shape[0]
    fc = _pick(f, 512)
    tm = _row_tile(s)

    def body(x_ref, mod_ref, g_ref, wgu_ref, wdn_ref, x3_ref, gu_ref, o2_ref):
        xf = x_ref[0]
        hb = ((xf * _rstd(xf) * g_ref[...]) * (1.0 + mod_ref[0, 4]) + mod_ref[0, 3]).astype(BF16)
        o2 = jnp.zeros((tm, d), F32)
        for c0 in range(0, f, fc):
            gb = _dot(hb, wgu_ref[:, c0:c0 + fc]).astype(BF16)
            ub = _dot(hb, wgu_ref[:, f + c0:f + c0 + fc]).astype(BF16)
            gu_ref[0, :, c0:c0 + fc] = gb
            gu_ref[0, :, f + c0:f + c0 + fc] = ub
            gf = gb.astype(F32)
            act = (gf * _sig(gf) * ub.astype(F32)).astype(BF16)
            o2 = o2 + _dot(act, wdn_ref[c0:c0 + fc, :])
        o2_ref[0] = o2.astype(BF16)
        x3_ref[0] = xf + mod_ref[0, 5] * o2

    return pl.pallas_call(
        body, name="ffn_fwd", grid=(b, s // tm),
        out_shape=(jax.ShapeDtypeStruct((b, s, d), F32), jax.ShapeDtypeStruct((b, s, 2 * f), BF16),
                   jax.ShapeDtypeStruct((b, s, d), BF16)),
        in_specs=[_rows(tm, d), _per_b(6, d), _const((1, d)), _const(w_gu.shape), _const(w_down.shape)],
        out_specs=(_rows(tm, d), _rows(tm, 2 * f), _rows(tm, d)),
        compiler_params=_params(("arbitrary", "arbitrary")),
    )(x2, mod, g_ffn, w_gu, w_down)


def _zero_at_first_tile(*refs):
    @pl.when(pl.program_id(1) == 0)
    def _():
        for ref in refs:
            ref[...] = jnp.zeros_like(ref)


def _accumulate(ref, idx, val):
    ref[idx] = ref[idx] + val


def _colsum(v):
    return jnp.sum(v, axis=0, keepdims=True)


def _loss_bwd(x, target, g_final):
    b, s, d = x.shape
    tm = _row_tile(s)

    def body(x_ref, t_ref, g_ref, dx_ref, loss_ref, dg_ref):
        _zero_at_first_tile(loss_ref, dg_ref)
        xf = x_ref[0]
        r = _rstd(xf)
        xh = xf * r
        diff = xh * g_ref[...] - t_ref[0]
        _accumulate(loss_ref, (0, 0), _colsum(diff * diff))
        dy = diff * (1.0 / d)
        _accumulate(dg_ref, (0, 0), _colsum(dy * xh))
        dyg = dy * g_ref[...]
        dx_ref[0] = r * (dyg - xh * jnp.mean(dyg * xh, axis=-1, keepdims=True))

    return pl.pallas_call(
        body, name="loss_bwd", grid=(b, s // tm),
        out_shape=(jax.ShapeDtypeStruct((b, s, d), F32), jax.ShapeDtypeStruct((b, 1, 1, d), F32),
                   jax.ShapeDtypeStruct((b, 1, 1, d), F32)),
        in_specs=[_rows(tm, d), _rows(tm, d), _const((1, d))],
        out_specs=(_rows(tm, d), _per_b(1, d), _per_b(1, d)),
        compiler_params=_params(("arbitrary", "arbitrary")),
    )(x, target, g_final)


def _ffn_bwd(dx3, x2, mod, g_ffn, gu, o2, w_gu_t, w_down_t):
    b, s, d = x2.shape
    f = w_down_t.shape[1]
    fc = _pick(f, 512)
    tm = _row_tile(s, 256)

    def body(dx3_ref, x_ref, mod_ref, g_ref, gu_ref, o2_ref, wgut_ref, wdnt_ref, dx2_ref, do2_ref, act_ref, dgu_ref, h_ref,
             dmod_ref, dg_ref):
        _zero_at_first_tile(dmod_ref, dg_ref)
        dx3 = dx3_ref[0]
        sh, sc, gt = mod_ref[0, 3], mod_ref[0, 4], mod_ref[0, 5]
        do2 = (dx3 * gt).astype(BF16)
        do2_ref[0] = do2
        _accumulate(dmod_ref, (0, 2), _colsum(dx3 * o2_ref[0].astype(F32)))
        dh = jnp.zeros((tm, d), F32)
        for c0 in range(0, f, fc):
            gf = gu_ref[0, :, c0:c0 + fc].astype(F32)
            uf = gu_ref[0, :, f + c0:f + c0 + fc].astype(F32)
            sg = _sig(gf)
            silu = gf * sg
            act_ref[0, :, c0:c0 + fc] = (silu * uf).astype(BF16)
            dact = _dot(do2, wdnt_ref[:, c0:c0 + fc])
            dg = (dact * uf * (sg * (1.0 + gf * (1.0 - sg)))).astype(BF16)
            du = (dact * silu).astype(BF16)
            dgu_ref[0, :, c0:c0 + fc] = dg
            dgu_ref[0, :, f + c0:f + c0 + fc] = du
            dh = dh + _dot(dg, wgut_ref[c0:c0 + fc, :]) + _dot(du, wgut_ref[f + c0:f + c0 + fc, :])
        xf = x_ref[0]
        r = _rstd(xf)
        xh = xf * r
        n = xh * g_ref[...]
        h_ref[0] = (n * (1.0 + sc) + sh).astype(BF16)
        _accumulate(dmod_ref, (0, 0), _colsum(dh))
        _accumulate(dmod_ref, (0, 1), _colsum(dh * n))
        dn = dh * (1.0 + sc)
        _accumulate(dg_ref, (0, 0), _colsum(dn * xh))
        dyg = dn * g_ref[...]
        dx2_ref[0] = dx3 + r * (dyg - xh * jnp.mean(dyg * xh, axis=-1, keepdims=True))

    return pl.pallas_call(
        body, name="ffn_bwd", grid=(b, s // tm),
        out_shape=(jax.ShapeDtypeStruct((b, s, d), F32), jax.ShapeDtypeStruct((b, s, d), BF16),
                   jax.ShapeDtypeStruct((b, s, f), BF16), jax.ShapeDtypeStruct((b, s, 2 * f), BF16),
                   jax.ShapeDtypeStruct((b, s, d), BF16), jax.ShapeDtypeStruct((b, 3, 1, d), F32),
                   jax.ShapeDtypeStruct((b, 1, 1, d), F32)),
        in_specs=[_rows(tm, d), _rows(tm, d), _per_b(6, d), _const((1, d)), _rows(tm, 2 * f), _rows(tm, d),
                  _const(w_gu_t.shape), _const(w_down_t.shape)],
        out_specs=(_rows(tm, d), _rows(tm, d), _rows(tm, f), _rows(tm, 2 * f), _rows(tm, d), _per_b(3, d), _per_b(1, d)),
        compiler_params=_params(("arbitrary", "arbitrary")),
    )(dx3, x2, mod, g_ffn, gu, o2, w_gu_t, w_down_t)


def _mix_post_bwd(dx2, mod, o1, y_attn, y_conv, gate, w_out_t, w_o_t):
    b, s, d = dx2.shape
    tm = _row_tile(s)

    def body(dx_ref, mod_ref, o1_ref, ya_ref, yc_ref, gate_ref, woutt_ref, wot_ref, do1_ref, dya_ref, dyc_ref, dgate_ref,
             y_ref, do_ref, dgt_ref):
        _zero_at_first_tile(dgt_ref)
        dx = dx_ref[0]
        do1 = (dx * mod_ref[0, 2]).astype(BF16)
        do1_ref[0] = do1
        _accumulate(dgt_ref, (0, 0), _colsum(dx * o1_ref[0].astype(F32)))
        dy = _dot(do1, woutt_ref[...])
        gv = gate_ref[0]
        sa, sb = _sig(gv[:, :d].astype(F32)), _sig(gv[:, d:].astype(F32))
        ya, yc = ya_ref[0].astype(F32), yc_ref[0].astype(F32)
        y_ref[0] = (sa * ya + sb * yc).astype(BF16)
        dya = (dy * sa).astype(BF16)
        dya_ref[0] = dya
        dyc_ref[0] = (dy * sb).astype(BF16)
        dgate_ref[0, :, :d] = (dy * ya * sa * (1.0 - sa)).astype(BF16)
        dgate_ref[0, :, d:] = (dy * yc * sb * (1.0 - sb)).astype(BF16)
        for h in range(N_HEADS):
            do_ref[0, h] = _dot(dya, wot_ref[h]).astype(BF16)

    row = jax.ShapeDtypeStruct((b, s, d), BF16)
    return pl.pallas_call(
        body, name="mix_post_bwd", grid=(b, s // tm),
        out_shape=(row, row, row, jax.ShapeDtypeStruct((b, s, 2 * d), BF16), row,
                   jax.ShapeDtypeStruct((b, N_HEADS, s, HEAD_PAD), BF16), jax.ShapeDtypeStruct((b, 1, 1, d), F32)),
        in_specs=[_rows(tm, d), _per_b(6, d), _rows(tm, d), _rows(tm, d), _rows(tm, d), _rows(tm, 2 * d),
                  _const(w_out_t.shape), _const(w_o_t.shape)],
        out_specs=(_rows(tm, d), _rows(tm, d), _rows(tm, d), _rows(tm, 2 * d), _rows(tm, d), _heads(tm), _per_b(1, d)),
        compiler_params=_params(("arbitrary", "arbitrary")),
    )(dx2, mod, o1, y_attn, y_conv, gate, w_out_t, w_o_t)


def _conv_bwd(dyc, u, glu, w_dw, g_cn, b_cn, w_pw2_t):
    b, s, cc = u.shape
    d = dyc.shape[2]
    tm = _row_tile(s)
    nt = s // tm
    te = tm + HALO

    def body(dyc_ref, dycn_ref, u_ref, un_ref, glu_ref, glup_ref, w_ref, g_ref, bcn_ref, wpt_ref, dglu_ref, s_ref, dw_ref,
             small_ref, du_ext, uin_ext):
        _zero_at_first_tile(dw_ref, small_ref)
        st = pl.program_id(1)
        dy_all = jnp.concatenate([dyc_ref[0], dycn_ref[0]], axis=0)
        u_all = jnp.concatenate([u_ref[0], un_ref[0]], axis=0).astype(F32)
        ds = _dot(dy_all, wpt_ref[...])
        mu = jnp.mean(u_all, axis=-1, keepdims=True)
        uc = u_all - mu
        rstd = lax.rsqrt(jnp.mean(uc * uc, axis=-1, keepdims=True) + EPS)
        uh = uc * rstd
        ln = uh * g_ref[...] + bcn_ref[...]
        sg = _sig(ln)
        s_ref[0] = (ln * sg)[:tm].astype(BF16)
        dln = ds * (sg * (1.0 + ln * (1.0 - sg)))
        duh = dln * g_ref[...]
        du = rstd * (duh - jnp.mean(duh, axis=-1, keepdims=True) - uh * jnp.mean(duh * uh, axis=-1, keepdims=True))
        row = lax.broadcasted_iota(jnp.int32, (te, 1), 0)
        du = jnp.where(jnp.logical_and(st == nt - 1, row >= tm), 0.0, du)
        du_ext[...] = du
        du_cur = du[:tm]
        _accumulate(small_ref, (0, 0), _colsum((dln * uh)[:tm]))
        _accumulate(small_ref, (0, 1), _colsum(dln[:tm]))
        _accumulate(small_ref, (0, 2), _colsum(du_cur))
        uin_ext[pl.ds(0, HALO), :] = jnp.where(st == 0, 0.0, _glu(glup_ref[0], cc))
        gv = glu_ref[0]
        ga, gb = gv[:, :cc].astype(F32), gv[:, cc:].astype(F32)
        sgb = _sig(gb)
        uin_ext[pl.ds(HALO, tm), :] = ga * sgb
        duin = jnp.zeros((tm, cc), F32)
        for kk in range(CONV_W):
            duin = duin + w_ref[pl.ds(kk, 1), :] * du_ext[pl.ds(CONV_W - 1 - kk, tm), :]
            part = _colsum(du_cur * uin_ext[pl.ds(HALO - CONV_W + 1 + kk, tm), :])
            _accumulate(dw_ref, (0, pl.ds(kk, 1)), part)
        dglu_ref[0, :, :cc] = (duin * sgb).astype(BF16)
        dglu_ref[0, :, cc:] = (duin * ga * sgb * (1.0 - sgb)).astype(BF16)

    return pl.pallas_call(
        body, name="conv_bwd", grid=(b, nt),
        out_shape=(jax.ShapeDtypeStruct((b, s, 2 * cc), BF16), jax.ShapeDtypeStruct((b, s, cc), BF16),
                   jax.ShapeDtypeStruct((b, HALO, cc), F32), jax.ShapeDtypeStruct((b, 3, 1, cc), F32)),
        in_specs=[_rows(tm, d), _halo_next(tm, d, nt), _rows(tm, cc), _halo_next(tm, cc, nt), _rows(tm, 2 * cc),
                  _halo_prev(tm, 2 * cc), _const(w_dw.shape), _const((1, cc)), _const((1, cc)), _const(w_pw2_t.shape)],
        out_specs=(_rows(tm, 2 * cc), _rows(tm, cc), pl.BlockSpec((1, HALO, cc), lambda i, j: (i, 0, 0)), _per_b(3, cc)),
        scratch_shapes=[pltpu.VMEM((te, cc), F32), pltpu.VMEM((te, cc), F32)],
        compiler_params=_params(("arbitrary", "arbitrary")),
    )(dyc, dyc, u, u, glu, glu, w_dw, g_cn, b_cn, w_pw2_t)


def _flash_bwd(q, k, v, o, lse, do):
    b, nh, s, hp = q.shape
    t = _row_tile(s)
    nt = s // t

    def body(q_ref, k_ref, v_ref, o_ref, lse_ref, do_ref, dq_ref, dk_ref, dv_ref, delta):
        j = pl.program_id(2)

        @pl.when(j == 0)
        def _():
            dq_ref[...] = jnp.zeros_like(dq_ref)
            for i in range(nt):
                rows = pl.ds(i * t, t)
                delta[rows, :] = jnp.sum(do_ref[0, 0, rows, :].astype(F32) * o_ref[0, 0, rows, :].astype(F32), axis=-1,
                                         keepdims=True)

        kv, vv = k_ref[0, 0], v_ref[0, 0]

        def step(i, carry):
            dk, dv = carry
            rows = pl.ds(pl.multiple_of(i * t, t), t)
            qv, dov = q_ref[0, 0, rows, :], do_ref[0, 0, rows, :]
            sc = _dot_nt(qv, kv) * ATTN_SCALE
            keep = (i * t + lax.broadcasted_iota(jnp.int32, (t, t), 0)) >= (j * t + lax.broadcasted_iota(jnp.int32, (t, t), 1))
            p = jnp.where(keep, jnp.exp(sc - lse_ref[0, 0, rows, :]), 0.0)
            dv = dv + _dot_tn(p.astype(BF16), dov)
            dp = _dot_nt(dov, vv)
            dsb = (p * (dp - delta[rows, :]) * ATTN_SCALE).astype(BF16)
            dk = dk + _dot_tn(dsb, qv)
            dq_ref[0, 0, rows, :] = dq_ref[0, 0, rows, :] + _dot(dsb, kv)
            return dk, dv

        dk, dv = lax.fori_loop(j, nt, step, (jnp.zeros((t, hp), F32), jnp.zeros((t, hp), F32)))
        dk_ref[0, 0] = dk.astype(BF16)
        dv_ref[0, 0] = dv.astype(BF16)

    tile = pl.BlockSpec((1, 1, t, hp), lambda bb, hh, jj: (bb, hh, jj, 0))
    full = pl.BlockSpec((1, 1, s, hp), lambda bb, hh, jj: (bb, hh, 0, 0))
    return pl.pallas_call(
        body, name="flash_bwd", grid=(b, nh, nt),
        out_shape=(jax.ShapeDtypeStruct((b, nh, s, hp), F32), jax.ShapeDtypeStruct((b, nh, s, hp), BF16),
                   jax.ShapeDtypeStruct((b, nh, s, hp), BF16)),
        in_specs=[full, tile, tile, full, pl.BlockSpec((1, 1, s, 1), lambda bb, hh, jj: (bb, hh, 0, 0)), full],
        out_specs=(full, tile, tile),
        scratch_shapes=[pltpu.VMEM((s, 1), F32)],
        compiler_params=_params(("arbitrary", "arbitrary", "arbitrary")),
    )(q, k, v, o, lse, do)


def _mix_pre_bwd(x, dx2, mod, cos_t, sin_t, g_mix, g_q, g_kv, lat, dq, dk, dv, dglu, dgate, w_lat_t, w_glu_t, w_gate_t,
                 w_q_t, w_k_t, w_v_t):
    b, s, d = x.shape
    ql, kl = g_q.shape[1], g_kv.shape[1]
    wl = lat.shape[2]
    tm = _row_tile(s)

    def body(x_ref, dx2_ref, mod_ref, cos_ref, sin_ref, gm_ref, gq_ref, gkv_ref, lat_ref, dq_ref, dk_ref, dv_ref, dglu_ref,
             dgate_ref, wlt_ref, wgt_ref, wtt_ref, wqt_ref, wkt_ref, wvt_ref, dx_ref, dlat_ref, dqr_ref, qn_ref, kvn_ref,
             h_ref, dmod_ref, dgm_ref, dgq_ref, dgkv_ref):
        _zero_at_first_tile(dmod_ref, dgm_ref, dgq_ref, dgkv_ref)
        cos_v, sin_v = cos_ref[0], sin_ref[0]
        latf = lat_ref[0].astype(F32)
        q_lat, kv_lat = latf[:, :ql], latf[:, ql:ql + kl]
        rq, rk = _rstd(q_lat), _rstd(kv_lat)
        qh, kh = q_lat * rq, kv_lat * rk
        qn_ref[0] = (qh * gq_ref[...]).astype(BF16)
        kvn_ref[0] = (kh * gkv_ref[...]).astype(BF16)
        dqn = jnp.zeros((tm, ql), F32)
        dkvn = jnp.zeros((tm, kl), F32)
        dk_sum = jnp.zeros((tm, HEAD_PAD), F32)
        for h in range(N_HEADS):
            draw = _rope_bwd(dq_ref[0, h], cos_v, sin_v).astype(BF16)
            dqr_ref[0, h] = draw
            dqn = dqn + _dot(draw, wqt_ref[h])
            dkh = dk_ref[0, h]
            dk_sum = dk_sum + dkh.astype(F32)
            dkvn = dkvn + _dot(dkh, wkt_ref[h]) + _dot(dv_ref[0, h], wvt_ref[h])
        lane = lax.broadcasted_iota(jnp.int32, dk_sum.shape, 1)
        dkr = _rope_bwd(jnp.where(lane >= QK_NOPE, dk_sum, 0.0), cos_v, sin_v)
        _accumulate(dgq_ref, (0, 0), _colsum(dqn * qh))
        _accumulate(dgkv_ref, (0, 0), _colsum(dkvn * kh))
        dqg, dkg = dqn * gq_ref[...], dkvn * gkv_ref[...]
        dlat_ref[0, :, :ql] = (rq * (dqg - qh * jnp.mean(dqg * qh, axis=-1, keepdims=True))).astype(BF16)
        dlat_ref[0, :, ql:ql + kl] = (rk * (dkg - kh * jnp.mean(dkg * kh, axis=-1, keepdims=True))).astype(BF16)
        dlat_ref[0, :, ql + kl:] = dkr.astype(BF16)
        dh = _dot(dlat_ref[0], wlt_ref[...]) + _dot(dglu_ref[0], wgt_ref[...]) + _dot(dgate_ref[0], wtt_ref[...])
        sh, sc = mod_ref[0, 0], mod_ref[0, 1]
        xf = x_ref[0]
        r = _rstd(xf)
        xh = xf * r
        n = xh * gm_ref[...]
        h_ref[0] = (n * (1.0 + sc) + sh).astype(BF16)
        _accumulate(dmod_ref, (0, 0), _colsum(dh))
        _accumulate(dmod_ref, (0, 1), _colsum(dh * n))
        dn = dh * (1.0 + sc)
        _accumulate(dgm_ref, (0, 0), _colsum(dn * xh))
        dyg = dn * gm_ref[...]
        dx_ref[0] = dx2_ref[0] + r * (dyg - xh * jnp.mean(dyg * xh, axis=-1, keepdims=True))

    return pl.pallas_call(
        body, name="mix_pre_bwd", grid=(b, s // tm),
        out_shape=(jax.ShapeDtypeStruct((b, s, d), F32), jax.ShapeDtypeStruct((b, s, wl), BF16),
                   jax.ShapeDtypeStruct((b, N_HEADS, s, HEAD_PAD), BF16), jax.ShapeDtypeStruct((b, s, ql), BF16),
                   jax.ShapeDtypeStruct((b, s, kl), BF16), jax.ShapeDtypeStruct((b, s, d), BF16),
                   jax.ShapeDtypeStruct((b, 2, 1, d), F32), jax.ShapeDtypeStruct((b, 1, 1, d), F32),
                   jax.ShapeDtypeStruct((b, 1, 1, ql), F32), jax.ShapeDtypeStruct((b, 1, 1, kl), F32)),
        in_specs=[_rows(tm, d), _rows(tm, d), _per_b(6, d), _rows(tm, HEAD_PAD), _rows(tm, HEAD_PAD), _const((1, d)),
                  _const((1, ql)), _const((1, kl)), _rows(tm, wl), _heads(tm), _heads(tm), _heads(tm),
                  _rows(tm, dglu.shape[2]), _rows(tm, 2 * d), _const(w_lat_t.shape), _const(w_glu_t.shape),
                  _const(w_gate_t.shape), _const(w_q_t.shape), _const(w_k_t.shape), _const(w_v_t.shape)],
        out_specs=(_rows(tm, d), _rows(tm, wl), _heads(tm), _rows(tm, ql), _rows(tm, kl), _rows(tm, d), _per_b(2, d),
                   _per_b(1, d), _per_b(1, ql), _per_b(1, kl)),
        compiler_params=_params(("arbitrary", "arbitrary")),
    )(x, dx2, mod, cos_t, sin_t, g_mix, g_q, g_kv, lat, dq, dk, dv, dglu, dgate, w_lat_t, w_glu_t, w_gate_t, w_q_t, w_k_t,
      w_v_t)


def _matmul_tn(a, bm, name):
    b, ga, s, kd = a.shape
    _, gb, _, nd = bm.shape
    g = max(ga, gb)
    tk, tn = _pick(kd, 1536), _pick(nd, 1536)
    ts = _row_tile(s)

    def body(a_ref, b_ref, o_ref):
        part = _dot_tn(a_ref[0, 0], b_ref[0, 0])
        first = jnp.logical_and(pl.program_id(3) == 0, pl.program_id(4) == 0)

        @pl.when(first)
        def _():
            o_ref[0] = part

        @pl.when(jnp.logical_not(first))
        def _():
            o_ref[0] = o_ref[0] + part

    return pl.pallas_call(
        body, name=name, grid=(g, kd // tk, nd // tn, b, s // ts),
        out_shape=jax.ShapeDtypeStruct((g, kd, nd), F32),
        in_specs=[pl.BlockSpec((1, 1, ts, tk), lambda gg, i, j, bb, ss: (bb, gg if ga > 1 else 0, ss, i)),
                  pl.BlockSpec((1, 1, ts, tn), lambda gg, i, j, bb, ss: (bb, gg if gb > 1 else 0, ss, j))],
        out_specs=pl.BlockSpec((1, tk, tn), lambda gg, i, j, bb, ss: (gg, i, j)),
        compiler_params=_params(("arbitrary",) * 5),
    )(a, bm)


def _adamw(w, g, m, v, name):
    shape = w.shape
    cols = shape[-1]
    rows = w.size // cols
    w2, g2, m2, v2 = (t.reshape(rows, cols) for t in (w, g, m, v))
    tr = rows
    if rows * cols * 4 > (1 << 20):
        for cand in range(8, rows, 8):
            if rows % cand == 0 and cand * cols * 4 <= (1 << 20):
                tr = cand
    c1 = 1.0 - ADAM_B1 ** ADAM_STEP
    c2 = 1.0 - ADAM_B2 ** ADAM_STEP

    def body(w_ref, g_ref, m_ref, v_ref, d_ref, nm_ref, nv_ref):
        gv = g_ref[...]
        nm = ADAM_B1 * m_ref[...] + (1.0 - ADAM_B1) * gv
        nv = ADAM_B2 * v_ref[...] + (1.0 - ADAM_B2) * (gv * gv)
        nm_ref[...] = nm
        nv_ref[...] = nv
        d_ref[...] = -ADAM_LR * ((nm / c1) / (jnp.sqrt(nv / c2) + ADAM_EPS) + ADAM_WD * w_ref[...])

    spec = pl.BlockSpec((tr, cols), lambda i: (i, 0))
    outs = pl.pallas_call(
        body, name=name, grid=(rows // tr,), out_shape=(jax.ShapeDtypeStruct((rows, cols), F32),) * 3,
        in_specs=[spec] * 4, out_specs=(spec,) * 3, compiler_params=_params(("arbitrary",)),
    )(w2, g2, m2, v2)
    return tuple(t.reshape(shape) for t in outs)


GATHERED = (("w_in", 2), ("w_uq", 2), ("w_ukv", 2), ("w_o_attn", 2), ("w_pw2", 2), ("w_out", 1), ("w_gu", 2), ("w_down", 1))


def _from_chunks(chunks, axis):
    _, nl, a, bb = chunks.shape
    if axis == 2:
        return jnp.transpose(chunks, (1, 2, 0, 3)).reshape(nl, a, N_DEV * bb)
    return jnp.transpose(chunks, (1, 0, 2, 3)).reshape(nl, N_DEV * a, bb)


def _to_chunks(full, axis):
    nl, a, bb = full.shape
    if axis == 2:
        t = jnp.transpose(full.reshape(nl, a, N_DEV, bb // N_DEV), (2, 0, 1, 3))
    else:
        t = jnp.transpose(full.reshape(nl, N_DEV, a // N_DEV, bb), (1, 0, 2, 3))
    return t.reshape(N_DEV, -1)


def _swap_halves(t):
    half = QK_ROPE // 2
    return jnp.concatenate([t[..., half:], t[..., :half]], axis=-1)


def _t(w):
    return jnp.swapaxes(w, -1, -2)


def kernel(x, c, positions, w_ada, b_ada, g_mix, w_in, g_q, w_uq, g_kv, w_ukv, w_o_attn, w_dw, b_dw, g_cn, b_cn, w_pw2, w_out, g_ffn, w_gu, w_down, g_final, loss_target, m_w_ada, m_b_ada, m_g_mix, m_w_in, m_g_q, m_w_uq, m_g_kv, m_w_ukv, m_w_o_attn, m_w_dw, m_b_dw, m_g_cn, m_b_cn, m_w_pw2, m_w_out, m_g_ffn, m_w_gu, m_w_down, m_g_final, v_w_ada, v_b_ada, v_g_mix, v_w_in, v_g_q, v_w_uq, v_g_kv, v_w_ukv, v_w_o_attn, v_w_dw, v_b_dw, v_g_cn, v_b_cn, v_w_pw2, v_w_out, v_g_ffn, v_w_gu, v_w_down, v_g_final):
    weights = dict(w_ada=w_ada, b_ada=b_ada, g_mix=g_mix, w_in=w_in, g_q=g_q, w_uq=w_uq, g_kv=g_kv, w_ukv=w_ukv,
                   w_o_attn=w_o_attn, w_dw=w_dw, b_dw=b_dw, g_cn=g_cn, b_cn=b_cn, w_pw2=w_pw2, w_out=w_out, g_ffn=g_ffn,
                   w_gu=w_gu, w_down=w_down, g_final=g_final)
    mom_m = dict(w_ada=m_w_ada, b_ada=m_b_ada, g_mix=m_g_mix, w_in=m_w_in, g_q=m_g_q, w_uq=m_w_uq, g_kv=m_g_kv,
                 w_ukv=m_w_ukv, w_o_attn=m_w_o_attn, w_dw=m_w_dw, b_dw=m_b_dw, g_cn=m_g_cn, b_cn=m_b_cn, w_pw2=m_w_pw2,
                 w_out=m_w_out, g_ffn=m_g_ffn, w_gu=m_w_gu, w_down=m_w_down, g_final=m_g_final)
    mom_v = dict(w_ada=v_w_ada, b_ada=v_b_ada, g_mix=v_g_mix, w_in=v_w_in, g_q=v_g_q, w_uq=v_w_uq, g_kv=v_g_kv,
                 w_ukv=v_w_ukv, w_o_attn=v_w_o_attn, w_dw=v_w_dw, b_dw=v_b_dw, g_cn=v_g_cn, b_cn=v_b_cn, w_pw2=v_w_pw2,
                 w_out=v_w_out, g_ffn=v_g_ffn, w_gu=v_w_gu, w_down=v_w_down, g_final=v_g_final)
    order = list(weights)

    nb, s, d = x.shape
    nl = w_in.shape[0]
    ql, kl, cc = g_q.shape[1], g_kv.shape[1], g_cn.shape[1]
    f = w_down.shape[1] * N_DEV
    h = N_HEADS
    xi, yi, ci = lax.axis_index("x"), lax.axis_index("y"), lax.axis_index("c")
    me = 4 * xi + 2 * yi + ci

    shards = [weights[n] for n, _ in GATHERED]
    packed = _pack(shards, PACK_COLS, 16, BF16)
    got = _all_gather(packed, vmem=False)
    full = {n: _from_chunks(t, ax) for (n, ax), t in zip(GATHERED, _unpack(got, [t.shape for t in shards], lead=(N_DEV,)))}
    c_all, dw_all = _gather_small([c, w_dw])
    c_full = c_all.reshape(N_DEV * nb, d)
    w_dw_full = jnp.transpose(dw_all, (1, 2, 0, 3)).reshape(nl, CONV_W, cc)
    w_dw_full = jnp.pad(w_dw_full, ((0, 0), (0, HALO - CONV_W), (0, 0)))

    o_q, o_kv, o_kr, o_glu, o_gate = 0, ql, ql + kl, ql + kl + QK_ROPE, ql + kl + QK_ROPE + 2 * cc
    wi = full["w_in"]
    w_kr = wi[:, :, o_kr:o_glu]
    w_lat = jnp.concatenate([wi[:, :, o_q:o_kr], jnp.zeros((nl, d, QK_NOPE), BF16), w_kr, _swap_halves(w_kr)], axis=2)
    w_glu, w_gate = wi[:, :, o_glu:o_gate], wi[:, :, o_gate:]
    wq = full["w_uq"].reshape(nl, ql, h, QK_NOPE + QK_ROPE)
    wq = jnp.concatenate([wq, _swap_halves(wq[..., QK_NOPE:])], axis=-1)
    w_q = jnp.transpose(wq, (0, 2, 1, 3))
    wkv = jnp.transpose(full["w_ukv"].reshape(nl, kl, h, QK_NOPE + V_HEAD), (0, 2, 1, 3))
    w_k = jnp.pad(wkv[..., :QK_NOPE], ((0, 0),) * 3 + ((0, HEAD_PAD - QK_NOPE),))
    w_v = jnp.pad(wkv[..., QK_NOPE:], ((0, 0),) * 3 + ((0, HEAD_PAD - V_HEAD),))
    w_o = jnp.pad(full["w_o_attn"].reshape(nl, h, V_HEAD, d), ((0, 0), (0, 0), (0, HEAD_PAD - V_HEAD), (0, 0)))
    w_pw, w_ou, w_g, w_dn = full["w_pw2"], full["w_out"], full["w_gu"], full["w_down"]

    n_ada = w_ada.shape[2]
    b_cols = lax.dynamic_slice_in_dim(b_ada, me * n_ada, n_ada, axis=1).reshape(nl, 1, n_ada)
    mod_part = _ada_fwd(c_full, w_ada, b_cols)
    (mod_all,) = _gather_small([mod_part])
    mod_all = jnp.transpose(mod_all, (1, 2, 0, 3)).reshape(nl, N_DEV * nb, N_DEV * n_ada)
    mod = lax.dynamic_slice_in_dim(mod_all, me * nb, nb, axis=1).reshape(nl, nb, 6, 1, d)

    inv_freq = ROPE_THETA ** (-jnp.arange(0, QK_ROPE, 2, dtype=F32) / QK_ROPE)
    zeros = lambda n: jnp.zeros((n,), F32)
    freq_row = jnp.concatenate([zeros(QK_NOPE), inv_freq, inv_freq, zeros(HEAD_PAD - QK_NOPE - QK_ROPE)]).reshape(1, -1)
    ones = jnp.ones((QK_ROPE // 2,), F32)
    sign_row = jnp.concatenate([zeros(QK_NOPE), -ones, ones, zeros(HEAD_PAD - QK_NOPE - QK_ROPE)]).reshape(1, -1)
    cos_t, sin_t = _rope_tables(positions.astype(F32).reshape(nb, s, 1), freq_row, sign_row)

    row = lambda t, l: t[l].reshape(1, -1)

    saved = []
    xc = x
    for l in range(nl):
        lat, glu, gate, qh, kh, vh = _mix_pre_fwd(xc, mod[l], cos_t, sin_t, row(g_mix, l), row(g_q, l), row(g_kv, l),
                                                  w_lat[l], w_glu[l], w_gate[l], w_q[l], w_k[l], w_v[l])
        o, lse = _flash_fwd(qh, kh, vh)
        u, y_conv = _conv_fwd(glu, w_dw_full[l], row(b_dw, l), row(g_cn, l), row(b_cn, l), w_pw[l])
        x2, y_attn, o1 = _mix_post_fwd(xc, mod[l], o, y_conv, gate, w_o[l], w_ou[l])
        x3, gu, o2 = _ffn_fwd(x2, mod[l], row(g_ffn, l), w_g[l], w_dn[l])
        saved.append(dict(x=xc, lat=lat, glu=glu, gate=gate, q=qh, k=kh, v=vh, o=o, lse=lse, u=u, y_conv=y_conv, x2=x2,
                          y_attn=y_attn, o1=o1, gu=gu, o2=o2))
        xc = x3

    dx, loss_part, dgf_part = _loss_bwd(xc, loss_target, g_final.reshape(1, d))
    loss = lax.psum(0.5 / d * jnp.sum(loss_part), AXES)

    g4 = lambda t: t[:, None]
    gw = {n: [None] * nl for n in ("w_in", "w_uq", "w_ukv", "w_o_attn", "w_pw2", "w_out", "w_gu", "w_down")}
    small = {n: [None] * nl for n in ("g_mix", "g_q", "g_kv", "b_dw", "g_cn", "b_cn", "g_ffn", "w_dw")}
    dmod = [None] * nl
    for l in reversed(range(nl)):
        sv = saved[l]
        dx2, do2, act, dgu, h2, dmod2, dgffn = _ffn_bwd(dx, sv["x2"], mod[l], row(g_ffn, l), sv["gu"], sv["o2"],
                                                       _t(w_g[l]), _t(w_dn[l]))
        gw["w_gu"][l] = _matmul_tn(g4(h2), g4(dgu), "grad_w_gu")[0]
        gw["w_down"][l] = _matmul_tn(g4(act), g4(do2), "grad_w_down")[0]
        do1, dya, dyc, dgate, yv, do_h, dgt1 = _mix_post_bwd(dx2, mod[l], sv["o1"], sv["y_attn"], sv["y_conv"], sv["gate"],
                                                           _t(w_ou[l]), _t(w_o[l]))
        gw["w_out"][l] = _matmul_tn(g4(yv), g4(do1), "grad_w_out")[0]
        dwo = _matmul_tn(sv["o"], g4(dya), "grad_w_o")
        gw["w_o_attn"][l] = dwo[:, :V_HEAD].reshape(h * V_HEAD, d)
        dglu, s_act, ddw, csmall = _conv_bwd(dyc, sv["u"], sv["glu"], w_dw_full[l], row(g_cn, l), row(b_cn, l), _t(w_pw[l]))
        gw["w_pw2"][l] = _matmul_tn(g4(s_act), g4(dyc), "grad_w_pw2")[0]
        dq, dk, dv = _flash_bwd(sv["q"], sv["k"], sv["v"], sv["o"], sv["lse"], do_h)
        dx, dlat, dqr, qn, kvn, h1, dmod1, dgm, dgq, dgkv = _mix_pre_bwd(
            sv["x"], dx2, mod[l], cos_t, sin_t, row(g_mix, l), row(g_q, l), row(g_kv, l), sv["lat"], dq, dk, dv, dglu,
            dgate, _t(w_lat[l]), _t(w_glu[l]), _t(w_gate[l]), _t(w_q[l]), _t(w_k[l]), _t(w_v[l]))
        dwl = _matmul_tn(g4(h1), g4(dlat), "grad_w_lat")[0]
        dwg = _matmul_tn(g4(h1), g4(dglu), "grad_w_glu")[0]
        dwt = _matmul_tn(g4(h1), g4(dgate), "grad_w_gate")[0]
        kr0 = o_kr + QK_NOPE
        dkr = dwl[:, kr0:kr0 + QK_ROPE] + _swap_halves(dwl[:, kr0 + QK_ROPE:])
        gw["w_in"][l] = jnp.concatenate([dwl[:, :o_kr], dkr, dwg, dwt], axis=1)
        dwq = _matmul_tn(g4(qn), dqr, "grad_w_q")
        dwq = jnp.concatenate([dwq[..., :QK_NOPE], dwq[..., QK_NOPE:QK_NOPE + QK_ROPE] + _swap_halves(dwq[..., QK_NOPE + QK_ROPE:])],
                              axis=-1)
        gw["w_uq"][l] = jnp.transpose(dwq, (1, 0, 2)).reshape(ql, h * (QK_NOPE + QK_ROPE))
        dwk = _matmul_tn(g4(kvn), dk, "grad_w_k")
        dwv = _matmul_tn(g4(kvn), dv, "grad_w_v")
        dwkv = jnp.concatenate([dwk[..., :QK_NOPE], dwv[..., :V_HEAD]], axis=-1)
        gw["w_ukv"][l] = jnp.transpose(dwkv, (1, 0, 2)).reshape(kl, h * (QK_NOPE + V_HEAD))
        dmod[l] = jnp.concatenate([dmod1[:, :, 0], dgt1[:, :, 0], dmod2[:, :, 0]], axis=1).reshape(nb, 6 * d)
        bsum = lambda t: jnp.sum(t, axis=0).reshape(-1)
        small["g_mix"][l], small["g_q"][l], small["g_kv"][l], small["g_ffn"][l] = bsum(dgm), bsum(dgq), bsum(dgkv), bsum(dgffn)
        cs = jnp.sum(csmall, axis=0)[:, 0]
        small["g_cn"][l], small["b_cn"][l], small["b_dw"][l] = cs[0], cs[1], cs[2]
        small["w_dw"][l] = jnp.sum(ddw, axis=0)[:CONV_W]
    grad_x = dx

    (dmod_all,) = _gather_small([jnp.stack(dmod)])
    dmod_full = jnp.transpose(dmod_all, (1, 0, 2, 3)).reshape(nl, N_DEV * nb, 6 * d)
    dmod_cols = lax.dynamic_slice_in_dim(dmod_full, me * n_ada, n_ada, axis=2)
    grad_w_ada, grad_b_ada = _ada_bwd(c_full, dmod_cols, dmod_full)
    grads = {"w_ada": grad_w_ada, "b_ada": grad_b_ada.reshape(nl, 6 * d)}

    small_names = ("g_mix", "g_q", "g_kv", "b_dw", "g_cn", "b_cn", "g_ffn", "w_dw")
    small_arrs = [jnp.stack(small[n]) for n in small_names] + [jnp.sum(dgf_part, axis=0).reshape(-1)]
    sm_packed = _pack(small_arrs, LANES, 8, F32)
    sm_sum = _sum_devices(_all_gather(sm_packed, vmem=True))
    sm_out = _unpack(sm_sum, [a.shape for a in small_arrs])
    for n, t in zip(small_names, sm_out):
        grads[n] = t
    grads["g_final"] = sm_out[-1]
    n_dw = w_dw.shape[2]
    grads["w_dw"] = lax.dynamic_slice_in_dim(grads["w_dw"], me * n_dw, n_dw, axis=2)

    chunks = jnp.concatenate([_to_chunks(jnp.stack(gw[n]), ax) for n, ax in GATHERED], axis=1)
    n_flat = chunks.shape[1]
    unit = PACK_COLS * 16
    n_pad = -(-n_flat // unit) * unit
    chunks = jnp.pad(chunks, ((0, 0), (0, n_pad - n_flat))).astype(BF16)
    r = n_pad // PACK_COLS
    by_core = jnp.transpose(chunks.reshape(2, 2, 2, r, PACK_COLS), (2, 0, 1, 3, 4)).reshape(N_DEV, r, PACK_COLS)
    from_sibling = _sibling_exchange(by_core)
    own_half = lax.dynamic_slice_in_dim(by_core, 4 * ci, 4, axis=0)
    p32, p16 = _add_pair(own_half, from_sibling)
    from_chips = _chip_exchange(p16)
    mine32 = lax.dynamic_index_in_dim(p32, 2 * xi + yi, axis=0, keepdims=False)
    reduced = _add_four(mine32, from_chips)
    for (n, _), t in zip(GATHERED, _unpack(reduced, [t.shape for t in shards])):
        grads[n] = t

    delta, new_m, new_v = {}, {}, {}
    for n in order:
        delta[n], new_m[n], new_v[n] = _adamw(weights[n], grads[n], mom_m[n], mom_v[n], "adamw_" + n)

    return (loss, grad_x, *[grads[n] for n in order], *[delta[n] for n in order], *[new_m[n] for n in order],
            *[new_v[n] for n in order])
```

```python
import functools

import jax
import jax.numpy as jnp
from jax import lax
from jax.experimental import pallas as pl
from jax.experimental.pallas import tpu as pltpu

F32, BF16 = jnp.float32, jnp.bfloat16
MESH = pl.DeviceIdType.MESH
AXES = ("x", "y", "c")
N_DEV = 8

N_HEADS = 8
QK_NOPE = 64
QK_ROPE = 32
V_HEAD = 64
HEAD_PAD = 128
CONV_W = 31
HALO = 32
EPS = 1e-6
ROPE_THETA = 10000.0
NEG_INF = -1e30
ATTN_SCALE = (QK_NOPE + QK_ROPE) ** -0.5

ADAM_LR, ADAM_B1, ADAM_B2, ADAM_EPS, ADAM_WD, ADAM_STEP = 0.001, 0.9, 0.999, 1e-08, 0.01, 10

LANES = 128
VMEM_LIMIT = 60 * 1024 * 1024


def _params(sem=None):
    return pltpu.CompilerParams(dimension_semantics=sem, vmem_limit_bytes=VMEM_LIMIT)


def _pick(n, cap):
    if n <= cap:
        return n
    best = None
    for d in range(LANES, cap + 1, LANES):
        if n % d == 0:
            best = d
    assert best is not None, (n, cap)
    return best


def _div_tile(n, cap, mult):
    best = None
    for d in range(mult, min(n, cap) + 1, mult):
        if n % d == 0:
            best = d
    assert best is not None, (n, cap, mult)
    return best


def _row_tile(s, cap=512):
    return cap if s % cap == 0 and s >= 2 * cap else s // 2


def _sig(v):
    return 1.0 / (1.0 + jnp.exp(-v))


def _rstd(v):
    return lax.rsqrt(jnp.mean(v * v, axis=-1, keepdims=True) + EPS)


def _dot(a, b):
    return jnp.dot(a, b, preferred_element_type=F32)


def _dot_nt(a, b):
    return lax.dot_general(a, b, (((1,), (1,)), ((), ())), preferred_element_type=F32)


def _dot_tn(a, b):
    return lax.dot_general(a, b, (((0,), (0,)), ((), ())), preferred_element_type=F32)


def _rope(v, cos_t, sin_t):
    return v * cos_t + pltpu.roll(v, HEAD_PAD - QK_ROPE, 1) * sin_t


def _rope_bwd(dv, cos_t, sin_t):
    return dv * cos_t + pltpu.roll(dv * sin_t, QK_ROPE, 1)


def _const(shape):
    n = len(shape)
    return pl.BlockSpec(shape, lambda *_: (0,) * n, pipeline_mode=pl.Buffered(1))


def _rows(tm, w):
    return pl.BlockSpec((1, tm, w), lambda b, s: (b, s, 0))


def _per_b(r, w):
    return pl.BlockSpec((1, r, 1, w), lambda b, s: (b, 0, 0, 0))


def _all_gather(arrs, vmem, name):
    n = len(arrs)
    space = pltpu.VMEM if vmem else pl.ANY

    def body(*refs):
        x_refs, out_refs = refs[:n], refs[n:2 * n]
        send_sems, recv_sems, local_sems = refs[2 * n:]
        x_, y_, c_ = lax.axis_index("x"), lax.axis_index("y"), lax.axis_index("c")
        me, sibling = (x_, y_, c_), (x_, y_, 1 - c_)
        chips = [(1 - x_, y_), (x_, 1 - y_), (1 - x_, 1 - y_)]

        def copy(a, k, block, to, own=False):
            px, py, pc = block
            slot = out_refs[a].at[4 * px + 2 * py + pc]
            return pltpu.make_async_remote_copy(
                src_ref=x_refs[a] if own else slot, dst_ref=slot, send_sem=send_sems.at[k * n + a],
                recv_sem=recv_sems.at[k * n + a], device_id=to, device_id_type=MESH)

        mine = [pltpu.make_async_copy(x_refs[a], out_refs[a].at[4 * x_ + 2 * y_ + c_], local_sems.at[a]) for a in range(n)]
        sent = []
        for a in range(n):
            mine[a].start()
            sent.append(copy(a, 0, me, sibling, own=True))
            sent += [copy(a, 1 + j, me, (*chip, c_), own=True) for j, chip in enumerate(chips)]
        for cp in sent:
            cp.start()
        for j, chip in enumerate(chips):
            for a in range(n):
                copy(a, 1 + j, (*chip, c_), me).wait_recv()
                passed = copy(a, 4 + j, (*chip, c_), sibling)
                passed.start()
                sent.append(passed)
        for a in range(n):
            copy(a, 0, sibling, me).wait_recv()
            for j, chip in enumerate(chips):
                copy(a, 4 + j, (*chip, 1 - c_), me).wait_recv()
        for cp in sent:
            cp.wait_send()
        for cp in mine:
            cp.wait()

    return pl.pallas_call(
        body, name=name,
        out_shape=[jax.ShapeDtypeStruct((N_DEV,) + t.shape, t.dtype) for t in arrs],
        in_specs=[pl.BlockSpec(memory_space=space)] * n, out_specs=[pl.BlockSpec(memory_space=space)] * n,
        scratch_shapes=[pltpu.SemaphoreType.DMA((7 * n,)), pltpu.SemaphoreType.DMA((7 * n,)), pltpu.SemaphoreType.DMA((n,))],
        compiler_params=pltpu.CompilerParams(vmem_limit_bytes=VMEM_LIMIT),
    )(*arrs)


def _sibling_exchange(gs):
    n = len(gs)

    def body(*refs):
        g_refs, out_refs = refs[:n], refs[n:2 * n]
        send_sems, recv_sems = refs[2 * n:]
        x_, y_, c_ = lax.axis_index("x"), lax.axis_index("y"), lax.axis_index("c")
        cps = [pltpu.make_async_remote_copy(
            src_ref=g_refs[a].at[pl.ds(4 * (1 - c_), 4)], dst_ref=out_refs[a], send_sem=send_sems.at[a],
            recv_sem=recv_sems.at[a], device_id=(x_, y_, 1 - c_), device_id_type=MESH) for a in range(n)]
        for cp in cps:
            cp.start()
        for cp in cps:
            cp.wait()

    return pl.pallas_call(
        body, name="grad_sibling_exchange",
        out_shape=[jax.ShapeDtypeStruct((4,) + g.shape[1:], g.dtype) for g in gs],
        in_specs=[pl.BlockSpec(memory_space=pl.ANY)] * n, out_specs=[pl.BlockSpec(memory_space=pl.ANY)] * n,
        scratch_shapes=[pltpu.SemaphoreType.DMA((n,)), pltpu.SemaphoreType.DMA((n,))],
    )(*gs)


def _chip_exchange(ps):
    n = len(ps)

    def body(*refs):
        p_refs, out_refs = refs[:n], refs[n:2 * n]
        send_sems, recv_sems = refs[2 * n:]
        x_, y_, c_ = lax.axis_index("x"), lax.axis_index("y"), lax.axis_index("c")
        chips = [(1 - x_, y_), (x_, 1 - y_), (1 - x_, 1 - y_)]
        cps = [pltpu.make_async_remote_copy(
            src_ref=p_refs[a].at[2 * px + py], dst_ref=out_refs[a].at[k], send_sem=send_sems.at[k * n + a],
            recv_sem=recv_sems.at[k * n + a], device_id=(px, py, c_), device_id_type=MESH)
            for k, (px, py) in enumerate(chips) for a in range(n)]
        for cp in cps:
            cp.start()
        for cp in cps:
            cp.wait()

    return pl.pallas_call(
        body, name="grad_chip_exchange",
        out_shape=[jax.ShapeDtypeStruct((3,) + p.shape[1:], p.dtype) for p in ps],
        in_specs=[pl.BlockSpec(memory_space=pl.ANY)] * n, out_specs=[pl.BlockSpec(memory_space=pl.ANY)] * n,
        scratch_shapes=[pltpu.SemaphoreType.DMA((3 * n,)), pltpu.SemaphoreType.DMA((3 * n,))],
    )(*ps)


def _rows_2d(t, lead):
    return t.reshape((lead, -1, t.shape[-1]) if lead else (-1, t.shape[-1]))


def _grad_row_tile(rows, cols):
    return _div_tile(rows, max(16, (1 << 19) // cols), 16)


def _add_halves(g, got, core, name):
    _, r, n = g.shape
    tr = _grad_row_tile(r, n)

    def body(core_ref, g_ref, r_ref, o_ref):
        o_ref[...] = (g_ref[...].astype(F32) + r_ref[...].astype(F32)).astype(BF16)

    spec = pl.BlockSpec((1, tr, n), lambda i, j, core_ref: (i, j, 0))
    return pl.pallas_call(
        body, name=name,
        grid_spec=pltpu.PrefetchScalarGridSpec(
            num_scalar_prefetch=1, grid=(4, r // tr),
            in_specs=[pl.BlockSpec((1, tr, n), lambda i, j, core_ref: (4 * core_ref[0] + i, j, 0)), spec], out_specs=spec),
        out_shape=jax.ShapeDtypeStruct((4, r, n), BF16), compiler_params=_params(("arbitrary", "arbitrary")),
    )(core, g, got)


def _sum_devices(g):
    _, m, n = g.shape

    def body(g_ref, o_ref):
        s = g_ref[0]
        for j in range(1, N_DEV):
            s = s + g_ref[j]
        o_ref[...] = s

    return pl.pallas_call(body, name="small_grad_sum", out_shape=jax.ShapeDtypeStruct((m, n), F32),
                          compiler_params=pltpu.CompilerParams(vmem_limit_bytes=VMEM_LIMIT))(g)


def _ada_fwd(c_full, w_ada, b_cols):
    nl, d, n = w_ada.shape
    nb = c_full.shape[0]

    def body(c_ref, w_ref, b_ref, o_ref):
        cv = c_ref[...]
        act = cv * _sig(cv)
        o_ref[0] = jnp.dot(act, w_ref[0], preferred_element_type=F32, precision=lax.Precision.HIGHEST) + b_ref[0]

    return pl.pallas_call(
        body, name="ada_fwd", grid=(nl,), out_shape=jax.ShapeDtypeStruct((nl, nb, n), F32),
        in_specs=[pl.BlockSpec((nb, d), lambda l: (0, 0)), pl.BlockSpec((1, d, n), lambda l: (l, 0, 0)),
                  pl.BlockSpec((1, 1, n), lambda l: (l, 0, 0))],
        out_specs=pl.BlockSpec((1, nb, n), lambda l: (l, 0, 0)), compiler_params=_params(("arbitrary",)),
    )(c_full, w_ada, b_cols)


def _ada_bwd(c_full, dmod_cols, dmod_full):
    nl, nb, n = dmod_cols.shape
    d = c_full.shape[1]
    nfull = dmod_full.shape[2]

    def body(c_ref, dc_ref, df_ref, gw_ref, gb_ref):
        cv = c_ref[...]
        act = cv * _sig(cv)
        gw_ref[0] = lax.dot_general(act, dc_ref[0], (((0,), (0,)), ((), ())), preferred_element_type=F32,
                                    precision=lax.Precision.HIGHEST)
        gb_ref[0] = jnp.sum(df_ref[0], axis=0, keepdims=True)

    return pl.pallas_call(
        body, name="ada_bwd", grid=(nl,),
        out_shape=(jax.ShapeDtypeStruct((nl, d, n), F32), jax.ShapeDtypeStruct((nl, 1, nfull), F32)),
        in_specs=[pl.BlockSpec((nb, d), lambda l: (0, 0)), pl.BlockSpec((1, nb, n), lambda l: (l, 0, 0)),
                  pl.BlockSpec((1, nb, nfull), lambda l: (l, 0, 0))],
        out_specs=(pl.BlockSpec((1, d, n), lambda l: (l, 0, 0)), pl.BlockSpec((1, 1, nfull), lambda l: (l, 0, 0))),
        compiler_params=_params(("arbitrary",)),
    )(c_full, dmod_cols, dmod_full)


def _rope_tables(pos, freq_row, sign_row):
    b, s, _ = pos.shape
    tm = _row_tile(s)

    def body(p_ref, f_ref, g_ref, c_ref, s_ref):
        ang = p_ref[0] * f_ref[...]
        lane = lax.broadcasted_iota(jnp.int32, ang.shape, 1)
        c_ref[0] = jnp.where(lane < QK_NOPE, 1.0, jnp.where(lane < QK_NOPE + QK_ROPE, jnp.cos(ang), 0.0))
        s_ref[0] = g_ref[...] * jnp.sin(ang)

    return pl.pallas_call(
        body, name="rope_tables", grid=(b, s // tm),
        out_shape=(jax.ShapeDtypeStruct((b, s, HEAD_PAD), F32),) * 2,
        in_specs=[_rows(tm, 1), pl.BlockSpec((1, HEAD_PAD), lambda i, j: (0, 0)),
                  pl.BlockSpec((1, HEAD_PAD), lambda i, j: (0, 0))],
        out_specs=(_rows(tm, HEAD_PAD),) * 2, compiler_params=_params(("arbitrary", "arbitrary")),
    )(pos, freq_row, sign_row)


def _mix_pre_fwd(x, mod, cos_t, sin_t, g_mix, g_q, g_kv, w_lat, w_glu, w_gate, w_q, w_k, w_v):
    b, s, d = x.shape
    ql, kl = g_q.shape[1], g_kv.shape[1]
    wl, wg, wt = w_lat.shape[1], w_glu.shape[1], w_gate.shape[1]
    hw = N_HEADS * HEAD_PAD
    tm = _row_tile(s)

    def body(x_ref, mod_ref, cos_ref, sin_ref, gm_ref, gq_ref, gkv_ref, wlat_ref, wglu_ref, wgate_ref, wq_ref, wk_ref,
             wv_ref, lat_ref, glu_ref, gate_ref, q_ref, k_ref, v_ref):
        xf = x_ref[0]
        sh, sc = mod_ref[0, 0], mod_ref[0, 1]
        hb = ((xf * _rstd(xf) * gm_ref[...]) * (1.0 + sc) + sh).astype(BF16)
        glu_ref[0] = _dot(hb, wglu_ref[...]).astype(BF16)
        gate_ref[0] = _dot(hb, wgate_ref[...]).astype(BF16)
        lat = _dot(hb, wlat_ref[...]).astype(BF16)
        lat_ref[0] = lat
        latf = lat.astype(F32)
        q_lat, kv_lat, kr_sec = latf[:, :ql], latf[:, ql:ql + kl], latf[:, ql + kl:]
        qn = (q_lat * _rstd(q_lat) * gq_ref[...]).astype(BF16)
        kvn = (kv_lat * _rstd(kv_lat) * gkv_ref[...]).astype(BF16)
        cos_v, sin_v = cos_ref[0], sin_ref[0]
        lane = lax.broadcasted_iota(jnp.int32, kr_sec.shape, 1)
        kr = jnp.where(lane >= QK_NOPE, _rope(kr_sec, cos_v, sin_v), 0.0)
        q_all, k_all = _dot(qn, wq_ref[...]), _dot(kvn, wk_ref[...])
        v_ref[0] = _dot(kvn, wv_ref[...]).astype(BF16)
        for h in range(N_HEADS):
            cols = slice(h * HEAD_PAD, (h + 1) * HEAD_PAD)
            q_ref[0, :, cols] = _rope(q_all[:, cols], cos_v, sin_v).astype(BF16)
            k_ref[0, :, cols] = (k_all[:, cols] + kr).astype(BF16)

    hshape = jax.ShapeDtypeStruct((b, s, hw), BF16)
    return pl.pallas_call(
        body, name="mix_pre_fwd", grid=(b, s // tm),
        out_shape=(jax.ShapeDtypeStruct((b, s, wl), BF16), jax.ShapeDtypeStruct((b, s, wg), BF16),
                   jax.ShapeDtypeStruct((b, s, wt), BF16), hshape, hshape, hshape),
        in_specs=[_rows(tm, d), _per_b(6, d), _rows(tm, HEAD_PAD), _rows(tm, HEAD_PAD), _const((1, d)), _const((1, ql)),
                  _const((1, kl)), _const(w_lat.shape), _const(w_glu.shape), _const(w_gate.shape), _const(w_q.shape),
                  _const(w_k.shape), _const(w_v.shape)],
        out_specs=(_rows(tm, wl), _rows(tm, wg), _rows(tm, wt), _rows(tm, hw), _rows(tm, hw), _rows(tm, hw)),
        compiler_params=_params(("arbitrary", "arbitrary")),
    )(x, mod, cos_t, sin_t, g_mix, g_q, g_kv, w_lat, w_glu, w_gate, w_q, w_k, w_v)


def _causal_mask(tq, tk):
    return lax.broadcasted_iota(jnp.int32, (tq, tk), 0) >= lax.broadcasted_iota(jnp.int32, (tq, tk), 1)


def _flash_fwd(q, k, v):
    b, s, hw = q.shape
    nh, hp = hw // HEAD_PAD, HEAD_PAD
    t = _row_tile(s)

    def body(q_ref, k_ref, v_ref, o_ref, lse_ref):
        i = pl.program_id(2)
        qv = q_ref[0]

        def step(j, carry, masked):
            m, l, acc = carry
            rows = pl.ds(pl.multiple_of(j * t, t), t)
            sc = _dot_nt(qv, k_ref[0, rows, :]) * ATTN_SCALE
            if masked:
                sc = jnp.where(_causal_mask(t, t), sc, NEG_INF)
            m_new = jnp.maximum(m, jnp.max(sc, axis=-1, keepdims=True))
            p = jnp.exp(sc - m_new)
            alpha = jnp.exp(m - m_new)
            l = alpha * l + jnp.sum(p, axis=-1, keepdims=True)
            acc = alpha * acc + _dot(p.astype(BF16), v_ref[0, rows, :])
            return m_new, l, acc

        init = (jnp.full((t, 1), NEG_INF, F32), jnp.zeros((t, 1), F32), jnp.zeros((t, hp), F32))
        carry = lax.fori_loop(0, i, lambda j, cr: step(j, cr, False), init)
        m, l, acc = step(i, carry, True)
        o_ref[0] = (acc / l).astype(BF16)
        lse_ref[0, 0] = m + jnp.log(l)

    tile = pl.BlockSpec((1, t, hp), lambda bb, hh, ii: (bb, ii, hh))
    full = pl.BlockSpec((1, s, hp), lambda bb, hh, ii: (bb, 0, hh))
    return pl.pallas_call(
        body, name="flash_fwd", grid=(b, nh, s // t),
        out_shape=(jax.ShapeDtypeStruct((b, s, hw), BF16), jax.ShapeDtypeStruct((b, nh, s, 1), F32)),
        in_specs=[tile, full, full],
        out_specs=(tile, pl.BlockSpec((1, 1, t, 1), lambda bb, hh, ii: (bb, hh, ii, 0))),
        compiler_params=_params(("arbitrary", "arbitrary", "arbitrary")),
    )(q, k, v)


def _halo_prev(tm, w):
    r = tm // HALO
    return pl.BlockSpec((1, HALO, w), lambda b, s: (b, jnp.maximum(s * r - 1, 0), 0))


def _halo_next(tm, w, n_tiles):
    r = tm // HALO
    return pl.BlockSpec((1, HALO, w), lambda b, s: (b, jnp.minimum((s + 1) * r, n_tiles * r - 1), 0))


def _glu(v, cc):
    a, g = v[:, :cc].astype(F32), v[:, cc:].astype(F32)
    return a * _sig(g)


def _conv_fwd(glu, w_dw, b_dw, g_cn, b_cn, w_pw2):
    b, s, w2 = glu.shape
    cc = w2 // 2
    d = w_pw2.shape[1]
    tm = _row_tile(s)

    def body(cur_ref, prev_ref, w_ref, bdw_ref, g_ref, bcn_ref, wp_ref, u_ref, y_ref, ext):
        first = pl.program_id(1) == 0
        ext[pl.ds(0, HALO), :] = jnp.where(first, 0.0, _glu(prev_ref[0], cc))
        ext[pl.ds(HALO, tm), :] = _glu(cur_ref[0], cc)
        u = jnp.zeros((tm, cc), F32) + bdw_ref[...]
        for kk in range(CONV_W):
            u = u + w_ref[pl.ds(kk, 1), :] * ext[pl.ds(HALO - CONV_W + 1 + kk, tm), :]
        ub = u.astype(BF16)
        u_ref[0] = ub
        uf = ub.astype(F32)
        mu = jnp.mean(uf, axis=-1, keepdims=True)
        uc = uf - mu
        ln = uc * lax.rsqrt(jnp.mean(uc * uc, axis=-1, keepdims=True) + EPS) * g_ref[...] + bcn_ref[...]
        y_ref[0] = _dot((ln * _sig(ln)).astype(BF16), wp_ref[...]).astype(BF16)

    return pl.pallas_call(
        body, name="conv_fwd", grid=(b, s // tm),
        out_shape=(jax.ShapeDtypeStruct((b, s, cc), BF16), jax.ShapeDtypeStruct((b, s, d), BF16)),
        in_specs=[_rows(tm, w2), _halo_prev(tm, w2), _const(w_dw.shape), _const((1, cc)), _const((1, cc)), _const((1, cc)),
                  _const(w_pw2.shape)],
        out_specs=(_rows(tm, cc), _rows(tm, d)),
        scratch_shapes=[pltpu.VMEM((tm + HALO, cc), F32)],
        compiler_params=_params(("arbitrary", "arbitrary")),
    )(glu, glu, w_dw, b_dw, g_cn, b_cn, w_pw2)


def _mix_post_fwd(x, mod, o, y_conv, gate, w_o, w_out):
    b, s, d = x.shape
    hw = o.shape[2]
    tm = _row_tile(s)

    def body(x_ref, mod_ref, o_ref, yc_ref, gate_ref, wo_ref, wout_ref, x2_ref, ya_ref, o1_ref):
        yab = _dot(o_ref[0], wo_ref[...]).astype(BF16)
        ya_ref[0] = yab
        gv = gate_ref[0]
        y = _sig(gv[:, :d].astype(F32)) * yab.astype(F32) + _sig(gv[:, d:].astype(F32)) * yc_ref[0].astype(F32)
        o1 = _dot(y.astype(BF16), wout_ref[...])
        o1_ref[0] = o1.astype(BF16)
        x2_ref[0] = x_ref[0] + mod_ref[0, 2] * o1

    return pl.pallas_call(
        body, name="mix_post_fwd", grid=(b, s // tm),
        out_shape=(jax.ShapeDtypeStruct((b, s, d), F32), jax.ShapeDtypeStruct((b, s, d), BF16),
                   jax.ShapeDtypeStruct((b, s, d), BF16)),
        in_specs=[_rows(tm, d), _per_b(6, d), _rows(tm, hw), _rows(tm, d), _rows(tm, 2 * d), _const(w_o.shape),
                  _const(w_out.shape)],
        out_specs=(_rows(tm, d), _rows(tm, d), _rows(tm, d)),
        compiler_params=_params(("arbitrary", "arbitrary")),
    )(x, mod, o, y_conv, gate, w_o, w_out)


def _ffn_fwd(x2, mod, g_ffn, w_gu, w_down):
    b, s, d = x2.shape
    f = w_down.shape[0]
    fc = _pick(f, 512)
    tm = _row_tile(s)

    def body(x_ref, mod_ref, g_ref, wgu_ref, wdn_ref, x3_ref, gu_ref, o2_ref):
        xf = x_ref[0]
        hb = ((xf * _rstd(xf) * g_ref[...]) * (1.0 + mod_ref[0, 4]) + mod_ref[0, 3]).astype(BF16)
        o2 = jnp.zeros((tm, d), F32)
        for c0 in range(0, f, fc):
            gb = _dot(hb, wgu_ref[:, c0:c0 + fc]).astype(BF16)
            ub = _dot(hb, wgu_ref[:, f + c0:f + c0 + fc]).astype(BF16)
            gu_ref[0, :, c0:c0 + fc] = gb
            gu_ref[0, :, f + c0:f + c0 + fc] = ub
            gf = gb.astype(F32)
            act = (gf * _sig(gf) * ub.astype(F32)).astype(BF16)
            o2 = o2 + _dot(act, wdn_ref[c0:c0 + fc, :])
        o2_ref[0] = o2.astype(BF16)
        x3_ref[0] = xf + mod_ref[0, 5] * o2

    return pl.pallas_call(
        body, name="ffn_fwd", grid=(b, s // tm),
        out_shape=(jax.ShapeDtypeStruct((b, s, d), F32), jax.ShapeDtypeStruct((b, s, 2 * f), BF16),
                   jax.ShapeDtypeStruct((b, s, d), BF16)),
        in_specs=[_rows(tm, d), _per_b(6, d), _const((1, d)), _const(w_gu.shape), _const(w_down.shape)],
        out_specs=(_rows(tm, d), _rows(tm, 2 * f), _rows(tm, d)),
        compiler_params=_params(("arbitrary", "arbitrary")),
    )(x2, mod, g_ffn, w_gu, w_down)


def _zero_at_first_tile(*refs):
    @pl.when(pl.program_id(1) == 0)
    def _():
        for ref in refs:
            ref[...] = jnp.zeros_like(ref)


def _accumulate(ref, idx, val):
    ref[idx] = ref[idx] + val


def _colsum(v):
    return jnp.sum(v, axis=0, keepdims=True)


def _loss_bwd(x, target, g_final):
    b, s, d = x.shape
    tm = _row_tile(s)

    def body(x_ref, t_ref, g_ref, dx_ref, loss_ref, dg_ref):
        _zero_at_first_tile(loss_ref, dg_ref)
        xf = x_ref[0]
        r = _rstd(xf)
        xh = xf * r
        diff = xh * g_ref[...] - t_ref[0]
        _accumulate(loss_ref, (0, 0), _colsum(diff * diff))
        dy = diff * (1.0 / d)
        _accumulate(dg_ref, (0, 0), _colsum(dy * xh))
        dyg = dy * g_ref[...]
        dx_ref[0] = r * (dyg - xh * jnp.mean(dyg * xh, axis=-1, keepdims=True))

    return pl.pallas_call(
        body, name="loss_bwd", grid=(b, s // tm),
        out_shape=(jax.ShapeDtypeStruct((b, s, d), F32), jax.ShapeDtypeStruct((b, 1, 1, d), F32),
                   jax.ShapeDtypeStruct((b, 1, 1, d), F32)),
        in_specs=[_rows(tm, d), _rows(tm, d), _const((1, d))],
        out_specs=(_rows(tm, d), _per_b(1, d), _per_b(1, d)),
        compiler_params=_params(("arbitrary", "arbitrary")),
    )(x, target, g_final)


def _ffn_bwd(dx3, x2, mod, g_ffn, gu, o2, w_gu_t, w_down_t):
    b, s, d = x2.shape
    f = w_down_t.shape[1]
    fc = _pick(f, 512)
    tm = _row_tile(s, 256)

    def body(dx3_ref, x_ref, mod_ref, g_ref, gu_ref, o2_ref, wgut_ref, wdnt_ref, dx2_ref, do2_ref, act_ref, dgu_ref, h_ref,
             dmod_ref, dg_ref):
        _zero_at_first_tile(dmod_ref, dg_ref)
        dx3 = dx3_ref[0]
        sh, sc, gt = mod_ref[0, 3], mod_ref[0, 4], mod_ref[0, 5]
        do2 = (dx3 * gt).astype(BF16)
        do2_ref[0] = do2
        _accumulate(dmod_ref, (0, 2), _colsum(dx3 * o2_ref[0].astype(F32)))
        dh = jnp.zeros((tm, d), F32)
        for c0 in range(0, f, fc):
            gf = gu_ref[0, :, c0:c0 + fc].astype(F32)
            uf = gu_ref[0, :, f + c0:f + c0 + fc].astype(F32)
            sg = _sig(gf)
            silu = gf * sg
            act_ref[0, :, c0:c0 + fc] = (silu * uf).astype(BF16)
            dact = _dot(do2, wdnt_ref[:, c0:c0 + fc])
            dg = (dact * uf * (sg * (1.0 + gf * (1.0 - sg)))).astype(BF16)
            du = (dact * silu).astype(BF16)
            dgu_ref[0, :, c0:c0 + fc] = dg
            dgu_ref[0, :, f + c0:f + c0 + fc] = du
            dh = dh + _dot(dg, wgut_ref[c0:c0 + fc, :]) + _dot(du, wgut_ref[f + c0:f + c0 + fc, :])
        xf = x_ref[0]
        r = _rstd(xf)
        xh = xf * r
        n = xh * g_ref[...]
        h_ref[0] = (n * (1.0 + sc) + sh).astype(BF16)
        _accumulate(dmod_ref, (0, 0), _colsum(dh))
        _accumulate(dmod_ref, (0, 1), _colsum(dh * n))
        dn = dh * (1.0 + sc)
        _accumulate(dg_ref, (0, 0), _colsum(dn * xh))
        dyg = dn * g_ref[...]
        dx2_ref[0] = dx3 + r * (dyg - xh * jnp.mean(dyg * xh, axis=-1, keepdims=True))

    return pl.pallas_call(
        body, name="ffn_bwd", grid=(b, s // tm),
        out_shape=(jax.ShapeDtypeStruct((b, s, d), F32), jax.ShapeDtypeStruct((b, s, d), BF16),
                   jax.ShapeDtypeStruct((b, s, f), BF16), jax.ShapeDtypeStruct((b, s, 2 * f), BF16),
                   jax.ShapeDtypeStruct((b, s, d), BF16), jax.ShapeDtypeStruct((b, 3, 1, d), F32),
                   jax.ShapeDtypeStruct((b, 1, 1, d), F32)),
        in_specs=[_rows(tm, d), _rows(tm, d), _per_b(6, d), _const((1, d)), _rows(tm, 2 * f), _rows(tm, d),
                  _const(w_gu_t.shape), _const(w_down_t.shape)],
        out_specs=(_rows(tm, d), _rows(tm, d), _rows(tm, f), _rows(tm, 2 * f), _rows(tm, d), _per_b(3, d), _per_b(1, d)),
        compiler_params=_params(("arbitrary", "arbitrary")),
    )(dx3, x2, mod, g_ffn, gu, o2, w_gu_t, w_down_t)


def _mix_post_bwd(dx2, mod, o1, y_attn, y_conv, gate, w_out_t, w_o_t):
    b, s, d = dx2.shape
    hw = w_o_t.shape[1]
    tm = _row_tile(s)

    def body(dx_ref, mod_ref, o1_ref, ya_ref, yc_ref, gate_ref, woutt_ref, wot_ref, do1_ref, dya_ref, dyc_ref, dgate_ref,
             y_ref, do_ref, dgt_ref):
        _zero_at_first_tile(dgt_ref)
        dx = dx_ref[0]
        do1 = (dx * mod_ref[0, 2]).astype(BF16)
        do1_ref[0] = do1
        _accumulate(dgt_ref, (0, 0), _colsum(dx * o1_ref[0].astype(F32)))
        dy = _dot(do1, woutt_ref[...])
        gv = gate_ref[0]
        sa, sb = _sig(gv[:, :d].astype(F32)), _sig(gv[:, d:].astype(F32))
        ya, yc = ya_ref[0].astype(F32), yc_ref[0].astype(F32)
        y_ref[0] = (sa * ya + sb * yc).astype(BF16)
        dya = (dy * sa).astype(BF16)
        dya_ref[0] = dya
        dyc_ref[0] = (dy * sb).astype(BF16)
        dgate_ref[0, :, :d] = (dy * ya * sa * (1.0 - sa)).astype(BF16)
        dgate_ref[0, :, d:] = (dy * yc * sb * (1.0 - sb)).astype(BF16)
        do_ref[0] = _dot(dya, wot_ref[...]).astype(BF16)

    row = jax.ShapeDtypeStruct((b, s, d), BF16)
    return pl.pallas_call(
        body, name="mix_post_bwd", grid=(b, s // tm),
        out_shape=(row, row, row, jax.ShapeDtypeStruct((b, s, 2 * d), BF16), row,
                   jax.ShapeDtypeStruct((b, s, hw), BF16), jax.ShapeDtypeStruct((b, 1, 1, d), F32)),
        in_specs=[_rows(tm, d), _per_b(6, d), _rows(tm, d), _rows(tm, d), _rows(tm, d), _rows(tm, 2 * d),
                  _const(w_out_t.shape), _const(w_o_t.shape)],
        out_specs=(_rows(tm, d), _rows(tm, d), _rows(tm, d), _rows(tm, 2 * d), _rows(tm, d), _rows(tm, hw), _per_b(1, d)),
        compiler_params=_params(("arbitrary", "arbitrary")),
    )(dx2, mod, o1, y_attn, y_conv, gate, w_out_t, w_o_t)


def _conv_bwd(dyc, u, glu, w_dw, g_cn, b_cn, w_pw2_t):
    b, s, cc = u.shape
    d = dyc.shape[2]
    tm = _row_tile(s)
    nt = s // tm
    te = tm + HALO

    def body(dyc_ref, dycn_ref, u_ref, un_ref, glu_ref, glup_ref, w_ref, g_ref, bcn_ref, wpt_ref, dglu_ref, s_ref, dw_ref,
             small_ref, du_ext, uin_ext):
        _zero_at_first_tile(dw_ref, small_ref)
        st = pl.program_id(1)
        dy_all = jnp.concatenate([dyc_ref[0], dycn_ref[0]], axis=0)
        u_all = jnp.concatenate([u_ref[0], un_ref[0]], axis=0).astype(F32)
        ds = _dot(dy_all, wpt_ref[...])
        mu = jnp.mean(u_all, axis=-1, keepdims=True)
        uc = u_all - mu
        rstd = lax.rsqrt(jnp.mean(uc * uc, axis=-1, keepdims=True) + EPS)
        uh = uc * rstd
        ln = uh * g_ref[...] + bcn_ref[...]
        sg = _sig(ln)
        s_ref[0] = (ln * sg)[:tm].astype(BF16)
        dln = ds * (sg * (1.0 + ln * (1.0 - sg)))
        duh = dln * g_ref[...]
        du = rstd * (duh - jnp.mean(duh, axis=-1, keepdims=True) - uh * jnp.mean(duh * uh, axis=-1, keepdims=True))
        row = lax.broadcasted_iota(jnp.int32, (te, 1), 0)
        du = jnp.where(jnp.logical_and(st == nt - 1, row >= tm), 0.0, du)
        du_ext[...] = du
        du_cur = du[:tm]
        _accumulate(small_ref, (0, 0), _colsum((dln * uh)[:tm]))
        _accumulate(small_ref, (0, 1), _colsum(dln[:tm]))
        _accumulate(small_ref, (0, 2), _colsum(du_cur))
        uin_ext[pl.ds(0, HALO), :] = jnp.where(st == 0, 0.0, _glu(glup_ref[0], cc))
        gv = glu_ref[0]
        ga, gb = gv[:, :cc].astype(F32), gv[:, cc:].astype(F32)
        sgb = _sig(gb)
        uin_ext[pl.ds(HALO, tm), :] = ga * sgb
        duin = jnp.zeros((tm, cc), F32)
        for kk in range(CONV_W):
            duin = duin + w_ref[pl.ds(kk, 1), :] * du_ext[pl.ds(CONV_W - 1 - kk, tm), :]
            part = _colsum(du_cur * uin_ext[pl.ds(HALO - CONV_W + 1 + kk, tm), :])
            _accumulate(dw_ref, (0, pl.ds(kk, 1)), part)
        dglu_ref[0, :, :cc] = (duin * sgb).astype(BF16)
        dglu_ref[0, :, cc:] = (duin * ga * sgb * (1.0 - sgb)).astype(BF16)

    return pl.pallas_call(
        body, name="conv_bwd", grid=(b, nt),
        out_shape=(jax.ShapeDtypeStruct((b, s, 2 * cc), BF16), jax.ShapeDtypeStruct((b, s, cc), BF16),
                   jax.ShapeDtypeStruct((b, HALO, cc), F32), jax.ShapeDtypeStruct((b, 3, 1, cc), F32)),
        in_specs=[_rows(tm, d), _halo_next(tm, d, nt), _rows(tm, cc), _halo_next(tm, cc, nt), _rows(tm, 2 * cc),
                  _halo_prev(tm, 2 * cc), _const(w_dw.shape), _const((1, cc)), _const((1, cc)), _const(w_pw2_t.shape)],
        out_specs=(_rows(tm, 2 * cc), _rows(tm, cc), pl.BlockSpec((1, HALO, cc), lambda i, j: (i, 0, 0)), _per_b(3, cc)),
        scratch_shapes=[pltpu.VMEM((te, cc), F32), pltpu.VMEM((te, cc), F32)],
        compiler_params=_params(("arbitrary", "arbitrary")),
    )(dyc, dyc, u, u, glu, glu, w_dw, g_cn, b_cn, w_pw2_t)


def _flash_bwd(q, k, v, o, lse, do):
    b, s, hw = q.shape
    nh, hp = hw // HEAD_PAD, HEAD_PAD
    t = _row_tile(s)
    nt = s // t

    def body(q_ref, k_ref, v_ref, o_ref, lse_ref, do_ref, dq_ref, dk_ref, dv_ref, delta):
        j = pl.program_id(2)

        @pl.when(j == 0)
        def _():
            dq_ref[...] = jnp.zeros_like(dq_ref)
            for i in range(nt):
                rows = pl.ds(i * t, t)
                delta[rows, :] = jnp.sum(do_ref[0, rows, :].astype(F32) * o_ref[0, rows, :].astype(F32), axis=-1,
                                         keepdims=True)

        kv, vv = k_ref[0], v_ref[0]

        def step(i, carry, masked):
            dk, dv = carry
            rows = pl.ds(pl.multiple_of(i * t, t), t)
            qv, dov = q_ref[0, rows, :], do_ref[0, rows, :]
            p = jnp.exp(_dot_nt(qv, kv) * ATTN_SCALE - lse_ref[0, 0, rows, :])
            if masked:
                p = jnp.where(_causal_mask(t, t), p, 0.0)
            dv = dv + _dot_tn(p.astype(BF16), dov)
            dsb = (p * (_dot_nt(dov, vv) - delta[rows, :]) * ATTN_SCALE).astype(BF16)
            dk = dk + _dot_tn(dsb, qv)
            dq_ref[0, rows, :] = dq_ref[0, rows, :] + _dot(dsb, kv)
            return dk, dv

        carry = step(j, (jnp.zeros((t, hp), F32), jnp.zeros((t, hp), F32)), True)
        dk, dv = lax.fori_loop(j + 1, nt, lambda i, cr: step(i, cr, False), carry)
        dk_ref[0] = dk.astype(BF16)
        dv_ref[0] = dv.astype(BF16)

    tile = pl.BlockSpec((1, t, hp), lambda bb, hh, jj: (bb, jj, hh))
    full = pl.BlockSpec((1, s, hp), lambda bb, hh, jj: (bb, 0, hh))
    return pl.pallas_call(
        body, name="flash_bwd", grid=(b, nh, nt),
        out_shape=(jax.ShapeDtypeStruct((b, s, hw), F32), jax.ShapeDtypeStruct((b, s, hw), BF16),
                   jax.ShapeDtypeStruct((b, s, hw), BF16)),
        in_specs=[full, tile, tile, full, pl.BlockSpec((1, 1, s, 1), lambda bb, hh, jj: (bb, hh, 0, 0)), full],
        out_specs=(full, tile, tile),
        scratch_shapes=[pltpu.VMEM((s, 1), F32)],
        compiler_params=_params(("arbitrary", "arbitrary", "arbitrary")),
    )(q, k, v, o, lse, do)


def _mix_pre_bwd(x, dx2, mod, cos_t, sin_t, g_mix, g_q, g_kv, lat, dq, dk, dv, dglu, dgate, w_lat_t, w_glu_t, w_gate_t,
                 w_q_t, w_k_t, w_v_t):
    b, s, d = x.shape
    ql, kl = g_q.shape[1], g_kv.shape[1]
    wl = lat.shape[2]
    hw = N_HEADS * HEAD_PAD
    tm = _row_tile(s)

    def body(x_ref, dx2_ref, mod_ref, cos_ref, sin_ref, gm_ref, gq_ref, gkv_ref, lat_ref, dq_ref, dk_ref, dv_ref, dglu_ref,
             dgate_ref, wlt_ref, wgt_ref, wtt_ref, wqt_ref, wkt_ref, wvt_ref, dx_ref, dlat_ref, dqr_ref, qn_ref, kvn_ref,
             h_ref, dmod_ref, dgm_ref, dgq_ref, dgkv_ref):
        _zero_at_first_tile(dmod_ref, dgm_ref, dgq_ref, dgkv_ref)
        cos_v, sin_v = cos_ref[0], sin_ref[0]
        latf = lat_ref[0].astype(F32)
        q_lat, kv_lat = latf[:, :ql], latf[:, ql:ql + kl]
        rq, rk = _rstd(q_lat), _rstd(kv_lat)
        qh, kh = q_lat * rq, kv_lat * rk
        qn_ref[0] = (qh * gq_ref[...]).astype(BF16)
        kvn_ref[0] = (kh * gkv_ref[...]).astype(BF16)
        dk_sum = jnp.zeros((tm, HEAD_PAD), F32)
        for h in range(N_HEADS):
            cols = slice(h * HEAD_PAD, (h + 1) * HEAD_PAD)
            dqr_ref[0, :, cols] = _rope_bwd(dq_ref[0, :, cols], cos_v, sin_v).astype(BF16)
            dk_sum = dk_sum + dk_ref[0, :, cols].astype(F32)
        dqn = _dot(dqr_ref[0], wqt_ref[...])
        dkvn = _dot(dk_ref[0], wkt_ref[...]) + _dot(dv_ref[0], wvt_ref[...])
        lane = lax.broadcasted_iota(jnp.int32, dk_sum.shape, 1)
        dkr = _rope_bwd(jnp.where(lane >= QK_NOPE, dk_sum, 0.0), cos_v, sin_v)
        _accumulate(dgq_ref, (0, 0), _colsum(dqn * qh))
        _accumulate(dgkv_ref, (0, 0), _colsum(dkvn * kh))
        dqg, dkg = dqn * gq_ref[...], dkvn * gkv_ref[...]
        dlat_ref[0, :, :ql] = (rq * (dqg - qh * jnp.mean(dqg * qh, axis=-1, keepdims=True))).astype(BF16)
        dlat_ref[0, :, ql:ql + kl] = (rk * (dkg - kh * jnp.mean(dkg * kh, axis=-1, keepdims=True))).astype(BF16)
        dlat_ref[0, :, ql + kl:] = dkr.astype(BF16)
        dh = _dot(dlat_ref[0], wlt_ref[...]) + _dot(dglu_ref[0], wgt_ref[...]) + _dot(dgate_ref[0], wtt_ref[...])
        sh, sc = mod_ref[0, 0], mod_ref[0, 1]
        xf = x_ref[0]
        r = _rstd(xf)
        xh = xf * r
        n = xh * gm_ref[...]
        h_ref[0] = (n * (1.0 + sc) + sh).astype(BF16)
        _accumulate(dmod_ref, (0, 0), _colsum(dh))
        _accumulate(dmod_ref, (0, 1), _colsum(dh * n))
        dn = dh * (1.0 + sc)
        _accumulate(dgm_ref, (0, 0), _colsum(dn * xh))
        dyg = dn * gm_ref[...]
        dx_ref[0] = dx2_ref[0] + r * (dyg - xh * jnp.mean(dyg * xh, axis=-1, keepdims=True))

    return pl.pallas_call(
        body, name="mix_pre_bwd", grid=(b, s // tm),
        out_shape=(jax.ShapeDtypeStruct((b, s, d), F32), jax.ShapeDtypeStruct((b, s, wl), BF16),
                   jax.ShapeDtypeStruct((b, s, hw), BF16), jax.ShapeDtypeStruct((b, s, ql), BF16),
                   jax.ShapeDtypeStruct((b, s, kl), BF16), jax.ShapeDtypeStruct((b, s, d), BF16),
                   jax.ShapeDtypeStruct((b, 2, 1, d), F32), jax.ShapeDtypeStruct((b, 1, 1, d), F32),
                   jax.ShapeDtypeStruct((b, 1, 1, ql), F32), jax.ShapeDtypeStruct((b, 1, 1, kl), F32)),
        in_specs=[_rows(tm, d), _rows(tm, d), _per_b(6, d), _rows(tm, HEAD_PAD), _rows(tm, HEAD_PAD), _const((1, d)),
                  _const((1, ql)), _const((1, kl)), _rows(tm, wl), _rows(tm, hw), _rows(tm, hw), _rows(tm, hw),
                  _rows(tm, dglu.shape[2]), _rows(tm, 2 * d), _const(w_lat_t.shape), _const(w_glu_t.shape),
                  _const(w_gate_t.shape), _const(w_q_t.shape), _const(w_k_t.shape), _const(w_v_t.shape)],
        out_specs=(_rows(tm, d), _rows(tm, wl), _rows(tm, hw), _rows(tm, ql), _rows(tm, kl), _rows(tm, d), _per_b(2, d),
                   _per_b(1, d), _per_b(1, ql), _per_b(1, kl)),
        compiler_params=_params(("arbitrary", "arbitrary")),
    )(x, dx2, mod, cos_t, sin_t, g_mix, g_q, g_kv, lat, dq, dk, dv, dglu, dgate, w_lat_t, w_glu_t, w_gate_t, w_q_t, w_k_t,
      w_v_t)


def _matmul_tn(a, bm, name):
    b, s, kd = a.shape
    nd = bm.shape[2]
    tk, tn = _pick(kd, 1536), _pick(nd, 1536)
    ts = _row_tile(s)

    def body(a_ref, b_ref, o_ref):
        part = _dot_tn(a_ref[0], b_ref[0])
        first = jnp.logical_and(pl.program_id(2) == 0, pl.program_id(3) == 0)

        @pl.when(first)
        def _():
            o_ref[...] = part

        @pl.when(jnp.logical_not(first))
        def _():
            o_ref[...] = o_ref[...] + part

    return pl.pallas_call(
        body, name=name, grid=(kd // tk, nd // tn, b, s // ts),
        out_shape=jax.ShapeDtypeStruct((kd, nd), F32),
        in_specs=[pl.BlockSpec((1, ts, tk), lambda i, j, bb, ss: (bb, ss, i)),
                  pl.BlockSpec((1, ts, tn), lambda i, j, bb, ss: (bb, ss, j))],
        out_specs=pl.BlockSpec((tk, tn), lambda i, j, bb, ss: (i, j)),
        compiler_params=_params(("arbitrary",) * 4),
    )(a, bm)


def _adamw_update(w, g, m, v):
    nm = ADAM_B1 * m + (1.0 - ADAM_B1) * g
    nv = ADAM_B2 * v + (1.0 - ADAM_B2) * (g * g)
    delta = -ADAM_LR * ((nm / (1.0 - ADAM_B1 ** ADAM_STEP)) / (jnp.sqrt(nv / (1.0 - ADAM_B2 ** ADAM_STEP)) + ADAM_EPS)
                        + ADAM_WD * w)
    return delta, nm, nv


def _adamw(w, g, m, v, name):
    shape = w.shape
    cols = shape[-1]
    rows = w.size // cols
    w2, g2, m2, v2 = (t.reshape(rows, cols) for t in (w, g, m, v))
    tr = rows
    if rows * cols * 4 > (1 << 20):
        tr = _div_tile(rows, max(8, (1 << 18) // cols), 8)

    def body(w_ref, g_ref, m_ref, v_ref, d_ref, nm_ref, nv_ref):
        d_ref[...], nm_ref[...], nv_ref[...] = _adamw_update(w_ref[...], g_ref[...], m_ref[...], v_ref[...])

    spec = pl.BlockSpec((tr, cols), lambda i: (i, 0))
    outs = pl.pallas_call(
        body, name=name, grid=(rows // tr,), out_shape=(jax.ShapeDtypeStruct((rows, cols), F32),) * 3,
        in_specs=[spec] * 4, out_specs=(spec,) * 3, compiler_params=_params(("arbitrary",)),
    )(w2, g2, m2, v2)
    return tuple(t.reshape(shape) for t in outs)


def _adamw_reduce(w, m, v, part, got, chip, name):
    shape = w.shape
    cols = shape[-1]
    w2, m2, v2 = (_rows_2d(t, 0) for t in (w, m, v))
    p3, g3 = _rows_2d(part, 4), _rows_2d(got, 3)
    rows = w2.shape[0]
    tr = _grad_row_tile(rows, cols)

    def body(chip_ref, w_ref, m_ref, v_ref, p_ref, r_ref, g_ref, d_ref, nm_ref, nv_ref):
        g = ((p_ref[0].astype(F32) + r_ref[0].astype(F32)) + r_ref[1].astype(F32)) + r_ref[2].astype(F32)
        g_ref[...] = g
        d_ref[...], nm_ref[...], nv_ref[...] = _adamw_update(w_ref[...], g, m_ref[...], v_ref[...])

    spec = pl.BlockSpec((tr, cols), lambda i, chip_ref: (i, 0))
    outs = pl.pallas_call(
        body, name=name,
        grid_spec=pltpu.PrefetchScalarGridSpec(
            num_scalar_prefetch=1, grid=(rows // tr,),
            in_specs=[spec, spec, spec, pl.BlockSpec((1, tr, cols), lambda i, chip_ref: (chip_ref[0], i, 0)),
                      pl.BlockSpec((3, tr, cols), lambda i, chip_ref: (0, i, 0))],
            out_specs=(spec,) * 4),
        out_shape=(jax.ShapeDtypeStruct((rows, cols), F32),) * 4, compiler_params=_params(("arbitrary",)),
    )(chip, w2, m2, v2, p3, g3)
    return tuple(t.reshape(shape) for t in outs)


GATHERED = (("w_in", 2), ("w_uq", 2), ("w_ukv", 2), ("w_o_attn", 2), ("w_pw2", 2), ("w_out", 1), ("w_gu", 2), ("w_down", 1))


def _from_chunks(chunks, axis):
    _, nl, a, bb = chunks.shape
    if axis == 2:
        return jnp.transpose(chunks, (1, 2, 0, 3)).reshape(nl, a, N_DEV * bb)
    return jnp.transpose(chunks, (1, 0, 2, 3)).reshape(nl, N_DEV * a, bb)


def _chunks_by_core(full, axis):
    nl, a, bb = full.shape
    if axis == 2:
        t = jnp.transpose(full.reshape(nl, a, 2, 2, 2, bb // N_DEV), (4, 2, 3, 0, 1, 5))
    else:
        t = jnp.transpose(full.reshape(nl, 2, 2, 2, a // N_DEV, bb), (3, 1, 2, 0, 4, 5))
    return t.reshape((N_DEV,) + t.shape[3:]).astype(BF16)


def _swap_halves(t):
    half = QK_ROPE // 2
    return jnp.concatenate([t[..., half:], t[..., :half]], axis=-1)


def _t(w):
    return jnp.swapaxes(w, -1, -2)


def _pad_rows(t, mult=8):
    return jnp.pad(t, ((0, -t.shape[0] % mult), (0, 0)))


def _pad_last(t, width):
    return jnp.pad(t, ((0, 0),) * (t.ndim - 1) + ((0, width - t.shape[-1]),))


def kernel(x, c, positions, w_ada, b_ada, g_mix, w_in, g_q, w_uq, g_kv, w_ukv, w_o_attn, w_dw, b_dw, g_cn, b_cn, w_pw2, w_out, g_ffn, w_gu, w_down, g_final, loss_target, m_w_ada, m_b_ada, m_g_mix, m_w_in, m_g_q, m_w_uq, m_g_kv, m_w_ukv, m_w_o_attn, m_w_dw, m_b_dw, m_g_cn, m_b_cn, m_w_pw2, m_w_out, m_g_ffn, m_w_gu, m_w_down, m_g_final, v_w_ada, v_b_ada, v_g_mix, v_w_in, v_g_q, v_w_uq, v_g_kv, v_w_ukv, v_w_o_attn, v_w_dw, v_b_dw, v_g_cn, v_b_cn, v_w_pw2, v_w_out, v_g_ffn, v_w_gu, v_w_down, v_g_final):
    weights = dict(w_ada=w_ada, b_ada=b_ada, g_mix=g_mix, w_in=w_in, g_q=g_q, w_uq=w_uq, g_kv=g_kv, w_ukv=w_ukv,
                   w_o_attn=w_o_attn, w_dw=w_dw, b_dw=b_dw, g_cn=g_cn, b_cn=b_cn, w_pw2=w_pw2, w_out=w_out, g_ffn=g_ffn,
                   w_gu=w_gu, w_down=w_down, g_final=g_final)
    mom_m = dict(w_ada=m_w_ada, b_ada=m_b_ada, g_mix=m_g_mix, w_in=m_w_in, g_q=m_g_q, w_uq=m_w_uq, g_kv=m_g_kv,
                 w_ukv=m_w_ukv, w_o_attn=m_w_o_attn, w_dw=m_w_dw, b_dw=m_b_dw, g_cn=m_g_cn, b_cn=m_b_cn, w_pw2=m_w_pw2,
                 w_out=m_w_out, g_ffn=m_g_ffn, w_gu=m_w_gu, w_down=m_w_down, g_final=m_g_final)
    mom_v = dict(w_ada=v_w_ada, b_ada=v_b_ada, g_mix=v_g_mix, w_in=v_w_in, g_q=v_g_q, w_uq=v_w_uq, g_kv=v_g_kv,
                 w_ukv=v_w_ukv, w_o_attn=v_w_o_attn, w_dw=v_w_dw, b_dw=v_b_dw, g_cn=v_g_cn, b_cn=v_b_cn, w_pw2=v_w_pw2,
                 w_out=v_w_out, g_ffn=v_g_ffn, w_gu=v_w_gu, w_down=v_w_down, g_final=v_g_final)
    order = list(weights)

    nb, s, d = x.shape
    nl = w_in.shape[0]
    ql, kl, cc = g_q.shape[1], g_kv.shape[1], g_cn.shape[1]
    h = N_HEADS
    qk = QK_NOPE + QK_ROPE
    xi, yi, ci = lax.axis_index("x"), lax.axis_index("y"), lax.axis_index("c")
    me = 4 * xi + 2 * yi + ci
    core = jnp.reshape(ci, (1,)).astype(jnp.int32)
    chip = jnp.reshape(2 * xi + yi, (1,)).astype(jnp.int32)

    got = _all_gather([weights[n].astype(BF16) for n, _ in GATHERED], vmem=False, name="weight_all_gather")
    full = {n: _from_chunks(t, ax) for (n, ax), t in zip(GATHERED, got)}
    n_dw = w_dw.shape[2]
    dw_rows = jnp.pad(w_dw, ((0, 0), (0, HALO - CONV_W), (0, LANES - n_dw))).reshape(nl * HALO, LANES)
    c_all, dw_all = _all_gather([_pad_rows(c), dw_rows], vmem=True, name="cond_all_gather")
    c_full = c_all[:, :nb].reshape(N_DEV * nb, d)
    w_dw_full = jnp.transpose(dw_all.reshape(N_DEV, nl, HALO, LANES)[..., :n_dw], (1, 2, 0, 3)).reshape(nl, HALO, cc)

    o_kr, o_glu, o_gate = ql + kl, ql + kl + QK_ROPE, ql + kl + QK_ROPE + 2 * cc
    wi = full["w_in"]
    w_kr = wi[:, :, o_kr:o_glu]
    w_lat = jnp.concatenate([wi[:, :, :o_kr], jnp.zeros((nl, d, QK_NOPE), BF16), w_kr, _swap_halves(w_kr)], axis=2)
    w_glu, w_gate = wi[:, :, o_glu:o_gate], wi[:, :, o_gate:]
    wq = full["w_uq"].reshape(nl, ql, h, qk)
    w_q = jnp.concatenate([wq, _swap_halves(wq[..., QK_NOPE:])], axis=-1).reshape(nl, ql, h * HEAD_PAD)
    wkv = full["w_ukv"].reshape(nl, kl, h, QK_NOPE + V_HEAD)
    w_k = _pad_last(wkv[..., :QK_NOPE], HEAD_PAD).reshape(nl, kl, h * HEAD_PAD)
    w_v = _pad_last(wkv[..., QK_NOPE:], HEAD_PAD).reshape(nl, kl, h * HEAD_PAD)
    w_o = jnp.pad(full["w_o_attn"].reshape(nl, h, V_HEAD, d), ((0, 0), (0, 0), (0, HEAD_PAD - V_HEAD), (0, 0)))
    w_o = w_o.reshape(nl, h * HEAD_PAD, d)
    w_pw, w_ou, w_g, w_dn = full["w_pw2"], full["w_out"], full["w_gu"], full["w_down"]

    n_ada = w_ada.shape[2]
    b_cols = lax.dynamic_slice_in_dim(b_ada, me * n_ada, n_ada, axis=1).reshape(nl, 1, n_ada)
    mod_part = _ada_fwd(c_full, w_ada, b_cols)
    (mod_all,) = _all_gather([mod_part.reshape(nl * N_DEV * nb, n_ada)], vmem=True, name="mod_all_gather")
    mod_all = jnp.transpose(mod_all.reshape(N_DEV, nl, N_DEV * nb, n_ada), (1, 2, 0, 3)).reshape(nl, N_DEV * nb, 6 * d)
    mod = lax.dynamic_slice_in_dim(mod_all, me * nb, nb, axis=1).reshape(nl, nb, 6, 1, d)

    inv_freq = ROPE_THETA ** (-jnp.arange(0, QK_ROPE, 2, dtype=F32) / QK_ROPE)
    zeros = lambda n: jnp.zeros((n,), F32)
    freq_row = jnp.concatenate([zeros(QK_NOPE), inv_freq, inv_freq, zeros(HEAD_PAD - qk)]).reshape(1, -1)
    ones = jnp.ones((QK_ROPE // 2,), F32)
    sign_row = jnp.concatenate([zeros(QK_NOPE), -ones, ones, zeros(HEAD_PAD - qk)]).reshape(1, -1)
    cos_t, sin_t = _rope_tables(positions.astype(F32).reshape(nb, s, 1), freq_row, sign_row)

    row = lambda t, l: t[l].reshape(1, -1)

    saved = []
    xc = x
    for l in range(nl):
        lat, glu, gate, qh, kh, vh = _mix_pre_fwd(xc, mod[l], cos_t, sin_t, row(g_mix, l), row(g_q, l), row(g_kv, l),
                                                  w_lat[l], w_glu[l], w_gate[l], w_q[l], w_k[l], w_v[l])
        o, lse = _flash_fwd(qh, kh, vh)
        u, y_conv = _conv_fwd(glu, w_dw_full[l], row(b_dw, l), row(g_cn, l), row(b_cn, l), w_pw[l])
        x2, y_attn, o1 = _mix_post_fwd(xc, mod[l], o, y_conv, gate, w_o[l], w_ou[l])
        x3, gu, o2 = _ffn_fwd(x2, mod[l], row(g_ffn, l), w_g[l], w_dn[l])
        saved.append(dict(x=xc, lat=lat, glu=glu, gate=gate, q=qh, k=kh, v=vh, o=o, lse=lse, u=u, y_conv=y_conv, x2=x2,
                          y_attn=y_attn, o1=o1, gu=gu, o2=o2))
        xc = x3

    dx, loss_part, dgf_part = _loss_bwd(xc, loss_target, g_final.reshape(1, d))
    loss = lax.psum(0.5 / d * jnp.sum(loss_part), AXES)

    gw = {n: [None] * nl for n, _ in GATHERED}
    small_rows, dw_taps, dmod = [None] * nl, [None] * nl, [None] * nl
    for l in reversed(range(nl)):
        sv = saved[l]
        dx2, do2, act, dgu, h2, dmod2, dgffn = _ffn_bwd(dx, sv["x2"], mod[l], row(g_ffn, l), sv["gu"], sv["o2"],
                                                       _t(w_g[l]), _t(w_dn[l]))
        gw["w_gu"][l] = _matmul_tn(h2, dgu, "grad_w_gu")
        gw["w_down"][l] = _matmul_tn(act, do2, "grad_w_down")
        do1, dya, dyc, dgate, yv, do_h, dgt1 = _mix_post_bwd(dx2, mod[l], sv["o1"], sv["y_attn"], sv["y_conv"], sv["gate"],
                                                           _t(w_ou[l]), _t(w_o[l]))
        gw["w_out"][l] = _matmul_tn(yv, do1, "grad_w_out")
        dwo = _matmul_tn(sv["o"], dya, "grad_w_o")
        gw["w_o_attn"][l] = dwo.reshape(h, HEAD_PAD, d)[:, :V_HEAD].reshape(h * V_HEAD, d)
        dglu, s_act, ddw, csmall = _conv_bwd(dyc, sv["u"], sv["glu"], w_dw_full[l], row(g_cn, l), row(b_cn, l), _t(w_pw[l]))
        gw["w_pw2"][l] = _matmul_tn(s_act, dyc, "grad_w_pw2")
        dq, dk, dv = _flash_bwd(sv["q"], sv["k"], sv["v"], sv["o"], sv["lse"], do_h)
        dx, dlat, dqr, qn, kvn, h1, dmod1, dgm, dgq, dgkv = _mix_pre_bwd(
            sv["x"], dx2, mod[l], cos_t, sin_t, row(g_mix, l), row(g_q, l), row(g_kv, l), sv["lat"], dq, dk, dv, dglu,
            dgate, _t(w_lat[l]), _t(w_glu[l]), _t(w_gate[l]), _t(w_q[l]), _t(w_k[l]), _t(w_v[l]))
        dwl = _matmul_tn(h1, dlat, "grad_w_lat")
        dwg = _matmul_tn(h1, dglu, "grad_w_glu")
        dwt = _matmul_tn(h1, dgate, "grad_w_gate")
        kr0 = o_kr + QK_NOPE
        dkr = dwl[:, kr0:kr0 + QK_ROPE] + _swap_halves(dwl[:, kr0 + QK_ROPE:])
        gw["w_in"][l] = jnp.concatenate([dwl[:, :o_kr], dkr, dwg, dwt], axis=1)
        dwq = _matmul_tn(qn, dqr, "grad_w_q").reshape(ql, h, HEAD_PAD)
        dwq = jnp.concatenate([dwq[..., :QK_NOPE], dwq[..., QK_NOPE:qk] + _swap_halves(dwq[..., qk:])], axis=-1)
        gw["w_uq"][l] = dwq.reshape(ql, h * qk)
        dwk = _matmul_tn(kvn, dk, "grad_w_k").reshape(kl, h, HEAD_PAD)
        dwv = _matmul_tn(kvn, dv, "grad_w_v").reshape(kl, h, HEAD_PAD)
        gw["w_ukv"][l] = jnp.concatenate([dwk[..., :QK_NOPE], dwv[..., :V_HEAD]], axis=-1).reshape(kl, h * (QK_NOPE + V_HEAD))
        dmod[l] = jnp.concatenate([dmod1[:, :, 0], dgt1[:, :, 0], dmod2[:, :, 0]], axis=1).reshape(nb, 6 * d)
        bsum = lambda t: jnp.sum(t, axis=0).reshape(1, -1)
        cs = jnp.sum(csmall, axis=0)[:, 0]
        small_rows[l] = jnp.concatenate([bsum(dgm), bsum(dgq), bsum(dgkv), cs[2:3], cs[0:1], cs[1:2], bsum(dgffn)], axis=1)
        dw_taps[l] = jnp.sum(ddw, axis=0)
    grad_x = dx

    dmod_rows = _pad_rows(jnp.stack(dmod).reshape(nl * nb, 6 * d))
    (dmod_all,) = _all_gather([dmod_rows], vmem=True, name="dmod_all_gather")
    dmod_full = jnp.transpose(dmod_all[:, :nl * nb].reshape(N_DEV, nl, nb, 6 * d), (1, 0, 2, 3)).reshape(nl, N_DEV * nb, 6 * d)
    dmod_cols = lax.dynamic_slice_in_dim(dmod_full, me * n_ada, n_ada, axis=2)
    grad_w_ada, grad_b_ada = _ada_bwd(c_full, dmod_cols, dmod_full)
    grads = {"w_ada": grad_w_ada, "b_ada": grad_b_ada.reshape(nl, 6 * d)}

    widths = (d, ql, kl, cc, cc, cc, d)
    wsum = sum(widths)
    final_row = _pad_last(jnp.sum(dgf_part, axis=0).reshape(1, d), wsum)
    small2d = _pad_rows(jnp.concatenate(small_rows + [final_row], axis=0))
    taps2d = jnp.concatenate(dw_taps, axis=0)
    small_all, taps_all = _all_gather([small2d, taps2d], vmem=True, name="small_grad_all_gather")
    small_sum, taps_sum = _sum_devices(small_all), _sum_devices(taps_all)
    off = 0
    for n, wdt in zip(("g_mix", "g_q", "g_kv", "b_dw", "g_cn", "b_cn", "g_ffn"), widths):
        grads[n] = small_sum[:nl, off:off + wdt]
        off += wdt
    grads["g_final"] = small_sum[nl, :d]
    taps = taps_sum.reshape(nl, HALO, cc)[:, :CONV_W]
    grads["w_dw"] = lax.dynamic_slice_in_dim(taps, me * n_dw, n_dw, axis=2)

    delta, new_m, new_v = {}, {}, {}
    for n in order:
        if n in grads:
            delta[n], new_m[n], new_v[n] = _adamw(weights[n], grads[n], mom_m[n], mom_v[n], "adamw_" + n)

    gs = [_chunks_by_core(jnp.stack(gw[n]), ax) for n, ax in GATHERED]
    from_sibling = _sibling_exchange(gs)
    parts = [_add_halves(_rows_2d(g, N_DEV), _rows_2d(r, 4), core, "grad_chip_partial_" + n).reshape((4,) + g.shape[1:])
             for (n, _), g, r in zip(GATHERED, gs, from_sibling)]
    from_chips = _chip_exchange(parts)
    for (n, _), part, rest in zip(GATHERED, parts, from_chips):
        grads[n], delta[n], new_m[n], new_v[n] = _adamw_reduce(weights[n], mom_m[n], mom_v[n], part, rest, chip,
                                                                 "adamw_" + n)

    return (loss, grad_x, *[grads[n] for n in order], *[delta[n] for n in order], *[new_m[n] for n in order],
            *[new_v[n] for n in order])
```

```python
import functools

import jax
import jax.numpy as jnp
from jax import lax
from jax.experimental import pallas as pl
from jax.experimental.pallas import tpu as pltpu

F32, BF16 = jnp.float32, jnp.bfloat16
MESH = pl.DeviceIdType.MESH
AXES = ("x", "y", "c")
N_DEV = 8

N_HEADS = 8
QK_NOPE = 64
QK_ROPE = 32
V_HEAD = 64
HEAD_PAD = 128
CONV_W = 31
HALO = 32
EPS = 1e-6
ROPE_THETA = 10000.0
NEG_INF = -1e30
ATTN_SCALE = (QK_NOPE + QK_ROPE) ** -0.5

ADAM_LR, ADAM_B1, ADAM_B2, ADAM_EPS, ADAM_WD, ADAM_STEP = 0.001, 0.9, 0.999, 1e-08, 0.01, 10

LANES = 128
VMEM_LIMIT = 60 * 1024 * 1024


def _params(sem=None):
    return pltpu.CompilerParams(dimension_semantics=sem, vmem_limit_bytes=VMEM_LIMIT)


def _pick(n, cap):
    if n <= cap:
        return n
    best = None
    for d in range(LANES, cap + 1, LANES):
        if n % d == 0:
            best = d
    assert best is not None, (n, cap)
    return best


def _div_tile(n, cap, mult):
    best = None
    for d in range(mult, min(n, cap) + 1, mult):
        if n % d == 0:
            best = d
    assert best is not None, (n, cap, mult)
    return best


def _row_tile(s, cap=512):
    return cap if s % cap == 0 and s >= 2 * cap else s // 2


def _sig(v):
    return 1.0 / (1.0 + jnp.exp(-v))


def _rstd(v):
    return lax.rsqrt(jnp.mean(v * v, axis=-1, keepdims=True) + EPS)


def _dot(a, b):
    return jnp.dot(a, b, preferred_element_type=F32)


def _dot_nt(a, b):
    return lax.dot_general(a, b, (((1,), (1,)), ((), ())), preferred_element_type=F32)


def _dot_tn(a, b):
    return lax.dot_general(a, b, (((0,), (0,)), ((), ())), preferred_element_type=F32)


def _rope(v, cos_t, sin_t):
    return v * cos_t + pltpu.roll(v, HEAD_PAD - QK_ROPE, 1) * sin_t


def _rope_bwd(dv, cos_t, sin_t):
    return dv * cos_t + pltpu.roll(dv * sin_t, QK_ROPE, 1)


def _const(shape):
    n = len(shape)
    return pl.BlockSpec(shape, lambda *_: (0,) * n, pipeline_mode=pl.Buffered(1))


def _rows(tm, w):
    return pl.BlockSpec((1, tm, w), lambda b, s: (b, s, 0))


def _per_b(r, w):
    return pl.BlockSpec((1, r, 1, w), lambda b, s: (b, 0, 0, 0))


def _all_gather(arrs, vmem, name):
    n = len(arrs)
    space = pltpu.VMEM if vmem else pl.ANY

    def body(*refs):
        x_refs, out_refs = refs[:n], refs[n:2 * n]
        send_sems, recv_sems, local_sems = refs[2 * n:]
        x_, y_, c_ = lax.axis_index("x"), lax.axis_index("y"), lax.axis_index("c")
        me, sibling = (x_, y_, c_), (x_, y_, 1 - c_)
        chips = [(1 - x_, y_), (x_, 1 - y_), (1 - x_, 1 - y_)]

        def copy(a, k, block, to, own=False):
            px, py, pc = block
            slot = out_refs[a].at[4 * px + 2 * py + pc]
            return pltpu.make_async_remote_copy(
                src_ref=x_refs[a] if own else slot, dst_ref=slot, send_sem=send_sems.at[k * n + a],
                recv_sem=recv_sems.at[k * n + a], device_id=to, device_id_type=MESH)

        mine = [pltpu.make_async_copy(x_refs[a], out_refs[a].at[4 * x_ + 2 * y_ + c_], local_sems.at[a]) for a in range(n)]
        sent = []
        for a in range(n):
            mine[a].start()
            sent.append(copy(a, 0, me, sibling, own=True))
            sent += [copy(a, 1 + j, me, (*chip, c_), own=True) for j, chip in enumerate(chips)]
        for cp in sent:
            cp.start()
        for j, chip in enumerate(chips):
            for a in range(n):
                copy(a, 1 + j, (*chip, c_), me).wait_recv()
                passed = copy(a, 4 + j, (*chip, c_), sibling)
                passed.start()
                sent.append(passed)
        for a in range(n):
            copy(a, 0, sibling, me).wait_recv()
            for j, chip in enumerate(chips):
                copy(a, 4 + j, (*chip, 1 - c_), me).wait_recv()
        for cp in sent:
            cp.wait_send()
        for cp in mine:
            cp.wait()

    return pl.pallas_call(
        body, name=name,
        out_shape=[jax.ShapeDtypeStruct((N_DEV,) + t.shape, t.dtype) for t in arrs],
        in_specs=[pl.BlockSpec(memory_space=space)] * n, out_specs=[pl.BlockSpec(memory_space=space)] * n,
        scratch_shapes=[pltpu.SemaphoreType.DMA((7 * n,)), pltpu.SemaphoreType.DMA((7 * n,)), pltpu.SemaphoreType.DMA((n,))],
        compiler_params=pltpu.CompilerParams(vmem_limit_bytes=VMEM_LIMIT),
    )(*arrs)


def _sibling_exchange(gs):
    n = len(gs)

    def body(*refs):
        g_refs, out_refs = refs[:n], refs[n:2 * n]
        send_sems, recv_sems = refs[2 * n:]
        x_, y_, c_ = lax.axis_index("x"), lax.axis_index("y"), lax.axis_index("c")
        cps = [pltpu.make_async_remote_copy(
            src_ref=g_refs[a].at[pl.ds(4 * (1 - c_), 4)], dst_ref=out_refs[a], send_sem=send_sems.at[a],
            recv_sem=recv_sems.at[a], device_id=(x_, y_, 1 - c_), device_id_type=MESH) for a in range(n)]
        for cp in cps:
            cp.start()
        for cp in cps:
            cp.wait()

    return pl.pallas_call(
        body, name="grad_sibling_exchange",
        out_shape=[jax.ShapeDtypeStruct((4,) + g.shape[1:], g.dtype) for g in gs],
        in_specs=[pl.BlockSpec(memory_space=pl.ANY)] * n, out_specs=[pl.BlockSpec(memory_space=pl.ANY)] * n,
        scratch_shapes=[pltpu.SemaphoreType.DMA((n,)), pltpu.SemaphoreType.DMA((n,))],
    )(*gs)


def _chip_exchange(ps):
    n = len(ps)

    def body(*refs):
        p_refs, out_refs = refs[:n], refs[n:2 * n]
        send_sems, recv_sems = refs[2 * n:]
        x_, y_, c_ = lax.axis_index("x"), lax.axis_index("y"), lax.axis_index("c")
        chips = [(1 - x_, y_), (x_, 1 - y_), (1 - x_, 1 - y_)]
        cps = [pltpu.make_async_remote_copy(
            src_ref=p_refs[a].at[2 * px + py], dst_ref=out_refs[a].at[k], send_sem=send_sems.at[k * n + a],
            recv_sem=recv_sems.at[k * n + a], device_id=(px, py, c_), device_id_type=MESH)
            for k, (px, py) in enumerate(chips) for a in range(n)]
        for cp in cps:
            cp.start()
        for cp in cps:
            cp.wait()

    return pl.pallas_call(
        body, name="grad_chip_exchange",
        out_shape=[jax.ShapeDtypeStruct((3,) + p.shape[1:], p.dtype) for p in ps],
        in_specs=[pl.BlockSpec(memory_space=pl.ANY)] * n, out_specs=[pl.BlockSpec(memory_space=pl.ANY)] * n,
        scratch_shapes=[pltpu.SemaphoreType.DMA((3 * n,)), pltpu.SemaphoreType.DMA((3 * n,))],
    )(*ps)


def _rows_2d(t, lead):
    return t.reshape((lead, -1, t.shape[-1]) if lead else (-1, t.shape[-1]))


def _grad_row_tile(rows, cols):
    return _div_tile(rows, max(16, (1 << 19) // cols), 16)


def _add_halves(g, got, core, name):
    _, r, n = g.shape
    tr = _grad_row_tile(r, n)

    def body(core_ref, g_ref, r_ref, o_ref):
        o_ref[...] = (g_ref[...].astype(F32) + r_ref[...].astype(F32)).astype(BF16)

    spec = pl.BlockSpec((1, tr, n), lambda i, j, core_ref: (i, j, 0))
    return pl.pallas_call(
        body, name=name,
        grid_spec=pltpu.PrefetchScalarGridSpec(
            num_scalar_prefetch=1, grid=(4, r // tr),
            in_specs=[pl.BlockSpec((1, tr, n), lambda i, j, core_ref: (4 * core_ref[0] + i, j, 0)), spec], out_specs=spec),
        out_shape=jax.ShapeDtypeStruct((4, r, n), BF16), compiler_params=_params(("arbitrary", "arbitrary")),
    )(core, g, got)


def _sum_devices(g):
    _, m, n = g.shape

    def body(g_ref, o_ref):
        s = g_ref[0]
        for j in range(1, N_DEV):
            s = s + g_ref[j]
        o_ref[...] = s

    return pl.pallas_call(body, name="small_grad_sum", out_shape=jax.ShapeDtypeStruct((m, n), F32),
                          compiler_params=pltpu.CompilerParams(vmem_limit_bytes=VMEM_LIMIT))(g)


def _ada_fwd(c_full, w_ada, b_cols):
    nl, d, n = w_ada.shape
    nb = c_full.shape[0]

    def body(c_ref, w_ref, b_ref, o_ref):
        cv = c_ref[...]
        act = cv * _sig(cv)
        o_ref[0] = jnp.dot(act, w_ref[0], preferred_element_type=F32, precision=lax.Precision.HIGHEST) + b_ref[0]

    return pl.pallas_call(
        body, name="ada_fwd", grid=(nl,), out_shape=jax.ShapeDtypeStruct((nl, nb, n), F32),
        in_specs=[pl.BlockSpec((nb, d), lambda l: (0, 0)), pl.BlockSpec((1, d, n), lambda l: (l, 0, 0)),
                  pl.BlockSpec((1, 1, n), lambda l: (l, 0, 0))],
        out_specs=pl.BlockSpec((1, nb, n), lambda l: (l, 0, 0)), compiler_params=_params(("arbitrary",)),
    )(c_full, w_ada, b_cols)


def _ada_bwd(c_full, dmod_cols, dmod_full):
    nl, nb, n = dmod_cols.shape
    d = c_full.shape[1]
    nfull = dmod_full.shape[2]

    def body(c_ref, dc_ref, df_ref, gw_ref, gb_ref):
        cv = c_ref[...]
        act = cv * _sig(cv)
        gw_ref[0] = lax.dot_general(act, dc_ref[0], (((0,), (0,)), ((), ())), preferred_element_type=F32,
                                    precision=lax.Precision.HIGHEST)
        gb_ref[0] = jnp.sum(df_ref[0], axis=0, keepdims=True)

    return pl.pallas_call(
        body, name="ada_bwd", grid=(nl,),
        out_shape=(jax.ShapeDtypeStruct((nl, d, n), F32), jax.ShapeDtypeStruct((nl, 1, nfull), F32)),
        in_specs=[pl.BlockSpec((nb, d), lambda l: (0, 0)), pl.BlockSpec((1, nb, n), lambda l: (l, 0, 0)),
                  pl.BlockSpec((1, nb, nfull), lambda l: (l, 0, 0))],
        out_specs=(pl.BlockSpec((1, d, n), lambda l: (l, 0, 0)), pl.BlockSpec((1, 1, nfull), lambda l: (l, 0, 0))),
        compiler_params=_params(("arbitrary",)),
    )(c_full, dmod_cols, dmod_full)


def _rope_tables(pos, freq_row, sign_row):
    b, s, _ = pos.shape
    tm = _row_tile(s)

    def body(p_ref, f_ref, g_ref, c_ref, s_ref):
        ang = p_ref[0] * f_ref[...]
        lane = lax.broadcasted_iota(jnp.int32, ang.shape, 1)
        c_ref[0] = jnp.where(lane < QK_NOPE, 1.0, jnp.where(lane < QK_NOPE + QK_ROPE, jnp.cos(ang), 0.0))
        s_ref[0] = g_ref[...] * jnp.sin(ang)

    return pl.pallas_call(
        body, name="rope_tables", grid=(b, s // tm),
        out_shape=(jax.ShapeDtypeStruct((b, s, HEAD_PAD), F32),) * 2,
        in_specs=[_rows(tm, 1), pl.BlockSpec((1, HEAD_PAD), lambda i, j: (0, 0)),
                  pl.BlockSpec((1, HEAD_PAD), lambda i, j: (0, 0))],
        out_specs=(_rows(tm, HEAD_PAD),) * 2, compiler_params=_params(("arbitrary", "arbitrary")),
    )(pos, freq_row, sign_row)


def _mix_pre_fwd(x, mod, cos_t, sin_t, g_mix, g_q, g_kv, w_lat, w_glu, w_gate, w_q, w_k, w_v):
    b, s, d = x.shape
    ql, kl = g_q.shape[1], g_kv.shape[1]
    wl, wg, wt = w_lat.shape[1], w_glu.shape[1], w_gate.shape[1]
    hw = N_HEADS * HEAD_PAD
    tm = _row_tile(s)

    def body(x_ref, mod_ref, cos_ref, sin_ref, gm_ref, gq_ref, gkv_ref, wlat_ref, wglu_ref, wgate_ref, wq_ref, wk_ref,
             wv_ref, lat_ref, glu_ref, gate_ref, q_ref, k_ref, v_ref):
        xf = x_ref[0]
        sh, sc = mod_ref[0, 0], mod_ref[0, 1]
        hb = ((xf * _rstd(xf) * gm_ref[...]) * (1.0 + sc) + sh).astype(BF16)
        glu_ref[0] = _dot(hb, wglu_ref[...]).astype(BF16)
        gate_ref[0] = _dot(hb, wgate_ref[...]).astype(BF16)
        lat = _dot(hb, wlat_ref[...]).astype(BF16)
        lat_ref[0] = lat
        latf = lat.astype(F32)
        q_lat, kv_lat, kr_sec = latf[:, :ql], latf[:, ql:ql + kl], latf[:, ql + kl:]
        qn = (q_lat * _rstd(q_lat) * gq_ref[...]).astype(BF16)
        kvn = (kv_lat * _rstd(kv_lat) * gkv_ref[...]).astype(BF16)
        cos_v, sin_v = cos_ref[0], sin_ref[0]
        lane = lax.broadcasted_iota(jnp.int32, kr_sec.shape, 1)
        kr = jnp.where(lane >= QK_NOPE, _rope(kr_sec, cos_v, sin_v), 0.0)
        q_all, k_all, v_all = _dot(qn, wq_ref[...]), _dot(kvn, wk_ref[...]), _dot(kvn, wv_ref[...])
        vlane = lax.broadcasted_iota(jnp.int32, v_all.shape, 1)
        v_ref[0] = jnp.where(vlane % HEAD_PAD == V_HEAD, 1.0, v_all).astype(BF16)
        for h in range(N_HEADS):
            cols = slice(h * HEAD_PAD, (h + 1) * HEAD_PAD)
            q_ref[0, :, cols] = (_rope(q_all[:, cols], cos_v, sin_v) * ATTN_SCALE).astype(BF16)
            k_ref[0, :, cols] = (k_all[:, cols] + kr).astype(BF16)

    hshape = jax.ShapeDtypeStruct((b, s, hw), BF16)
    return pl.pallas_call(
        body, name="mix_pre_fwd", grid=(b, s // tm),
        out_shape=(jax.ShapeDtypeStruct((b, s, wl), BF16), jax.ShapeDtypeStruct((b, s, wg), BF16),
                   jax.ShapeDtypeStruct((b, s, wt), BF16), hshape, hshape, hshape),
        in_specs=[_rows(tm, d), _per_b(6, d), _rows(tm, HEAD_PAD), _rows(tm, HEAD_PAD), _const((1, d)), _const((1, ql)),
                  _const((1, kl)), _const(w_lat.shape), _const(w_glu.shape), _const(w_gate.shape), _const(w_q.shape),
                  _const(w_k.shape), _const(w_v.shape)],
        out_specs=(_rows(tm, wl), _rows(tm, wg), _rows(tm, wt), _rows(tm, hw), _rows(tm, hw), _rows(tm, hw)),
        compiler_params=_params(("arbitrary", "arbitrary")),
    )(x, mod, cos_t, sin_t, g_mix, g_q, g_kv, w_lat, w_glu, w_gate, w_q, w_k, w_v)


def _causal_mask(tq, tk):
    return lax.broadcasted_iota(jnp.int32, (tq, tk), 0) >= lax.broadcasted_iota(jnp.int32, (tq, tk), 1)


def _flash_fwd(q, k, v):
    b, s, hw = q.shape
    nh, hp = hw // HEAD_PAD, HEAD_PAD
    t = _row_tile(s)

    def body(q_ref, k_ref, v_ref, o_ref, lse_ref):
        i = pl.program_id(2)
        qv = q_ref[0]

        def step(j, carry, masked):
            m, acc = carry
            rows = pl.ds(pl.multiple_of(j * t, t), t)
            sc = _dot_nt(qv, k_ref[0, rows, :])
            if masked:
                sc = jnp.where(_causal_mask(t, t), sc, NEG_INF)
            m_new = jnp.maximum(m, jnp.max(sc, axis=-1, keepdims=True))
            p = jnp.exp((sc - m_new).astype(BF16))
            acc = jnp.exp(m - m_new) * acc + _dot(p, v_ref[0, rows, :])
            return m_new, acc

        init = (jnp.full((t, 1), NEG_INF, F32), jnp.zeros((t, hp), F32))
        carry = lax.fori_loop(0, i, lambda j, cr: step(j, cr, False), init)
        m, acc = step(i, carry, True)
        lane = lax.broadcasted_iota(jnp.int32, acc.shape, 1)
        l = jnp.sum(jnp.where(lane == V_HEAD, acc, 0.0), axis=-1, keepdims=True)
        o_ref[0] = (acc / l).astype(BF16)
        lse_ref[0, 0] = m + jnp.log(l)

    tile = pl.BlockSpec((1, t, hp), lambda bb, hh, ii: (bb, ii, hh))
    full = pl.BlockSpec((1, s, hp), lambda bb, hh, ii: (bb, 0, hh))
    return pl.pallas_call(
        body, name="flash_fwd", grid=(b, nh, s // t),
        out_shape=(jax.ShapeDtypeStruct((b, s, hw), BF16), jax.ShapeDtypeStruct((b, nh, s, 1), F32)),
        in_specs=[tile, full, full],
        out_specs=(tile, pl.BlockSpec((1, 1, t, 1), lambda bb, hh, ii: (bb, hh, ii, 0))),
        compiler_params=_params(("arbitrary", "arbitrary", "arbitrary")),
    )(q, k, v)


def _halo_prev(tm, w):
    r = tm // HALO
    return pl.BlockSpec((1, HALO, w), lambda b, s: (b, jnp.maximum(s * r - 1, 0), 0))


def _halo_next(tm, w, n_tiles):
    r = tm // HALO
    return pl.BlockSpec((1, HALO, w), lambda b, s: (b, jnp.minimum((s + 1) * r, n_tiles * r - 1), 0))


def _glu(v, cc):
    a, g = v[:, :cc].astype(F32), v[:, cc:].astype(F32)
    return a * _sig(g)


def _conv_fwd(glu, w_dw, b_dw, g_cn, b_cn, w_pw2):
    b, s, w2 = glu.shape
    cc = w2 // 2
    d = w_pw2.shape[1]
    tm = _row_tile(s)

    def body(cur_ref, prev_ref, w_ref, bdw_ref, g_ref, bcn_ref, wp_ref, u_ref, y_ref, ext):
        first = pl.program_id(1) == 0
        ext[pl.ds(0, HALO), :] = jnp.where(first, 0.0, _glu(prev_ref[0], cc))
        ext[pl.ds(HALO, tm), :] = _glu(cur_ref[0], cc)
        u = jnp.zeros((tm, cc), F32) + bdw_ref[...]
        for kk in range(CONV_W):
            u = u + w_ref[pl.ds(kk, 1), :] * ext[pl.ds(HALO - CONV_W + 1 + kk, tm), :]
        ub = u.astype(BF16)
        u_ref[0] = ub
        uf = ub.astype(F32)
        mu = jnp.mean(uf, axis=-1, keepdims=True)
        uc = uf - mu
        ln = uc * lax.rsqrt(jnp.mean(uc * uc, axis=-1, keepdims=True) + EPS) * g_ref[...] + bcn_ref[...]
        y_ref[0] = _dot((ln * _sig(ln)).astype(BF16), wp_ref[...]).astype(BF16)

    return pl.pallas_call(
        body, name="conv_fwd", grid=(b, s // tm),
        out_shape=(jax.ShapeDtypeStruct((b, s, cc), BF16), jax.ShapeDtypeStruct((b, s, d), BF16)),
        in_specs=[_rows(tm, w2), _halo_prev(tm, w2), _const(w_dw.shape), _const((1, cc)), _const((1, cc)), _const((1, cc)),
                  _const(w_pw2.shape)],
        out_specs=(_rows(tm, cc), _rows(tm, d)),
        scratch_shapes=[pltpu.VMEM((tm + HALO, cc), F32)],
        compiler_params=_params(("arbitrary", "arbitrary")),
    )(glu, glu, w_dw, b_dw, g_cn, b_cn, w_pw2)


def _mix_post_fwd(x, mod, o, y_conv, gate, w_o, w_out):
    b, s, d = x.shape
    hw = o.shape[2]
    tm = _row_tile(s)

    def body(x_ref, mod_ref, o_ref, yc_ref, gate_ref, wo_ref, wout_ref, x2_ref, ya_ref, o1_ref):
        yab = _dot(o_ref[0], wo_ref[...]).astype(BF16)
        ya_ref[0] = yab
        gv = gate_ref[0]
        y = _sig(gv[:, :d].astype(F32)) * yab.astype(F32) + _sig(gv[:, d:].astype(F32)) * yc_ref[0].astype(F32)
        o1 = _dot(y.astype(BF16), wout_ref[...])
        o1_ref[0] = o1.astype(BF16)
        x2_ref[0] = x_ref[0] + mod_ref[0, 2] * o1

    return pl.pallas_call(
        body, name="mix_post_fwd", grid=(b, s // tm),
        out_shape=(jax.ShapeDtypeStruct((b, s, d), F32), jax.ShapeDtypeStruct((b, s, d), BF16),
                   jax.ShapeDtypeStruct((b, s, d), BF16)),
        in_specs=[_rows(tm, d), _per_b(6, d), _rows(tm, hw), _rows(tm, d), _rows(tm, 2 * d), _const(w_o.shape),
                  _const(w_out.shape)],
        out_specs=(_rows(tm, d), _rows(tm, d), _rows(tm, d)),
        compiler_params=_params(("arbitrary", "arbitrary")),
    )(x, mod, o, y_conv, gate, w_o, w_out)


def _ffn_fwd(x2, mod, g_ffn, w_gu, w_down):
    b, s, d = x2.shape
    f = w_down.shape[0]
    fc = _pick(f, 512)
    tm = _row_tile(s)

    def body(x_ref, mod_ref, g_ref, wgu_ref, wdn_ref, x3_ref, gu_ref, o2_ref):
        xf = x_ref[0]
        hb = ((xf * _rstd(xf) * g_ref[...]) * (1.0 + mod_ref[0, 4]) + mod_ref[0, 3]).astype(BF16)
        o2 = jnp.zeros((tm, d), F32)
        for c0 in range(0, f, fc):
            gb = _dot(hb, wgu_ref[:, c0:c0 + fc]).astype(BF16)
            ub = _dot(hb, wgu_ref[:, f + c0:f + c0 + fc]).astype(BF16)
            gu_ref[0, :, c0:c0 + fc] = gb
            gu_ref[0, :, f + c0:f + c0 + fc] = ub
            gf = gb.astype(F32)
            act = (gf * _sig(gf) * ub.astype(F32)).astype(BF16)
            o2 = o2 + _dot(act, wdn_ref[c0:c0 + fc, :])
        o2_ref[0] = o2.astype(BF16)
        x3_ref[0] = xf + mod_ref[0, 5] * o2

    return pl.pallas_call(
        body, name="ffn_fwd", grid=(b, s // tm),
        out_shape=(jax.ShapeDtypeStruct((b, s, d), F32), jax.ShapeDtypeStruct((b, s, 2 * f), BF16),
                   jax.ShapeDtypeStruct((b, s, d), BF16)),
        in_specs=[_rows(tm, d), _per_b(6, d), _const((1, d)), _const(w_gu.shape), _const(w_down.shape)],
        out_specs=(_rows(tm, d), _rows(tm, 2 * f), _rows(tm, d)),
        compiler_params=_params(("arbitrary", "arbitrary")),
    )(x2, mod, g_ffn, w_gu, w_down)


def _zero_at_first_tile(*refs):
    @pl.when(pl.program_id(1) == 0)
    def _():
        for ref in refs:
            ref[...] = jnp.zeros_like(ref)


def _accumulate(ref, idx, val):
    ref[idx] = ref[idx] + val


def _colsum(v):
    return jnp.sum(v, axis=0, keepdims=True)


def _loss_bwd(x, target, g_final):
    b, s, d = x.shape
    tm = _row_tile(s)

    def body(x_ref, t_ref, g_ref, dx_ref, loss_ref, dg_ref):
        _zero_at_first_tile(loss_ref, dg_ref)
        xf = x_ref[0]
        r = _rstd(xf)
        xh = xf * r
        diff = xh * g_ref[...] - t_ref[0]
        _accumulate(loss_ref, (0, 0), _colsum(diff * diff))
        dy = diff * (1.0 / d)
        _accumulate(dg_ref, (0, 0), _colsum(dy * xh))
        dyg = dy * g_ref[...]
        dx_ref[0] = r * (dyg - xh * jnp.mean(dyg * xh, axis=-1, keepdims=True))

    return pl.pallas_call(
        body, name="loss_bwd", grid=(b, s // tm),
        out_shape=(jax.ShapeDtypeStruct((b, s, d), F32), jax.ShapeDtypeStruct((b, 1, 1, d), F32),
                   jax.ShapeDtypeStruct((b, 1, 1, d), F32)),
        in_specs=[_rows(tm, d), _rows(tm, d), _const((1, d))],
        out_specs=(_rows(tm, d), _per_b(1, d), _per_b(1, d)),
        compiler_params=_params(("arbitrary", "arbitrary")),
    )(x, target, g_final)


def _ffn_bwd(dx3, x2, mod, g_ffn, gu, o2, w_gu_t, w_down_t):
    b, s, d = x2.shape
    f = w_down_t.shape[1]
    fc = _pick(f, 512)
    tm = _row_tile(s, 256)

    def body(dx3_ref, x_ref, mod_ref, g_ref, gu_ref, o2_ref, wgut_ref, wdnt_ref, dx2_ref, do2_ref, act_ref, dgu_ref, h_ref,
             dmod_ref, dg_ref):
        _zero_at_first_tile(dmod_ref, dg_ref)
        dx3 = dx3_ref[0]
        sh, sc, gt = mod_ref[0, 3], mod_ref[0, 4], mod_ref[0, 5]
        do2 = (dx3 * gt).astype(BF16)
        do2_ref[0] = do2
        _accumulate(dmod_ref, (0, 2), _colsum(dx3 * o2_ref[0].astype(F32)))
        dh = jnp.zeros((tm, d), F32)
        for c0 in range(0, f, fc):
            gf = gu_ref[0, :, c0:c0 + fc].astype(F32)
            uf = gu_ref[0, :, f + c0:f + c0 + fc].astype(F32)
            sg = _sig(gf)
            silu = gf * sg
            act_ref[0, :, c0:c0 + fc] = (silu * uf).astype(BF16)
            dact = _dot(do2, wdnt_ref[:, c0:c0 + fc])
            dg = (dact * uf * (sg * (1.0 + gf * (1.0 - sg)))).astype(BF16)
            du = (dact * silu).astype(BF16)
            dgu_ref[0, :, c0:c0 + fc] = dg
            dgu_ref[0, :, f + c0:f + c0 + fc] = du
            dh = dh + _dot(dg, wgut_ref[c0:c0 + fc, :]) + _dot(du, wgut_ref[f + c0:f + c0 + fc, :])
        xf = x_ref[0]
        r = _rstd(xf)
        xh = xf * r
        n = xh * g_ref[...]
        h_ref[0] = (n * (1.0 + sc) + sh).astype(BF16)
        _accumulate(dmod_ref, (0, 0), _colsum(dh))
        _accumulate(dmod_ref, (0, 1), _colsum(dh * n))
        dn = dh * (1.0 + sc)
        _accumulate(dg_ref, (0, 0), _colsum(dn * xh))
        dyg = dn * g_ref[...]
        dx2_ref[0] = dx3 + r * (dyg - xh * jnp.mean(dyg * xh, axis=-1, keepdims=True))

    return pl.pallas_call(
        body, name="ffn_bwd", grid=(b, s // tm),
        out_shape=(jax.ShapeDtypeStruct((b, s, d), F32), jax.ShapeDtypeStruct((b, s, d), BF16),
                   jax.ShapeDtypeStruct((b, s, f), BF16), jax.ShapeDtypeStruct((b, s, 2 * f), BF16),
                   jax.ShapeDtypeStruct((b, s, d), BF16), jax.ShapeDtypeStruct((b, 3, 1, d), F32),
                   jax.ShapeDtypeStruct((b, 1, 1, d), F32)),
        in_specs=[_rows(tm, d), _rows(tm, d), _per_b(6, d), _const((1, d)), _rows(tm, 2 * f), _rows(tm, d),
                  _const(w_gu_t.shape), _const(w_down_t.shape)],
        out_specs=(_rows(tm, d), _rows(tm, d), _rows(tm, f), _rows(tm, 2 * f), _rows(tm, d), _per_b(3, d), _per_b(1, d)),
        compiler_params=_params(("arbitrary", "arbitrary")),
    )(dx3, x2, mod, g_ffn, gu, o2, w_gu_t, w_down_t)


def _mix_post_bwd(dx2, mod, o1, y_attn, y_conv, gate, w_out_t, w_o_t):
    b, s, d = dx2.shape
    hw = w_o_t.shape[1]
    tm = _row_tile(s)

    def body(dx_ref, mod_ref, o1_ref, ya_ref, yc_ref, gate_ref, woutt_ref, wot_ref, do1_ref, dya_ref, dyc_ref, dgate_ref,
             y_ref, do_ref, dgt_ref):
        _zero_at_first_tile(dgt_ref)
        dx = dx_ref[0]
        do1 = (dx * mod_ref[0, 2]).astype(BF16)
        do1_ref[0] = do1
        _accumulate(dgt_ref, (0, 0), _colsum(dx * o1_ref[0].astype(F32)))
        dy = _dot(do1, woutt_ref[...])
        gv = gate_ref[0]
        sa, sb = _sig(gv[:, :d].astype(F32)), _sig(gv[:, d:].astype(F32))
        ya, yc = ya_ref[0].astype(F32), yc_ref[0].astype(F32)
        y_ref[0] = (sa * ya + sb * yc).astype(BF16)
        dya = (dy * sa).astype(BF16)
        dya_ref[0] = dya
        dyc_ref[0] = (dy * sb).astype(BF16)
        dgate_ref[0, :, :d] = (dy * ya * sa * (1.0 - sa)).astype(BF16)
        dgate_ref[0, :, d:] = (dy * yc * sb * (1.0 - sb)).astype(BF16)
        do_ref[0] = _dot(dya, wot_ref[...]).astype(BF16)

    row = jax.ShapeDtypeStruct((b, s, d), BF16)
    return pl.pallas_call(
        body, name="mix_post_bwd", grid=(b, s // tm),
        out_shape=(row, row, row, jax.ShapeDtypeStruct((b, s, 2 * d), BF16), row,
                   jax.ShapeDtypeStruct((b, s, hw), BF16), jax.ShapeDtypeStruct((b, 1, 1, d), F32)),
        in_specs=[_rows(tm, d), _per_b(6, d), _rows(tm, d), _rows(tm, d), _rows(tm, d), _rows(tm, 2 * d),
                  _const(w_out_t.shape), _const(w_o_t.shape)],
        out_specs=(_rows(tm, d), _rows(tm, d), _rows(tm, d), _rows(tm, 2 * d), _rows(tm, d), _rows(tm, hw), _per_b(1, d)),
        compiler_params=_params(("arbitrary", "arbitrary")),
    )(dx2, mod, o1, y_attn, y_conv, gate, w_out_t, w_o_t)


def _conv_bwd(dyc, u, glu, w_dw, g_cn, b_cn, w_pw2_t):
    b, s, cc = u.shape
    d = dyc.shape[2]
    tm = _row_tile(s)
    nt = s // tm
    te = tm + HALO

    def body(dyc_ref, dycn_ref, u_ref, un_ref, glu_ref, glup_ref, w_ref, g_ref, bcn_ref, wpt_ref, dglu_ref, s_ref, dw_ref,
             small_ref, du_ext, uin_ext):
        _zero_at_first_tile(dw_ref, small_ref)
        st = pl.program_id(1)
        dy_all = jnp.concatenate([dyc_ref[0], dycn_ref[0]], axis=0)
        u_all = jnp.concatenate([u_ref[0], un_ref[0]], axis=0).astype(F32)
        ds = _dot(dy_all, wpt_ref[...])
        mu = jnp.mean(u_all, axis=-1, keepdims=True)
        uc = u_all - mu
        rstd = lax.rsqrt(jnp.mean(uc * uc, axis=-1, keepdims=True) + EPS)
        uh = uc * rstd
        ln = uh * g_ref[...] + bcn_ref[...]
        sg = _sig(ln)
        s_ref[0] = (ln * sg)[:tm].astype(BF16)
        dln = ds * (sg * (1.0 + ln * (1.0 - sg)))
        duh = dln * g_ref[...]
        du = rstd * (duh - jnp.mean(duh, axis=-1, keepdims=True) - uh * jnp.mean(duh * uh, axis=-1, keepdims=True))
        row = lax.broadcasted_iota(jnp.int32, (te, 1), 0)
        du = jnp.where(jnp.logical_and(st == nt - 1, row >= tm), 0.0, du)
        du_ext[...] = du
        du_cur = du[:tm]
        _accumulate(small_ref, (0, 0), _colsum((dln * uh)[:tm]))
        _accumulate(small_ref, (0, 1), _colsum(dln[:tm]))
        _accumulate(small_ref, (0, 2), _colsum(du_cur))
        uin_ext[pl.ds(0, HALO), :] = jnp.where(st == 0, 0.0, _glu(glup_ref[0], cc))
        gv = glu_ref[0]
        ga, gb = gv[:, :cc].astype(F32), gv[:, cc:].astype(F32)
        sgb = _sig(gb)
        uin_ext[pl.ds(HALO, tm), :] = ga * sgb
        duin = jnp.zeros((tm, cc), F32)
        for kk in range(CONV_W):
            duin = duin + w_ref[pl.ds(kk, 1), :] * du_ext[pl.ds(CONV_W - 1 - kk, tm), :]
            part = _colsum(du_cur * uin_ext[pl.ds(HALO - CONV_W + 1 + kk, tm), :])
            _accumulate(dw_ref, (0, pl.ds(kk, 1)), part)
        dglu_ref[0, :, :cc] = (duin * sgb).astype(BF16)
        dglu_ref[0, :, cc:] = (duin * ga * sgb * (1.0 - sgb)).astype(BF16)

    return pl.pallas_call(
        body, name="conv_bwd", grid=(b, nt),
        out_shape=(jax.ShapeDtypeStruct((b, s, 2 * cc), BF16), jax.ShapeDtypeStruct((b, s, cc), BF16),
                   jax.ShapeDtypeStruct((b, HALO, cc), F32), jax.ShapeDtypeStruct((b, 3, 1, cc), F32)),
        in_specs=[_rows(tm, d), _halo_next(tm, d, nt), _rows(tm, cc), _halo_next(tm, cc, nt), _rows(tm, 2 * cc),
                  _halo_prev(tm, 2 * cc), _const(w_dw.shape), _const((1, cc)), _const((1, cc)), _const(w_pw2_t.shape)],
        out_specs=(_rows(tm, 2 * cc), _rows(tm, cc), pl.BlockSpec((1, HALO, cc), lambda i, j: (i, 0, 0)), _per_b(3, cc)),
        scratch_shapes=[pltpu.VMEM((te, cc), F32), pltpu.VMEM((te, cc), F32)],
        compiler_params=_params(("arbitrary", "arbitrary")),
    )(dyc, dyc, u, u, glu, glu, w_dw, g_cn, b_cn, w_pw2_t)


def _flash_bwd(q, k, v, o, lse, do):
    b, s, hw = q.shape
    nh, hp = hw // HEAD_PAD, HEAD_PAD
    t = _row_tile(s)
    nt = s // t

    def body(q_ref, k_ref, v_ref, o_ref, lse_ref, do_ref, dq_ref, dk_ref, dv_ref, delta):
        j = pl.program_id(2)

        @pl.when(j == 0)
        def _():
            dq_ref[...] = jnp.zeros_like(dq_ref)
            for i in range(nt):
                rows = pl.ds(i * t, t)
                delta[rows, :] = jnp.sum(do_ref[0, rows, :].astype(F32) * o_ref[0, rows, :].astype(F32), axis=-1,
                                         keepdims=True)

        kv, vv = k_ref[0], v_ref[0]

        def step(i, carry, masked):
            dk, dv = carry
            rows = pl.ds(pl.multiple_of(i * t, t), t)
            qv, dov = q_ref[0, rows, :], do_ref[0, rows, :]
            p = jnp.exp((_dot_nt(qv, kv) - lse_ref[0, 0, rows, :]).astype(BF16))
            if masked:
                p = jnp.where(_causal_mask(t, t), p, jnp.zeros((), BF16))
            dv = dv + _dot_tn(p, dov)
            dsb = p * (_dot_nt(dov, vv) - delta[rows, :]).astype(BF16)
            dk = dk + _dot_tn(dsb, qv)
            dq_ref[0, rows, :] = dq_ref[0, rows, :] + _dot(dsb, kv)
            return dk, dv

        carry = step(j, (jnp.zeros((t, hp), F32), jnp.zeros((t, hp), F32)), True)
        dk, dv = lax.fori_loop(j + 1, nt, lambda i, cr: step(i, cr, False), carry)
        dk_ref[0] = dk.astype(BF16)
        dv_ref[0] = dv.astype(BF16)

    tile = pl.BlockSpec((1, t, hp), lambda bb, hh, jj: (bb, jj, hh))
    full = pl.BlockSpec((1, s, hp), lambda bb, hh, jj: (bb, 0, hh))
    return pl.pallas_call(
        body, name="flash_bwd", grid=(b, nh, nt),
        out_shape=(jax.ShapeDtypeStruct((b, s, hw), F32), jax.ShapeDtypeStruct((b, s, hw), BF16),
                   jax.ShapeDtypeStruct((b, s, hw), BF16)),
        in_specs=[full, tile, tile, full, pl.BlockSpec((1, 1, s, 1), lambda bb, hh, jj: (bb, hh, 0, 0)), full],
        out_specs=(full, tile, tile),
        scratch_shapes=[pltpu.VMEM((s, 1), F32)],
        compiler_params=_params(("arbitrary", "arbitrary", "arbitrary")),
    )(q, k, v, o, lse, do)


def _mix_pre_bwd(x, dx2, mod, cos_t, sin_t, g_mix, g_q, g_kv, lat, dq, dk, dv, dglu, dgate, w_lat_t, w_glu_t, w_gate_t,
                 w_q_t, w_k_t, w_v_t):
    b, s, d = x.shape
    ql, kl = g_q.shape[1], g_kv.shape[1]
    wl = lat.shape[2]
    hw = N_HEADS * HEAD_PAD
    tm = _row_tile(s)

    def body(x_ref, dx2_ref, mod_ref, cos_ref, sin_ref, gm_ref, gq_ref, gkv_ref, lat_ref, dq_ref, dk_ref, dv_ref, dglu_ref,
             dgate_ref, wlt_ref, wgt_ref, wtt_ref, wqt_ref, wkt_ref, wvt_ref, dx_ref, dlat_ref, dqr_ref, qn_ref, kvn_ref,
             h_ref, dmod_ref, dgm_ref, dgq_ref, dgkv_ref):
        _zero_at_first_tile(dmod_ref, dgm_ref, dgq_ref, dgkv_ref)
        cos_v, sin_v = cos_ref[0], sin_ref[0]
        latf = lat_ref[0].astype(F32)
        q_lat, kv_lat = latf[:, :ql], latf[:, ql:ql + kl]
        rq, rk = _rstd(q_lat), _rstd(kv_lat)
        qh, kh = q_lat * rq, kv_lat * rk
        qn_ref[0] = (qh * gq_ref[...]).astype(BF16)
        kvn_ref[0] = (kh * gkv_ref[...]).astype(BF16)
        dk_sum = jnp.zeros((tm, HEAD_PAD), F32)
        for h in range(N_HEADS):
            cols = slice(h * HEAD_PAD, (h + 1) * HEAD_PAD)
            dqr_ref[0, :, cols] = _rope_bwd(dq_ref[0, :, cols] * ATTN_SCALE, cos_v, sin_v).astype(BF16)
            dk_sum = dk_sum + dk_ref[0, :, cols].astype(F32)
        dqn = _dot(dqr_ref[0], wqt_ref[...])
        dkvn = _dot(dk_ref[0], wkt_ref[...]) + _dot(dv_ref[0], wvt_ref[...])
        lane = lax.broadcasted_iota(jnp.int32, dk_sum.shape, 1)
        dkr = _rope_bwd(jnp.where(lane >= QK_NOPE, dk_sum, 0.0), cos_v, sin_v)
        _accumulate(dgq_ref, (0, 0), _colsum(dqn * qh))
        _accumulate(dgkv_ref, (0, 0), _colsum(dkvn * kh))
        dqg, dkg = dqn * gq_ref[...], dkvn * gkv_ref[...]
        dlat_ref[0, :, :ql] = (rq * (dqg - qh * jnp.mean(dqg * qh, axis=-1, keepdims=True))).astype(BF16)
        dlat_ref[0, :, ql:ql + kl] = (rk * (dkg - kh * jnp.mean(dkg * kh, axis=-1, keepdims=True))).astype(BF16)
        dlat_ref[0, :, ql + kl:] = dkr.astype(BF16)
        dh = _dot(dlat_ref[0], wlt_ref[...]) + _dot(dglu_ref[0], wgt_ref[...]) + _dot(dgate_ref[0], wtt_ref[...])
        sh, sc = mod_ref[0, 0], mod_ref[0, 1]
        xf = x_ref[0]
        r = _rstd(xf)
        xh = xf * r
        n = xh * gm_ref[...]
        h_ref[0] = (n * (1.0 + sc) + sh).astype(BF16)
        _accumulate(dmod_ref, (0, 0), _colsum(dh))
        _accumulate(dmod_ref, (0, 1), _colsum(dh * n))
        dn = dh * (1.0 + sc)
        _accumulate(dgm_ref, (0, 0), _colsum(dn * xh))
        dyg = dn * gm_ref[...]
        dx_ref[0] = dx2_ref[0] + r * (dyg - xh * jnp.mean(dyg * xh, axis=-1, keepdims=True))

    return pl.pallas_call(
        body, name="mix_pre_bwd", grid=(b, s // tm),
        out_shape=(jax.ShapeDtypeStruct((b, s, d), F32), jax.ShapeDtypeStruct((b, s, wl), BF16),
                   jax.ShapeDtypeStruct((b, s, hw), BF16), jax.ShapeDtypeStruct((b, s, ql), BF16),
                   jax.ShapeDtypeStruct((b, s, kl), BF16), jax.ShapeDtypeStruct((b, s, d), BF16),
                   jax.ShapeDtypeStruct((b, 2, 1, d), F32), jax.ShapeDtypeStruct((b, 1, 1, d), F32),
                   jax.ShapeDtypeStruct((b, 1, 1, ql), F32), jax.ShapeDtypeStruct((b, 1, 1, kl), F32)),
        in_specs=[_rows(tm, d), _rows(tm, d), _per_b(6, d), _rows(tm, HEAD_PAD), _rows(tm, HEAD_PAD), _const((1, d)),
                  _const((1, ql)), _const((1, kl)), _rows(tm, wl), _rows(tm, hw), _rows(tm, hw), _rows(tm, hw),
                  _rows(tm, dglu.shape[2]), _rows(tm, 2 * d), _const(w_lat_t.shape), _const(w_glu_t.shape),
                  _const(w_gate_t.shape), _const(w_q_t.shape), _const(w_k_t.shape), _const(w_v_t.shape)],
        out_specs=(_rows(tm, d), _rows(tm, wl), _rows(tm, hw), _rows(tm, ql), _rows(tm, kl), _rows(tm, d), _per_b(2, d),
                   _per_b(1, d), _per_b(1, ql), _per_b(1, kl)),
        compiler_params=_params(("arbitrary", "arbitrary")),
    )(x, dx2, mod, cos_t, sin_t, g_mix, g_q, g_kv, lat, dq, dk, dv, dglu, dgate, w_lat_t, w_glu_t, w_gate_t, w_q_t, w_k_t,
      w_v_t)


def _matmul_tn(a, bm, name):
    b, s, kd = a.shape
    nd = bm.shape[2]
    tk, tn = _pick(kd, 1536), _pick(nd, 1536)
    ts = _row_tile(s, 2048)

    def body(a_ref, b_ref, o_ref):
        part = _dot_tn(a_ref[0], b_ref[0])
        first = jnp.logical_and(pl.program_id(2) == 0, pl.program_id(3) == 0)

        @pl.when(first)
        def _():
            o_ref[...] = part

        @pl.when(jnp.logical_not(first))
        def _():
            o_ref[...] = o_ref[...] + part

    return pl.pallas_call(
        body, name=name, grid=(kd // tk, nd // tn, b, s // ts),
        out_shape=jax.ShapeDtypeStruct((kd, nd), F32),
        in_specs=[pl.BlockSpec((1, ts, tk), lambda i, j, bb, ss: (bb, ss, i)),
                  pl.BlockSpec((1, ts, tn), lambda i, j, bb, ss: (bb, ss, j))],
        out_specs=pl.BlockSpec((tk, tn), lambda i, j, bb, ss: (i, j)),
        compiler_params=_params(("arbitrary",) * 4),
    )(a, bm)


def _adamw_update(w, g, m, v):
    nm = ADAM_B1 * m + (1.0 - ADAM_B1) * g
    nv = ADAM_B2 * v + (1.0 - ADAM_B2) * (g * g)
    delta = -ADAM_LR * ((nm / (1.0 - ADAM_B1 ** ADAM_STEP)) / (jnp.sqrt(nv / (1.0 - ADAM_B2 ** ADAM_STEP)) + ADAM_EPS)
                        + ADAM_WD * w)
    return delta, nm, nv


def _adamw(w, g, m, v, name):
    shape = w.shape
    cols = shape[-1]
    rows = w.size // cols
    w2, g2, m2, v2 = (t.reshape(rows, cols) for t in (w, g, m, v))
    tr = rows
    if rows * cols * 4 > (1 << 20):
        tr = _div_tile(rows, max(8, (1 << 18) // cols), 8)

    def body(w_ref, g_ref, m_ref, v_ref, d_ref, nm_ref, nv_ref):
        d_ref[...], nm_ref[...], nv_ref[...] = _adamw_update(w_ref[...], g_ref[...], m_ref[...], v_ref[...])

    spec = pl.BlockSpec((tr, cols), lambda i: (i, 0))
    outs = pl.pallas_call(
        body, name=name, grid=(rows // tr,), out_shape=(jax.ShapeDtypeStruct((rows, cols), F32),) * 3,
        in_specs=[spec] * 4, out_specs=(spec,) * 3, compiler_params=_params(("arbitrary",)),
    )(w2, g2, m2, v2)
    return tuple(t.reshape(shape) for t in outs)


def _adamw_reduce(w, m, v, part, got, chip, name):
    shape = w.shape
    cols = shape[-1]
    w2, m2, v2 = (_rows_2d(t, 0) for t in (w, m, v))
    p3, g3 = _rows_2d(part, 4), _rows_2d(got, 3)
    rows = w2.shape[0]
    tr = _grad_row_tile(rows, cols)

    def body(chip_ref, w_ref, m_ref, v_ref, p_ref, r_ref, g_ref, d_ref, nm_ref, nv_ref):
        g = ((p_ref[0].astype(F32) + r_ref[0].astype(F32)) + r_ref[1].astype(F32)) + r_ref[2].astype(F32)
        g_ref[...] = g
        d_ref[...], nm_ref[...], nv_ref[...] = _adamw_update(w_ref[...], g, m_ref[...], v_ref[...])

    spec = pl.BlockSpec((tr, cols), lambda i, chip_ref: (i, 0))
    outs = pl.pallas_call(
        body, name=name,
        grid_spec=pltpu.PrefetchScalarGridSpec(
            num_scalar_prefetch=1, grid=(rows // tr,),
            in_specs=[spec, spec, spec, pl.BlockSpec((1, tr, cols), lambda i, chip_ref: (chip_ref[0], i, 0)),
                      pl.BlockSpec((3, tr, cols), lambda i, chip_ref: (0, i, 0))],
            out_specs=(spec,) * 4),
        out_shape=(jax.ShapeDtypeStruct((rows, cols), F32),) * 4, compiler_params=_params(("arbitrary",)),
    )(chip, w2, m2, v2, p3, g3)
    return tuple(t.reshape(shape) for t in outs)


GATHERED = (("w_in", 2), ("w_uq", 2), ("w_ukv", 2), ("w_o_attn", 2), ("w_pw2", 2), ("w_out", 1), ("w_gu", 2), ("w_down", 1))


def _from_chunks(chunks, axis):
    _, nl, a, bb = chunks.shape
    if axis == 2:
        return jnp.transpose(chunks, (1, 2, 0, 3)).reshape(nl, a, N_DEV * bb)
    return jnp.transpose(chunks, (1, 0, 2, 3)).reshape(nl, N_DEV * a, bb)


def _chunks_by_core(full, axis):
    nl, a, bb = full.shape
    if axis == 2:
        t = jnp.transpose(full.reshape(nl, a, 2, 2, 2, bb // N_DEV), (4, 2, 3, 0, 1, 5))
    else:
        t = jnp.transpose(full.reshape(nl, 2, 2, 2, a // N_DEV, bb), (3, 1, 2, 0, 4, 5))
    return t.reshape((N_DEV,) + t.shape[3:]).astype(BF16)


def _swap_halves(t):
    half = QK_ROPE // 2
    return jnp.concatenate([t[..., half:], t[..., :half]], axis=-1)


def _t(w):
    return jnp.swapaxes(w, -1, -2)


def _pad_rows(t, mult=8):
    return jnp.pad(t, ((0, -t.shape[0] % mult), (0, 0)))


def _pad_last(t, width):
    return jnp.pad(t, ((0, 0),) * (t.ndim - 1) + ((0, width - t.shape[-1]),))


def kernel(x, c, positions, w_ada, b_ada, g_mix, w_in, g_q, w_uq, g_kv, w_ukv, w_o_attn, w_dw, b_dw, g_cn, b_cn, w_pw2, w_out, g_ffn, w_gu, w_down, g_final, loss_target, m_w_ada, m_b_ada, m_g_mix, m_w_in, m_g_q, m_w_uq, m_g_kv, m_w_ukv, m_w_o_attn, m_w_dw, m_b_dw, m_g_cn, m_b_cn, m_w_pw2, m_w_out, m_g_ffn, m_w_gu, m_w_down, m_g_final, v_w_ada, v_b_ada, v_g_mix, v_w_in, v_g_q, v_w_uq, v_g_kv, v_w_ukv, v_w_o_attn, v_w_dw, v_b_dw, v_g_cn, v_b_cn, v_w_pw2, v_w_out, v_g_ffn, v_w_gu, v_w_down, v_g_final):
    weights = dict(w_ada=w_ada, b_ada=b_ada, g_mix=g_mix, w_in=w_in, g_q=g_q, w_uq=w_uq, g_kv=g_kv, w_ukv=w_ukv,
                   w_o_attn=w_o_attn, w_dw=w_dw, b_dw=b_dw, g_cn=g_cn, b_cn=b_cn, w_pw2=w_pw2, w_out=w_out, g_ffn=g_ffn,
                   w_gu=w_gu, w_down=w_down, g_final=g_final)
    mom_m = dict(w_ada=m_w_ada, b_ada=m_b_ada, g_mix=m_g_mix, w_in=m_w_in, g_q=m_g_q, w_uq=m_w_uq, g_kv=m_g_kv,
                 w_ukv=m_w_ukv, w_o_attn=m_w_o_attn, w_dw=m_w_dw, b_dw=m_b_dw, g_cn=m_g_cn, b_cn=m_b_cn, w_pw2=m_w_pw2,
                 w_out=m_w_out, g_ffn=m_g_ffn, w_gu=m_w_gu, w_down=m_w_down, g_final=m_g_final)
    mom_v = dict(w_ada=v_w_ada, b_ada=v_b_ada, g_mix=v_g_mix, w_in=v_w_in, g_q=v_g_q, w_uq=v_w_uq, g_kv=v_g_kv,
                 w_ukv=v_w_ukv, w_o_attn=v_w_o_attn, w_dw=v_w_dw, b_dw=v_b_dw, g_cn=v_g_cn, b_cn=v_b_cn, w_pw2=v_w_pw2,
                 w_out=v_w_out, g_ffn=v_g_ffn, w_gu=v_w_gu, w_down=v_w_down, g_final=v_g_final)
    order = list(weights)

    nb, s, d = x.shape
    nl = w_in.shape[0]
    ql, kl, cc = g_q.shape[1], g_kv.shape[1], g_cn.shape[1]
    h = N_HEADS
    qk = QK_NOPE + QK_ROPE
    xi, yi, ci = lax.axis_index("x"), lax.axis_index("y"), lax.axis_index("c")
    me = 4 * xi + 2 * yi + ci
    core = jnp.reshape(ci, (1,)).astype(jnp.int32)
    chip = jnp.reshape(2 * xi + yi, (1,)).astype(jnp.int32)

    got = _all_gather([weights[n].astype(BF16) for n, _ in GATHERED], vmem=False, name="weight_all_gather")
    full = {n: _from_chunks(t, ax) for (n, ax), t in zip(GATHERED, got)}
    n_dw = w_dw.shape[2]
    dw_rows = jnp.pad(w_dw, ((0, 0), (0, HALO - CONV_W), (0, LANES - n_dw))).reshape(nl * HALO, LANES)
    c_all, dw_all = _all_gather([_pad_rows(c), dw_rows], vmem=True, name="cond_all_gather")
    c_full = c_all[:, :nb].reshape(N_DEV * nb, d)
    w_dw_full = jnp.transpose(dw_all.reshape(N_DEV, nl, HALO, LANES)[..., :n_dw], (1, 2, 0, 3)).reshape(nl, HALO, cc)

    o_kr, o_glu, o_gate = ql + kl, ql + kl + QK_ROPE, ql + kl + QK_ROPE + 2 * cc
    wi = full["w_in"]
    w_kr = wi[:, :, o_kr:o_glu]
    w_lat = jnp.concatenate([wi[:, :, :o_kr], jnp.zeros((nl, d, QK_NOPE), BF16), w_kr, _swap_halves(w_kr)], axis=2)
    w_glu, w_gate = wi[:, :, o_glu:o_gate], wi[:, :, o_gate:]
    wq = full["w_uq"].reshape(nl, ql, h, qk)
    w_q = jnp.concatenate([wq, _swap_halves(wq[..., QK_NOPE:])], axis=-1).reshape(nl, ql, h * HEAD_PAD)
    wkv = full["w_ukv"].reshape(nl, kl, h, QK_NOPE + V_HEAD)
    w_k = _pad_last(wkv[..., :QK_NOPE], HEAD_PAD).reshape(nl, kl, h * HEAD_PAD)
    w_v = _pad_last(wkv[..., QK_NOPE:], HEAD_PAD).reshape(nl, kl, h * HEAD_PAD)
    w_o = jnp.pad(full["w_o_attn"].reshape(nl, h, V_HEAD, d), ((0, 0), (0, 0), (0, HEAD_PAD - V_HEAD), (0, 0)))
    w_o = w_o.reshape(nl, h * HEAD_PAD, d)
    w_pw, w_ou, w_g, w_dn = full["w_pw2"], full["w_out"], full["w_gu"], full["w_down"]

    n_ada = w_ada.shape[2]
    b_cols = lax.dynamic_slice_in_dim(b_ada, me * n_ada, n_ada, axis=1).reshape(nl, 1, n_ada)
    mod_part = _ada_fwd(c_full, w_ada, b_cols)
    (mod_all,) = _all_gather([mod_part.reshape(nl * N_DEV * nb, n_ada)], vmem=True, name="mod_all_gather")
    mod_all = jnp.transpose(mod_all.reshape(N_DEV, nl, N_DEV * nb, n_ada), (1, 2, 0, 3)).reshape(nl, N_DEV * nb, 6 * d)
    mod = lax.dynamic_slice_in_dim(mod_all, me * nb, nb, axis=1).reshape(nl, nb, 6, 1, d)

    inv_freq = ROPE_THETA ** (-jnp.arange(0, QK_ROPE, 2, dtype=F32) / QK_ROPE)
    zeros = lambda n: jnp.zeros((n,), F32)
    freq_row = jnp.concatenate([zeros(QK_NOPE), inv_freq, inv_freq, zeros(HEAD_PAD - qk)]).reshape(1, -1)
    ones = jnp.ones((QK_ROPE // 2,), F32)
    sign_row = jnp.concatenate([zeros(QK_NOPE), -ones, ones, zeros(HEAD_PAD - qk)]).reshape(1, -1)
    cos_t, sin_t = _rope_tables(positions.astype(F32).reshape(nb, s, 1), freq_row, sign_row)

    row = lambda t, l: t[l].reshape(1, -1)

    saved = []
    xc = x
    for l in range(nl):
        lat, glu, gate, qh, kh, vh = _mix_pre_fwd(xc, mod[l], cos_t, sin_t, row(g_mix, l), row(g_q, l), row(g_kv, l),
                                                  w_lat[l], w_glu[l], w_gate[l], w_q[l], w_k[l], w_v[l])
        o, lse = _flash_fwd(qh, kh, vh)
        u, y_conv = _conv_fwd(glu, w_dw_full[l], row(b_dw, l), row(g_cn, l), row(b_cn, l), w_pw[l])
        x2, y_attn, o1 = _mix_post_fwd(xc, mod[l], o, y_conv, gate, w_o[l], w_ou[l])
        x3, gu, o2 = _ffn_fwd(x2, mod[l], row(g_ffn, l), w_g[l], w_dn[l])
        saved.append(dict(x=xc, lat=lat, glu=glu, gate=gate, q=qh, k=kh, v=vh, o=o, lse=lse, u=u, y_conv=y_conv, x2=x2,
                          y_attn=y_attn, o1=o1, gu=gu, o2=o2))
        xc = x3

    dx, loss_part, dgf_part = _loss_bwd(xc, loss_target, g_final.reshape(1, d))
    loss = lax.psum(0.5 / d * jnp.sum(loss_part), AXES)

    gw = {n: [None] * nl for n, _ in GATHERED}
    small_rows, dw_taps, dmod = [None] * nl, [None] * nl, [None] * nl
    for l in reversed(range(nl)):
        sv = saved[l]
        dx2, do2, act, dgu, h2, dmod2, dgffn = _ffn_bwd(dx, sv["x2"], mod[l], row(g_ffn, l), sv["gu"], sv["o2"],
                                                       _t(w_g[l]), _t(w_dn[l]))
        gw["w_gu"][l] = _matmul_tn(h2, dgu, "grad_w_gu")
        gw["w_down"][l] = _matmul_tn(act, do2, "grad_w_down")
        do1, dya, dyc, dgate, yv, do_h, dgt1 = _mix_post_bwd(dx2, mod[l], sv["o1"], sv["y_attn"], sv["y_conv"], sv["gate"],
                                                           _t(w_ou[l]), _t(w_o[l]))
        gw["w_out"][l] = _matmul_tn(yv, do1, "grad_w_out")
        dwo = _matmul_tn(sv["o"], dya, "grad_w_o")
        gw["w_o_attn"][l] = dwo.reshape(h, HEAD_PAD, d)[:, :V_HEAD].reshape(h * V_HEAD, d)
        dglu, s_act, ddw, csmall = _conv_bwd(dyc, sv["u"], sv["glu"], w_dw_full[l], row(g_cn, l), row(b_cn, l), _t(w_pw[l]))
        gw["w_pw2"][l] = _matmul_tn(s_act, dyc, "grad_w_pw2")
        dq, dk, dv = _flash_bwd(sv["q"], sv["k"], sv["v"], sv["o"], sv["lse"], do_h)
        dx, dlat, dqr, qn, kvn, h1, dmod1, dgm, dgq, dgkv = _mix_pre_bwd(
            sv["x"], dx2, mod[l], cos_t, sin_t, row(g_mix, l), row(g_q, l), row(g_kv, l), sv["lat"], dq, dk, dv, dglu,
            dgate, _t(w_lat[l]), _t(w_glu[l]), _t(w_gate[l]), _t(w_q[l]), _t(w_k[l]), _t(w_v[l]))
        dwl = _matmul_tn(h1, dlat, "grad_w_lat")
        dwg = _matmul_tn(h1, dglu, "grad_w_glu")
        dwt = _matmul_tn(h1, dgate, "grad_w_gate")
        kr0 = o_kr + QK_NOPE
        dkr = dwl[:, kr0:kr0 + QK_ROPE] + _swap_halves(dwl[:, kr0 + QK_ROPE:])
        gw["w_in"][l] = jnp.concatenate([dwl[:, :o_kr], dkr, dwg, dwt], axis=1)
        dwq = _matmul_tn(qn, dqr, "grad_w_q").reshape(ql, h, HEAD_PAD)
        dwq = jnp.concatenate([dwq[..., :QK_NOPE], dwq[..., QK_NOPE:qk] + _swap_halves(dwq[..., qk:])], axis=-1)
        gw["w_uq"][l] = dwq.reshape(ql, h * qk)
        dwk = _matmul_tn(kvn, dk, "grad_w_k").reshape(kl, h, HEAD_PAD)
        dwv = _matmul_tn(kvn, dv, "grad_w_v").reshape(kl, h, HEAD_PAD)
        gw["w_ukv"][l] = jnp.concatenate([dwk[..., :QK_NOPE], dwv[..., :V_HEAD]], axis=-1).reshape(kl, h * (QK_NOPE + V_HEAD))
        dmod[l] = jnp.concatenate([dmod1[:, :, 0], dgt1[:, :, 0], dmod2[:, :, 0]], axis=1).reshape(nb, 6 * d)
        bsum = lambda t: jnp.sum(t, axis=0).reshape(1, -1)
        cs = jnp.sum(csmall, axis=0)[:, 0]
        small_rows[l] = jnp.concatenate([bsum(dgm), bsum(dgq), bsum(dgkv), cs[2:3], cs[0:1], cs[1:2], bsum(dgffn)], axis=1)
        dw_taps[l] = jnp.sum(ddw, axis=0)
    grad_x = dx

    dmod_rows = _pad_rows(jnp.stack(dmod).reshape(nl * nb, 6 * d))
    (dmod_all,) = _all_gather([dmod_rows], vmem=True, name="dmod_all_gather")
    dmod_full = jnp.transpose(dmod_all[:, :nl * nb].reshape(N_DEV, nl, nb, 6 * d), (1, 0, 2, 3)).reshape(nl, N_DEV * nb, 6 * d)
    dmod_cols = lax.dynamic_slice_in_dim(dmod_full, me * n_ada, n_ada, axis=2)
    grad_w_ada, grad_b_ada = _ada_bwd(c_full, dmod_cols, dmod_full)
    grads = {"w_ada": grad_w_ada, "b_ada": grad_b_ada.reshape(nl, 6 * d)}

    widths = (d, ql, kl, cc, cc, cc, d)
    wsum = sum(widths)
    final_row = _pad_last(jnp.sum(dgf_part, axis=0).reshape(1, d), wsum)
    small2d = _pad_rows(jnp.concatenate(small_rows + [final_row], axis=0))
    taps2d = jnp.concatenate(dw_taps, axis=0)
    small_all, taps_all = _all_gather([small2d, taps2d], vmem=True, name="small_grad_all_gather")
    small_sum, taps_sum = _sum_devices(small_all), _sum_devices(taps_all)
    off = 0
    for n, wdt in zip(("g_mix", "g_q", "g_kv", "b_dw", "g_cn", "b_cn", "g_ffn"), widths):
        grads[n] = small_sum[:nl, off:off + wdt]
        off += wdt
    grads["g_final"] = small_sum[nl, :d]
    taps = taps_sum.reshape(nl, HALO, cc)[:, :CONV_W]
    grads["w_dw"] = lax.dynamic_slice_in_dim(taps, me * n_dw, n_dw, axis=2)

    delta, new_m, new_v = {}, {}, {}
    for n in order:
        if n in grads:
            delta[n], new_m[n], new_v[n] = _adamw(weights[n], grads[n], mom_m[n], mom_v[n], "adamw_" + n)

    gs = [_chunks_by_core(jnp.stack(gw[n]), ax) for n, ax in GATHERED]
    from_sibling = _sibling_exchange(gs)
    parts = [_add_halves(_rows_2d(g, N_DEV), _rows_2d(r, 4), core, "grad_chip_partial_" + n).reshape((4,) + g.shape[1:])
             for (n, _), g, r in zip(GATHERED, gs, from_sibling)]
    from_chips = _chip_exchange(parts)
    for (n, _), part, rest in zip(GATHERED, parts, from_chips):
        grads[n], delta[n], new_m[n], new_v[n] = _adamw_reduce(weights[n], mom_m[n], mom_v[n], part, rest, chip,
                                                                 "adamw_" + n)

    return (loss, grad_x, *[grads[n] for n in order], *[delta[n] for n in order], *[new_m[n] for n in order],
            *[new_v[n] for n in order])
```

```python
import functools

import jax
import jax.numpy as jnp
from jax import lax
from jax.experimental import pallas as pl
from jax.experimental.pallas import tpu as pltpu

F32, BF16 = jnp.float32, jnp.bfloat16
MESH = pl.DeviceIdType.MESH
AXES = ("x", "y", "c")
N_DEV = 8

N_HEADS = 8
QK_NOPE = 64
QK_ROPE = 32
V_HEAD = 64
HEAD_PAD = 128
CONV_W = 31
HALO = 32
EPS = 1e-6
ROPE_THETA = 10000.0
NEG_INF = -1e30
ATTN_SCALE = (QK_NOPE + QK_ROPE) ** -0.5

ADAM_LR, ADAM_B1, ADAM_B2, ADAM_EPS, ADAM_WD, ADAM_STEP = 0.001, 0.9, 0.999, 1e-08, 0.01, 10

LANES = 128
VMEM_LIMIT = 60 * 1024 * 1024


def _params(sem=None):
    return pltpu.CompilerParams(dimension_semantics=sem, vmem_limit_bytes=VMEM_LIMIT)


def _pick(n, cap):
    if n <= cap:
        return n
    best = None
    for d in range(LANES, cap + 1, LANES):
        if n % d == 0:
            best = d
    assert best is not None, (n, cap)
    return best


def _div_tile(n, cap, mult):
    best = None
    for d in range(mult, min(n, cap) + 1, mult):
        if n % d == 0:
            best = d
    assert best is not None, (n, cap, mult)
    return best


def _row_tile(s, cap=512):
    return cap if s % cap == 0 and s >= 2 * cap else s // 2


def _sig(v):
    return 1.0 / (1.0 + jnp.exp(-v))


def _rstd(v):
    return lax.rsqrt(jnp.mean(v * v, axis=-1, keepdims=True) + EPS)


def _dot(a, b):
    return jnp.dot(a, b, preferred_element_type=F32)


def _dot_nt(a, b):
    return lax.dot_general(a, b, (((1,), (1,)), ((), ())), preferred_element_type=F32)


def _dot_tn(a, b):
    return lax.dot_general(a, b, (((0,), (0,)), ((), ())), preferred_element_type=F32)


def _rope(v, cos_t, sin_t):
    return v * cos_t + pltpu.roll(v, HEAD_PAD - QK_ROPE, 1) * sin_t


def _rope_bwd(dv, cos_t, sin_t):
    return dv * cos_t + pltpu.roll(dv * sin_t, QK_ROPE, 1)


def _const(shape):
    n = len(shape)
    return pl.BlockSpec(shape, lambda *_: (0,) * n, pipeline_mode=pl.Buffered(1))


def _rows(tm, w):
    return pl.BlockSpec((1, tm, w), lambda b, s: (b, s, 0))


def _per_b(r, w):
    return pl.BlockSpec((1, r, 1, w), lambda b, s: (b, 0, 0, 0))


def _all_gather(arrs, vmem, name):
    n = len(arrs)
    space = pltpu.VMEM if vmem else pl.ANY

    def body(*refs):
        x_refs, out_refs = refs[:n], refs[n:2 * n]
        send_sems, recv_sems, local_sems = refs[2 * n:]
        x_, y_, c_ = lax.axis_index("x"), lax.axis_index("y"), lax.axis_index("c")
        me, sibling = (x_, y_, c_), (x_, y_, 1 - c_)
        chips = [(1 - x_, y_), (x_, 1 - y_), (1 - x_, 1 - y_)]

        def copy(a, k, block, to, own=False):
            px, py, pc = block
            slot = out_refs[a].at[4 * px + 2 * py + pc]
            return pltpu.make_async_remote_copy(
                src_ref=x_refs[a] if own else slot, dst_ref=slot, send_sem=send_sems.at[k * n + a],
                recv_sem=recv_sems.at[k * n + a], device_id=to, device_id_type=MESH)

        mine = [pltpu.make_async_copy(x_refs[a], out_refs[a].at[4 * x_ + 2 * y_ + c_], local_sems.at[a]) for a in range(n)]
        sent = []
        for a in range(n):
            mine[a].start()
            sent.append(copy(a, 0, me, sibling, own=True))
            sent += [copy(a, 1 + j, me, (*chip, c_), own=True) for j, chip in enumerate(chips)]
        for cp in sent:
            cp.start()
        for j, chip in enumerate(chips):
            for a in range(n):
                copy(a, 1 + j, (*chip, c_), me).wait_recv()
                passed = copy(a, 4 + j, (*chip, c_), sibling)
                passed.start()
                sent.append(passed)
        for a in range(n):
            copy(a, 0, sibling, me).wait_recv()
            for j, chip in enumerate(chips):
                copy(a, 4 + j, (*chip, 1 - c_), me).wait_recv()
        for cp in sent:
            cp.wait_send()
        for cp in mine:
            cp.wait()

    return pl.pallas_call(
        body, name=name,
        out_shape=[jax.ShapeDtypeStruct((N_DEV,) + t.shape, t.dtype) for t in arrs],
        in_specs=[pl.BlockSpec(memory_space=space)] * n, out_specs=[pl.BlockSpec(memory_space=space)] * n,
        scratch_shapes=[pltpu.SemaphoreType.DMA((7 * n,)), pltpu.SemaphoreType.DMA((7 * n,)), pltpu.SemaphoreType.DMA((n,))],
        compiler_params=pltpu.CompilerParams(vmem_limit_bytes=VMEM_LIMIT),
    )(*arrs)


def _sibling_exchange(gs):
    n = len(gs)

    def body(*refs):
        g_refs, out_refs = refs[:n], refs[n:2 * n]
        send_sems, recv_sems = refs[2 * n:]
        x_, y_, c_ = lax.axis_index("x"), lax.axis_index("y"), lax.axis_index("c")
        cps = [pltpu.make_async_remote_copy(
            src_ref=g_refs[a].at[pl.ds(4 * (1 - c_), 4)], dst_ref=out_refs[a], send_sem=send_sems.at[a],
            recv_sem=recv_sems.at[a], device_id=(x_, y_, 1 - c_), device_id_type=MESH) for a in range(n)]
        for cp in cps:
            cp.start()
        for cp in cps:
            cp.wait()

    return pl.pallas_call(
        body, name="grad_sibling_exchange",
        out_shape=[jax.ShapeDtypeStruct((4,) + g.shape[1:], g.dtype) for g in gs],
        in_specs=[pl.BlockSpec(memory_space=pl.ANY)] * n, out_specs=[pl.BlockSpec(memory_space=pl.ANY)] * n,
        scratch_shapes=[pltpu.SemaphoreType.DMA((n,)), pltpu.SemaphoreType.DMA((n,))],
    )(*gs)


def _chip_exchange(ps):
    n = len(ps)

    def body(*refs):
        p_refs, out_refs = refs[:n], refs[n:2 * n]
        send_sems, recv_sems = refs[2 * n:]
        x_, y_, c_ = lax.axis_index("x"), lax.axis_index("y"), lax.axis_index("c")
        chips = [(1 - x_, y_), (x_, 1 - y_), (1 - x_, 1 - y_)]
        cps = [pltpu.make_async_remote_copy(
            src_ref=p_refs[a].at[2 * px + py], dst_ref=out_refs[a].at[k], send_sem=send_sems.at[k * n + a],
            recv_sem=recv_sems.at[k * n + a], device_id=(px, py, c_), device_id_type=MESH)
            for k, (px, py) in enumerate(chips) for a in range(n)]
        for cp in cps:
            cp.start()
        for cp in cps:
            cp.wait()

    return pl.pallas_call(
        body, name="grad_chip_exchange",
        out_shape=[jax.ShapeDtypeStruct((3,) + p.shape[1:], p.dtype) for p in ps],
        in_specs=[pl.BlockSpec(memory_space=pl.ANY)] * n, out_specs=[pl.BlockSpec(memory_space=pl.ANY)] * n,
        scratch_shapes=[pltpu.SemaphoreType.DMA((3 * n,)), pltpu.SemaphoreType.DMA((3 * n,))],
    )(*ps)


def _rows_2d(t, lead):
    return t.reshape((lead, -1, t.shape[-1]) if lead else (-1, t.shape[-1]))


def _grad_row_tile(rows, cols):
    return _div_tile(rows, max(16, (1 << 19) // cols), 16)


def _add_halves(g, got, core, name):
    _, r, n = g.shape
    tr = _grad_row_tile(r, n)

    def body(core_ref, g_ref, r_ref, o_ref):
        o_ref[...] = (g_ref[...].astype(F32) + r_ref[...].astype(F32)).astype(BF16)

    spec = pl.BlockSpec((1, tr, n), lambda i, j, core_ref: (i, j, 0))
    return pl.pallas_call(
        body, name=name,
        grid_spec=pltpu.PrefetchScalarGridSpec(
            num_scalar_prefetch=1, grid=(4, r // tr),
            in_specs=[pl.BlockSpec((1, tr, n), lambda i, j, core_ref: (4 * core_ref[0] + i, j, 0)), spec], out_specs=spec),
        out_shape=jax.ShapeDtypeStruct((4, r, n), BF16), compiler_params=_params(("arbitrary", "arbitrary")),
    )(core, g, got)


def _sum_devices(g):
    _, m, n = g.shape

    def body(g_ref, o_ref):
        s = g_ref[0]
        for j in range(1, N_DEV):
            s = s + g_ref[j]
        o_ref[...] = s

    return pl.pallas_call(body, name="small_grad_sum", out_shape=jax.ShapeDtypeStruct((m, n), F32),
                          compiler_params=pltpu.CompilerParams(vmem_limit_bytes=VMEM_LIMIT))(g)


def _ada_fwd(c_full, w_ada, b_cols):
    nl, d, n = w_ada.shape
    nb = c_full.shape[0]

    def body(c_ref, w_ref, b_ref, o_ref):
        cv = c_ref[...]
        act = cv * _sig(cv)
        o_ref[0] = jnp.dot(act, w_ref[0], preferred_element_type=F32, precision=lax.Precision.HIGHEST) + b_ref[0]

    return pl.pallas_call(
        body, name="ada_fwd", grid=(nl,), out_shape=jax.ShapeDtypeStruct((nl, nb, n), F32),
        in_specs=[pl.BlockSpec((nb, d), lambda l: (0, 0)), pl.BlockSpec((1, d, n), lambda l: (l, 0, 0)),
                  pl.BlockSpec((1, 1, n), lambda l: (l, 0, 0))],
        out_specs=pl.BlockSpec((1, nb, n), lambda l: (l, 0, 0)), compiler_params=_params(("arbitrary",)),
    )(c_full, w_ada, b_cols)


def _ada_bwd(c_full, dmod_cols, dmod_full):
    nl, nb, n = dmod_cols.shape
    d = c_full.shape[1]
    nfull = dmod_full.shape[2]

    def body(c_ref, dc_ref, df_ref, gw_ref, gb_ref):
        cv = c_ref[...]
        act = cv * _sig(cv)
        gw_ref[0] = lax.dot_general(act, dc_ref[0], (((0,), (0,)), ((), ())), preferred_element_type=F32,
                                    precision=lax.Precision.HIGHEST)
        gb_ref[0] = jnp.sum(df_ref[0], axis=0, keepdims=True)

    return pl.pallas_call(
        body, name="ada_bwd", grid=(nl,),
        out_shape=(jax.ShapeDtypeStruct((nl, d, n), F32), jax.ShapeDtypeStruct((nl, 1, nfull), F32)),
        in_specs=[pl.BlockSpec((nb, d), lambda l: (0, 0)), pl.BlockSpec((1, nb, n), lambda l: (l, 0, 0)),
                  pl.BlockSpec((1, nb, nfull), lambda l: (l, 0, 0))],
        out_specs=(pl.BlockSpec((1, d, n), lambda l: (l, 0, 0)), pl.BlockSpec((1, 1, nfull), lambda l: (l, 0, 0))),
        compiler_params=_params(("arbitrary",)),
    )(c_full, dmod_cols, dmod_full)


def _rope_tables(pos, freq_row, sign_row):
    b, s, _ = pos.shape
    tm = _row_tile(s)

    def body(p_ref, f_ref, g_ref, c_ref, s_ref):
        ang = p_ref[0] * f_ref[...]
        lane = lax.broadcasted_iota(jnp.int32, ang.shape, 1)
        c_ref[0] = jnp.where(lane < QK_NOPE, 1.0, jnp.where(lane < QK_NOPE + QK_ROPE, jnp.cos(ang), 0.0))
        s_ref[0] = g_ref[...] * jnp.sin(ang)

    return pl.pallas_call(
        body, name="rope_tables", grid=(b, s // tm),
        out_shape=(jax.ShapeDtypeStruct((b, s, HEAD_PAD), F32),) * 2,
        in_specs=[_rows(tm, 1), pl.BlockSpec((1, HEAD_PAD), lambda i, j: (0, 0)),
                  pl.BlockSpec((1, HEAD_PAD), lambda i, j: (0, 0))],
        out_specs=(_rows(tm, HEAD_PAD),) * 2, compiler_params=_params(("arbitrary", "arbitrary")),
    )(pos, freq_row, sign_row)


def _mix_pre_fwd(x, mod, cos_t, sin_t, g_mix, g_q, g_kv, w_lat, w_glu, w_gate, w_q, w_k, w_v):
    b, s, d = x.shape
    ql, kl = g_q.shape[1], g_kv.shape[1]
    wl, wg, wt = w_lat.shape[1], w_glu.shape[1], w_gate.shape[1]
    hw = N_HEADS * HEAD_PAD
    tm = _row_tile(s)

    def body(x_ref, mod_ref, cos_ref, sin_ref, gm_ref, gq_ref, gkv_ref, wlat_ref, wglu_ref, wgate_ref, wq_ref, wk_ref,
             wv_ref, lat_ref, glu_ref, gate_ref, q_ref, k_ref, v_ref):
        xf = x_ref[0]
        sh, sc = mod_ref[0, 0], mod_ref[0, 1]
        hb = ((xf * _rstd(xf) * gm_ref[...]) * (1.0 + sc) + sh).astype(BF16)
        glu_ref[0] = _dot(hb, wglu_ref[...]).astype(BF16)
        gate_ref[0] = _dot(hb, wgate_ref[...]).astype(BF16)
        lat = _dot(hb, wlat_ref[...]).astype(BF16)
        lat_ref[0] = lat
        latf = lat.astype(F32)
        q_lat, kv_lat, kr_sec = latf[:, :ql], latf[:, ql:ql + kl], latf[:, ql + kl:]
        qn = (q_lat * _rstd(q_lat) * gq_ref[...]).astype(BF16)
        kvn = (kv_lat * _rstd(kv_lat) * gkv_ref[...]).astype(BF16)
        cos_v, sin_v = cos_ref[0], sin_ref[0]
        lane = lax.broadcasted_iota(jnp.int32, kr_sec.shape, 1)
        kr = jnp.where(lane >= QK_NOPE, _rope(kr_sec, cos_v, sin_v), 0.0)
        q_all, k_all, v_all = _dot(qn, wq_ref[...]), _dot(kvn, wk_ref[...]), _dot(kvn, wv_ref[...])
        vlane = lax.broadcasted_iota(jnp.int32, v_all.shape, 1)
        v_ref[0] = jnp.where(vlane % HEAD_PAD == V_HEAD, 1.0, v_all).astype(BF16)
        for h in range(N_HEADS):
            cols = slice(h * HEAD_PAD, (h + 1) * HEAD_PAD)
            q_ref[0, :, cols] = (_rope(q_all[:, cols], cos_v, sin_v) * ATTN_SCALE).astype(BF16)
            k_ref[0, :, cols] = (k_all[:, cols] + kr).astype(BF16)

    hshape = jax.ShapeDtypeStruct((b, s, hw), BF16)
    return pl.pallas_call(
        body, name="mix_pre_fwd", grid=(b, s // tm),
        out_shape=(jax.ShapeDtypeStruct((b, s, wl), BF16), jax.ShapeDtypeStruct((b, s, wg), BF16),
                   jax.ShapeDtypeStruct((b, s, wt), BF16), hshape, hshape, hshape),
        in_specs=[_rows(tm, d), _per_b(6, d), _rows(tm, HEAD_PAD), _rows(tm, HEAD_PAD), _const((1, d)), _const((1, ql)),
                  _const((1, kl)), _const(w_lat.shape), _const(w_glu.shape), _const(w_gate.shape), _const(w_q.shape),
                  _const(w_k.shape), _const(w_v.shape)],
        out_specs=(_rows(tm, wl), _rows(tm, wg), _rows(tm, wt), _rows(tm, hw), _rows(tm, hw), _rows(tm, hw)),
        compiler_params=_params(("arbitrary", "arbitrary")),
    )(x, mod, cos_t, sin_t, g_mix, g_q, g_kv, w_lat, w_glu, w_gate, w_q, w_k, w_v)


def _causal_mask(tq, tk):
    return lax.broadcasted_iota(jnp.int32, (tq, tk), 0) >= lax.broadcasted_iota(jnp.int32, (tq, tk), 1)


def _flash_fwd(q, k, v):
    b, s, hw = q.shape
    nh, hp = hw // HEAD_PAD, HEAD_PAD
    t = _row_tile(s)

    def body(q_ref, k_ref, v_ref, o_ref, lse_ref):
        i = pl.program_id(2)
        qv = q_ref[0]

        def step(j, carry, masked):
            m, acc = carry
            rows = pl.ds(pl.multiple_of(j * t, t), t)
            sc = _dot_nt(qv, k_ref[0, rows, :])
            if masked:
                sc = jnp.where(_causal_mask(t, t), sc, NEG_INF)
            m_new = jnp.maximum(m, jnp.max(sc, axis=-1, keepdims=True))
            p = jnp.exp((sc - m_new).astype(BF16))
            acc = jnp.exp(m - m_new) * acc + _dot(p, v_ref[0, rows, :])
            return m_new, acc

        init = (jnp.full((t, 1), NEG_INF, F32), jnp.zeros((t, hp), F32))
        carry = lax.fori_loop(0, i, lambda j, cr: step(j, cr, False), init)
        m, acc = step(i, carry, True)
        lane = lax.broadcasted_iota(jnp.int32, acc.shape, 1)
        l = jnp.sum(jnp.where(lane == V_HEAD, acc, 0.0), axis=-1, keepdims=True)
        o_ref[0] = (acc / l).astype(BF16)
        lse_ref[0, 0] = m + jnp.log(l)

    tile = pl.BlockSpec((1, t, hp), lambda bb, hh, ii: (bb, ii, hh))
    full = pl.BlockSpec((1, s, hp), lambda bb, hh, ii: (bb, 0, hh))
    return pl.pallas_call(
        body, name="flash_fwd", grid=(b, nh, s // t),
        out_shape=(jax.ShapeDtypeStruct((b, s, hw), BF16), jax.ShapeDtypeStruct((b, nh, s, 1), F32)),
        in_specs=[tile, full, full],
        out_specs=(tile, pl.BlockSpec((1, 1, t, 1), lambda bb, hh, ii: (bb, hh, ii, 0))),
        compiler_params=_params(("arbitrary", "arbitrary", "arbitrary")),
    )(q, k, v)


def _halo_prev(tm, w):
    r = tm // HALO
    return pl.BlockSpec((1, HALO, w), lambda b, s: (b, jnp.maximum(s * r - 1, 0), 0))


def _halo_next(tm, w, n_tiles):
    r = tm // HALO
    return pl.BlockSpec((1, HALO, w), lambda b, s: (b, jnp.minimum((s + 1) * r, n_tiles * r - 1), 0))


def _glu(v, cc):
    a, g = v[:, :cc].astype(F32), v[:, cc:].astype(F32)
    return a * _sig(g)


def _conv_fwd(glu, w_dw, b_dw, g_cn, b_cn, w_pw2):
    b, s, w2 = glu.shape
    cc = w2 // 2
    d = w_pw2.shape[1]
    tm = _row_tile(s)

    def body(cur_ref, prev_ref, w_ref, bdw_ref, g_ref, bcn_ref, wp_ref, u_ref, y_ref, ext):
        first = pl.program_id(1) == 0
        ext[pl.ds(0, HALO), :] = jnp.where(first, 0.0, _glu(prev_ref[0], cc))
        ext[pl.ds(HALO, tm), :] = _glu(cur_ref[0], cc)
        u = jnp.zeros((tm, cc), F32) + bdw_ref[...]
        for kk in range(CONV_W):
            u = u + w_ref[pl.ds(kk, 1), :] * ext[pl.ds(HALO - CONV_W + 1 + kk, tm), :]
        ub = u.astype(BF16)
        u_ref[0] = ub
        uf = ub.astype(F32)
        mu = jnp.mean(uf, axis=-1, keepdims=True)
        uc = uf - mu
        ln = uc * lax.rsqrt(jnp.mean(uc * uc, axis=-1, keepdims=True) + EPS) * g_ref[...] + bcn_ref[...]
        y_ref[0] = _dot((ln * _sig(ln)).astype(BF16), wp_ref[...]).astype(BF16)

    return pl.pallas_call(
        body, name="conv_fwd", grid=(b, s // tm),
        out_shape=(jax.ShapeDtypeStruct((b, s, cc), BF16), jax.ShapeDtypeStruct((b, s, d), BF16)),
        in_specs=[_rows(tm, w2), _halo_prev(tm, w2), _const(w_dw.shape), _const((1, cc)), _const((1, cc)), _const((1, cc)),
                  _const(w_pw2.shape)],
        out_specs=(_rows(tm, cc), _rows(tm, d)),
        scratch_shapes=[pltpu.VMEM((tm + HALO, cc), F32)],
        compiler_params=_params(("arbitrary", "arbitrary")),
    )(glu, glu, w_dw, b_dw, g_cn, b_cn, w_pw2)


def _mix_post_fwd(x, mod, o, y_conv, gate, w_o, w_out):
    b, s, d = x.shape
    hw = o.shape[2]
    tm = _row_tile(s)

    def body(x_ref, mod_ref, o_ref, yc_ref, gate_ref, wo_ref, wout_ref, x2_ref, ya_ref, o1_ref):
        yab = _dot(o_ref[0], wo_ref[...]).astype(BF16)
        ya_ref[0] = yab
        gv = gate_ref[0]
        y = _sig(gv[:, :d].astype(F32)) * yab.astype(F32) + _sig(gv[:, d:].astype(F32)) * yc_ref[0].astype(F32)
        o1 = _dot(y.astype(BF16), wout_ref[...])
        o1_ref[0] = o1.astype(BF16)
        x2_ref[0] = x_ref[0] + mod_ref[0, 2] * o1

    return pl.pallas_call(
        body, name="mix_post_fwd", grid=(b, s // tm),
        out_shape=(jax.ShapeDtypeStruct((b, s, d), F32), jax.ShapeDtypeStruct((b, s, d), BF16),
                   jax.ShapeDtypeStruct((b, s, d), BF16)),
        in_specs=[_rows(tm, d), _per_b(6, d), _rows(tm, hw), _rows(tm, d), _rows(tm, 2 * d), _const(w_o.shape),
                  _const(w_out.shape)],
        out_specs=(_rows(tm, d), _rows(tm, d), _rows(tm, d)),
        compiler_params=_params(("arbitrary", "arbitrary")),
    )(x, mod, o, y_conv, gate, w_o, w_out)


def _ffn_fwd(x2, mod, g_ffn, w_gu, w_down):
    b, s, d = x2.shape
    f = w_down.shape[0]
    fc = _pick(f, 512)
    tm = _row_tile(s)

    def body(x_ref, mod_ref, g_ref, wgu_ref, wdn_ref, x3_ref, gu_ref, o2_ref):
        xf = x_ref[0]
        hb = ((xf * _rstd(xf) * g_ref[...]) * (1.0 + mod_ref[0, 4]) + mod_ref[0, 3]).astype(BF16)
        o2 = jnp.zeros((tm, d), F32)
        for c0 in range(0, f, fc):
            gb = _dot(hb, wgu_ref[:, c0:c0 + fc]).astype(BF16)
            ub = _dot(hb, wgu_ref[:, f + c0:f + c0 + fc]).astype(BF16)
            gu_ref[0, :, c0:c0 + fc] = gb
            gu_ref[0, :, f + c0:f + c0 + fc] = ub
            gf = gb.astype(F32)
            act = (gf * _sig(gf) * ub.astype(F32)).astype(BF16)
            o2 = o2 + _dot(act, wdn_ref[c0:c0 + fc, :])
        o2_ref[0] = o2.astype(BF16)
        x3_ref[0] = xf + mod_ref[0, 5] * o2

    return pl.pallas_call(
        body, name="ffn_fwd", grid=(b, s // tm),
        out_shape=(jax.ShapeDtypeStruct((b, s, d), F32), jax.ShapeDtypeStruct((b, s, 2 * f), BF16),
                   jax.ShapeDtypeStruct((b, s, d), BF16)),
        in_specs=[_rows(tm, d), _per_b(6, d), _const((1, d)), _const(w_gu.shape), _const(w_down.shape)],
        out_specs=(_rows(tm, d), _rows(tm, 2 * f), _rows(tm, d)),
        compiler_params=_params(("arbitrary", "arbitrary")),
    )(x2, mod, g_ffn, w_gu, w_down)


def _zero_at_first_tile(*refs):
    @pl.when(pl.program_id(1) == 0)
    def _():
        for ref in refs:
            ref[...] = jnp.zeros_like(ref)


def _accumulate(ref, idx, val):
    ref[idx] = ref[idx] + val


def _colsum(v):
    return jnp.sum(v, axis=0, keepdims=True)


def _loss_bwd(x, target, g_final):
    b, s, d = x.shape
    tm = _row_tile(s)

    def body(x_ref, t_ref, g_ref, dx_ref, loss_ref, dg_ref):
        _zero_at_first_tile(loss_ref, dg_ref)
        xf = x_ref[0]
        r = _rstd(xf)
        xh = xf * r
        diff = xh * g_ref[...] - t_ref[0]
        _accumulate(loss_ref, (0, 0), _colsum(diff * diff))
        dy = diff * (1.0 / d)
        _accumulate(dg_ref, (0, 0), _colsum(dy * xh))
        dyg = dy * g_ref[...]
        dx_ref[0] = r * (dyg - xh * jnp.mean(dyg * xh, axis=-1, keepdims=True))

    return pl.pallas_call(
        body, name="loss_bwd", grid=(b, s // tm),
        out_shape=(jax.ShapeDtypeStruct((b, s, d), F32), jax.ShapeDtypeStruct((b, 1, 1, d), F32),
                   jax.ShapeDtypeStruct((b, 1, 1, d), F32)),
        in_specs=[_rows(tm, d), _rows(tm, d), _const((1, d))],
        out_specs=(_rows(tm, d), _per_b(1, d), _per_b(1, d)),
        compiler_params=_params(("arbitrary", "arbitrary")),
    )(x, target, g_final)


def _ffn_bwd(dx3, x2, mod, g_ffn, gu, o2, w_gu_t, w_down_t):
    b, s, d = x2.shape
    f = w_down_t.shape[1]
    fc = _pick(f, 512)
    tm = _row_tile(s, 256)

    def body(dx3_ref, x_ref, mod_ref, g_ref, gu_ref, o2_ref, wgut_ref, wdnt_ref, dx2_ref, do2_ref, act_ref, dgu_ref, h_ref,
             dmod_ref, dg_ref):
        _zero_at_first_tile(dmod_ref, dg_ref)
        dx3 = dx3_ref[0]
        sh, sc, gt = mod_ref[0, 3], mod_ref[0, 4], mod_ref[0, 5]
        do2 = (dx3 * gt).astype(BF16)
        do2_ref[0] = do2
        _accumulate(dmod_ref, (0, 2), _colsum(dx3 * o2_ref[0].astype(F32)))
        dh = jnp.zeros((tm, d), F32)
        for c0 in range(0, f, fc):
            gf = gu_ref[0, :, c0:c0 + fc].astype(F32)
            uf = gu_ref[0, :, f + c0:f + c0 + fc].astype(F32)
            sg = _sig(gf)
            silu = gf * sg
            act_ref[0, :, c0:c0 + fc] = (silu * uf).astype(BF16)
            dact = _dot(do2, wdnt_ref[:, c0:c0 + fc])
            dg = (dact * uf * (sg * (1.0 + gf * (1.0 - sg)))).astype(BF16)
            du = (dact * silu).astype(BF16)
            dgu_ref[0, :, c0:c0 + fc] = dg
            dgu_ref[0, :, f + c0:f + c0 + fc] = du
            dh = dh + _dot(dg, wgut_ref[c0:c0 + fc, :]) + _dot(du, wgut_ref[f + c0:f + c0 + fc, :])
        xf = x_ref[0]
        r = _rstd(xf)
        xh = xf * r
        n = xh * g_ref[...]
        h_ref[0] = (n * (1.0 + sc) + sh).astype(BF16)
        _accumulate(dmod_ref, (0, 0), _colsum(dh))
        _accumulate(dmod_ref, (0, 1), _colsum(dh * n))
        dn = dh * (1.0 + sc)
        _accumulate(dg_ref, (0, 0), _colsum(dn * xh))
        dyg = dn * g_ref[...]
        dx2_ref[0] = dx3 + r * (dyg - xh * jnp.mean(dyg * xh, axis=-1, keepdims=True))

    return pl.pallas_call(
        body, name="ffn_bwd", grid=(b, s // tm),
        out_shape=(jax.ShapeDtypeStruct((b, s, d), F32), jax.ShapeDtypeStruct((b, s, d), BF16),
                   jax.ShapeDtypeStruct((b, s, f), BF16), jax.ShapeDtypeStruct((b, s, 2 * f), BF16),
                   jax.ShapeDtypeStruct((b, s, d), BF16), jax.ShapeDtypeStruct((b, 3, 1, d), F32),
                   jax.ShapeDtypeStruct((b, 1, 1, d), F32)),
        in_specs=[_rows(tm, d), _rows(tm, d), _per_b(6, d), _const((1, d)), _rows(tm, 2 * f), _rows(tm, d),
                  _const(w_gu_t.shape), _const(w_down_t.shape)],
        out_specs=(_rows(tm, d), _rows(tm, d), _rows(tm, f), _rows(tm, 2 * f), _rows(tm, d), _per_b(3, d), _per_b(1, d)),
        compiler_params=_params(("arbitrary", "arbitrary")),
    )(dx3, x2, mod, g_ffn, gu, o2, w_gu_t, w_down_t)


def _mix_post_bwd(dx2, mod, o1, y_attn, y_conv, gate, w_out_t, w_o_t):
    b, s, d = dx2.shape
    hw = w_o_t.shape[1]
    tm = _row_tile(s)

    def body(dx_ref, mod_ref, o1_ref, ya_ref, yc_ref, gate_ref, woutt_ref, wot_ref, do1_ref, dya_ref, dyc_ref, dgate_ref,
             y_ref, do_ref, dgt_ref):
        _zero_at_first_tile(dgt_ref)
        dx = dx_ref[0]
        do1 = (dx * mod_ref[0, 2]).astype(BF16)
        do1_ref[0] = do1
        _accumulate(dgt_ref, (0, 0), _colsum(dx * o1_ref[0].astype(F32)))
        dy = _dot(do1, woutt_ref[...])
        gv = gate_ref[0]
        sa, sb = _sig(gv[:, :d].astype(F32)), _sig(gv[:, d:].astype(F32))
        ya, yc = ya_ref[0].astype(F32), yc_ref[0].astype(F32)
        y_ref[0] = (sa * ya + sb * yc).astype(BF16)
        dya = (dy * sa).astype(BF16)
        dya_ref[0] = dya
        dyc_ref[0] = (dy * sb).astype(BF16)
        dgate_ref[0, :, :d] = (dy * ya * sa * (1.0 - sa)).astype(BF16)
        dgate_ref[0, :, d:] = (dy * yc * sb * (1.0 - sb)).astype(BF16)
        do_ref[0] = _dot(dya, wot_ref[...]).astype(BF16)

    row = jax.ShapeDtypeStruct((b, s, d), BF16)
    return pl.pallas_call(
        body, name="mix_post_bwd", grid=(b, s // tm),
        out_shape=(row, row, row, jax.ShapeDtypeStruct((b, s, 2 * d), BF16), row,
                   jax.ShapeDtypeStruct((b, s, hw), BF16), jax.ShapeDtypeStruct((b, 1, 1, d), F32)),
        in_specs=[_rows(tm, d), _per_b(6, d), _rows(tm, d), _rows(tm, d), _rows(tm, d), _rows(tm, 2 * d),
                  _const(w_out_t.shape), _const(w_o_t.shape)],
        out_specs=(_rows(tm, d), _rows(tm, d), _rows(tm, d), _rows(tm, 2 * d), _rows(tm, d), _rows(tm, hw), _per_b(1, d)),
        compiler_params=_params(("arbitrary", "arbitrary")),
    )(dx2, mod, o1, y_attn, y_conv, gate, w_out_t, w_o_t)


def _conv_bwd(dyc, u, glu, w_dw, g_cn, b_cn, w_pw2_t):
    b, s, cc = u.shape
    d = dyc.shape[2]
    tm = _row_tile(s)
    nt = s // tm
    te = tm + HALO

    def body(dyc_ref, dycn_ref, u_ref, un_ref, glu_ref, glup_ref, w_ref, g_ref, bcn_ref, wpt_ref, dglu_ref, s_ref, dw_ref,
             small_ref, du_ext, uin_ext):
        _zero_at_first_tile(dw_ref, small_ref)
        st = pl.program_id(1)
        dy_all = jnp.concatenate([dyc_ref[0], dycn_ref[0]], axis=0)
        u_all = jnp.concatenate([u_ref[0], un_ref[0]], axis=0).astype(F32)
        ds = _dot(dy_all, wpt_ref[...])
        mu = jnp.mean(u_all, axis=-1, keepdims=True)
        uc = u_all - mu
        rstd = lax.rsqrt(jnp.mean(uc * uc, axis=-1, keepdims=True) + EPS)
        uh = uc * rstd
        ln = uh * g_ref[...] + bcn_ref[...]
        sg = _sig(ln)
        s_ref[0] = (ln * sg)[:tm].astype(BF16)
        dln = ds * (sg * (1.0 + ln * (1.0 - sg)))
        duh = dln * g_ref[...]
        du = rstd * (duh - jnp.mean(duh, axis=-1, keepdims=True) - uh * jnp.mean(duh * uh, axis=-1, keepdims=True))
        row = lax.broadcasted_iota(jnp.int32, (te, 1), 0)
        du = jnp.where(jnp.logical_and(st == nt - 1, row >= tm), 0.0, du)
        du_ext[...] = du
        du_cur = du[:tm]
        _accumulate(small_ref, (0, 0), _colsum((dln * uh)[:tm]))
        _accumulate(small_ref, (0, 1), _colsum(dln[:tm]))
        _accumulate(small_ref, (0, 2), _colsum(du_cur))
        uin_ext[pl.ds(0, HALO), :] = jnp.where(st == 0, 0.0, _glu(glup_ref[0], cc))
        gv = glu_ref[0]
        ga, gb = gv[:, :cc].astype(F32), gv[:, cc:].astype(F32)
        sgb = _sig(gb)
        uin_ext[pl.ds(HALO, tm), :] = ga * sgb
        duin = jnp.zeros((tm, cc), F32)
        for kk in range(CONV_W):
            duin = duin + w_ref[pl.ds(kk, 1), :] * du_ext[pl.ds(CONV_W - 1 - kk, tm), :]
            part = _colsum(du_cur * uin_ext[pl.ds(HALO - CONV_W + 1 + kk, tm), :])
            _accumulate(dw_ref, (0, pl.ds(kk, 1)), part)
        dglu_ref[0, :, :cc] = (duin * sgb).astype(BF16)
        dglu_ref[0, :, cc:] = (duin * ga * sgb * (1.0 - sgb)).astype(BF16)

    return pl.pallas_call(
        body, name="conv_bwd", grid=(b, nt),
        out_shape=(jax.ShapeDtypeStruct((b, s, 2 * cc), BF16), jax.ShapeDtypeStruct((b, s, cc), BF16),
                   jax.ShapeDtypeStruct((b, HALO, cc), F32), jax.ShapeDtypeStruct((b, 3, 1, cc), F32)),
        in_specs=[_rows(tm, d), _halo_next(tm, d, nt), _rows(tm, cc), _halo_next(tm, cc, nt), _rows(tm, 2 * cc),
                  _halo_prev(tm, 2 * cc), _const(w_dw.shape), _const((1, cc)), _const((1, cc)), _const(w_pw2_t.shape)],
        out_specs=(_rows(tm, 2 * cc), _rows(tm, cc), pl.BlockSpec((1, HALO, cc), lambda i, j: (i, 0, 0)), _per_b(3, cc)),
        scratch_shapes=[pltpu.VMEM((te, cc), F32), pltpu.VMEM((te, cc), F32)],
        compiler_params=_params(("arbitrary", "arbitrary")),
    )(dyc, dyc, u, u, glu, glu, w_dw, g_cn, b_cn, w_pw2_t)


def _flash_bwd(q, k, v, o, lse, do):
    b, s, hw = q.shape
    nh, hp = hw // HEAD_PAD, HEAD_PAD
    t = _row_tile(s)
    nt = s // t

    def to_row(sel, cols):
        return lax.dot_general(sel, cols, (((1,), (1,)), ((), ())), preferred_element_type=F32,
                               precision=lax.Precision.HIGHEST)

    def body(q_ref, k_ref, v_ref, o_ref, lse_ref, do_ref, dqt_ref, dk_ref, dv_ref, lse_row, delta_row):
        j = pl.program_id(2)

        @pl.when(j == 0)
        def _():
            dqt_ref[...] = jnp.zeros_like(dqt_ref)
            first_lane = (lax.broadcasted_iota(jnp.int32, (8, hp), 1) == 0).astype(F32)
            for i in range(nt):
                rows = pl.ds(i * t, t)
                prod = do_ref[0, rows, :].astype(F32) * o_ref[0, rows, :].astype(F32)
                delta_row[i] = to_row(jnp.ones((8, hp), F32), prod)
                lse_row[i] = to_row(first_lane, jnp.broadcast_to(lse_ref[0, 0, rows, :], (t, hp)))

        kv, vv = k_ref[0], v_ref[0]
        kt = kv.T
        query_not_before_key = (lax.broadcasted_iota(jnp.int32, (t, t), 1) >= lax.broadcasted_iota(jnp.int32, (t, t), 0))

        def step(i, carry, masked):
            dk, dv = carry
            rows = pl.ds(pl.multiple_of(i * t, t), t)
            qv, dov = q_ref[0, rows, :], do_ref[0, rows, :]
            pt = jnp.exp((_dot_nt(kv, qv) - lse_row[i, 0:1, :]).astype(BF16))
            if masked:
                pt = jnp.where(query_not_before_key, pt, jnp.zeros((), BF16))
            dv = dv + _dot(pt, dov)
            dst = pt * (_dot_nt(vv, dov) - delta_row[i, 0:1, :]).astype(BF16)
            dk = dk + _dot(dst, qv)
            dqt_ref[0, i] = dqt_ref[0, i] + _dot(kt, dst)
            return dk, dv

        carry = step(j, (jnp.zeros((t, hp), F32), jnp.zeros((t, hp), F32)), True)
        dk, dv = lax.fori_loop(j + 1, nt, lambda i, cr: step(i, cr, False), carry)
        dk_ref[0] = dk.astype(BF16)
        dv_ref[0] = dv.astype(BF16)

    tile = pl.BlockSpec((1, t, hp), lambda bb, hh, jj: (bb, jj, hh))
    full = pl.BlockSpec((1, s, hp), lambda bb, hh, jj: (bb, 0, hh))
    return pl.pallas_call(
        body, name="flash_bwd", grid=(b, nh, nt),
        out_shape=(jax.ShapeDtypeStruct((b, nt, hw, t), F32), jax.ShapeDtypeStruct((b, s, hw), BF16),
                   jax.ShapeDtypeStruct((b, s, hw), BF16)),
        in_specs=[full, tile, tile, full, pl.BlockSpec((1, 1, s, 1), lambda bb, hh, jj: (bb, hh, 0, 0)), full],
        out_specs=(pl.BlockSpec((1, nt, hp, t), lambda bb, hh, jj: (bb, 0, hh, 0)), tile, tile),
        scratch_shapes=[pltpu.VMEM((nt, 8, t), F32), pltpu.VMEM((nt, 8, t), F32)],
        compiler_params=_params(("arbitrary", "arbitrary", "arbitrary")),
    )(q, k, v, o, lse, do)


def _mix_pre_bwd(x, dx2, mod, cos_t, sin_t, g_mix, g_q, g_kv, lat, dq, dk, dv, dglu, dgate, w_lat_t, w_glu_t, w_gate_t,
                 w_q_t, w_k_t, w_v_t):
    b, s, d = x.shape
    ql, kl = g_q.shape[1], g_kv.shape[1]
    wl = lat.shape[2]
    hw = N_HEADS * HEAD_PAD
    tm = _row_tile(s)

    def body(x_ref, dx2_ref, mod_ref, cos_ref, sin_ref, gm_ref, gq_ref, gkv_ref, lat_ref, dq_ref, dk_ref, dv_ref, dglu_ref,
             dgate_ref, wlt_ref, wgt_ref, wtt_ref, wqt_ref, wkt_ref, wvt_ref, dx_ref, dlat_ref, dqr_ref, qn_ref, kvn_ref,
             h_ref, dmod_ref, dgm_ref, dgq_ref, dgkv_ref):
        _zero_at_first_tile(dmod_ref, dgm_ref, dgq_ref, dgkv_ref)
        cos_v, sin_v = cos_ref[0], sin_ref[0]
        latf = lat_ref[0].astype(F32)
        q_lat, kv_lat = latf[:, :ql], latf[:, ql:ql + kl]
        rq, rk = _rstd(q_lat), _rstd(kv_lat)
        qh, kh = q_lat * rq, kv_lat * rk
        qn_ref[0] = (qh * gq_ref[...]).astype(BF16)
        kvn_ref[0] = (kh * gkv_ref[...]).astype(BF16)
        dk_sum = jnp.zeros((tm, HEAD_PAD), F32)
        for h in range(N_HEADS):
            cols = slice(h * HEAD_PAD, (h + 1) * HEAD_PAD)
            dq_head = dq_ref[0, 0, cols, :].T
            dqr_ref[0, :, cols] = _rope_bwd(dq_head * ATTN_SCALE, cos_v, sin_v).astype(BF16)
            dk_sum = dk_sum + dk_ref[0, :, cols].astype(F32)
        dqn = _dot(dqr_ref[0], wqt_ref[...])
        dkvn = _dot(dk_ref[0], wkt_ref[...]) + _dot(dv_ref[0], wvt_ref[...])
        lane = lax.broadcasted_iota(jnp.int32, dk_sum.shape, 1)
        dkr = _rope_bwd(jnp.where(lane >= QK_NOPE, dk_sum, 0.0), cos_v, sin_v)
        _accumulate(dgq_ref, (0, 0), _colsum(dqn * qh))
        _accumulate(dgkv_ref, (0, 0), _colsum(dkvn * kh))
        dqg, dkg = dqn * gq_ref[...], dkvn * gkv_ref[...]
        dlat_ref[0, :, :ql] = (rq * (dqg - qh * jnp.mean(dqg * qh, axis=-1, keepdims=True))).astype(BF16)
        dlat_ref[0, :, ql:ql + kl] = (rk * (dkg - kh * jnp.mean(dkg * kh, axis=-1, keepdims=True))).astype(BF16)
        dlat_ref[0, :, ql + kl:] = dkr.astype(BF16)
        dh = _dot(dlat_ref[0], wlt_ref[...]) + _dot(dglu_ref[0], wgt_ref[...]) + _dot(dgate_ref[0], wtt_ref[...])
        sh, sc = mod_ref[0, 0], mod_ref[0, 1]
        xf = x_ref[0]
        r = _rstd(xf)
        xh = xf * r
        n = xh * gm_ref[...]
        h_ref[0] = (n * (1.0 + sc) + sh).astype(BF16)
        _accumulate(dmod_ref, (0, 0), _colsum(dh))
        _accumulate(dmod_ref, (0, 1), _colsum(dh * n))
        dn = dh * (1.0 + sc)
        _accumulate(dgm_ref, (0, 0), _colsum(dn * xh))
        dyg = dn * gm_ref[...]
        dx_ref[0] = dx2_ref[0] + r * (dyg - xh * jnp.mean(dyg * xh, axis=-1, keepdims=True))

    return pl.pallas_call(
        body, name="mix_pre_bwd", grid=(b, s // tm),
        out_shape=(jax.ShapeDtypeStruct((b, s, d), F32), jax.ShapeDtypeStruct((b, s, wl), BF16),
                   jax.ShapeDtypeStruct((b, s, hw), BF16), jax.ShapeDtypeStruct((b, s, ql), BF16),
                   jax.ShapeDtypeStruct((b, s, kl), BF16), jax.ShapeDtypeStruct((b, s, d), BF16),
                   jax.ShapeDtypeStruct((b, 2, 1, d), F32), jax.ShapeDtypeStruct((b, 1, 1, d), F32),
                   jax.ShapeDtypeStruct((b, 1, 1, ql), F32), jax.ShapeDtypeStruct((b, 1, 1, kl), F32)),
        in_specs=[_rows(tm, d), _rows(tm, d), _per_b(6, d), _rows(tm, HEAD_PAD), _rows(tm, HEAD_PAD), _const((1, d)),
                  _const((1, ql)), _const((1, kl)), _rows(tm, wl),
                  pl.BlockSpec((1, 1, hw, tm), lambda b, s: (b, s, 0, 0)), _rows(tm, hw), _rows(tm, hw),
                  _rows(tm, dglu.shape[2]), _rows(tm, 2 * d), _const(w_lat_t.shape), _const(w_glu_t.shape),
                  _const(w_gate_t.shape), _const(w_q_t.shape), _const(w_k_t.shape), _const(w_v_t.shape)],
        out_specs=(_rows(tm, d), _rows(tm, wl), _rows(tm, hw), _rows(tm, ql), _rows(tm, kl), _rows(tm, d), _per_b(2, d),
                   _per_b(1, d), _per_b(1, ql), _per_b(1, kl)),
        compiler_params=_params(("arbitrary", "arbitrary")),
    )(x, dx2, mod, cos_t, sin_t, g_mix, g_q, g_kv, lat, dq, dk, dv, dglu, dgate, w_lat_t, w_glu_t, w_gate_t, w_q_t, w_k_t,
      w_v_t)


def _matmul_tn(a, bm, name):
    b, s, kd = a.shape
    nd = bm.shape[2]
    tk, tn = _pick(kd, 1536), _pick(nd, 1536)
    ts = _row_tile(s, 2048)

    def body(a_ref, b_ref, o_ref):
        part = _dot_tn(a_ref[0], b_ref[0])
        first = jnp.logical_and(pl.program_id(2) == 0, pl.program_id(3) == 0)

        @pl.when(first)
        def _():
            o_ref[...] = part

        @pl.when(jnp.logical_not(first))
        def _():
            o_ref[...] = o_ref[...] + part

    return pl.pallas_call(
        body, name=name, grid=(kd // tk, nd // tn, b, s // ts),
        out_shape=jax.ShapeDtypeStruct((kd, nd), F32),
        in_specs=[pl.BlockSpec((1, ts, tk), lambda i, j, bb, ss: (bb, ss, i)),
                  pl.BlockSpec((1, ts, tn), lambda i, j, bb, ss: (bb, ss, j))],
        out_specs=pl.BlockSpec((tk, tn), lambda i, j, bb, ss: (i, j)),
        compiler_params=_params(("arbitrary",) * 4),
    )(a, bm)


def _adamw_update(w, g, m, v):
    nm = ADAM_B1 * m + (1.0 - ADAM_B1) * g
    nv = ADAM_B2 * v + (1.0 - ADAM_B2) * (g * g)
    delta = -ADAM_LR * ((nm / (1.0 - ADAM_B1 ** ADAM_STEP)) / (jnp.sqrt(nv / (1.0 - ADAM_B2 ** ADAM_STEP)) + ADAM_EPS)
                        + ADAM_WD * w)
    return delta, nm, nv


def _adamw(w, g, m, v, name):
    shape = w.shape
    cols = shape[-1]
    rows = w.size // cols
    w2, g2, m2, v2 = (t.reshape(rows, cols) for t in (w, g, m, v))
    tr = rows
    if rows * cols * 4 > (1 << 20):
        tr = _div_tile(rows, max(8, (1 << 18) // cols), 8)

    def body(w_ref, g_ref, m_ref, v_ref, d_ref, nm_ref, nv_ref):
        d_ref[...], nm_ref[...], nv_ref[...] = _adamw_update(w_ref[...], g_ref[...], m_ref[...], v_ref[...])

    spec = pl.BlockSpec((tr, cols), lambda i: (i, 0))
    outs = pl.pallas_call(
        body, name=name, grid=(rows // tr,), out_shape=(jax.ShapeDtypeStruct((rows, cols), F32),) * 3,
        in_specs=[spec] * 4, out_specs=(spec,) * 3, compiler_params=_params(("arbitrary",)),
    )(w2, g2, m2, v2)
    return tuple(t.reshape(shape) for t in outs)


def _adamw_reduce(w, m, v, part, got, chip, name):
    shape = w.shape
    cols = shape[-1]
    w2, m2, v2 = (_rows_2d(t, 0) for t in (w, m, v))
    p3, g3 = _rows_2d(part, 4), _rows_2d(got, 3)
    rows = w2.shape[0]
    tr = _grad_row_tile(rows, cols)

    def body(chip_ref, w_ref, m_ref, v_ref, p_ref, r_ref, g_ref, d_ref, nm_ref, nv_ref):
        g = ((p_ref[0].astype(F32) + r_ref[0].astype(F32)) + r_ref[1].astype(F32)) + r_ref[2].astype(F32)
        g_ref[...] = g
        d_ref[...], nm_ref[...], nv_ref[...] = _adamw_update(w_ref[...], g, m_ref[...], v_ref[...])

    spec = pl.BlockSpec((tr, cols), lambda i, chip_ref: (i, 0))
    outs = pl.pallas_call(
        body, name=name,
        grid_spec=pltpu.PrefetchScalarGridSpec(
            num_scalar_prefetch=1, grid=(rows // tr,),
            in_specs=[spec, spec, spec, pl.BlockSpec((1, tr, cols), lambda i, chip_ref: (chip_ref[0], i, 0)),
                      pl.BlockSpec((3, tr, cols), lambda i, chip_ref: (0, i, 0))],
            out_specs=(spec,) * 4),
        out_shape=(jax.ShapeDtypeStruct((rows, cols), F32),) * 4, compiler_params=_params(("arbitrary",)),
    )(chip, w2, m2, v2, p3, g3)
    return tuple(t.reshape(shape) for t in outs)


GATHERED = (("w_in", 2), ("w_uq", 2), ("w_ukv", 2), ("w_o_attn", 2), ("w_pw2", 2), ("w_out", 1), ("w_gu", 2), ("w_down", 1))


def _from_chunks(chunks, axis):
    _, nl, a, bb = chunks.shape
    if axis == 2:
        return jnp.transpose(chunks, (1, 2, 0, 3)).reshape(nl, a, N_DEV * bb)
    return jnp.transpose(chunks, (1, 0, 2, 3)).reshape(nl, N_DEV * a, bb)


def _chunks_by_core(full, axis):
    nl, a, bb = full.shape
    if axis == 2:
        t = jnp.transpose(full.reshape(nl, a, 2, 2, 2, bb // N_DEV), (4, 2, 3, 0, 1, 5))
    else:
        t = jnp.transpose(full.reshape(nl, 2, 2, 2, a // N_DEV, bb), (3, 1, 2, 0, 4, 5))
    return t.reshape((N_DEV,) + t.shape[3:]).astype(BF16)


def _swap_halves(t):
    half = QK_ROPE // 2
    return jnp.concatenate([t[..., half:], t[..., :half]], axis=-1)


def _t(w):
    return jnp.swapaxes(w, -1, -2)


def _pad_rows(t, mult=8):
    return jnp.pad(t, ((0, -t.shape[0] % mult), (0, 0)))


def _pad_last(t, width):
    return jnp.pad(t, ((0, 0),) * (t.ndim - 1) + ((0, width - t.shape[-1]),))


def kernel(x, c, positions, w_ada, b_ada, g_mix, w_in, g_q, w_uq, g_kv, w_ukv, w_o_attn, w_dw, b_dw, g_cn, b_cn, w_pw2, w_out, g_ffn, w_gu, w_down, g_final, loss_target, m_w_ada, m_b_ada, m_g_mix, m_w_in, m_g_q, m_w_uq, m_g_kv, m_w_ukv, m_w_o_attn, m_w_dw, m_b_dw, m_g_cn, m_b_cn, m_w_pw2, m_w_out, m_g_ffn, m_w_gu, m_w_down, m_g_final, v_w_ada, v_b_ada, v_g_mix, v_w_in, v_g_q, v_w_uq, v_g_kv, v_w_ukv, v_w_o_attn, v_w_dw, v_b_dw, v_g_cn, v_b_cn, v_w_pw2, v_w_out, v_g_ffn, v_w_gu, v_w_down, v_g_final):
    weights = dict(w_ada=w_ada, b_ada=b_ada, g_mix=g_mix, w_in=w_in, g_q=g_q, w_uq=w_uq, g_kv=g_kv, w_ukv=w_ukv,
                   w_o_attn=w_o_attn, w_dw=w_dw, b_dw=b_dw, g_cn=g_cn, b_cn=b_cn, w_pw2=w_pw2, w_out=w_out, g_ffn=g_ffn,
                   w_gu=w_gu, w_down=w_down, g_final=g_final)
    mom_m = dict(w_ada=m_w_ada, b_ada=m_b_ada, g_mix=m_g_mix, w_in=m_w_in, g_q=m_g_q, w_uq=m_w_uq, g_kv=m_g_kv,
                 w_ukv=m_w_ukv, w_o_attn=m_w_o_attn, w_dw=m_w_dw, b_dw=m_b_dw, g_cn=m_g_cn, b_cn=m_b_cn, w_pw2=m_w_pw2,
                 w_out=m_w_out, g_ffn=m_g_ffn, w_gu=m_w_gu, w_down=m_w_down, g_final=m_g_final)
    mom_v = dict(w_ada=v_w_ada, b_ada=v_b_ada, g_mix=v_g_mix, w_in=v_w_in, g_q=v_g_q, w_uq=v_w_uq, g_kv=v_g_kv,
                 w_ukv=v_w_ukv, w_o_attn=v_w_o_attn, w_dw=v_w_dw, b_dw=v_b_dw, g_cn=v_g_cn, b_cn=v_b_cn, w_pw2=v_w_pw2,
                 w_out=v_w_out, g_ffn=v_g_ffn, w_gu=v_w_gu, w_down=v_w_down, g_final=v_g_final)
    order = list(weights)

    nb, s, d = x.shape
    nl = w_in.shape[0]
    ql, kl, cc = g_q.shape[1], g_kv.shape[1], g_cn.shape[1]
    h = N_HEADS
    qk = QK_NOPE + QK_ROPE
    xi, yi, ci = lax.axis_index("x"), lax.axis_index("y"), lax.axis_index("c")
    me = 4 * xi + 2 * yi + ci
    core = jnp.reshape(ci, (1,)).astype(jnp.int32)
    chip = jnp.reshape(2 * xi + yi, (1,)).astype(jnp.int32)

    got = _all_gather([weights[n].astype(BF16) for n, _ in GATHERED], vmem=False, name="weight_all_gather")
    full = {n: _from_chunks(t, ax) for (n, ax), t in zip(GATHERED, got)}
    n_dw = w_dw.shape[2]
    dw_rows = jnp.pad(w_dw, ((0, 0), (0, HALO - CONV_W), (0, LANES - n_dw))).reshape(nl * HALO, LANES)
    c_all, dw_all = _all_gather([_pad_rows(c), dw_rows], vmem=True, name="cond_all_gather")
    c_full = c_all[:, :nb].reshape(N_DEV * nb, d)
    w_dw_full = jnp.transpose(dw_all.reshape(N_DEV, nl, HALO, LANES)[..., :n_dw], (1, 2, 0, 3)).reshape(nl, HALO, cc)

    o_kr, o_glu, o_gate = ql + kl, ql + kl + QK_ROPE, ql + kl + QK_ROPE + 2 * cc
    wi = full["w_in"]
    w_kr = wi[:, :, o_kr:o_glu]
    w_lat = jnp.concatenate([wi[:, :, :o_kr], jnp.zeros((nl, d, QK_NOPE), BF16), w_kr, _swap_halves(w_kr)], axis=2)
    w_glu, w_gate = wi[:, :, o_glu:o_gate], wi[:, :, o_gate:]
    wq = full["w_uq"].reshape(nl, ql, h, qk)
    w_q = jnp.concatenate([wq, _swap_halves(wq[..., QK_NOPE:])], axis=-1).reshape(nl, ql, h * HEAD_PAD)
    wkv = full["w_ukv"].reshape(nl, kl, h, QK_NOPE + V_HEAD)
    w_k = _pad_last(wkv[..., :QK_NOPE], HEAD_PAD).reshape(nl, kl, h * HEAD_PAD)
    w_v = _pad_last(wkv[..., QK_NOPE:], HEAD_PAD).reshape(nl, kl, h * HEAD_PAD)
    w_o = jnp.pad(full["w_o_attn"].reshape(nl, h, V_HEAD, d), ((0, 0), (0, 0), (0, HEAD_PAD - V_HEAD), (0, 0)))
    w_o = w_o.reshape(nl, h * HEAD_PAD, d)
    w_pw, w_ou, w_g, w_dn = full["w_pw2"], full["w_out"], full["w_gu"], full["w_down"]

    n_ada = w_ada.shape[2]
    b_cols = lax.dynamic_slice_in_dim(b_ada, me * n_ada, n_ada, axis=1).reshape(nl, 1, n_ada)
    mod_part = _ada_fwd(c_full, w_ada, b_cols)
    (mod_all,) = _all_gather([mod_part.reshape(nl * N_DEV * nb, n_ada)], vmem=True, name="mod_all_gather")
    mod_all = jnp.transpose(mod_all.reshape(N_DEV, nl, N_DEV * nb, n_ada), (1, 2, 0, 3)).reshape(nl, N_DEV * nb, 6 * d)
    mod = lax.dynamic_slice_in_dim(mod_all, me * nb, nb, axis=1).reshape(nl, nb, 6, 1, d)

    inv_freq = ROPE_THETA ** (-jnp.arange(0, QK_ROPE, 2, dtype=F32) / QK_ROPE)
    zeros = lambda n: jnp.zeros((n,), F32)
    freq_row = jnp.concatenate([zeros(QK_NOPE), inv_freq, inv_freq, zeros(HEAD_PAD - qk)]).reshape(1, -1)
    ones = jnp.ones((QK_ROPE // 2,), F32)
    sign_row = jnp.concatenate([zeros(QK_NOPE), -ones, ones, zeros(HEAD_PAD - qk)]).reshape(1, -1)
    cos_t, sin_t = _rope_tables(positions.astype(F32).reshape(nb, s, 1), freq_row, sign_row)

    row = lambda t, l: t[l].reshape(1, -1)

    saved = []
    xc = x
    for l in range(nl):
        lat, glu, gate, qh, kh, vh = _mix_pre_fwd(xc, mod[l], cos_t, sin_t, row(g_mix, l), row(g_q, l), row(g_kv, l),
                                                  w_lat[l], w_glu[l], w_gate[l], w_q[l], w_k[l], w_v[l])
        o, lse = _flash_fwd(qh, kh, vh)
        u, y_conv = _conv_fwd(glu, w_dw_full[l], row(b_dw, l), row(g_cn, l), row(b_cn, l), w_pw[l])
        x2, y_attn, o1 = _mix_post_fwd(xc, mod[l], o, y_conv, gate, w_o[l], w_ou[l])
        x3, gu, o2 = _ffn_fwd(x2, mod[l], row(g_ffn, l), w_g[l], w_dn[l])
        saved.append(dict(x=xc, lat=lat, glu=glu, gate=gate, q=qh, k=kh, v=vh, o=o, lse=lse, u=u, y_conv=y_conv, x2=x2,
                          y_attn=y_attn, o1=o1, gu=gu, o2=o2))
        xc = x3

    dx, loss_part, dgf_part = _loss_bwd(xc, loss_target, g_final.reshape(1, d))
    loss = lax.psum(0.5 / d * jnp.sum(loss_part), AXES)

    gw = {n: [None] * nl for n, _ in GATHERED}
    small_rows, dw_taps, dmod = [None] * nl, [None] * nl, [None] * nl
    for l in reversed(range(nl)):
        sv = saved[l]
        dx2, do2, act, dgu, h2, dmod2, dgffn = _ffn_bwd(dx, sv["x2"], mod[l], row(g_ffn, l), sv["gu"], sv["o2"],
                                                       _t(w_g[l]), _t(w_dn[l]))
        gw["w_gu"][l] = _matmul_tn(h2, dgu, "grad_w_gu")
        gw["w_down"][l] = _matmul_tn(act, do2, "grad_w_down")
        do1, dya, dyc, dgate, yv, do_h, dgt1 = _mix_post_bwd(dx2, mod[l], sv["o1"], sv["y_attn"], sv["y_conv"], sv["gate"],
                                                           _t(w_ou[l]), _t(w_o[l]))
        gw["w_out"][l] = _matmul_tn(yv, do1, "grad_w_out")
        dwo = _matmul_tn(sv["o"], dya, "grad_w_o")
        gw["w_o_attn"][l] = dwo.reshape(h, HEAD_PAD, d)[:, :V_HEAD].reshape(h * V_HEAD, d)
        dglu, s_act, ddw, csmall = _conv_bwd(dyc, sv["u"], sv["glu"], w_dw_full[l], row(g_cn, l), row(b_cn, l), _t(w_pw[l]))
        gw["w_pw2"][l] = _matmul_tn(s_act, dyc, "grad_w_pw2")
        dq, dk, dv = _flash_bwd(sv["q"], sv["k"], sv["v"], sv["o"], sv["lse"], do_h)
        dx, dlat, dqr, qn, kvn, h1, dmod1, dgm, dgq, dgkv = _mix_pre_bwd(
            sv["x"], dx2, mod[l], cos_t, sin_t, row(g_mix, l), row(g_q, l), row(g_kv, l), sv["lat"], dq, dk, dv, dglu,
            dgate, _t(w_lat[l]), _t(w_glu[l]), _t(w_gate[l]), _t(w_q[l]), _t(w_k[l]), _t(w_v[l]))
        dwl = _matmul_tn(h1, dlat, "grad_w_lat")
        dwg = _matmul_tn(h1, dglu, "grad_w_glu")
        dwt = _matmul_tn(h1, dgate, "grad_w_gate")
        kr0 = o_kr + QK_NOPE
        dkr = dwl[:, kr0:kr0 + QK_ROPE] + _swap_halves(dwl[:, kr0 + QK_ROPE:])
        gw["w_in"][l] = jnp.concatenate([dwl[:, :o_kr], dkr, dwg, dwt], axis=1)
        dwq = _matmul_tn(qn, dqr, "grad_w_q").reshape(ql, h, HEAD_PAD)
        dwq = jnp.concatenate([dwq[..., :QK_NOPE], dwq[..., QK_NOPE:qk] + _swap_halves(dwq[..., qk:])], axis=-1)
        gw["w_uq"][l] = dwq.reshape(ql, h * qk)
        dwk = _matmul_tn(kvn, dk, "grad_w_k").reshape(kl, h, HEAD_PAD)
        dwv = _matmul_tn(kvn, dv, "grad_w_v").reshape(kl, h, HEAD_PAD)
        gw["w_ukv"][l] = jnp.concatenate([dwk[..., :QK_NOPE], dwv[..., :V_HEAD]], axis=-1).reshape(kl, h * (QK_NOPE + V_HEAD))
        dmod[l] = jnp.concatenate([dmod1[:, :, 0], dgt1[:, :, 0], dmod2[:, :, 0]], axis=1).reshape(nb, 6 * d)
        bsum = lambda t: jnp.sum(t, axis=0).reshape(1, -1)
        cs = jnp.sum(csmall, axis=0)[:, 0]
        small_rows[l] = jnp.concatenate([bsum(dgm), bsum(dgq), bsum(dgkv), cs[2:3], cs[0:1], cs[1:2], bsum(dgffn)], axis=1)
        dw_taps[l] = jnp.sum(ddw, axis=0)
    grad_x = dx

    dmod_rows = _pad_rows(jnp.stack(dmod).reshape(nl * nb, 6 * d))
    (dmod_all,) = _all_gather([dmod_rows], vmem=True, name="dmod_all_gather")
    dmod_full = jnp.transpose(dmod_all[:, :nl * nb].reshape(N_DEV, nl, nb, 6 * d), (1, 0, 2, 3)).reshape(nl, N_DEV * nb, 6 * d)
    dmod_cols = lax.dynamic_slice_in_dim(dmod_full, me * n_ada, n_ada, axis=2)
    grad_w_ada, grad_b_ada = _ada_bwd(c_full, dmod_cols, dmod_full)
    grads = {"w_ada": grad_w_ada, "b_ada": grad_b_ada.reshape(nl, 6 * d)}

    widths = (d, ql, kl, cc, cc, cc, d)
    wsum = sum(widths)
    final_row = _pad_last(jnp.sum(dgf_part, axis=0).reshape(1, d), wsum)
    small2d = _pad_rows(jnp.concatenate(small_rows + [final_row], axis=0))
    taps2d = jnp.concatenate(dw_taps, axis=0)
    small_all, taps_all = _all_gather([small2d, taps2d], vmem=True, name="small_grad_all_gather")
    small_sum, taps_sum = _sum_devices(small_all), _sum_devices(taps_all)
    off = 0
    for n, wdt in zip(("g_mix", "g_q", "g_kv", "b_dw", "g_cn", "b_cn", "g_ffn"), widths):
        grads[n] = small_sum[:nl, off:off + wdt]
        off += wdt
    grads["g_final"] = small_sum[nl, :d]
    taps = taps_sum.reshape(nl, HALO, cc)[:, :CONV_W]
    grads["w_dw"] = lax.dynamic_slice_in_dim(taps, me * n_dw, n_dw, axis=2)

    delta, new_m, new_v = {}, {}, {}
    for n in order:
        if n in grads:
            delta[n], new_m[n], new_v[n] = _adamw(weights[n], grads[n], mom_m[n], mom_v[n], "adamw_" + n)

    gs = [_chunks_by_core(jnp.stack(gw[n]), ax) for n, ax in GATHERED]
    from_sibling = _sibling_exchange(gs)
    parts = [_add_halves(_rows_2d(g, N_DEV), _rows_2d(r, 4), core, "grad_chip_partial_" + n).reshape((4,) + g.shape[1:])
             for (n, _), g, r in zip(GATHERED, gs, from_sibling)]
    from_chips = _chip_exchange(parts)
    for (n, _), part, rest in zip(GATHERED, parts, from_chips):
        grads[n], delta[n], new_m[n], new_v[n] = _adamw_reduce(weights[n], mom_m[n], mom_v[n], part, rest, chip,
                                                                 "adamw_" + n)

    return (loss, grad_x, *[grads[n] for n in order], *[delta[n] for n in order], *[new_m[n] for n in order],
            *[new_v[n] for n in order])
```

```python
import jax
import jax.numpy as jnp
from jax import lax
from jax.experimental import pallas as pl
from jax.experimental.pallas import tpu as pltpu

F32, BF16 = jnp.float32, jnp.bfloat16
MESH = pl.DeviceIdType.MESH
AXES = ("x", "y", "c")
N_DEV = 8

N_HEADS = 8
QK_NOPE = 64
QK_ROPE = 32
V_HEAD = 64
HEAD_PAD = 128
CONV_W = 31
HALO = 32
EPS = 1e-6
ROPE_THETA = 10000.0
NEG_INF = -1e30
ATTN_SCALE = (QK_NOPE + QK_ROPE) ** -0.5

ADAM_LR, ADAM_B1, ADAM_B2, ADAM_EPS, ADAM_WD, ADAM_STEP = 0.001, 0.9, 0.999, 1e-08, 0.01, 10

LANES = 128
VMEM_LIMIT = 60 * 1024 * 1024


def _params(sem=None):
    return pltpu.CompilerParams(dimension_semantics=sem, vmem_limit_bytes=VMEM_LIMIT)


def _pick(n, cap):
    if n <= cap:
        return n
    best = None
    for d in range(LANES, cap + 1, LANES):
        if n % d == 0:
            best = d
    assert best is not None, (n, cap)
    return best


def _div_tile(n, cap, mult):
    best = None
    for d in range(mult, min(n, cap) + 1, mult):
        if n % d == 0:
            best = d
    assert best is not None, (n, cap, mult)
    return best


def _row_tile(s, cap=512):
    return cap if s % cap == 0 and s >= 2 * cap else s // 2


def _sig(v):
    return 1.0 / (1.0 + jnp.exp(-v))


def _rstd(v):
    return lax.rsqrt(jnp.mean(v * v, axis=-1, keepdims=True) + EPS)


def _dot(a, b):
    return jnp.dot(a, b, preferred_element_type=F32)


def _dot_nt(a, b):
    return lax.dot_general(a, b, (((1,), (1,)), ((), ())), preferred_element_type=F32)


def _dot_tn(a, b):
    return lax.dot_general(a, b, (((0,), (0,)), ((), ())), preferred_element_type=F32)


def _rope(v, cos_t, sin_t):
    return v * cos_t + pltpu.roll(v, HEAD_PAD - QK_ROPE, 1) * sin_t


def _rope_bwd(dv, cos_t, sin_t):
    return dv * cos_t + pltpu.roll(dv * sin_t, QK_ROPE, 1)


def _const(shape):
    n = len(shape)
    return pl.BlockSpec(shape, lambda *_: (0,) * n, pipeline_mode=pl.Buffered(1))


def _rows(tm, w):
    return pl.BlockSpec((1, tm, w), lambda b, s: (b, s, 0))


def _per_b(r, w):
    return pl.BlockSpec((1, r, 1, w), lambda b, s: (b, 0, 0, 0))


def _all_gather(arrs, vmem, name):
    n = len(arrs)
    space = pltpu.VMEM if vmem else pl.ANY

    def body(*refs):
        x_refs, out_refs = refs[:n], refs[n:2 * n]
        send_sems, recv_sems, local_sems = refs[2 * n:]
        x_, y_, c_ = lax.axis_index("x"), lax.axis_index("y"), lax.axis_index("c")
        me, sibling = (x_, y_, c_), (x_, y_, 1 - c_)
        chips = [(1 - x_, y_), (x_, 1 - y_), (1 - x_, 1 - y_)]

        def copy(a, k, block, to, own=False):
            px, py, pc = block
            slot = out_refs[a].at[4 * px + 2 * py + pc]
            return pltpu.make_async_remote_copy(
                src_ref=x_refs[a] if own else slot, dst_ref=slot, send_sem=send_sems.at[k * n + a],
                recv_sem=recv_sems.at[k * n + a], device_id=to, device_id_type=MESH)

        mine = [pltpu.make_async_copy(x_refs[a], out_refs[a].at[4 * x_ + 2 * y_ + c_], local_sems.at[a]) for a in range(n)]
        sent = []
        for a in range(n):
            mine[a].start()
            sent.append(copy(a, 0, me, sibling, own=True))
            sent += [copy(a, 1 + j, me, (*chip, c_), own=True) for j, chip in enumerate(chips)]
        for cp in sent:
            cp.start()
        for j, chip in enumerate(chips):
            for a in range(n):
                copy(a, 1 + j, (*chip, c_), me).wait_recv()
                passed = copy(a, 4 + j, (*chip, c_), sibling)
                passed.start()
                sent.append(passed)
        for a in range(n):
            copy(a, 0, sibling, me).wait_recv()
            for j, chip in enumerate(chips):
                copy(a, 4 + j, (*chip, 1 - c_), me).wait_recv()
        for cp in sent:
            cp.wait_send()
        for cp in mine:
            cp.wait()

    return pl.pallas_call(
        body, name=name,
        out_shape=[jax.ShapeDtypeStruct((N_DEV,) + t.shape, t.dtype) for t in arrs],
        in_specs=[pl.BlockSpec(memory_space=space)] * n, out_specs=[pl.BlockSpec(memory_space=space)] * n,
        scratch_shapes=[pltpu.SemaphoreType.DMA((7 * n,)), pltpu.SemaphoreType.DMA((7 * n,)), pltpu.SemaphoreType.DMA((n,))],
        compiler_params=pltpu.CompilerParams(vmem_limit_bytes=VMEM_LIMIT),
    )(*arrs)


FLIPS = tuple((fx, fy, fc) for fx in (0, 1) for fy in (0, 1) for fc in (0, 1))[1:]


class _DirectExchange:
    def __init__(self, kind, in_refs, out_refs, send_sems, recv_sems, local_sems):
        n = len(in_refs)
        x_, y_, c_ = lax.axis_index("x"), lax.axis_index("y"), lax.axis_index("c")
        me = 4 * x_ + 2 * y_ + c_
        self.copies, self.local = [], []
        for r, (fx, fy, fc) in enumerate(FLIPS):
            px, py, pc = (1 - x_ if fx else x_), (1 - y_ if fy else y_), (1 - c_ if fc else c_)
            for a in range(n):
                src = in_refs[a] if kind == "gather" else in_refs[a].at[4 * px + 2 * py + pc]
                dst = out_refs[a].at[me] if kind == "gather" else out_refs[a].at[r]
                self.copies.append(pltpu.make_async_remote_copy(
                    src_ref=src, dst_ref=dst, send_sem=send_sems.at[r * n + a], recv_sem=recv_sems.at[r * n + a],
                    device_id=(px, py, pc), device_id_type=MESH))
        if kind == "gather":
            self.local = [pltpu.make_async_copy(in_refs[a], out_refs[a].at[me], local_sems.at[a]) for a in range(n)]

    def start(self):
        for cp in self.local + self.copies:
            cp.start()

    def wait(self):
        for cp in self.copies + self.local:
            cp.wait()

    @staticmethod
    def out_shapes(kind, arrs):
        if kind == "gather":
            return [jax.ShapeDtypeStruct((N_DEV,) + t.shape, t.dtype) for t in arrs]
        return [jax.ShapeDtypeStruct((N_DEV - 1,) + t.shape[1:], t.dtype) for t in arrs]

    @staticmethod
    def scratch(n):
        return [pltpu.SemaphoreType.DMA((7 * n,)), pltpu.SemaphoreType.DMA((7 * n,)), pltpu.SemaphoreType.DMA((n,))]


def _grad_scatter(gs):
    n = len(gs)

    def body(*refs):
        ex = _DirectExchange("scatter", refs[:n], refs[n:2 * n], *refs[2 * n:])
        ex.start()
        ex.wait()

    return pl.pallas_call(
        body, name="grad_scatter", out_shape=_DirectExchange.out_shapes("scatter", gs),
        in_specs=[pl.BlockSpec(memory_space=pl.ANY)] * n, out_specs=[pl.BlockSpec(memory_space=pl.ANY)] * n,
        scratch_shapes=_DirectExchange.scratch(n),
    )(*gs)


def _rows_2d(t, lead):
    return t.reshape((lead, -1, t.shape[-1]) if lead else (-1, t.shape[-1]))


def _grad_row_tile(rows, cols):
    return _div_tile(rows, max(16, (1 << 18) // cols), 16)


def _sum_devices(g):
    _, m, n = g.shape

    def body(g_ref, o_ref):
        s = g_ref[0]
        for j in range(1, N_DEV):
            s = s + g_ref[j]
        o_ref[...] = s

    return pl.pallas_call(body, name="small_grad_sum", out_shape=jax.ShapeDtypeStruct((m, n), F32),
                          compiler_params=pltpu.CompilerParams(vmem_limit_bytes=VMEM_LIMIT))(g)


def _ada_fwd(c_full, w_ada, b_cols):
    nl, d, n = w_ada.shape
    nb = c_full.shape[0]

    def body(c_ref, w_ref, b_ref, o_ref):
        cv = c_ref[...]
        act = cv * _sig(cv)
        o_ref[0] = jnp.dot(act, w_ref[0], preferred_element_type=F32, precision=lax.Precision.HIGHEST) + b_ref[0]

    return pl.pallas_call(
        body, name="ada_fwd", grid=(nl,), out_shape=jax.ShapeDtypeStruct((nl, nb, n), F32),
        in_specs=[pl.BlockSpec((nb, d), lambda l: (0, 0)), pl.BlockSpec((1, d, n), lambda l: (l, 0, 0)),
                  pl.BlockSpec((1, 1, n), lambda l: (l, 0, 0))],
        out_specs=pl.BlockSpec((1, nb, n), lambda l: (l, 0, 0)), compiler_params=_params(("arbitrary",)),
    )(c_full, w_ada, b_cols)


def _ada_bwd(c_full, dmod_cols, dmod_full):
    nl, nb, n = dmod_cols.shape
    d = c_full.shape[1]
    nfull = dmod_full.shape[2]

    def body(c_ref, dc_ref, df_ref, gw_ref, gb_ref):
        cv = c_ref[...]
        act = cv * _sig(cv)
        gw_ref[0] = lax.dot_general(act, dc_ref[0], (((0,), (0,)), ((), ())), preferred_element_type=F32,
                                    precision=lax.Precision.HIGHEST)
        gb_ref[0] = jnp.sum(df_ref[0], axis=0, keepdims=True)

    return pl.pallas_call(
        body, name="ada_bwd", grid=(nl,),
        out_shape=(jax.ShapeDtypeStruct((nl, d, n), F32), jax.ShapeDtypeStruct((nl, 1, nfull), F32)),
        in_specs=[pl.BlockSpec((nb, d), lambda l: (0, 0)), pl.BlockSpec((1, nb, n), lambda l: (l, 0, 0)),
                  pl.BlockSpec((1, nb, nfull), lambda l: (l, 0, 0))],
        out_specs=(pl.BlockSpec((1, d, n), lambda l: (l, 0, 0)), pl.BlockSpec((1, 1, nfull), lambda l: (l, 0, 0))),
        compiler_params=_params(("arbitrary",)),
    )(c_full, dmod_cols, dmod_full)


def _rope_tables(pos, freq_row, sign_row):
    b, s, _ = pos.shape
    tm = _row_tile(s)

    def body(p_ref, f_ref, g_ref, c_ref, s_ref):
        ang = p_ref[0] * f_ref[...]
        lane = lax.broadcasted_iota(jnp.int32, ang.shape, 1)
        c_ref[0] = jnp.where(lane < QK_NOPE, 1.0, jnp.where(lane < QK_NOPE + QK_ROPE, jnp.cos(ang), 0.0))
        s_ref[0] = g_ref[...] * jnp.sin(ang)

    return pl.pallas_call(
        body, name="rope_tables", grid=(b, s // tm),
        out_shape=(jax.ShapeDtypeStruct((b, s, HEAD_PAD), F32),) * 2,
        in_specs=[_rows(tm, 1), pl.BlockSpec((1, HEAD_PAD), lambda i, j: (0, 0)),
                  pl.BlockSpec((1, HEAD_PAD), lambda i, j: (0, 0))],
        out_specs=(_rows(tm, HEAD_PAD),) * 2, compiler_params=_params(("arbitrary", "arbitrary")),
    )(pos, freq_row, sign_row)


def _mix_pre_fwd(x, mod, cos_t, sin_t, g_mix, g_q, g_kv, w_lat, w_glu, w_gate, w_q, w_k, w_v):
    b, s, d = x.shape
    ql, kl = g_q.shape[1], g_kv.shape[1]
    wl, wg, wt = w_lat.shape[1], w_glu.shape[1], w_gate.shape[1]
    hw = N_HEADS * HEAD_PAD
    tm = _row_tile(s)

    def body(x_ref, mod_ref, cos_ref, sin_ref, gm_ref, gq_ref, gkv_ref, wlat_ref, wglu_ref, wgate_ref, wq_ref, wk_ref,
             wv_ref, lat_ref, glu_ref, gate_ref, q_ref, k_ref, v_ref):
        xf = x_ref[0]
        sh, sc = mod_ref[0, 0], mod_ref[0, 1]
        hb = ((xf * _rstd(xf) * gm_ref[...]) * (1.0 + sc) + sh).astype(BF16)
        glu_ref[0] = _dot(hb, wglu_ref[...]).astype(BF16)
        gate_ref[0] = _dot(hb, wgate_ref[...]).astype(BF16)
        lat = _dot(hb, wlat_ref[...]).astype(BF16)
        lat_ref[0] = lat
        latf = lat.astype(F32)
        q_lat, kv_lat, kr_sec = latf[:, :ql], latf[:, ql:ql + kl], latf[:, ql + kl:]
        qn = (q_lat * _rstd(q_lat) * gq_ref[...]).astype(BF16)
        kvn = (kv_lat * _rstd(kv_lat) * gkv_ref[...]).astype(BF16)
        cos_v, sin_v = cos_ref[0], sin_ref[0]
        lane = lax.broadcasted_iota(jnp.int32, kr_sec.shape, 1)
        kr = jnp.where(lane >= QK_NOPE, _rope(kr_sec, cos_v, sin_v), 0.0)
        q_all, k_all, v_all = _dot(qn, wq_ref[...]), _dot(kvn, wk_ref[...]), _dot(kvn, wv_ref[...])
        vlane = lax.broadcasted_iota(jnp.int32, v_all.shape, 1)
        v_ref[0] = jnp.where(vlane % HEAD_PAD == V_HEAD, 1.0, v_all).astype(BF16)
        for h in range(N_HEADS):
            cols = slice(h * HEAD_PAD, (h + 1) * HEAD_PAD)
            q_ref[0, :, cols] = (_rope(q_all[:, cols], cos_v, sin_v) * ATTN_SCALE).astype(BF16)
            k_ref[0, :, cols] = (k_all[:, cols] + kr).astype(BF16)

    hshape = jax.ShapeDtypeStruct((b, s, hw), BF16)
    return pl.pallas_call(
        body, name="mix_pre_fwd", grid=(b, s // tm),
        out_shape=(jax.ShapeDtypeStruct((b, s, wl), BF16), jax.ShapeDtypeStruct((b, s, wg), BF16),
                   jax.ShapeDtypeStruct((b, s, wt), BF16), hshape, hshape, hshape),
        in_specs=[_rows(tm, d), _per_b(6, d), _rows(tm, HEAD_PAD), _rows(tm, HEAD_PAD), _const((1, d)), _const((1, ql)),
                  _const((1, kl)), _const(w_lat.shape), _const(w_glu.shape), _const(w_gate.shape), _const(w_q.shape),
                  _const(w_k.shape), _const(w_v.shape)],
        out_specs=(_rows(tm, wl), _rows(tm, wg), _rows(tm, wt), _rows(tm, hw), _rows(tm, hw), _rows(tm, hw)),
        compiler_params=_params(("arbitrary", "arbitrary")),
    )(x, mod, cos_t, sin_t, g_mix, g_q, g_kv, w_lat, w_glu, w_gate, w_q, w_k, w_v)


def _causal_mask(tq, tk):
    return lax.broadcasted_iota(jnp.int32, (tq, tk), 0) >= lax.broadcasted_iota(jnp.int32, (tq, tk), 1)


def _riding_exchange(kind, n, refs, grid):
    if not n:
        return
    ex = _DirectExchange(kind, refs[:n], refs[n:2 * n], *refs[2 * n:])
    ids = [pl.program_id(a) for a in range(len(grid))]
    first, last = ids[0] == 0, ids[0] == grid[0] - 1
    for a in range(1, len(grid)):
        first, last = jnp.logical_and(first, ids[a] == 0), jnp.logical_and(last, ids[a] == grid[a] - 1)
    pl.when(first)(ex.start)
    return lambda: pl.when(last)(ex.wait)


def _flash_fwd(q, k, v, gather=()):
    b, s, hw = q.shape
    nh, hp = hw // HEAD_PAD, HEAD_PAD
    t = _row_tile(s)
    n = len(gather)
    grid = (b, nh, s // t)

    def body(q_ref, k_ref, v_ref, *rest):
        o_ref, lse_ref = rest[n], rest[n + 1]
        finish = _riding_exchange("gather", n, rest[:n] + rest[n + 2:], grid)
        i = pl.program_id(2)
        qv = q_ref[0]

        def step(j, carry, masked):
            m, acc = carry
            rows = pl.ds(pl.multiple_of(j * t, t), t)
            sc = _dot_nt(qv, k_ref[0, rows, :])
            if masked:
                sc = jnp.where(_causal_mask(t, t), sc, NEG_INF)
            m_new = jnp.maximum(m, jnp.max(sc, axis=-1, keepdims=True))
            p = jnp.exp((sc - m_new).astype(BF16))
            acc = jnp.exp(m - m_new) * acc + _dot(p, v_ref[0, rows, :])
            return m_new, acc

        init = (jnp.full((t, 1), NEG_INF, F32), jnp.zeros((t, hp), F32))
        carry = lax.fori_loop(0, i, lambda j, cr: step(j, cr, False), init)
        m, acc = step(i, carry, True)
        lane = lax.broadcasted_iota(jnp.int32, acc.shape, 1)
        l = jnp.sum(jnp.where(lane == V_HEAD, acc, 0.0), axis=-1, keepdims=True)
        o_ref[0] = (acc / l).astype(BF16)
        lse_ref[0, 0] = m + jnp.log(l)
        if finish:
            finish()

    tile = pl.BlockSpec((1, t, hp), lambda bb, hh, ii: (bb, ii, hh))
    full = pl.BlockSpec((1, s, hp), lambda bb, hh, ii: (bb, 0, hh))
    hbm = [pl.BlockSpec(memory_space=pl.ANY)] * n
    outs = pl.pallas_call(
        body, name="flash_fwd_gather" if n else "flash_fwd", grid=grid,
        out_shape=[jax.ShapeDtypeStruct((b, s, hw), BF16), jax.ShapeDtypeStruct((b, nh, s, 1), F32)]
        + _DirectExchange.out_shapes("gather", gather),
        in_specs=[tile, full, full] + hbm,
        out_specs=[tile, pl.BlockSpec((1, 1, t, 1), lambda bb, hh, ii: (bb, hh, ii, 0))] + hbm,
        scratch_shapes=_DirectExchange.scratch(n) if n else [],
        compiler_params=_params(("arbitrary", "arbitrary", "arbitrary")),
    )(q, k, v, *gather)
    return outs[0], outs[1], outs[2:]


def _halo_prev(tm, w):
    r = tm // HALO
    return pl.BlockSpec((1, HALO, w), lambda b, s: (b, jnp.maximum(s * r - 1, 0), 0))


def _halo_next(tm, w, n_tiles):
    r = tm // HALO
    return pl.BlockSpec((1, HALO, w), lambda b, s: (b, jnp.minimum((s + 1) * r, n_tiles * r - 1), 0))


def _glu(v, cc):
    a, g = v[:, :cc].astype(F32), v[:, cc:].astype(F32)
    return a * _sig(g)


def _conv_fwd(glu, w_dw, b_dw, g_cn, b_cn, w_pw2):
    b, s, w2 = glu.shape
    cc = w2 // 2
    d = w_pw2.shape[1]
    tm = _row_tile(s)

    def body(cur_ref, prev_ref, w_ref, bdw_ref, g_ref, bcn_ref, wp_ref, u_ref, y_ref, ext):
        first = pl.program_id(1) == 0
        ext[pl.ds(0, HALO), :] = jnp.where(first, 0.0, _glu(prev_ref[0], cc))
        ext[pl.ds(HALO, tm), :] = _glu(cur_ref[0], cc)
        u = jnp.zeros((tm, cc), F32) + bdw_ref[...]
        for kk in range(CONV_W):
            u = u + w_ref[pl.ds(kk, 1), :] * ext[pl.ds(HALO - CONV_W + 1 + kk, tm), :]
        ub = u.astype(BF16)
        u_ref[0] = ub
        uf = ub.astype(F32)
        mu = jnp.mean(uf, axis=-1, keepdims=True)
        uc = uf - mu
        ln = uc * lax.rsqrt(jnp.mean(uc * uc, axis=-1, keepdims=True) + EPS) * g_ref[...] + bcn_ref[...]
        y_ref[0] = _dot((ln * _sig(ln)).astype(BF16), wp_ref[...]).astype(BF16)

    return pl.pallas_call(
        body, name="conv_fwd", grid=(b, s // tm),
        out_shape=(jax.ShapeDtypeStruct((b, s, cc), BF16), jax.ShapeDtypeStruct((b, s, d), BF16)),
        in_specs=[_rows(tm, w2), _halo_prev(tm, w2), _const(w_dw.shape), _const((1, cc)), _const((1, cc)), _const((1, cc)),
                  _const(w_pw2.shape)],
        out_specs=(_rows(tm, cc), _rows(tm, d)),
        scratch_shapes=[pltpu.VMEM((tm + HALO, cc), F32)],
        compiler_params=_params(("arbitrary", "arbitrary")),
    )(glu, glu, w_dw, b_dw, g_cn, b_cn, w_pw2)


def _mix_post_fwd(x, mod, o, y_conv, gate, w_o, w_out):
    b, s, d = x.shape
    hw = o.shape[2]
    tm = _row_tile(s)

    def body(x_ref, mod_ref, o_ref, yc_ref, gate_ref, wo_ref, wout_ref, x2_ref, ya_ref, o1_ref):
        yab = _dot(o_ref[0], wo_ref[...]).astype(BF16)
        ya_ref[0] = yab
        gv = gate_ref[0]
        y = _sig(gv[:, :d].astype(F32)) * yab.astype(F32) + _sig(gv[:, d:].astype(F32)) * yc_ref[0].astype(F32)
        o1 = _dot(y.astype(BF16), wout_ref[...])
        o1_ref[0] = o1.astype(BF16)
        x2_ref[0] = x_ref[0] + mod_ref[0, 2] * o1

    return pl.pallas_call(
        body, name="mix_post_fwd", grid=(b, s // tm),
        out_shape=(jax.ShapeDtypeStruct((b, s, d), F32), jax.ShapeDtypeStruct((b, s, d), BF16),
                   jax.ShapeDtypeStruct((b, s, d), BF16)),
        in_specs=[_rows(tm, d), _per_b(6, d), _rows(tm, hw), _rows(tm, d), _rows(tm, 2 * d), _const(w_o.shape),
                  _const(w_out.shape)],
        out_specs=(_rows(tm, d), _rows(tm, d), _rows(tm, d)),
        compiler_params=_params(("arbitrary", "arbitrary")),
    )(x, mod, o, y_conv, gate, w_o, w_out)


def _ffn_fwd(x2, mod, g_ffn, w_gu, w_down):
    b, s, d = x2.shape
    f = w_down.shape[0]
    fc = _pick(f, 512)
    tm = _row_tile(s)

    def body(x_ref, mod_ref, g_ref, wgu_ref, wdn_ref, x3_ref, gu_ref, o2_ref):
        xf = x_ref[0]
        hb = ((xf * _rstd(xf) * g_ref[...]) * (1.0 + mod_ref[0, 4]) + mod_ref[0, 3]).astype(BF16)
        o2 = jnp.zeros((tm, d), F32)
        for c0 in range(0, f, fc):
            gb = _dot(hb, wgu_ref[:, c0:c0 + fc]).astype(BF16)
            ub = _dot(hb, wgu_ref[:, f + c0:f + c0 + fc]).astype(BF16)
            gu_ref[0, :, c0:c0 + fc] = gb
            gu_ref[0, :, f + c0:f + c0 + fc] = ub
            gf = gb.astype(F32)
            act = (gf * _sig(gf) * ub.astype(F32)).astype(BF16)
            o2 = o2 + _dot(act, wdn_ref[c0:c0 + fc, :])
        o2_ref[0] = o2.astype(BF16)
        x3_ref[0] = xf + mod_ref[0, 5] * o2

    return pl.pallas_call(
        body, name="ffn_fwd", grid=(b, s // tm),
        out_shape=(jax.ShapeDtypeStruct((b, s, d), F32), jax.ShapeDtypeStruct((b, s, 2 * f), BF16),
                   jax.ShapeDtypeStruct((b, s, d), BF16)),
        in_specs=[_rows(tm, d), _per_b(6, d), _const((1, d)), _const(w_gu.shape), _const(w_down.shape)],
        out_specs=(_rows(tm, d), _rows(tm, 2 * f), _rows(tm, d)),
        compiler_params=_params(("arbitrary", "arbitrary")),
    )(x2, mod, g_ffn, w_gu, w_down)


def _zero_at_first_tile(*refs):
    @pl.when(pl.program_id(1) == 0)
    def _():
        for ref in refs:
            ref[...] = jnp.zeros_like(ref)


def _accumulate(ref, idx, val):
    ref[idx] = ref[idx] + val


def _colsum(v):
    return jnp.sum(v, axis=0, keepdims=True)


def _loss_bwd(x, target, g_final):
    b, s, d = x.shape
    tm = _row_tile(s)

    def body(x_ref, t_ref, g_ref, dx_ref, loss_ref, dg_ref):
        _zero_at_first_tile(loss_ref, dg_ref)
        xf = x_ref[0]
        r = _rstd(xf)
        xh = xf * r
        diff = xh * g_ref[...] - t_ref[0]
        _accumulate(loss_ref, (0, 0), _colsum(diff * diff))
        dy = diff * (1.0 / d)
        _accumulate(dg_ref, (0, 0), _colsum(dy * xh))
        dyg = dy * g_ref[...]
        dx_ref[0] = r * (dyg - xh * jnp.mean(dyg * xh, axis=-1, keepdims=True))

    return pl.pallas_call(
        body, name="loss_bwd", grid=(b, s // tm),
        out_shape=(jax.ShapeDtypeStruct((b, s, d), F32), jax.ShapeDtypeStruct((b, 1, 1, d), F32),
                   jax.ShapeDtypeStruct((b, 1, 1, d), F32)),
        in_specs=[_rows(tm, d), _rows(tm, d), _const((1, d))],
        out_specs=(_rows(tm, d), _per_b(1, d), _per_b(1, d)),
        compiler_params=_params(("arbitrary", "arbitrary")),
    )(x, target, g_final)


def _ffn_bwd(dx3, x2, mod, g_ffn, gu, o2, w_gu_t, w_down_t):
    b, s, d = x2.shape
    f = w_down_t.shape[1]
    fc = _pick(f, 512)
    tm = _row_tile(s, 256)

    def body(dx3_ref, x_ref, mod_ref, g_ref, gu_ref, o2_ref, wgut_ref, wdnt_ref, dx2_ref, do2_ref, act_ref, dgu_ref, h_ref,
             dmod_ref, dg_ref):
        _zero_at_first_tile(dmod_ref, dg_ref)
        dx3 = dx3_ref[0]
        sh, sc, gt = mod_ref[0, 3], mod_ref[0, 4], mod_ref[0, 5]
        do2 = (dx3 * gt).astype(BF16)
        do2_ref[0] = do2
        _accumulate(dmod_ref, (0, 2), _colsum(dx3 * o2_ref[0].astype(F32)))
        dh = jnp.zeros((tm, d), F32)
        for c0 in range(0, f, fc):
            gf = gu_ref[0, :, c0:c0 + fc].astype(F32)
            uf = gu_ref[0, :, f + c0:f + c0 + fc].astype(F32)
            sg = _sig(gf)
            silu = gf * sg
            act_ref[0, :, c0:c0 + fc] = (silu * uf).astype(BF16)
            dact = _dot(do2, wdnt_ref[:, c0:c0 + fc])
            dg = (dact * uf * (sg * (1.0 + gf * (1.0 - sg)))).astype(BF16)
            du = (dact * silu).astype(BF16)
            dgu_ref[0, :, c0:c0 + fc] = dg
            dgu_ref[0, :, f + c0:f + c0 + fc] = du
            dh = dh + _dot(dg, wgut_ref[c0:c0 + fc, :]) + _dot(du, wgut_ref[f + c0:f + c0 + fc, :])
        xf = x_ref[0]
        r = _rstd(xf)
        xh = xf * r
        n = xh * g_ref[...]
        h_ref[0] = (n * (1.0 + sc) + sh).astype(BF16)
        _accumulate(dmod_ref, (0, 0), _colsum(dh))
        _accumulate(dmod_ref, (0, 1), _colsum(dh * n))
        dn = dh * (1.0 + sc)
        _accumulate(dg_ref, (0, 0), _colsum(dn * xh))
        dyg = dn * g_ref[...]
        dx2_ref[0] = dx3 + r * (dyg - xh * jnp.mean(dyg * xh, axis=-1, keepdims=True))

    return pl.pallas_call(
        body, name="ffn_bwd", grid=(b, s // tm),
        out_shape=(jax.ShapeDtypeStruct((b, s, d), F32), jax.ShapeDtypeStruct((b, s, d), BF16),
                   jax.ShapeDtypeStruct((b, s, f), BF16), jax.ShapeDtypeStruct((b, s, 2 * f), BF16),
                   jax.ShapeDtypeStruct((b, s, d), BF16), jax.ShapeDtypeStruct((b, 3, 1, d), F32),
                   jax.ShapeDtypeStruct((b, 1, 1, d), F32)),
        in_specs=[_rows(tm, d), _rows(tm, d), _per_b(6, d), _const((1, d)), _rows(tm, 2 * f), _rows(tm, d),
                  _const(w_gu_t.shape), _const(w_down_t.shape)],
        out_specs=(_rows(tm, d), _rows(tm, d), _rows(tm, f), _rows(tm, 2 * f), _rows(tm, d), _per_b(3, d), _per_b(1, d)),
        compiler_params=_params(("arbitrary", "arbitrary")),
    )(dx3, x2, mod, g_ffn, gu, o2, w_gu_t, w_down_t)


def _mix_post_bwd(dx2, mod, o1, y_attn, y_conv, gate, w_out_t, w_o_t):
    b, s, d = dx2.shape
    hw = w_o_t.shape[1]
    tm = _row_tile(s)

    def body(dx_ref, mod_ref, o1_ref, ya_ref, yc_ref, gate_ref, woutt_ref, wot_ref, do1_ref, dya_ref, dyc_ref, dgate_ref,
             y_ref, do_ref, dgt_ref):
        _zero_at_first_tile(dgt_ref)
        dx = dx_ref[0]
        do1 = (dx * mod_ref[0, 2]).astype(BF16)
        do1_ref[0] = do1
        _accumulate(dgt_ref, (0, 0), _colsum(dx * o1_ref[0].astype(F32)))
        dy = _dot(do1, woutt_ref[...])
        gv = gate_ref[0]
        sa, sb = _sig(gv[:, :d].astype(F32)), _sig(gv[:, d:].astype(F32))
        ya, yc = ya_ref[0].astype(F32), yc_ref[0].astype(F32)
        y_ref[0] = (sa * ya + sb * yc).astype(BF16)
        dya = (dy * sa).astype(BF16)
        dya_ref[0] = dya
        dyc_ref[0] = (dy * sb).astype(BF16)
        dgate_ref[0, :, :d] = (dy * ya * sa * (1.0 - sa)).astype(BF16)
        dgate_ref[0, :, d:] = (dy * yc * sb * (1.0 - sb)).astype(BF16)
        do_ref[0] = _dot(dya, wot_ref[...]).astype(BF16)

    row = jax.ShapeDtypeStruct((b, s, d), BF16)
    return pl.pallas_call(
        body, name="mix_post_bwd", grid=(b, s // tm),
        out_shape=(row, row, row, jax.ShapeDtypeStruct((b, s, 2 * d), BF16), row,
                   jax.ShapeDtypeStruct((b, s, hw), BF16), jax.ShapeDtypeStruct((b, 1, 1, d), F32)),
        in_specs=[_rows(tm, d), _per_b(6, d), _rows(tm, d), _rows(tm, d), _rows(tm, d), _rows(tm, 2 * d),
                  _const(w_out_t.shape), _const(w_o_t.shape)],
        out_specs=(_rows(tm, d), _rows(tm, d), _rows(tm, d), _rows(tm, 2 * d), _rows(tm, d), _rows(tm, hw), _per_b(1, d)),
        compiler_params=_params(("arbitrary", "arbitrary")),
    )(dx2, mod, o1, y_attn, y_conv, gate, w_out_t, w_o_t)


def _conv_bwd(dyc, u, glu, w_dw, g_cn, b_cn, w_pw2_t):
    b, s, cc = u.shape
    d = dyc.shape[2]
    tm = _row_tile(s)
    nt = s // tm
    te = tm + HALO

    def body(dyc_ref, dycn_ref, u_ref, un_ref, glu_ref, glup_ref, w_ref, g_ref, bcn_ref, wpt_ref, dglu_ref, s_ref, dw_ref,
             small_ref, du_ext, uin_ext):
        _zero_at_first_tile(dw_ref, small_ref)
        st = pl.program_id(1)
        dy_all = jnp.concatenate([dyc_ref[0], dycn_ref[0]], axis=0)
        u_all = jnp.concatenate([u_ref[0], un_ref[0]], axis=0).astype(F32)
        ds = _dot(dy_all, wpt_ref[...])
        mu = jnp.mean(u_all, axis=-1, keepdims=True)
        uc = u_all - mu
        rstd = lax.rsqrt(jnp.mean(uc * uc, axis=-1, keepdims=True) + EPS)
        uh = uc * rstd
        ln = uh * g_ref[...] + bcn_ref[...]
        sg = _sig(ln)
        s_ref[0] = (ln * sg)[:tm].astype(BF16)
        dln = ds * (sg * (1.0 + ln * (1.0 - sg)))
        duh = dln * g_ref[...]
        du = rstd * (duh - jnp.mean(duh, axis=-1, keepdims=True) - uh * jnp.mean(duh * uh, axis=-1, keepdims=True))
        row = lax.broadcasted_iota(jnp.int32, (te, 1), 0)
        du = jnp.where(jnp.logical_and(st == nt - 1, row >= tm), 0.0, du)
        du_ext[...] = du
        du_cur = du[:tm]
        _accumulate(small_ref, (0, 0), _colsum((dln * uh)[:tm]))
        _accumulate(small_ref, (0, 1), _colsum(dln[:tm]))
        _accumulate(small_ref, (0, 2), _colsum(du_cur))
        uin_ext[pl.ds(0, HALO), :] = jnp.where(st == 0, 0.0, _glu(glup_ref[0], cc))
        gv = glu_ref[0]
        ga, gb = gv[:, :cc].astype(F32), gv[:, cc:].astype(F32)
        sgb = _sig(gb)
        uin_ext[pl.ds(HALO, tm), :] = ga * sgb
        duin = jnp.zeros((tm, cc), F32)
        for kk in range(CONV_W):
            duin = duin + w_ref[pl.ds(kk, 1), :] * du_ext[pl.ds(CONV_W - 1 - kk, tm), :]
            part = _colsum(du_cur * uin_ext[pl.ds(HALO - CONV_W + 1 + kk, tm), :])
            _accumulate(dw_ref, (0, pl.ds(kk, 1)), part)
        dglu_ref[0, :, :cc] = (duin * sgb).astype(BF16)
        dglu_ref[0, :, cc:] = (duin * ga * sgb * (1.0 - sgb)).astype(BF16)

    return pl.pallas_call(
        body, name="conv_bwd", grid=(b, nt),
        out_shape=(jax.ShapeDtypeStruct((b, s, 2 * cc), BF16), jax.ShapeDtypeStruct((b, s, cc), BF16),
                   jax.ShapeDtypeStruct((b, HALO, cc), F32), jax.ShapeDtypeStruct((b, 3, 1, cc), F32)),
        in_specs=[_rows(tm, d), _halo_next(tm, d, nt), _rows(tm, cc), _halo_next(tm, cc, nt), _rows(tm, 2 * cc),
                  _halo_prev(tm, 2 * cc), _const(w_dw.shape), _const((1, cc)), _const((1, cc)), _const(w_pw2_t.shape)],
        out_specs=(_rows(tm, 2 * cc), _rows(tm, cc), pl.BlockSpec((1, HALO, cc), lambda i, j: (i, 0, 0)), _per_b(3, cc)),
        scratch_shapes=[pltpu.VMEM((te, cc), F32), pltpu.VMEM((te, cc), F32)],
        compiler_params=_params(("arbitrary", "arbitrary")),
    )(dyc, dyc, u, u, glu, glu, w_dw, g_cn, b_cn, w_pw2_t)


def _flash_bwd(q, k, v, o, lse, do, scatter=()):
    b, s, hw = q.shape
    nh, hp = hw // HEAD_PAD, HEAD_PAD
    t = _row_tile(s)
    nt = s // t
    n = len(scatter)
    grid = (b, nh, nt)

    def to_row(sel, cols):
        return lax.dot_general(sel, cols, (((1,), (1,)), ((), ())), preferred_element_type=F32,
                               precision=lax.Precision.HIGHEST)

    def body(q_ref, k_ref, v_ref, o_ref, lse_ref, do_ref, *rest):
        dqt_ref, dk_ref, dv_ref = rest[n:n + 3]
        lse_row, delta_row = rest[2 * n + 3:2 * n + 5]
        finish = _riding_exchange("scatter", n, rest[:n] + rest[n + 3:2 * n + 3] + rest[2 * n + 5:], grid)
        j = pl.program_id(2)

        @pl.when(j == 0)
        def _():
            dqt_ref[...] = jnp.zeros_like(dqt_ref)
            first_lane = (lax.broadcasted_iota(jnp.int32, (8, hp), 1) == 0).astype(F32)
            for i in range(nt):
                rows = pl.ds(i * t, t)
                prod = do_ref[0, rows, :].astype(F32) * o_ref[0, rows, :].astype(F32)
                delta_row[i] = to_row(jnp.ones((8, hp), F32), prod)
                lse_row[i] = to_row(first_lane, jnp.broadcast_to(lse_ref[0, 0, rows, :], (t, hp)))

        kv, vv = k_ref[0], v_ref[0]
        kt = kv.T
        query_not_before_key = (lax.broadcasted_iota(jnp.int32, (t, t), 1) >= lax.broadcasted_iota(jnp.int32, (t, t), 0))

        def step(i, carry, masked):
            dk, dv = carry
            rows = pl.ds(pl.multiple_of(i * t, t), t)
            qv, dov = q_ref[0, rows, :], do_ref[0, rows, :]
            pt = jnp.exp((_dot_nt(kv, qv) - lse_row[i, 0:1, :]).astype(BF16))
            if masked:
                pt = jnp.where(query_not_before_key, pt, jnp.zeros((), BF16))
            dv = dv + _dot(pt, dov)
            dst = pt * (_dot_nt(vv, dov) - delta_row[i, 0:1, :]).astype(BF16)
            dk = dk + _dot(dst, qv)
            dqt_ref[0, i] = dqt_ref[0, i] + _dot(kt, dst)
            return dk, dv

        carry = step(j, (jnp.zeros((t, hp), F32), jnp.zeros((t, hp), F32)), True)
        dk, dv = lax.fori_loop(j + 1, nt, lambda i, cr: step(i, cr, False), carry)
        dk_ref[0] = dk.astype(BF16)
        dv_ref[0] = dv.astype(BF16)
        if finish:
            finish()

    tile = pl.BlockSpec((1, t, hp), lambda bb, hh, jj: (bb, jj, hh))
    full = pl.BlockSpec((1, s, hp), lambda bb, hh, jj: (bb, 0, hh))
    hbm = [pl.BlockSpec(memory_space=pl.ANY)] * n
    outs = pl.pallas_call(
        body, name="flash_bwd_scatter" if n else "flash_bwd", grid=grid,
        out_shape=[jax.ShapeDtypeStruct((b, nt, hw, t), F32), jax.ShapeDtypeStruct((b, s, hw), BF16),
                   jax.ShapeDtypeStruct((b, s, hw), BF16)] + _DirectExchange.out_shapes("scatter", scatter),
        in_specs=[full, tile, tile, full, pl.BlockSpec((1, 1, s, 1), lambda bb, hh, jj: (bb, hh, 0, 0)), full] + hbm,
        out_specs=[pl.BlockSpec((1, nt, hp, t), lambda bb, hh, jj: (bb, 0, hh, 0)), tile, tile] + hbm,
        scratch_shapes=[pltpu.VMEM((nt, 8, t), F32), pltpu.VMEM((nt, 8, t), F32)]
        + (_DirectExchange.scratch(n) if n else []),
        compiler_params=_params(("arbitrary", "arbitrary", "arbitrary")),
    )(q, k, v, o, lse, do, *scatter)
    return outs[0], outs[1], outs[2], outs[3:]


def _mix_pre_bwd(x, dx2, mod, cos_t, sin_t, g_mix, g_q, g_kv, lat, dq, dk, dv, dglu, dgate, w_lat_t, w_glu_t, w_gate_t,
                 w_q_t, w_k_t, w_v_t):
    b, s, d = x.shape
    ql, kl = g_q.shape[1], g_kv.shape[1]
    wl = lat.shape[2]
    hw = N_HEADS * HEAD_PAD
    tm = _row_tile(s)

    def body(x_ref, dx2_ref, mod_ref, cos_ref, sin_ref, gm_ref, gq_ref, gkv_ref, lat_ref, dq_ref, dk_ref, dv_ref, dglu_ref,
             dgate_ref, wlt_ref, wgt_ref, wtt_ref, wqt_ref, wkt_ref, wvt_ref, dx_ref, dlat_ref, dqr_ref, qn_ref, kvn_ref,
             h_ref, dmod_ref, dgm_ref, dgq_ref, dgkv_ref):
        _zero_at_first_tile(dmod_ref, dgm_ref, dgq_ref, dgkv_ref)
        cos_v, sin_v = cos_ref[0], sin_ref[0]
        latf = lat_ref[0].astype(F32)
        q_lat, kv_lat = latf[:, :ql], latf[:, ql:ql + kl]
        rq, rk = _rstd(q_lat), _rstd(kv_lat)
        qh, kh = q_lat * rq, kv_lat * rk
        qn_ref[0] = (qh * gq_ref[...]).astype(BF16)
        kvn_ref[0] = (kh * gkv_ref[...]).astype(BF16)
        dk_sum = jnp.zeros((tm, HEAD_PAD), F32)
        for h in range(N_HEADS):
            cols = slice(h * HEAD_PAD, (h + 1) * HEAD_PAD)
            dq_head = dq_ref[0, 0, cols, :].T
            dqr_ref[0, :, cols] = _rope_bwd(dq_head * ATTN_SCALE, cos_v, sin_v).astype(BF16)
            dk_sum = dk_sum + dk_ref[0, :, cols].astype(F32)
        dqn = _dot(dqr_ref[0], wqt_ref[...])
        dkvn = _dot(dk_ref[0], wkt_ref[...]) + _dot(dv_ref[0], wvt_ref[...])
        lane = lax.broadcasted_iota(jnp.int32, dk_sum.shape, 1)
        dkr = _rope_bwd(jnp.where(lane >= QK_NOPE, dk_sum, 0.0), cos_v, sin_v)
        _accumulate(dgq_ref, (0, 0), _colsum(dqn * qh))
        _accumulate(dgkv_ref, (0, 0), _colsum(dkvn * kh))
        dqg, dkg = dqn * gq_ref[...], dkvn * gkv_ref[...]
        dlat_ref[0, :, :ql] = (rq * (dqg - qh * jnp.mean(dqg * qh, axis=-1, keepdims=True))).astype(BF16)
        dlat_ref[0, :, ql:ql + kl] = (rk * (dkg - kh * jnp.mean(dkg * kh, axis=-1, keepdims=True))).astype(BF16)
        dlat_ref[0, :, ql + kl:] = dkr.astype(BF16)
        dh = _dot(dlat_ref[0], wlt_ref[...]) + _dot(dglu_ref[0], wgt_ref[...]) + _dot(dgate_ref[0], wtt_ref[...])
        sh, sc = mod_ref[0, 0], mod_ref[0, 1]
        xf = x_ref[0]
        r = _rstd(xf)
        xh = xf * r
        n = xh * gm_ref[...]
        h_ref[0] = (n * (1.0 + sc) + sh).astype(BF16)
        _accumulate(dmod_ref, (0, 0), _colsum(dh))
        _accumulate(dmod_ref, (0, 1), _colsum(dh * n))
        dn = dh * (1.0 + sc)
        _accumulate(dgm_ref, (0, 0), _colsum(dn * xh))
        dyg = dn * gm_ref[...]
        dx_ref[0] = dx2_ref[0] + r * (dyg - xh * jnp.mean(dyg * xh, axis=-1, keepdims=True))

    return pl.pallas_call(
        body, name="mix_pre_bwd", grid=(b, s // tm),
        out_shape=(jax.ShapeDtypeStruct((b, s, d), F32), jax.ShapeDtypeStruct((b, s, wl), BF16),
                   jax.ShapeDtypeStruct((b, s, hw), BF16), jax.ShapeDtypeStruct((b, s, ql), BF16),
                   jax.ShapeDtypeStruct((b, s, kl), BF16), jax.ShapeDtypeStruct((b, s, d), BF16),
                   jax.ShapeDtypeStruct((b, 2, 1, d), F32), jax.ShapeDtypeStruct((b, 1, 1, d), F32),
                   jax.ShapeDtypeStruct((b, 1, 1, ql), F32), jax.ShapeDtypeStruct((b, 1, 1, kl), F32)),
        in_specs=[_rows(tm, d), _rows(tm, d), _per_b(6, d), _rows(tm, HEAD_PAD), _rows(tm, HEAD_PAD), _const((1, d)),
                  _const((1, ql)), _const((1, kl)), _rows(tm, wl),
                  pl.BlockSpec((1, 1, hw, tm), lambda b, s: (b, s, 0, 0)), _rows(tm, hw), _rows(tm, hw),
                  _rows(tm, dglu.shape[2]), _rows(tm, 2 * d), _const(w_lat_t.shape), _const(w_glu_t.shape),
                  _const(w_gate_t.shape), _const(w_q_t.shape), _const(w_k_t.shape), _const(w_v_t.shape)],
        out_specs=(_rows(tm, d), _rows(tm, wl), _rows(tm, hw), _rows(tm, ql), _rows(tm, kl), _rows(tm, d), _per_b(2, d),
                   _per_b(1, d), _per_b(1, ql), _per_b(1, kl)),
        compiler_params=_params(("arbitrary", "arbitrary")),
    )(x, dx2, mod, cos_t, sin_t, g_mix, g_q, g_kv, lat, dq, dk, dv, dglu, dgate, w_lat_t, w_glu_t, w_gate_t, w_q_t, w_k_t,
      w_v_t)


def _matmul_tn(a, bm, name):
    b, s, kd = a.shape
    nd = bm.shape[2]
    tk, tn = _pick(kd, 1536), _pick(nd, 1536)
    ts = _row_tile(s, 2048)

    def body(a_ref, b_ref, o_ref):
        part = _dot_tn(a_ref[0], b_ref[0])
        first = jnp.logical_and(pl.program_id(2) == 0, pl.program_id(3) == 0)

        @pl.when(first)
        def _():
            o_ref[...] = part

        @pl.when(jnp.logical_not(first))
        def _():
            o_ref[...] = o_ref[...] + part

    return pl.pallas_call(
        body, name=name, grid=(kd // tk, nd // tn, b, s // ts),
        out_shape=jax.ShapeDtypeStruct((kd, nd), F32),
        in_specs=[pl.BlockSpec((1, ts, tk), lambda i, j, bb, ss: (bb, ss, i)),
                  pl.BlockSpec((1, ts, tn), lambda i, j, bb, ss: (bb, ss, j))],
        out_specs=pl.BlockSpec((tk, tn), lambda i, j, bb, ss: (i, j)),
        compiler_params=_params(("arbitrary",) * 4),
    )(a, bm)


def _adamw_update(w, g, m, v):
    nm = ADAM_B1 * m + (1.0 - ADAM_B1) * g
    nv = ADAM_B2 * v + (1.0 - ADAM_B2) * (g * g)
    delta = -ADAM_LR * ((nm / (1.0 - ADAM_B1 ** ADAM_STEP)) / (jnp.sqrt(nv / (1.0 - ADAM_B2 ** ADAM_STEP)) + ADAM_EPS)
                        + ADAM_WD * w)
    return delta, nm, nv


def _adamw(w, g, m, v, name):
    shape = w.shape
    cols = shape[-1]
    rows = w.size // cols
    w2, g2, m2, v2 = (t.reshape(rows, cols) for t in (w, g, m, v))
    tr = rows
    if rows * cols * 4 > (1 << 20):
        tr = _div_tile(rows, max(8, (1 << 18) // cols), 8)

    def body(w_ref, g_ref, m_ref, v_ref, d_ref, nm_ref, nv_ref):
        d_ref[...], nm_ref[...], nv_ref[...] = _adamw_update(w_ref[...], g_ref[...], m_ref[...], v_ref[...])

    spec = pl.BlockSpec((tr, cols), lambda i: (i, 0))
    outs = pl.pallas_call(
        body, name=name, grid=(rows // tr,), out_shape=(jax.ShapeDtypeStruct((rows, cols), F32),) * 3,
        in_specs=[spec] * 4, out_specs=(spec,) * 3, compiler_params=_params(("arbitrary",)),
    )(w2, g2, m2, v2)
    return tuple(t.reshape(shape) for t in outs)


def _adamw_reduce(w, m, v, own, got, name):
    shape = w.shape
    cols = shape[-1]
    w2, m2, v2, o2 = (_rows_2d(t, 0) for t in (w, m, v, own))
    g3 = _rows_2d(got, N_DEV - 1)
    rows = w2.shape[0]
    tr = _grad_row_tile(rows, cols)

    def body(w_ref, m_ref, v_ref, o_ref, r_ref, g_ref, d_ref, nm_ref, nv_ref):
        g = o_ref[...].astype(F32)
        for r in range(N_DEV - 1):
            g = g + r_ref[r].astype(F32)
        g_ref[...] = g
        d_ref[...], nm_ref[...], nv_ref[...] = _adamw_update(w_ref[...], g, m_ref[...], v_ref[...])

    spec = pl.BlockSpec((tr, cols), lambda i: (i, 0))
    outs = pl.pallas_call(
        body, name=name, grid=(rows // tr,),
        in_specs=[spec, spec, spec, spec, pl.BlockSpec((N_DEV - 1, tr, cols), lambda i: (0, i, 0))],
        out_specs=(spec,) * 4,
        out_shape=(jax.ShapeDtypeStruct((rows, cols), F32),) * 4, compiler_params=_params(("arbitrary",)),
    )(w2, m2, v2, o2, g3)
    return tuple(t.reshape(shape) for t in outs)


GATHERED = (("w_in", 2), ("w_uq", 2), ("w_ukv", 2), ("w_o_attn", 2), ("w_pw2", 2), ("w_out", 1), ("w_gu", 2), ("w_down", 1))


def _from_chunks(chunks, axis):
    _, a, bb = chunks.shape
    if axis == 2:
        return jnp.transpose(chunks, (1, 0, 2)).reshape(a, N_DEV * bb)
    return chunks.reshape(N_DEV * a, bb)


def _to_chunks(full, axis):
    a, bb = full.shape
    if axis == 2:
        return jnp.transpose(full.reshape(a, N_DEV, bb // N_DEV), (1, 0, 2)).astype(BF16)
    return full.reshape(N_DEV, a // N_DEV, bb).astype(BF16)


def _swap_halves(t):
    half = QK_ROPE // 2
    return jnp.concatenate([t[..., half:], t[..., :half]], axis=-1)


def _t(w):
    return jnp.swapaxes(w, -1, -2)


def _pad_rows(t, mult=8):
    return jnp.pad(t, ((0, -t.shape[0] % mult), (0, 0)))


def _pad_last(t, width):
    return jnp.pad(t, ((0, 0),) * (t.ndim - 1) + ((0, width - t.shape[-1]),))


def kernel(x, c, positions, w_ada, b_ada, g_mix, w_in, g_q, w_uq, g_kv, w_ukv, w_o_attn, w_dw, b_dw, g_cn, b_cn, w_pw2, w_out, g_ffn, w_gu, w_down, g_final, loss_target, m_w_ada, m_b_ada, m_g_mix, m_w_in, m_g_q, m_w_uq, m_g_kv, m_w_ukv, m_w_o_attn, m_w_dw, m_b_dw, m_g_cn, m_b_cn, m_w_pw2, m_w_out, m_g_ffn, m_w_gu, m_w_down, m_g_final, v_w_ada, v_b_ada, v_g_mix, v_w_in, v_g_q, v_w_uq, v_g_kv, v_w_ukv, v_w_o_attn, v_w_dw, v_b_dw, v_g_cn, v_b_cn, v_w_pw2, v_w_out, v_g_ffn, v_w_gu, v_w_down, v_g_final):
    weights = dict(w_ada=w_ada, b_ada=b_ada, g_mix=g_mix, w_in=w_in, g_q=g_q, w_uq=w_uq, g_kv=g_kv, w_ukv=w_ukv,
                   w_o_attn=w_o_attn, w_dw=w_dw, b_dw=b_dw, g_cn=g_cn, b_cn=b_cn, w_pw2=w_pw2, w_out=w_out, g_ffn=g_ffn,
                   w_gu=w_gu, w_down=w_down, g_final=g_final)
    mom_m = dict(w_ada=m_w_ada, b_ada=m_b_ada, g_mix=m_g_mix, w_in=m_w_in, g_q=m_g_q, w_uq=m_w_uq, g_kv=m_g_kv,
                 w_ukv=m_w_ukv, w_o_attn=m_w_o_attn, w_dw=m_w_dw, b_dw=m_b_dw, g_cn=m_g_cn, b_cn=m_b_cn, w_pw2=m_w_pw2,
                 w_out=m_w_out, g_ffn=m_g_ffn, w_gu=m_w_gu, w_down=m_w_down, g_final=m_g_final)
    mom_v = dict(w_ada=v_w_ada, b_ada=v_b_ada, g_mix=v_g_mix, w_in=v_w_in, g_q=v_g_q, w_uq=v_w_uq, g_kv=v_g_kv,
                 w_ukv=v_w_ukv, w_o_attn=v_w_o_attn, w_dw=v_w_dw, b_dw=v_b_dw, g_cn=v_g_cn, b_cn=v_b_cn, w_pw2=v_w_pw2,
                 w_out=v_w_out, g_ffn=v_g_ffn, w_gu=v_w_gu, w_down=v_w_down, g_final=v_g_final)
    order = list(weights)

    nb, s, d = x.shape
    nl = w_in.shape[0]
    ql, kl, cc = g_q.shape[1], g_kv.shape[1], g_cn.shape[1]
    h = N_HEADS
    qk = QK_NOPE + QK_ROPE
    xi, yi, ci = lax.axis_index("x"), lax.axis_index("y"), lax.axis_index("c")
    me = 4 * xi + 2 * yi + ci

    shards = [weights[n].astype(BF16) for n, _ in GATHERED]
    gathered = _all_gather([t[0] for t in shards], vmem=False, name="weight_all_gather")
    n_dw = w_dw.shape[2]
    dw_rows = jnp.pad(w_dw, ((0, 0), (0, HALO - CONV_W), (0, LANES - n_dw))).reshape(nl * HALO, LANES)
    c_all, dw_all = _all_gather([_pad_rows(c), dw_rows], vmem=True, name="cond_all_gather")
    c_full = c_all[:, :nb].reshape(N_DEV * nb, d)
    w_dw_full = jnp.transpose(dw_all.reshape(N_DEV, nl, HALO, LANES)[..., :n_dw], (1, 2, 0, 3)).reshape(nl, HALO, cc)

    o_kr, o_glu, o_gate = ql + kl, ql + kl + QK_ROPE, ql + kl + QK_ROPE + 2 * cc

    def layouts(chunks):
        full = {n: _from_chunks(t, ax) for (n, ax), t in zip(GATHERED, chunks)}
        wi = full["w_in"]
        w_kr = wi[:, o_kr:o_glu]
        wq = full["w_uq"].reshape(ql, h, qk)
        wkv = full["w_ukv"].reshape(kl, h, QK_NOPE + V_HEAD)
        w_o = jnp.pad(full["w_o_attn"].reshape(h, V_HEAD, d), ((0, 0), (0, HEAD_PAD - V_HEAD), (0, 0)))
        return dict(
            lat=jnp.concatenate([wi[:, :o_kr], jnp.zeros((d, QK_NOPE), BF16), w_kr, _swap_halves(w_kr)], axis=1),
            glu=wi[:, o_glu:o_gate], gate=wi[:, o_gate:],
            q=jnp.concatenate([wq, _swap_halves(wq[..., QK_NOPE:])], axis=-1).reshape(ql, h * HEAD_PAD),
            k=_pad_last(wkv[..., :QK_NOPE], HEAD_PAD).reshape(kl, h * HEAD_PAD),
            v=_pad_last(wkv[..., QK_NOPE:], HEAD_PAD).reshape(kl, h * HEAD_PAD),
            o=w_o.reshape(h * HEAD_PAD, d), pw=full["w_pw2"], out=full["w_out"], gu=full["w_gu"], down=full["w_down"])

    n_ada = w_ada.shape[2]
    b_cols = lax.dynamic_slice_in_dim(b_ada, me * n_ada, n_ada, axis=1).reshape(nl, 1, n_ada)
    mod_part = _ada_fwd(c_full, w_ada, b_cols)
    (mod_all,) = _all_gather([mod_part.reshape(nl * N_DEV * nb, n_ada)], vmem=True, name="mod_all_gather")
    mod_all = jnp.transpose(mod_all.reshape(N_DEV, nl, N_DEV * nb, n_ada), (1, 2, 0, 3)).reshape(nl, N_DEV * nb, 6 * d)
    mod = lax.dynamic_slice_in_dim(mod_all, me * nb, nb, axis=1).reshape(nl, nb, 6, 1, d)

    inv_freq = ROPE_THETA ** (-jnp.arange(0, QK_ROPE, 2, dtype=F32) / QK_ROPE)
    zeros = lambda n: jnp.zeros((n,), F32)
    freq_row = jnp.concatenate([zeros(QK_NOPE), inv_freq, inv_freq, zeros(HEAD_PAD - qk)]).reshape(1, -1)
    ones = jnp.ones((QK_ROPE // 2,), F32)
    sign_row = jnp.concatenate([zeros(QK_NOPE), -ones, ones, zeros(HEAD_PAD - qk)]).reshape(1, -1)
    cos_t, sin_t = _rope_tables(positions.astype(F32).reshape(nb, s, 1), freq_row, sign_row)

    row = lambda t, l: t[l].reshape(1, -1)

    saved, wts = [], []
    xc = x
    for l in range(nl):
        wl = layouts(gathered)
        wts.append(wl)
        lat, glu, gate, qh, kh, vh = _mix_pre_fwd(xc, mod[l], cos_t, sin_t, row(g_mix, l), row(g_q, l), row(g_kv, l),
                                                  wl["lat"], wl["glu"], wl["gate"], wl["q"], wl["k"], wl["v"])
        o, lse, gathered = _flash_fwd(qh, kh, vh, gather=[t[l + 1] for t in shards] if l + 1 < nl else ())
        u, y_conv = _conv_fwd(glu, w_dw_full[l], row(b_dw, l), row(g_cn, l), row(b_cn, l), wl["pw"])
        x2, y_attn, o1 = _mix_post_fwd(xc, mod[l], o, y_conv, gate, wl["o"], wl["out"])
        x3, gu, o2 = _ffn_fwd(x2, mod[l], row(g_ffn, l), wl["gu"], wl["down"])
        saved.append(dict(x=xc, lat=lat, glu=glu, gate=gate, q=qh, k=kh, v=vh, o=o, lse=lse, u=u, y_conv=y_conv, x2=x2,
                          y_attn=y_attn, o1=o1, gu=gu, o2=o2))
        xc = x3

    dx, loss_part, dgf_part = _loss_bwd(xc, loss_target, g_final.reshape(1, d))
    loss = lax.psum(0.5 / d * jnp.sum(loss_part), AXES)

    small_rows, dw_taps, dmod, own, got = [None] * nl, [None] * nl, [None] * nl, [None] * nl, [None] * nl
    pending = ()
    for l in reversed(range(nl)):
        sv, wl, gw = saved[l], wts[l], {}
        dx2, do2, act, dgu, h2, dmod2, dgffn = _ffn_bwd(dx, sv["x2"], mod[l], row(g_ffn, l), sv["gu"], sv["o2"],
                                                       _t(wl["gu"]), _t(wl["down"]))
        gw["w_gu"] = _matmul_tn(h2, dgu, "grad_w_gu")
        gw["w_down"] = _matmul_tn(act, do2, "grad_w_down")
        do1, dya, dyc, dgate, yv, do_h, dgt1 = _mix_post_bwd(dx2, mod[l], sv["o1"], sv["y_attn"], sv["y_conv"], sv["gate"],
                                                           _t(wl["out"]), _t(wl["o"]))
        gw["w_out"] = _matmul_tn(yv, do1, "grad_w_out")
        dwo = _matmul_tn(sv["o"], dya, "grad_w_o")
        gw["w_o_attn"] = dwo.reshape(h, HEAD_PAD, d)[:, :V_HEAD].reshape(h * V_HEAD, d)
        dglu, s_act, ddw, csmall = _conv_bwd(dyc, sv["u"], sv["glu"], w_dw_full[l], row(g_cn, l), row(b_cn, l), _t(wl["pw"]))
        gw["w_pw2"] = _matmul_tn(s_act, dyc, "grad_w_pw2")
        dq, dk, dv, arrived = _flash_bwd(sv["q"], sv["k"], sv["v"], sv["o"], sv["lse"], do_h, scatter=pending)
        if pending:
            got[l + 1] = arrived
        dx, dlat, dqr, qn, kvn, h1, dmod1, dgm, dgq, dgkv = _mix_pre_bwd(
            sv["x"], dx2, mod[l], cos_t, sin_t, row(g_mix, l), row(g_q, l), row(g_kv, l), sv["lat"], dq, dk, dv, dglu,
            dgate, _t(wl["lat"]), _t(wl["glu"]), _t(wl["gate"]), _t(wl["q"]), _t(wl["k"]), _t(wl["v"]))
        dwl = _matmul_tn(h1, dlat, "grad_w_lat")
        dwg = _matmul_tn(h1, dglu, "grad_w_glu")
        dwt = _matmul_tn(h1, dgate, "grad_w_gate")
        kr0 = o_kr + QK_NOPE
        dkr = dwl[:, kr0:kr0 + QK_ROPE] + _swap_halves(dwl[:, kr0 + QK_ROPE:])
        gw["w_in"] = jnp.concatenate([dwl[:, :o_kr], dkr, dwg, dwt], axis=1)
        dwq = _matmul_tn(qn, dqr, "grad_w_q").reshape(ql, h, HEAD_PAD)
        dwq = jnp.concatenate([dwq[..., :QK_NOPE], dwq[..., QK_NOPE:qk] + _swap_halves(dwq[..., qk:])], axis=-1)
        gw["w_uq"] = dwq.reshape(ql, h * qk)
        dwk = _matmul_tn(kvn, dk, "grad_w_k").reshape(kl, h, HEAD_PAD)
        dwv = _matmul_tn(kvn, dv, "grad_w_v").reshape(kl, h, HEAD_PAD)
        gw["w_ukv"] = jnp.concatenate([dwk[..., :QK_NOPE], dwv[..., :V_HEAD]], axis=-1).reshape(kl, h * (QK_NOPE + V_HEAD))
        pending = [_to_chunks(gw[n], ax) for n, ax in GATHERED]
        own[l] = [lax.dynamic_index_in_dim(t, me, axis=0, keepdims=False) for t in pending]
        dmod[l] =jnp.concatenate([dmod1[:, :, 0], dgt1[:, :, 0], dmod2[:, :, 0]], axis=1).reshape(nb, 6 * d)
        bsum = lambda t: jnp.sum(t, axis=0).reshape(1, -1)
        cs = jnp.sum(csmall, axis=0)[:, 0]
        small_rows[l] = jnp.concatenate([bsum(dgm), bsum(dgq), bsum(dgkv), cs[2:3], cs[0:1], cs[1:2], bsum(dgffn)], axis=1)
        dw_taps[l] = jnp.sum(ddw, axis=0)
    grad_x = dx

    dmod_rows = _pad_rows(jnp.stack(dmod).reshape(nl * nb, 6 * d))
    (dmod_all,) = _all_gather([dmod_rows], vmem=True, name="dmod_all_gather")
    dmod_full = jnp.transpose(dmod_all[:, :nl * nb].reshape(N_DEV, nl, nb, 6 * d), (1, 0, 2, 3)).reshape(nl, N_DEV * nb, 6 * d)
    dmod_cols = lax.dynamic_slice_in_dim(dmod_full, me * n_ada, n_ada, axis=2)
    grad_w_ada, grad_b_ada = _ada_bwd(c_full, dmod_cols, dmod_full)
    grads = {"w_ada": grad_w_ada, "b_ada": grad_b_ada.reshape(nl, 6 * d)}

    widths = (d, ql, kl, cc, cc, cc, d)
    wsum = sum(widths)
    final_row = _pad_last(jnp.sum(dgf_part, axis=0).reshape(1, d), wsum)
    small2d = _pad_rows(jnp.concatenate(small_rows + [final_row], axis=0))
    taps2d = jnp.concatenate(dw_taps, axis=0)
    small_all, taps_all = _all_gather([small2d, taps2d], vmem=True, name="small_grad_all_gather")
    small_sum, taps_sum = _sum_devices(small_all), _sum_devices(taps_all)
    off = 0
    for n, wdt in zip(("g_mix", "g_q", "g_kv", "b_dw", "g_cn", "b_cn", "g_ffn"), widths):
        grads[n] = small_sum[:nl, off:off + wdt]
        off += wdt
    grads["g_final"] = small_sum[nl, :d]
    taps = taps_sum.reshape(nl, HALO, cc)[:, :CONV_W]
    grads["w_dw"] = lax.dynamic_slice_in_dim(taps, me * n_dw, n_dw, axis=2)

    delta, new_m, new_v = {}, {}, {}
    for n in order:
        if n in grads:
            delta[n], new_m[n], new_v[n] = _adamw(weights[n], grads[n], mom_m[n], mom_v[n], "adamw_" + n)

    got[0] = _grad_scatter(pending)
    for a, (n, _) in enumerate(GATHERED):
        own_n = jnp.stack([own[l][a] for l in range(nl)])
        got_n = jnp.stack([got[l][a] for l in range(nl)], axis=1)
        grads[n], delta[n], new_m[n], new_v[n] = _adamw_reduce(weights[n], mom_m[n], mom_v[n], own_n, got_n, "adamw_" + n)

    return (loss, grad_x, *[grads[n] for n in order], *[delta[n] for n in order], *[new_m[n] for n in order],
            *[new_v[n] for n in order])
```

```python
import jax
import jax.numpy as jnp
from jax import lax
from jax.experimental import pallas as pl
from jax.experimental.pallas import tpu as pltpu

F32, BF16 = jnp.float32, jnp.bfloat16
MESH = pl.DeviceIdType.MESH
AXES = ("x", "y", "c")
N_DEV = 8

N_HEADS = 8
QK_NOPE = 64
QK_ROPE = 32
V_HEAD = 64
HEAD_PAD = 128
CONV_W = 31
HALO = 32
EPS = 1e-6
ROPE_THETA = 10000.0
NEG_INF = -1e30
ATTN_SCALE = (QK_NOPE + QK_ROPE) ** -0.5

ADAM_LR, ADAM_B1, ADAM_B2, ADAM_EPS, ADAM_WD, ADAM_STEP = 0.001, 0.9, 0.999, 1e-08, 0.01, 10

LANES = 128
VMEM_LIMIT = 60 * 1024 * 1024


def _params(sem=None):
    return pltpu.CompilerParams(dimension_semantics=sem, vmem_limit_bytes=VMEM_LIMIT)


def _pick(n, cap):
    if n <= cap:
        return n
    best = None
    for d in range(LANES, cap + 1, LANES):
        if n % d == 0:
            best = d
    assert best is not None, (n, cap)
    return best


def _div_tile(n, cap, mult):
    best = None
    for d in range(mult, min(n, cap) + 1, mult):
        if n % d == 0:
            best = d
    assert best is not None, (n, cap, mult)
    return best


def _row_tile(s, cap=512):
    return cap if s % cap == 0 and s >= 2 * cap else s // 2


def _sig(v):
    return 1.0 / (1.0 + jnp.exp(-v))


def _rstd(v):
    return lax.rsqrt(jnp.mean(v * v, axis=-1, keepdims=True) + EPS)


def _dot(a, b):
    return jnp.dot(a, b, preferred_element_type=F32)


def _dot_nt(a, b):
    return lax.dot_general(a, b, (((1,), (1,)), ((), ())), preferred_element_type=F32)


def _dot_tn(a, b):
    return lax.dot_general(a, b, (((0,), (0,)), ((), ())), preferred_element_type=F32)


def _rope(v, cos_t, sin_t):
    return v * cos_t + pltpu.roll(v, HEAD_PAD - QK_ROPE, 1) * sin_t


def _rope_bwd(dv, cos_t, sin_t):
    return dv * cos_t + pltpu.roll(dv * sin_t, QK_ROPE, 1)


def _const(shape):
    n = len(shape)
    return pl.BlockSpec(shape, lambda *_: (0,) * n, pipeline_mode=pl.Buffered(1))


def _rows(tm, w):
    return pl.BlockSpec((1, tm, w), lambda b, s: (b, s, 0))


def _per_b(r, w):
    return pl.BlockSpec((1, r, 1, w), lambda b, s: (b, 0, 0, 0))


def _all_gather(arrs, vmem, name):
    n = len(arrs)
    space = pltpu.VMEM if vmem else pl.ANY

    def body(*refs):
        x_refs, out_refs = refs[:n], refs[n:2 * n]
        send_sems, recv_sems, local_sems = refs[2 * n:]
        x_, y_, c_ = lax.axis_index("x"), lax.axis_index("y"), lax.axis_index("c")
        me, sibling = (x_, y_, c_), (x_, y_, 1 - c_)
        chips = [(1 - x_, y_), (x_, 1 - y_), (1 - x_, 1 - y_)]

        def copy(a, k, block, to, own=False):
            px, py, pc = block
            slot = out_refs[a].at[4 * px + 2 * py + pc]
            return pltpu.make_async_remote_copy(
                src_ref=x_refs[a] if own else slot, dst_ref=slot, send_sem=send_sems.at[k * n + a],
                recv_sem=recv_sems.at[k * n + a], device_id=to, device_id_type=MESH)

        mine = [pltpu.make_async_copy(x_refs[a], out_refs[a].at[4 * x_ + 2 * y_ + c_], local_sems.at[a]) for a in range(n)]
        sent = []
        for a in range(n):
            mine[a].start()
            sent.append(copy(a, 0, me, sibling, own=True))
            sent += [copy(a, 1 + j, me, (*chip, c_), own=True) for j, chip in enumerate(chips)]
        for cp in sent:
            cp.start()
        for j, chip in enumerate(chips):
            for a in range(n):
                copy(a, 1 + j, (*chip, c_), me).wait_recv()
                passed = copy(a, 4 + j, (*chip, c_), sibling)
                passed.start()
                sent.append(passed)
        for a in range(n):
            copy(a, 0, sibling, me).wait_recv()
            for j, chip in enumerate(chips):
                copy(a, 4 + j, (*chip, 1 - c_), me).wait_recv()
        for cp in sent:
            cp.wait_send()
        for cp in mine:
            cp.wait()

    return pl.pallas_call(
        body, name=name,
        out_shape=[jax.ShapeDtypeStruct((N_DEV,) + t.shape, t.dtype) for t in arrs],
        in_specs=[pl.BlockSpec(memory_space=space)] * n, out_specs=[pl.BlockSpec(memory_space=space)] * n,
        scratch_shapes=[pltpu.SemaphoreType.DMA((7 * n,)), pltpu.SemaphoreType.DMA((7 * n,)), pltpu.SemaphoreType.DMA((n,))],
        compiler_params=pltpu.CompilerParams(vmem_limit_bytes=VMEM_LIMIT),
    )(*arrs)


FLIPS = tuple((fx, fy, fc) for fx in (0, 1) for fy in (0, 1) for fc in (0, 1))[1:]


class _DirectExchange:
    def __init__(self, kind, in_refs, out_refs, send_sems, recv_sems, local_sems):
        n = len(in_refs)
        x_, y_, c_ = lax.axis_index("x"), lax.axis_index("y"), lax.axis_index("c")
        me = 4 * x_ + 2 * y_ + c_
        self.copies, self.local = [], []
        for r, (fx, fy, fc) in enumerate(FLIPS):
            px, py, pc = (1 - x_ if fx else x_), (1 - y_ if fy else y_), (1 - c_ if fc else c_)
            for a in range(n):
                src = in_refs[a] if kind == "gather" else in_refs[a].at[4 * px + 2 * py + pc]
                dst = out_refs[a].at[me] if kind == "gather" else out_refs[a].at[r]
                self.copies.append(pltpu.make_async_remote_copy(
                    src_ref=src, dst_ref=dst, send_sem=send_sems.at[r * n + a], recv_sem=recv_sems.at[r * n + a],
                    device_id=(px, py, pc), device_id_type=MESH))
        if kind == "gather":
            self.local = [pltpu.make_async_copy(in_refs[a], out_refs[a].at[me], local_sems.at[a]) for a in range(n)]

    def start(self):
        for cp in self.local + self.copies:
            cp.start()

    def wait(self):
        for cp in self.copies + self.local:
            cp.wait()

    @staticmethod
    def out_shapes(kind, arrs):
        if kind == "gather":
            return [jax.ShapeDtypeStruct((N_DEV,) + t.shape, t.dtype) for t in arrs]
        return [jax.ShapeDtypeStruct((N_DEV - 1,) + t.shape[1:], t.dtype) for t in arrs]

    @staticmethod
    def scratch(n):
        return [pltpu.SemaphoreType.DMA((7 * n,)), pltpu.SemaphoreType.DMA((7 * n,)), pltpu.SemaphoreType.DMA((n,))]


def _grad_scatter(gs):
    n = len(gs)

    def body(*refs):
        ex = _DirectExchange("scatter", refs[:n], refs[n:2 * n], *refs[2 * n:])
        ex.start()
        ex.wait()

    return pl.pallas_call(
        body, name="grad_scatter", out_shape=_DirectExchange.out_shapes("scatter", gs),
        in_specs=[pl.BlockSpec(memory_space=pl.ANY)] * n, out_specs=[pl.BlockSpec(memory_space=pl.ANY)] * n,
        scratch_shapes=_DirectExchange.scratch(n),
    )(*gs)


def _rows_2d(t, lead):
    return t.reshape((lead, -1, t.shape[-1]) if lead else (-1, t.shape[-1]))


def _grad_row_tile(rows, cols):
    return _div_tile(rows, max(16, (1 << 18) // cols), 16)


def _sum_devices(g):
    _, m, n = g.shape

    def body(g_ref, o_ref):
        s = g_ref[0]
        for j in range(1, N_DEV):
            s = s + g_ref[j]
        o_ref[...] = s

    return pl.pallas_call(body, name="small_grad_sum", out_shape=jax.ShapeDtypeStruct((m, n), F32),
                          compiler_params=pltpu.CompilerParams(vmem_limit_bytes=VMEM_LIMIT))(g)


def _ada_fwd(c_full, w_ada, b_cols):
    nl, d, n = w_ada.shape
    nb = c_full.shape[0]

    def body(c_ref, w_ref, b_ref, o_ref):
        cv = c_ref[...]
        act = cv * _sig(cv)
        o_ref[0] = jnp.dot(act, w_ref[0], preferred_element_type=F32, precision=lax.Precision.HIGHEST) + b_ref[0]

    return pl.pallas_call(
        body, name="ada_fwd", grid=(nl,), out_shape=jax.ShapeDtypeStruct((nl, nb, n), F32),
        in_specs=[pl.BlockSpec((nb, d), lambda l: (0, 0)), pl.BlockSpec((1, d, n), lambda l: (l, 0, 0)),
                  pl.BlockSpec((1, 1, n), lambda l: (l, 0, 0))],
        out_specs=pl.BlockSpec((1, nb, n), lambda l: (l, 0, 0)), compiler_params=_params(("arbitrary",)),
    )(c_full, w_ada, b_cols)


def _ada_bwd(c_full, dmod_cols, dmod_full):
    nl, nb, n = dmod_cols.shape
    d = c_full.shape[1]
    nfull = dmod_full.shape[2]

    def body(c_ref, dc_ref, df_ref, gw_ref, gb_ref):
        cv = c_ref[...]
        act = cv * _sig(cv)
        gw_ref[0] = lax.dot_general(act, dc_ref[0], (((0,), (0,)), ((), ())), preferred_element_type=F32,
                                    precision=lax.Precision.HIGHEST)
        gb_ref[0] = jnp.sum(df_ref[0], axis=0, keepdims=True)

    return pl.pallas_call(
        body, name="ada_bwd", grid=(nl,),
        out_shape=(jax.ShapeDtypeStruct((nl, d, n), F32), jax.ShapeDtypeStruct((nl, 1, nfull), F32)),
        in_specs=[pl.BlockSpec((nb, d), lambda l: (0, 0)), pl.BlockSpec((1, nb, n), lambda l: (l, 0, 0)),
                  pl.BlockSpec((1, nb, nfull), lambda l: (l, 0, 0))],
        out_specs=(pl.BlockSpec((1, d, n), lambda l: (l, 0, 0)), pl.BlockSpec((1, 1, nfull), lambda l: (l, 0, 0))),
        compiler_params=_params(("arbitrary",)),
    )(c_full, dmod_cols, dmod_full)


def _rope_tables(pos, freq_row, sign_row):
    b, s, _ = pos.shape
    tm = _row_tile(s)

    def body(p_ref, f_ref, g_ref, c_ref, s_ref):
        ang = p_ref[0] * f_ref[...]
        lane = lax.broadcasted_iota(jnp.int32, ang.shape, 1)
        c_ref[0] = jnp.where(lane < QK_NOPE, 1.0, jnp.where(lane < QK_NOPE + QK_ROPE, jnp.cos(ang), 0.0))
        s_ref[0] = g_ref[...] * jnp.sin(ang)

    return pl.pallas_call(
        body, name="rope_tables", grid=(b, s // tm),
        out_shape=(jax.ShapeDtypeStruct((b, s, HEAD_PAD), F32),) * 2,
        in_specs=[_rows(tm, 1), pl.BlockSpec((1, HEAD_PAD), lambda i, j: (0, 0)),
                  pl.BlockSpec((1, HEAD_PAD), lambda i, j: (0, 0))],
        out_specs=(_rows(tm, HEAD_PAD),) * 2, compiler_params=_params(("arbitrary", "arbitrary")),
    )(pos, freq_row, sign_row)


def _mix_pre_fwd(x, mod, cos_t, sin_t, g_mix, g_q, g_kv, w_lat, w_glu, w_gate, w_q, w_k, w_v):
    b, s, d = x.shape
    ql, kl = g_q.shape[1], g_kv.shape[1]
    wl, wg, wt = w_lat.shape[1], w_glu.shape[1], w_gate.shape[1]
    hw = N_HEADS * HEAD_PAD
    tm = _row_tile(s)

    def body(x_ref, mod_ref, cos_ref, sin_ref, gm_ref, gq_ref, gkv_ref, wlat_ref, wglu_ref, wgate_ref, wq_ref, wk_ref,
             wv_ref, lat_ref, glu_ref, gate_ref, q_ref, k_ref, v_ref):
        xf = x_ref[0]
        sh, sc = mod_ref[0, 0], mod_ref[0, 1]
        hb = ((xf * _rstd(xf) * gm_ref[...]) * (1.0 + sc) + sh).astype(BF16)
        glu_ref[0] = _dot(hb, wglu_ref[...]).astype(BF16)
        gate_ref[0] = _dot(hb, wgate_ref[...]).astype(BF16)
        lat = _dot(hb, wlat_ref[...]).astype(BF16)
        lat_ref[0] = lat
        latf = lat.astype(F32)
        q_lat, kv_lat, kr_sec = latf[:, :ql], latf[:, ql:ql + kl], latf[:, ql + kl:]
        qn = (q_lat * _rstd(q_lat) * gq_ref[...]).astype(BF16)
        kvn = (kv_lat * _rstd(kv_lat) * gkv_ref[...]).astype(BF16)
        cos_v, sin_v = cos_ref[0], sin_ref[0]
        lane = lax.broadcasted_iota(jnp.int32, kr_sec.shape, 1)
        kr = jnp.where(lane >= QK_NOPE, _rope(kr_sec, cos_v, sin_v), 0.0)
        q_all, k_all, v_all = _dot(qn, wq_ref[...]), _dot(kvn, wk_ref[...]), _dot(kvn, wv_ref[...])
        vlane = lax.broadcasted_iota(jnp.int32, v_all.shape, 1)
        v_ref[0] = jnp.where(vlane % HEAD_PAD == V_HEAD, 1.0, v_all).astype(BF16)
        for h in range(N_HEADS):
            cols = slice(h * HEAD_PAD, (h + 1) * HEAD_PAD)
            q_ref[0, :, cols] = (_rope(q_all[:, cols], cos_v, sin_v) * ATTN_SCALE).astype(BF16)
            k_ref[0, :, cols] = (k_all[:, cols] + kr).astype(BF16)

    hshape = jax.ShapeDtypeStruct((b, s, hw), BF16)
    return pl.pallas_call(
        body, name="mix_pre_fwd", grid=(b, s // tm),
        out_shape=(jax.ShapeDtypeStruct((b, s, wl), BF16), jax.ShapeDtypeStruct((b, s, wg), BF16),
                   jax.ShapeDtypeStruct((b, s, wt), BF16), hshape, hshape, hshape),
        in_specs=[_rows(tm, d), _per_b(6, d), _rows(tm, HEAD_PAD), _rows(tm, HEAD_PAD), _const((1, d)), _const((1, ql)),
                  _const((1, kl)), _const(w_lat.shape), _const(w_glu.shape), _const(w_gate.shape), _const(w_q.shape),
                  _const(w_k.shape), _const(w_v.shape)],
        out_specs=(_rows(tm, wl), _rows(tm, wg), _rows(tm, wt), _rows(tm, hw), _rows(tm, hw), _rows(tm, hw)),
        compiler_params=_params(("arbitrary", "arbitrary")),
    )(x, mod, cos_t, sin_t, g_mix, g_q, g_kv, w_lat, w_glu, w_gate, w_q, w_k, w_v)


def _causal_mask(tq, tk):
    return lax.broadcasted_iota(jnp.int32, (tq, tk), 0) >= lax.broadcasted_iota(jnp.int32, (tq, tk), 1)


def _riding_exchange(kind, n, refs, grid):
    if not n:
        return
    ex = _DirectExchange(kind, refs[:n], refs[n:2 * n], *refs[2 * n:])
    ids = [pl.program_id(a) for a in range(len(grid))]
    first, last = ids[0] == 0, ids[0] == grid[0] - 1
    for a in range(1, len(grid)):
        first, last = jnp.logical_and(first, ids[a] == 0), jnp.logical_and(last, ids[a] == grid[a] - 1)
    pl.when(first)(ex.start)
    return lambda: pl.when(last)(ex.wait)


def _flash_fwd(q, k, v, gather=()):
    b, s, hw = q.shape
    nh, hp = hw // HEAD_PAD, HEAD_PAD
    t = _row_tile(s)
    n = len(gather)
    grid = (b, nh, s // t)

    def body(q_ref, k_ref, v_ref, *rest):
        o_ref, lse_ref = rest[n], rest[n + 1]
        finish = _riding_exchange("gather", n, rest[:n] + rest[n + 2:], grid)
        i = pl.program_id(2)
        qv = q_ref[0]

        def step(j, carry, masked):
            m, acc = carry
            rows = pl.ds(pl.multiple_of(j * t, t), t)
            sc = _dot_nt(qv, k_ref[0, rows, :])
            if masked:
                sc = jnp.where(_causal_mask(t, t), sc, NEG_INF)
            m_new = jnp.maximum(m, jnp.max(sc, axis=-1, keepdims=True))
            p = jnp.exp((sc - m_new).astype(BF16))
            acc = jnp.exp(m - m_new) * acc + _dot(p, v_ref[0, rows, :])
            return m_new, acc

        init = (jnp.full((t, 1), NEG_INF, F32), jnp.zeros((t, hp), F32))
        carry = lax.fori_loop(0, i, lambda j, cr: step(j, cr, False), init)
        m, acc = step(i, carry, True)
        lane = lax.broadcasted_iota(jnp.int32, acc.shape, 1)
        l = jnp.sum(jnp.where(lane == V_HEAD, acc, 0.0), axis=-1, keepdims=True)
        o_ref[0] = (acc / l).astype(BF16)
        lse_ref[0, 0] = m + jnp.log(l)
        if finish:
            finish()

    tile = pl.BlockSpec((1, t, hp), lambda bb, hh, ii: (bb, ii, hh))
    full = pl.BlockSpec((1, s, hp), lambda bb, hh, ii: (bb, 0, hh))
    hbm = [pl.BlockSpec(memory_space=pl.ANY)] * n
    outs = pl.pallas_call(
        body, name="flash_fwd_gather" if n else "flash_fwd", grid=grid,
        out_shape=[jax.ShapeDtypeStruct((b, s, hw), BF16), jax.ShapeDtypeStruct((b, nh, s, 1), F32)]
        + _DirectExchange.out_shapes("gather", gather),
        in_specs=[tile, full, full] + hbm,
        out_specs=[tile, pl.BlockSpec((1, 1, t, 1), lambda bb, hh, ii: (bb, hh, ii, 0))] + hbm,
        scratch_shapes=_DirectExchange.scratch(n) if n else [],
        compiler_params=_params(("arbitrary", "arbitrary", "arbitrary")),
    )(q, k, v, *gather)
    return outs[0], outs[1], outs[2:]


def _halo_prev(tm, w):
    r = tm // HALO
    return pl.BlockSpec((1, HALO, w), lambda b, s: (b, jnp.maximum(s * r - 1, 0), 0))


def _halo_next(tm, w, n_tiles):
    r = tm // HALO
    return pl.BlockSpec((1, HALO, w), lambda b, s: (b, jnp.minimum((s + 1) * r, n_tiles * r - 1), 0))


def _conv_rows(cc):
    return max(8, 16 * 8 * LANES // cc)


def _glu(v, cc):
    a, g = v[:, :cc].astype(F32), v[:, cc:].astype(F32)
    return a * _sig(g)


def _conv_fwd(glu, w_dw, b_dw, g_cn, b_cn, w_pw2):
    b, s, w2 = glu.shape
    cc = w2 // 2
    d = w_pw2.shape[1]
    tm = _row_tile(s)

    rc = _conv_rows(cc)

    def body(cur_ref, prev_ref, w_ref, bdw_ref, g_ref, bcn_ref, wp_ref, u_ref, y_ref, ext, u_all):
        first = pl.program_id(1) == 0
        ext[pl.ds(0, HALO), :] = jnp.where(first, 0.0, _glu(prev_ref[0], cc))
        ext[pl.ds(HALO, tm), :] = _glu(cur_ref[0], cc)
        for c0 in range(0, tm, rc):
            acc = jnp.zeros((rc, cc), F32) + bdw_ref[...]
            for kk in range(CONV_W):
                acc = acc + w_ref[pl.ds(kk, 1), :] * ext[pl.ds(c0 + HALO - CONV_W + 1 + kk, rc), :]
            u_all[pl.ds(c0, rc), :] = acc
        ub = u_all[...].astype(BF16)
        u_ref[0] = ub
        uf = ub.astype(F32)
        mu = jnp.mean(uf, axis=-1, keepdims=True)
        uc = uf - mu
        ln = uc * lax.rsqrt(jnp.mean(uc * uc, axis=-1, keepdims=True) + EPS) * g_ref[...] + bcn_ref[...]
        y_ref[0] = _dot((ln * _sig(ln)).astype(BF16), wp_ref[...]).astype(BF16)

    return pl.pallas_call(
        body, name="conv_fwd", grid=(b, s // tm),
        out_shape=(jax.ShapeDtypeStruct((b, s, cc), BF16), jax.ShapeDtypeStruct((b, s, d), BF16)),
        in_specs=[_rows(tm, w2), _halo_prev(tm, w2), _const(w_dw.shape), _const((1, cc)), _const((1, cc)), _const((1, cc)),
                  _const(w_pw2.shape)],
        out_specs=(_rows(tm, cc), _rows(tm, d)),
        scratch_shapes=[pltpu.VMEM((tm + HALO, cc), F32), pltpu.VMEM((tm, cc), F32)],
        compiler_params=_params(("arbitrary", "arbitrary")),
    )(glu, glu, w_dw, b_dw, g_cn, b_cn, w_pw2)


def _mix_post_fwd(x, mod, o, y_conv, gate, w_o, w_out):
    b, s, d = x.shape
    hw = o.shape[2]
    tm = _row_tile(s)

    def body(x_ref, mod_ref, o_ref, yc_ref, gate_ref, wo_ref, wout_ref, x2_ref, ya_ref, o1_ref):
        yab = _dot(o_ref[0], wo_ref[...]).astype(BF16)
        ya_ref[0] = yab
        gv = gate_ref[0]
        y = _sig(gv[:, :d].astype(F32)) * yab.astype(F32) + _sig(gv[:, d:].astype(F32)) * yc_ref[0].astype(F32)
        o1 = _dot(y.astype(BF16), wout_ref[...])
        o1_ref[0] = o1.astype(BF16)
        x2_ref[0] = x_ref[0] + mod_ref[0, 2] * o1

    return pl.pallas_call(
        body, name="mix_post_fwd", grid=(b, s // tm),
        out_shape=(jax.ShapeDtypeStruct((b, s, d), F32), jax.ShapeDtypeStruct((b, s, d), BF16),
                   jax.ShapeDtypeStruct((b, s, d), BF16)),
        in_specs=[_rows(tm, d), _per_b(6, d), _rows(tm, hw), _rows(tm, d), _rows(tm, 2 * d), _const(w_o.shape),
                  _const(w_out.shape)],
        out_specs=(_rows(tm, d), _rows(tm, d), _rows(tm, d)),
        compiler_params=_params(("arbitrary", "arbitrary")),
    )(x, mod, o, y_conv, gate, w_o, w_out)


def _ffn_fwd(x2, mod, g_ffn, w_gu, w_down):
    b, s, d = x2.shape
    f = w_down.shape[0]
    fc = _pick(f, 512)
    tm = _row_tile(s)

    def body(x_ref, mod_ref, g_ref, wgu_ref, wdn_ref, x3_ref, gu_ref, o2_ref):
        xf = x_ref[0]
        hb = ((xf * _rstd(xf) * g_ref[...]) * (1.0 + mod_ref[0, 4]) + mod_ref[0, 3]).astype(BF16)
        o2 = jnp.zeros((tm, d), F32)
        for c0 in range(0, f, fc):
            gb = _dot(hb, wgu_ref[:, c0:c0 + fc]).astype(BF16)
            ub = _dot(hb, wgu_ref[:, f + c0:f + c0 + fc]).astype(BF16)
            gu_ref[0, :, c0:c0 + fc] = gb
            gu_ref[0, :, f + c0:f + c0 + fc] = ub
            gf = gb.astype(F32)
            act = (gf * _sig(gf) * ub.astype(F32)).astype(BF16)
            o2 = o2 + _dot(act, wdn_ref[c0:c0 + fc, :])
        o2_ref[0] = o2.astype(BF16)
        x3_ref[0] = xf + mod_ref[0, 5] * o2

    return pl.pallas_call(
        body, name="ffn_fwd", grid=(b, s // tm),
        out_shape=(jax.ShapeDtypeStruct((b, s, d), F32), jax.ShapeDtypeStruct((b, s, 2 * f), BF16),
                   jax.ShapeDtypeStruct((b, s, d), BF16)),
        in_specs=[_rows(tm, d), _per_b(6, d), _const((1, d)), _const(w_gu.shape), _const(w_down.shape)],
        out_specs=(_rows(tm, d), _rows(tm, 2 * f), _rows(tm, d)),
        compiler_params=_params(("arbitrary", "arbitrary")),
    )(x2, mod, g_ffn, w_gu, w_down)


def _zero_at_first_tile(*refs):
    @pl.when(pl.program_id(1) == 0)
    def _():
        for ref in refs:
            ref[...] = jnp.zeros_like(ref)


def _accumulate(ref, idx, val):
    ref[idx] = ref[idx] + val


def _colsum(v):
    return jnp.sum(v, axis=0, keepdims=True)


def _loss_bwd(x, target, g_final):
    b, s, d = x.shape
    tm = _row_tile(s)

    def body(x_ref, t_ref, g_ref, dx_ref, loss_ref, dg_ref):
        _zero_at_first_tile(loss_ref, dg_ref)
        xf = x_ref[0]
        r = _rstd(xf)
        xh = xf * r
        diff = xh * g_ref[...] - t_ref[0]
        _accumulate(loss_ref, (0, 0), _colsum(diff * diff))
        dy = diff * (1.0 / d)
        _accumulate(dg_ref, (0, 0), _colsum(dy * xh))
        dyg = dy * g_ref[...]
        dx_ref[0] = r * (dyg - xh * jnp.mean(dyg * xh, axis=-1, keepdims=True))

    return pl.pallas_call(
        body, name="loss_bwd", grid=(b, s // tm),
        out_shape=(jax.ShapeDtypeStruct((b, s, d), F32), jax.ShapeDtypeStruct((b, 1, 1, d), F32),
                   jax.ShapeDtypeStruct((b, 1, 1, d), F32)),
        in_specs=[_rows(tm, d), _rows(tm, d), _const((1, d))],
        out_specs=(_rows(tm, d), _per_b(1, d), _per_b(1, d)),
        compiler_params=_params(("arbitrary", "arbitrary")),
    )(x, target, g_final)


def _ffn_bwd(dx3, x2, mod, g_ffn, gu, o2, w_gu_t, w_down_t):
    b, s, d = x2.shape
    f = w_down_t.shape[1]
    fc = _pick(f, 512)
    tm = _row_tile(s, 256)

    def body(dx3_ref, x_ref, mod_ref, g_ref, gu_ref, o2_ref, wgut_ref, wdnt_ref, dx2_ref, do2_ref, act_ref, dgu_ref, h_ref,
             dmod_ref, dg_ref):
        _zero_at_first_tile(dmod_ref, dg_ref)
        dx3 = dx3_ref[0]
        sh, sc, gt = mod_ref[0, 3], mod_ref[0, 4], mod_ref[0, 5]
        do2 = (dx3 * gt).astype(BF16)
        do2_ref[0] = do2
        _accumulate(dmod_ref, (0, 2), _colsum(dx3 * o2_ref[0].astype(F32)))
        dh = jnp.zeros((tm, d), F32)
        for c0 in range(0, f, fc):
            gf = gu_ref[0, :, c0:c0 + fc].astype(F32)
            uf = gu_ref[0, :, f + c0:f + c0 + fc].astype(F32)
            sg = _sig(gf)
            silu = gf * sg
            act_ref[0, :, c0:c0 + fc] = (silu * uf).astype(BF16)
            dact = _dot(do2, wdnt_ref[:, c0:c0 + fc])
            dg = (dact * uf * (sg * (1.0 + gf * (1.0 - sg)))).astype(BF16)
            du = (dact * silu).astype(BF16)
            dgu_ref[0, :, c0:c0 + fc] = dg
            dgu_ref[0, :, f + c0:f + c0 + fc] = du
            dh = dh + _dot(dg, wgut_ref[c0:c0 + fc, :]) + _dot(du, wgut_ref[f + c0:f + c0 + fc, :])
        xf = x_ref[0]
        r = _rstd(xf)
        xh = xf * r
        n = xh * g_ref[...]
        h_ref[0] = (n * (1.0 + sc) + sh).astype(BF16)
        _accumulate(dmod_ref, (0, 0), _colsum(dh))
        _accumulate(dmod_ref, (0, 1), _colsum(dh * n))
        dn = dh * (1.0 + sc)
        _accumulate(dg_ref, (0, 0), _colsum(dn * xh))
        dyg = dn * g_ref[...]
        dx2_ref[0] = dx3 + r * (dyg - xh * jnp.mean(dyg * xh, axis=-1, keepdims=True))

    return pl.pallas_call(
        body, name="ffn_bwd", grid=(b, s // tm),
        out_shape=(jax.ShapeDtypeStruct((b, s, d), F32), jax.ShapeDtypeStruct((b, s, d), BF16),
                   jax.ShapeDtypeStruct((b, s, f), BF16), jax.ShapeDtypeStruct((b, s, 2 * f), BF16),
                   jax.ShapeDtypeStruct((b, s, d), BF16), jax.ShapeDtypeStruct((b, 3, 1, d), F32),
                   jax.ShapeDtypeStruct((b, 1, 1, d), F32)),
        in_specs=[_rows(tm, d), _rows(tm, d), _per_b(6, d), _const((1, d)), _rows(tm, 2 * f), _rows(tm, d),
                  _const(w_gu_t.shape), _const(w_down_t.shape)],
        out_specs=(_rows(tm, d), _rows(tm, d), _rows(tm, f), _rows(tm, 2 * f), _rows(tm, d), _per_b(3, d), _per_b(1, d)),
        compiler_params=_params(("arbitrary", "arbitrary")),
    )(dx3, x2, mod, g_ffn, gu, o2, w_gu_t, w_down_t)


def _mix_post_bwd(dx2, mod, o1, y_attn, y_conv, gate, w_out_t, w_o_t):
    b, s, d = dx2.shape
    hw = w_o_t.shape[1]
    tm = _row_tile(s)

    def body(dx_ref, mod_ref, o1_ref, ya_ref, yc_ref, gate_ref, woutt_ref, wot_ref, do1_ref, dya_ref, dyc_ref, dgate_ref,
             y_ref, do_ref, dgt_ref):
        _zero_at_first_tile(dgt_ref)
        dx = dx_ref[0]
        do1 = (dx * mod_ref[0, 2]).astype(BF16)
        do1_ref[0] = do1
        _accumulate(dgt_ref, (0, 0), _colsum(dx * o1_ref[0].astype(F32)))
        dy = _dot(do1, woutt_ref[...])
        gv = gate_ref[0]
        sa, sb = _sig(gv[:, :d].astype(F32)), _sig(gv[:, d:].astype(F32))
        ya, yc = ya_ref[0].astype(F32), yc_ref[0].astype(F32)
        y_ref[0] = (sa * ya + sb * yc).astype(BF16)
        dya = (dy * sa).astype(BF16)
        dya_ref[0] = dya
        dyc_ref[0] = (dy * sb).astype(BF16)
        dgate_ref[0, :, :d] = (dy * ya * sa * (1.0 - sa)).astype(BF16)
        dgate_ref[0, :, d:] = (dy * yc * sb * (1.0 - sb)).astype(BF16)
        do_ref[0] = _dot(dya, wot_ref[...]).astype(BF16)

    row = jax.ShapeDtypeStruct((b, s, d), BF16)
    return pl.pallas_call(
        body, name="mix_post_bwd", grid=(b, s // tm),
        out_shape=(row, row, row, jax.ShapeDtypeStruct((b, s, 2 * d), BF16), row,
                   jax.ShapeDtypeStruct((b, s, hw), BF16), jax.ShapeDtypeStruct((b, 1, 1, d), F32)),
        in_specs=[_rows(tm, d), _per_b(6, d), _rows(tm, d), _rows(tm, d), _rows(tm, d), _rows(tm, 2 * d),
                  _const(w_out_t.shape), _const(w_o_t.shape)],
        out_specs=(_rows(tm, d), _rows(tm, d), _rows(tm, d), _rows(tm, 2 * d), _rows(tm, d), _rows(tm, hw), _per_b(1, d)),
        compiler_params=_params(("arbitrary", "arbitrary")),
    )(dx2, mod, o1, y_attn, y_conv, gate, w_out_t, w_o_t)


def _conv_bwd(dyc, u, glu, w_dw, g_cn, b_cn, w_pw2_t):
    b, s, cc = u.shape
    d = dyc.shape[2]
    tm = _row_tile(s)
    nt = s // tm
    te = tm + HALO
    rc = _conv_rows(cc)

    def body(dyc_ref, dycn_ref, u_ref, un_ref, glu_ref, glup_ref, w_ref, g_ref, bcn_ref, wpt_ref, dglu_ref, s_ref, dw_ref,
             small_ref, du_ext, uin_ext, duin_all):
        _zero_at_first_tile(dw_ref, small_ref)
        st = pl.program_id(1)
        dy_all = jnp.concatenate([dyc_ref[0], dycn_ref[0]], axis=0)
        u_all = jnp.concatenate([u_ref[0], un_ref[0]], axis=0).astype(F32)
        ds = _dot(dy_all, wpt_ref[...])
        mu = jnp.mean(u_all, axis=-1, keepdims=True)
        uc = u_all - mu
        rstd = lax.rsqrt(jnp.mean(uc * uc, axis=-1, keepdims=True) + EPS)
        uh = uc * rstd
        ln = uh * g_ref[...] + bcn_ref[...]
        sg = _sig(ln)
        s_ref[0] = (ln * sg)[:tm].astype(BF16)
        dln = ds * (sg * (1.0 + ln * (1.0 - sg)))
        duh = dln * g_ref[...]
        du = rstd * (duh - jnp.mean(duh, axis=-1, keepdims=True) - uh * jnp.mean(duh * uh, axis=-1, keepdims=True))
        row = lax.broadcasted_iota(jnp.int32, (te, 1), 0)
        du = jnp.where(jnp.logical_and(st == nt - 1, row >= tm), 0.0, du)
        du_ext[...] = du
        du_cur = du[:tm]
        _accumulate(small_ref, (0, 0), _colsum((dln * uh)[:tm]))
        _accumulate(small_ref, (0, 1), _colsum(dln[:tm]))
        _accumulate(small_ref, (0, 2), _colsum(du_cur))
        uin_ext[pl.ds(0, HALO), :] = jnp.where(st == 0, 0.0, _glu(glup_ref[0], cc))
        gv = glu_ref[0]
        ga, gb = gv[:, :cc].astype(F32), gv[:, cc:].astype(F32)
        sgb = _sig(gb)
        uin_ext[pl.ds(HALO, tm), :] = ga * sgb
        for c0 in range(0, tm, rc):
            du_c = du_ext[pl.ds(c0, rc), :]
            acc = jnp.zeros((rc, cc), F32)
            for kk in range(CONV_W):
                acc = acc + w_ref[pl.ds(kk, 1), :] * du_ext[pl.ds(c0 + CONV_W - 1 - kk, rc), :]
                prod = du_c * uin_ext[pl.ds(c0 + HALO - CONV_W + 1 + kk, rc), :]
                taps = pl.ds(8 * kk, 8)
                dw_ref[0, taps, :] = dw_ref[0, taps, :] + jnp.sum(prod.reshape(rc // 8, 8, cc), axis=0)
            duin_all[pl.ds(c0, rc), :] = acc
        duin = duin_all[...]
        dglu_ref[0, :, :cc] = (duin * sgb).astype(BF16)
        dglu_ref[0, :, cc:] = (duin * ga * sgb * (1.0 - sgb)).astype(BF16)

    return pl.pallas_call(
        body, name="conv_bwd", grid=(b, nt),
        out_shape=(jax.ShapeDtypeStruct((b, s, 2 * cc), BF16), jax.ShapeDtypeStruct((b, s, cc), BF16),
                   jax.ShapeDtypeStruct((b, 8 * HALO, cc), F32), jax.ShapeDtypeStruct((b, 3, 1, cc), F32)),
        in_specs=[_rows(tm, d), _halo_next(tm, d, nt), _rows(tm, cc), _halo_next(tm, cc, nt), _rows(tm, 2 * cc),
                  _halo_prev(tm, 2 * cc), _const(w_dw.shape), _const((1, cc)), _const((1, cc)), _const(w_pw2_t.shape)],
        out_specs=(_rows(tm, 2 * cc), _rows(tm, cc), pl.BlockSpec((1, 8 * HALO, cc), lambda i, j: (i, 0, 0)),
                   _per_b(3, cc)),
        scratch_shapes=[pltpu.VMEM((te, cc), F32), pltpu.VMEM((te, cc), F32), pltpu.VMEM((tm, cc), F32)],
        compiler_params=_params(("arbitrary", "arbitrary")),
    )(dyc, dyc, u, u, glu, glu, w_dw, g_cn, b_cn, w_pw2_t)


def _flash_bwd(q, k, v, o, lse, do, scatter=()):
    b, s, hw = q.shape
    nh, hp = hw // HEAD_PAD, HEAD_PAD
    t = _row_tile(s)
    nt = s // t
    n = len(scatter)
    grid = (b, nh, nt)

    def to_row(sel, cols):
        return lax.dot_general(sel, cols, (((1,), (1,)), ((), ())), preferred_element_type=F32,
                               precision=lax.Precision.HIGHEST)

    def body(q_ref, k_ref, v_ref, o_ref, lse_ref, do_ref, *rest):
        dqt_ref, dk_ref, dv_ref = rest[n:n + 3]
        lse_row, delta_row = rest[2 * n + 3:2 * n + 5]
        finish = _riding_exchange("scatter", n, rest[:n] + rest[n + 3:2 * n + 3] + rest[2 * n + 5:], grid)
        j = pl.program_id(2)

        @pl.when(j == 0)
        def _():
            dqt_ref[...] = jnp.zeros_like(dqt_ref)
            first_lane = (lax.broadcasted_iota(jnp.int32, (8, hp), 1) == 0).astype(F32)
            for i in range(nt):
                rows = pl.ds(i * t, t)
                prod = do_ref[0, rows, :].astype(F32) * o_ref[0, rows, :].astype(F32)
                delta_row[i] = to_row(jnp.ones((8, hp), F32), prod)
                lse_row[i] = to_row(first_lane, jnp.broadcast_to(lse_ref[0, 0, rows, :], (t, hp)))

        kv, vv = k_ref[0], v_ref[0]
        kt = kv.T
        query_not_before_key = (lax.broadcasted_iota(jnp.int32, (t, t), 1) >= lax.broadcasted_iota(jnp.int32, (t, t), 0))

        def step(i, carry, masked):
            dk, dv = carry
            rows = pl.ds(pl.multiple_of(i * t, t), t)
            qv, dov = q_ref[0, rows, :], do_ref[0, rows, :]
            pt = jnp.exp((_dot_nt(kv, qv) - lse_row[i, 0:1, :]).astype(BF16))
            if masked:
                pt = jnp.where(query_not_before_key, pt, jnp.zeros((), BF16))
            dv = dv + _dot(pt, dov)
            dst = pt * (_dot_nt(vv, dov) - delta_row[i, 0:1, :]).astype(BF16)
            dk = dk + _dot(dst, qv)
            dqt_ref[0, i] = dqt_ref[0, i] + _dot(kt, dst)
            return dk, dv

        carry = step(j, (jnp.zeros((t, hp), F32), jnp.zeros((t, hp), F32)), True)
        dk, dv = lax.fori_loop(j + 1, nt, lambda i, cr: step(i, cr, False), carry)
        dk_ref[0] = dk.astype(BF16)
        dv_ref[0] = dv.astype(BF16)
        if finish:
            finish()

    tile = pl.BlockSpec((1, t, hp), lambda bb, hh, jj: (bb, jj, hh))
    full = pl.BlockSpec((1, s, hp), lambda bb, hh, jj: (bb, 0, hh))
    hbm = [pl.BlockSpec(memory_space=pl.ANY)] * n
    outs = pl.pallas_call(
        body, name="flash_bwd_scatter" if n else "flash_bwd", grid=grid,
        out_shape=[jax.ShapeDtypeStruct((b, nt, hw, t), F32), jax.ShapeDtypeStruct((b, s, hw), BF16),
                   jax.ShapeDtypeStruct((b, s, hw), BF16)] + _DirectExchange.out_shapes("scatter", scatter),
        in_specs=[full, tile, tile, full, pl.BlockSpec((1, 1, s, 1), lambda bb, hh, jj: (bb, hh, 0, 0)), full] + hbm,
        out_specs=[pl.BlockSpec((1, nt, hp, t), lambda bb, hh, jj: (bb, 0, hh, 0)), tile, tile] + hbm,
        scratch_shapes=[pltpu.VMEM((nt, 8, t), F32), pltpu.VMEM((nt, 8, t), F32)]
        + (_DirectExchange.scratch(n) if n else []),
        compiler_params=_params(("arbitrary", "arbitrary", "arbitrary")),
    )(q, k, v, o, lse, do, *scatter)
    return outs[0], outs[1], outs[2], outs[3:]


def _mix_pre_bwd(x, dx2, mod, cos_t, sin_t, g_mix, g_q, g_kv, lat, dq, dk, dv, dglu, dgate, w_lat_t, w_glu_t, w_gate_t,
                 w_q_t, w_k_t, w_v_t):
    b, s, d = x.shape
    ql, kl = g_q.shape[1], g_kv.shape[1]
    wl = lat.shape[2]
    hw = N_HEADS * HEAD_PAD
    tm = _row_tile(s)

    def body(x_ref, dx2_ref, mod_ref, cos_ref, sin_ref, gm_ref, gq_ref, gkv_ref, lat_ref, dq_ref, dk_ref, dv_ref, dglu_ref,
             dgate_ref, wlt_ref, wgt_ref, wtt_ref, wqt_ref, wkt_ref, wvt_ref, dx_ref, dlat_ref, dqr_ref, qn_ref, kvn_ref,
             h_ref, dmod_ref, dgm_ref, dgq_ref, dgkv_ref):
        _zero_at_first_tile(dmod_ref, dgm_ref, dgq_ref, dgkv_ref)
        cos_v, sin_v = cos_ref[0], sin_ref[0]
        latf = lat_ref[0].astype(F32)
        q_lat, kv_lat = latf[:, :ql], latf[:, ql:ql + kl]
        rq, rk = _rstd(q_lat), _rstd(kv_lat)
        qh, kh = q_lat * rq, kv_lat * rk
        qn_ref[0] = (qh * gq_ref[...]).astype(BF16)
        kvn_ref[0] = (kh * gkv_ref[...]).astype(BF16)
        dk_sum = jnp.zeros((tm, HEAD_PAD), F32)
        for h in range(N_HEADS):
            cols = slice(h * HEAD_PAD, (h + 1) * HEAD_PAD)
            dq_head = dq_ref[0, 0, cols, :].T
            dqr_ref[0, :, cols] = _rope_bwd(dq_head * ATTN_SCALE, cos_v, sin_v).astype(BF16)
            dk_sum = dk_sum + dk_ref[0, :, cols].astype(F32)
        dqn = _dot(dqr_ref[0], wqt_ref[...])
        dkvn = _dot(dk_ref[0], wkt_ref[...]) + _dot(dv_ref[0], wvt_ref[...])
        lane = lax.broadcasted_iota(jnp.int32, dk_sum.shape, 1)
        dkr = _rope_bwd(jnp.where(lane >= QK_NOPE, dk_sum, 0.0), cos_v, sin_v)
        _accumulate(dgq_ref, (0, 0), _colsum(dqn * qh))
        _accumulate(dgkv_ref, (0, 0), _colsum(dkvn * kh))
        dqg, dkg = dqn * gq_ref[...], dkvn * gkv_ref[...]
        dlat_ref[0, :, :ql] = (rq * (dqg - qh * jnp.mean(dqg * qh, axis=-1, keepdims=True))).astype(BF16)
        dlat_ref[0, :, ql:ql + kl] = (rk * (dkg - kh * jnp.mean(dkg * kh, axis=-1, keepdims=True))).astype(BF16)
        dlat_ref[0, :, ql + kl:] = dkr.astype(BF16)
        dh = _dot(dlat_ref[0], wlt_ref[...]) + _dot(dglu_ref[0], wgt_ref[...]) + _dot(dgate_ref[0], wtt_ref[...])
        sh, sc = mod_ref[0, 0], mod_ref[0, 1]
        xf = x_ref[0]
        r = _rstd(xf)
        xh = xf * r
        n = xh * gm_ref[...]
        h_ref[0] = (n * (1.0 + sc) + sh).astype(BF16)
        _accumulate(dmod_ref, (0, 0), _colsum(dh))
        _accumulate(dmod_ref, (0, 1), _colsum(dh * n))
        dn = dh * (1.0 + sc)
        _accumulate(dgm_ref, (0, 0), _colsum(dn * xh))
        dyg = dn * gm_ref[...]
        dx_ref[0] = dx2_ref[0] + r * (dyg - xh * jnp.mean(dyg * xh, axis=-1, keepdims=True))

    return pl.pallas_call(
        body, name="mix_pre_bwd", grid=(b, s // tm),
        out_shape=(jax.ShapeDtypeStruct((b, s, d), F32), jax.ShapeDtypeStruct((b, s, wl), BF16),
                   jax.ShapeDtypeStruct((b, s, hw), BF16), jax.ShapeDtypeStruct((b, s, ql), BF16),
                   jax.ShapeDtypeStruct((b, s, kl), BF16), jax.ShapeDtypeStruct((b, s, d), BF16),
                   jax.ShapeDtypeStruct((b, 2, 1, d), F32), jax.ShapeDtypeStruct((b, 1, 1, d), F32),
                   jax.ShapeDtypeStruct((b, 1, 1, ql), F32), jax.ShapeDtypeStruct((b, 1, 1, kl), F32)),
        in_specs=[_rows(tm, d), _rows(tm, d), _per_b(6, d), _rows(tm, HEAD_PAD), _rows(tm, HEAD_PAD), _const((1, d)),
                  _const((1, ql)), _const((1, kl)), _rows(tm, wl),
                  pl.BlockSpec((1, 1, hw, tm), lambda b, s: (b, s, 0, 0)), _rows(tm, hw), _rows(tm, hw),
                  _rows(tm, dglu.shape[2]), _rows(tm, 2 * d), _const(w_lat_t.shape), _const(w_glu_t.shape),
                  _const(w_gate_t.shape), _const(w_q_t.shape), _const(w_k_t.shape), _const(w_v_t.shape)],
        out_specs=(_rows(tm, d), _rows(tm, wl), _rows(tm, hw), _rows(tm, ql), _rows(tm, kl), _rows(tm, d), _per_b(2, d),
                   _per_b(1, d), _per_b(1, ql), _per_b(1, kl)),
        compiler_params=_params(("arbitrary", "arbitrary")),
    )(x, dx2, mod, cos_t, sin_t, g_mix, g_q, g_kv, lat, dq, dk, dv, dglu, dgate, w_lat_t, w_glu_t, w_gate_t, w_q_t, w_k_t,
      w_v_t)


def _matmul_tn(a, bm, name):
    b, s, kd = a.shape
    nd = bm.shape[2]
    tk, tn = _pick(kd, 1536), _pick(nd, 1536)
    ts = _row_tile(s, 2048)

    def body(a_ref, b_ref, o_ref):
        part = _dot_tn(a_ref[0], b_ref[0])
        first = jnp.logical_and(pl.program_id(2) == 0, pl.program_id(3) == 0)

        @pl.when(first)
        def _():
            o_ref[...] = part

        @pl.when(jnp.logical_not(first))
        def _():
            o_ref[...] = o_ref[...] + part

    return pl.pallas_call(
        body, name=name, grid=(kd // tk, nd // tn, b, s // ts),
        out_shape=jax.ShapeDtypeStruct((kd, nd), F32),
        in_specs=[pl.BlockSpec((1, ts, tk), lambda i, j, bb, ss: (bb, ss, i)),
                  pl.BlockSpec((1, ts, tn), lambda i, j, bb, ss: (bb, ss, j))],
        out_specs=pl.BlockSpec((tk, tn), lambda i, j, bb, ss: (i, j)),
        compiler_params=_params(("arbitrary",) * 4),
    )(a, bm)


def _adamw_update(w, g, m, v):
    nm = ADAM_B1 * m + (1.0 - ADAM_B1) * g
    nv = ADAM_B2 * v + (1.0 - ADAM_B2) * (g * g)
    delta = -ADAM_LR * ((nm / (1.0 - ADAM_B1 ** ADAM_STEP)) / (jnp.sqrt(nv / (1.0 - ADAM_B2 ** ADAM_STEP)) + ADAM_EPS)
                        + ADAM_WD * w)
    return delta, nm, nv


def _adamw(w, g, m, v, name):
    shape = w.shape
    cols = shape[-1]
    rows = w.size // cols
    w2, g2, m2, v2 = (t.reshape(rows, cols) for t in (w, g, m, v))
    tr = rows
    if rows * cols * 4 > (1 << 20):
        tr = _div_tile(rows, max(8, (1 << 18) // cols), 8)

    def body(w_ref, g_ref, m_ref, v_ref, d_ref, nm_ref, nv_ref):
        d_ref[...], nm_ref[...], nv_ref[...] = _adamw_update(w_ref[...], g_ref[...], m_ref[...], v_ref[...])

    spec = pl.BlockSpec((tr, cols), lambda i: (i, 0))
    outs = pl.pallas_call(
        body, name=name, grid=(rows // tr,), out_shape=(jax.ShapeDtypeStruct((rows, cols), F32),) * 3,
        in_specs=[spec] * 4, out_specs=(spec,) * 3, compiler_params=_params(("arbitrary",)),
    )(w2, g2, m2, v2)
    return tuple(t.reshape(shape) for t in outs)


def _adamw_reduce(w, m, v, own, got, name):
    shape = w.shape
    cols = shape[-1]
    w2, m2, v2, o2 = (_rows_2d(t, 0) for t in (w, m, v, own))
    g3 = _rows_2d(got, N_DEV - 1)
    rows = w2.shape[0]
    tr = _grad_row_tile(rows, cols)

    def body(w_ref, m_ref, v_ref, o_ref, r_ref, g_ref, d_ref, nm_ref, nv_ref):
        g = o_ref[...].astype(F32)
        for r in range(N_DEV - 1):
            g = g + r_ref[r].astype(F32)
        g_ref[...] = g
        d_ref[...], nm_ref[...], nv_ref[...] = _adamw_update(w_ref[...], g, m_ref[...], v_ref[...])

    spec = pl.BlockSpec((tr, cols), lambda i: (i, 0))
    outs = pl.pallas_call(
        body, name=name, grid=(rows // tr,),
        in_specs=[spec, spec, spec, spec, pl.BlockSpec((N_DEV - 1, tr, cols), lambda i: (0, i, 0))],
        out_specs=(spec,) * 4,
        out_shape=(jax.ShapeDtypeStruct((rows, cols), F32),) * 4, compiler_params=_params(("arbitrary",)),
    )(w2, m2, v2, o2, g3)
    return tuple(t.reshape(shape) for t in outs)


GATHERED = (("w_in", 2), ("w_uq", 2), ("w_ukv", 2), ("w_o_attn", 2), ("w_pw2", 2), ("w_out", 1), ("w_gu", 2), ("w_down", 1))
PRE, POST = (0, 1, 2), (3, 4, 5, 6, 7)


def _from_chunks(chunks, axis):
    _, a, bb = chunks.shape
    if axis == 2:
        return jnp.transpose(chunks, (1, 0, 2)).reshape(a, N_DEV * bb)
    return chunks.reshape(N_DEV * a, bb)


def _to_chunks(full, axis):
    a, bb = full.shape
    if axis == 2:
        return jnp.transpose(full.reshape(a, N_DEV, bb // N_DEV), (1, 0, 2)).astype(BF16)
    return full.reshape(N_DEV, a // N_DEV, bb).astype(BF16)


def _swap_halves(t):
    half = QK_ROPE // 2
    return jnp.concatenate([t[..., half:], t[..., :half]], axis=-1)


def _t(w):
    return jnp.swapaxes(w, -1, -2)


def _pad_rows(t, mult=8):
    return jnp.pad(t, ((0, -t.shape[0] % mult), (0, 0)))


def _pad_last(t, width):
    return jnp.pad(t, ((0, 0),) * (t.ndim - 1) + ((0, width - t.shape[-1]),))


def kernel(x, c, positions, w_ada, b_ada, g_mix, w_in, g_q, w_uq, g_kv, w_ukv, w_o_attn, w_dw, b_dw, g_cn, b_cn, w_pw2, w_out, g_ffn, w_gu, w_down, g_final, loss_target, m_w_ada, m_b_ada, m_g_mix, m_w_in, m_g_q, m_w_uq, m_g_kv, m_w_ukv, m_w_o_attn, m_w_dw, m_b_dw, m_g_cn, m_b_cn, m_w_pw2, m_w_out, m_g_ffn, m_w_gu, m_w_down, m_g_final, v_w_ada, v_b_ada, v_g_mix, v_w_in, v_g_q, v_w_uq, v_g_kv, v_w_ukv, v_w_o_attn, v_w_dw, v_b_dw, v_g_cn, v_b_cn, v_w_pw2, v_w_out, v_g_ffn, v_w_gu, v_w_down, v_g_final):
    weights = dict(w_ada=w_ada, b_ada=b_ada, g_mix=g_mix, w_in=w_in, g_q=g_q, w_uq=w_uq, g_kv=g_kv, w_ukv=w_ukv,
                   w_o_attn=w_o_attn, w_dw=w_dw, b_dw=b_dw, g_cn=g_cn, b_cn=b_cn, w_pw2=w_pw2, w_out=w_out, g_ffn=g_ffn,
                   w_gu=w_gu, w_down=w_down, g_final=g_final)
    mom_m = dict(w_ada=m_w_ada, b_ada=m_b_ada, g_mix=m_g_mix, w_in=m_w_in, g_q=m_g_q, w_uq=m_w_uq, g_kv=m_g_kv,
                 w_ukv=m_w_ukv, w_o_attn=m_w_o_attn, w_dw=m_w_dw, b_dw=m_b_dw, g_cn=m_g_cn, b_cn=m_b_cn, w_pw2=m_w_pw2,
                 w_out=m_w_out, g_ffn=m_g_ffn, w_gu=m_w_gu, w_down=m_w_down, g_final=m_g_final)
    mom_v = dict(w_ada=v_w_ada, b_ada=v_b_ada, g_mix=v_g_mix, w_in=v_w_in, g_q=v_g_q, w_uq=v_w_uq, g_kv=v_g_kv,
                 w_ukv=v_w_ukv, w_o_attn=v_w_o_attn, w_dw=v_w_dw, b_dw=v_b_dw, g_cn=v_g_cn, b_cn=v_b_cn, w_pw2=v_w_pw2,
                 w_out=v_w_out, g_ffn=v_g_ffn, w_gu=v_w_gu, w_down=v_w_down, g_final=v_g_final)
    order = list(weights)

    nb, s, d = x.shape
    nl = w_in.shape[0]
    ql, kl, cc = g_q.shape[1], g_kv.shape[1], g_cn.shape[1]
    h = N_HEADS
    qk = QK_NOPE + QK_ROPE
    xi, yi, ci = lax.axis_index("x"), lax.axis_index("y"), lax.axis_index("c")
    me = 4 * xi + 2 * yi + ci

    shards = [weights[n].astype(BF16) for n, _ in GATHERED]
    gathered_pre = _all_gather([shards[a][0] for a in PRE], vmem=False, name="weight_all_gather")
    n_dw = w_dw.shape[2]
    dw_rows = jnp.pad(w_dw, ((0, 0), (0, HALO - CONV_W), (0, LANES - n_dw))).reshape(nl * HALO, LANES)
    c_all, dw_all = _all_gather([_pad_rows(c), dw_rows], vmem=True, name="cond_all_gather")
    c_full = c_all[:, :nb].reshape(N_DEV * nb, d)
    w_dw_full = jnp.transpose(dw_all.reshape(N_DEV, nl, HALO, LANES)[..., :n_dw], (1, 2, 0, 3)).reshape(nl, HALO, cc)

    o_kr, o_glu, o_gate = ql + kl, ql + kl + QK_ROPE, ql + kl + QK_ROPE + 2 * cc

    def layouts_pre(chunks):
        wi, w_uq_l, w_ukv_l = (_from_chunks(t, GATHERED[a][1]) for a, t in zip(PRE, chunks))
        w_kr = wi[:, o_kr:o_glu]
        wq = w_uq_l.reshape(ql, h, qk)
        wkv = w_ukv_l.reshape(kl, h, QK_NOPE + V_HEAD)
        return dict(
            lat=jnp.concatenate([wi[:, :o_kr], jnp.zeros((d, QK_NOPE), BF16), w_kr, _swap_halves(w_kr)], axis=1),
            glu=wi[:, o_glu:o_gate], gate=wi[:, o_gate:],
            q=jnp.concatenate([wq, _swap_halves(wq[..., QK_NOPE:])], axis=-1).reshape(ql, h * HEAD_PAD),
            k=_pad_last(wkv[..., :QK_NOPE], HEAD_PAD).reshape(kl, h * HEAD_PAD),
            v=_pad_last(wkv[..., QK_NOPE:], HEAD_PAD).reshape(kl, h * HEAD_PAD))

    def layouts_post(chunks):
        w_o_l, w_pw_l, w_out_l, w_gu_l, w_down_l = (_from_chunks(t, GATHERED[a][1]) for a, t in zip(POST, chunks))
        w_o = jnp.pad(w_o_l.reshape(h, V_HEAD, d), ((0, 0), (0, HEAD_PAD - V_HEAD), (0, 0)))
        return dict(o=w_o.reshape(h * HEAD_PAD, d), pw=w_pw_l, out=w_out_l, gu=w_gu_l, down=w_down_l)

    n_ada = w_ada.shape[2]
    b_cols = lax.dynamic_slice_in_dim(b_ada, me * n_ada, n_ada, axis=1).reshape(nl, 1, n_ada)
    mod_part = _ada_fwd(c_full, w_ada, b_cols)
    (mod_all,) = _all_gather([mod_part.reshape(nl * N_DEV * nb, n_ada)], vmem=True, name="mod_all_gather")
    mod_all = jnp.transpose(mod_all.reshape(N_DEV, nl, N_DEV * nb, n_ada), (1, 2, 0, 3)).reshape(nl, N_DEV * nb, 6 * d)
    mod = lax.dynamic_slice_in_dim(mod_all, me * nb, nb, axis=1).reshape(nl, nb, 6, 1, d)

    inv_freq = ROPE_THETA ** (-jnp.arange(0, QK_ROPE, 2, dtype=F32) / QK_ROPE)
    zeros = lambda n: jnp.zeros((n,), F32)
    freq_row = jnp.concatenate([zeros(QK_NOPE), inv_freq, inv_freq, zeros(HEAD_PAD - qk)]).reshape(1, -1)
    ones = jnp.ones((QK_ROPE // 2,), F32)
    sign_row = jnp.concatenate([zeros(QK_NOPE), -ones, ones, zeros(HEAD_PAD - qk)]).reshape(1, -1)
    cos_t, sin_t = _rope_tables(positions.astype(F32).reshape(nb, s, 1), freq_row, sign_row)

    row = lambda t, l: t[l].reshape(1, -1)

    saved, wts = [], []
    xc = x
    for l in range(nl):
        wl = layouts_pre(gathered_pre)
        lat, glu, gate, qh, kh, vh = _mix_pre_fwd(xc, mod[l], cos_t, sin_t, row(g_mix, l), row(g_q, l), row(g_kv, l),
                                                  wl["lat"], wl["glu"], wl["gate"], wl["q"], wl["k"], wl["v"])
        riders = [shards[a][l] for a in POST] + ([shards[a][l + 1] for a in PRE] if l + 1 < nl else [])
        o, lse, arrived = _flash_fwd(qh, kh, vh, gather=riders)
        wl.update(layouts_post(arrived[:len(POST)]))
        gathered_pre = arrived[len(POST):]
        wts.append(wl)
        u, y_conv = _conv_fwd(glu, w_dw_full[l], row(b_dw, l), row(g_cn, l), row(b_cn, l), wl["pw"])
        x2, y_attn, o1 = _mix_post_fwd(xc, mod[l], o, y_conv, gate, wl["o"], wl["out"])
        x3, gu, o2 = _ffn_fwd(x2, mod[l], row(g_ffn, l), wl["gu"], wl["down"])
        saved.append(dict(x=xc, lat=lat, glu=glu, gate=gate, q=qh, k=kh, v=vh, o=o, lse=lse, u=u, y_conv=y_conv, x2=x2,
                          y_attn=y_attn, o1=o1, gu=gu, o2=o2))
        xc = x3

    dx, loss_part, dgf_part = _loss_bwd(xc, loss_target, g_final.reshape(1, d))
    loss = lax.psum(0.5 / d * jnp.sum(loss_part), AXES)

    small_rows, dw_taps, dmod = [None] * nl, [None] * nl, [None] * nl
    own, got = {}, {}
    chunk = lambda g, a: _to_chunks(g[GATHERED[a][0]], GATHERED[a][1])
    mine = lambda t: lax.dynamic_index_in_dim(t, me, axis=0, keepdims=False)
    pending = []
    for l in reversed(range(nl)):
        sv, wl, gw = saved[l], wts[l], {}
        dx2, do2, act, dgu, h2, dmod2, dgffn = _ffn_bwd(dx, sv["x2"], mod[l], row(g_ffn, l), sv["gu"], sv["o2"],
                                                       _t(wl["gu"]), _t(wl["down"]))
        gw["w_gu"] = _matmul_tn(h2, dgu, "grad_w_gu")
        gw["w_down"] = _matmul_tn(act, do2, "grad_w_down")
        do1, dya, dyc, dgate, yv, do_h, dgt1 = _mix_post_bwd(dx2, mod[l], sv["o1"], sv["y_attn"], sv["y_conv"], sv["gate"],
                                                           _t(wl["out"]), _t(wl["o"]))
        gw["w_out"] = _matmul_tn(yv, do1, "grad_w_out")
        dwo = _matmul_tn(sv["o"], dya, "grad_w_o")
        gw["w_o_attn"] = dwo.reshape(h, HEAD_PAD, d)[:, :V_HEAD].reshape(h * V_HEAD, d)
        dglu, s_act, ddw, csmall = _conv_bwd(dyc, sv["u"], sv["glu"], w_dw_full[l], row(g_cn, l), row(b_cn, l), _t(wl["pw"]))
        gw["w_pw2"] = _matmul_tn(s_act, dyc, "grad_w_pw2")
        ready = [chunk(gw, a) for a in POST]
        dq, dk, dv, arrived = _flash_bwd(sv["q"], sv["k"], sv["v"], sv["o"], sv["lse"], do_h, scatter=ready + pending)
        for i, a in enumerate(POST):
            own[l, a], got[l, a] = mine(ready[i]), arrived[i]
        for i, a in enumerate(PRE if pending else ()):
            got[l + 1, a] = arrived[len(POST) + i]
        dx, dlat, dqr, qn, kvn, h1, dmod1, dgm, dgq, dgkv = _mix_pre_bwd(
            sv["x"], dx2, mod[l], cos_t, sin_t, row(g_mix, l), row(g_q, l), row(g_kv, l), sv["lat"], dq, dk, dv, dglu,
            dgate, _t(wl["lat"]), _t(wl["glu"]), _t(wl["gate"]), _t(wl["q"]), _t(wl["k"]), _t(wl["v"]))
        dwl = _matmul_tn(h1, dlat, "grad_w_lat")
        dwg = _matmul_tn(h1, dglu, "grad_w_glu")
        dwt = _matmul_tn(h1, dgate, "grad_w_gate")
        kr0 = o_kr + QK_NOPE
        dkr = dwl[:, kr0:kr0 + QK_ROPE] + _swap_halves(dwl[:, kr0 + QK_ROPE:])
        gw["w_in"] = jnp.concatenate([dwl[:, :o_kr], dkr, dwg, dwt], axis=1)
        dwq = _matmul_tn(qn, dqr, "grad_w_q").reshape(ql, h, HEAD_PAD)
        dwq = jnp.concatenate([dwq[..., :QK_NOPE], dwq[..., QK_NOPE:qk] + _swap_halves(dwq[..., qk:])], axis=-1)
        gw["w_uq"] = dwq.reshape(ql, h * qk)
        dwk = _matmul_tn(kvn, dk, "grad_w_k").reshape(kl, h, HEAD_PAD)
        dwv = _matmul_tn(kvn, dv, "grad_w_v").reshape(kl, h, HEAD_PAD)
        gw["w_ukv"] = jnp.concatenate([dwk[..., :QK_NOPE], dwv[..., :V_HEAD]], axis=-1).reshape(kl, h * (QK_NOPE + V_HEAD))
        pending = [chunk(gw, a) for a in PRE]
        for i, a in enumerate(PRE):
            own[l, a] = mine(pending[i])
        dmod[l] = jnp.concatenate([dmod1[:, :, 0], dgt1[:, :, 0], dmod2[:, :, 0]], axis=1).reshape(nb, 6 * d)
        bsum = lambda t: jnp.sum(t, axis=0).reshape(1, -1)
        cs = jnp.sum(csmall, axis=0)[:, 0]
        small_rows[l] = jnp.concatenate([bsum(dgm), bsum(dgq), bsum(dgkv), cs[2:3], cs[0:1], cs[1:2], bsum(dgffn)], axis=1)
        dw_taps[l] = jnp.sum(ddw.reshape(nb, HALO, 8, cc), axis=(0, 2))
    grad_x = dx

    dmod_rows = _pad_rows(jnp.stack(dmod).reshape(nl * nb, 6 * d))
    (dmod_all,) = _all_gather([dmod_rows], vmem=True, name="dmod_all_gather")
    dmod_full = jnp.transpose(dmod_all[:, :nl * nb].reshape(N_DEV, nl, nb, 6 * d), (1, 0, 2, 3)).reshape(nl, N_DEV * nb, 6 * d)
    dmod_cols = lax.dynamic_slice_in_dim(dmod_full, me * n_ada, n_ada, axis=2)
    grad_w_ada, grad_b_ada = _ada_bwd(c_full, dmod_cols, dmod_full)
    grads = {"w_ada": grad_w_ada, "b_ada": grad_b_ada.reshape(nl, 6 * d)}

    widths = (d, ql, kl, cc, cc, cc, d)
    wsum = sum(widths)
    final_row = _pad_last(jnp.sum(dgf_part, axis=0).reshape(1, d), wsum)
    small2d = _pad_rows(jnp.concatenate(small_rows + [final_row], axis=0))
    taps2d = jnp.concatenate(dw_taps, axis=0)
    small_all, taps_all = _all_gather([small2d, taps2d], vmem=True, name="small_grad_all_gather")
    small_sum, taps_sum = _sum_devices(small_all), _sum_devices(taps_all)
    off = 0
    for n, wdt in zip(("g_mix", "g_q", "g_kv", "b_dw", "g_cn", "b_cn", "g_ffn"), widths):
        grads[n] = small_sum[:nl, off:off + wdt]
        off += wdt
    grads["g_final"] = small_sum[nl, :d]
    taps = taps_sum.reshape(nl, HALO, cc)[:, :CONV_W]
    grads["w_dw"] = lax.dynamic_slice_in_dim(taps, me * n_dw, n_dw, axis=2)

    delta, new_m, new_v = {}, {}, {}
    for n in order:
        if n in grads:
            delta[n], new_m[n], new_v[n] = _adamw(weights[n], grads[n], mom_m[n], mom_v[n], "adamw_" + n)

    for a, t in zip(PRE, _grad_scatter(pending)):
        got[0, a] = t
    for a, (n, _) in enumerate(GATHERED):
        own_n = jnp.stack([own[l, a] for l in range(nl)])
        got_n = jnp.stack([got[l, a] for l in range(nl)], axis=1)
        grads[n], delta[n], new_m[n], new_v[n] = _adamw_reduce(weights[n], mom_m[n], mom_v[n], own_n, got_n, "adamw_" + n)

    return (loss, grad_x, *[grads[n] for n in order], *[delta[n] for n in order], *[new_m[n] for n in order],
            *[new_v[n] for n in order])
```

```python
import jax
import jax.numpy as jnp
from jax import lax
from jax.experimental import pallas as pl
from jax.experimental.pallas import tpu as pltpu

F32, BF16 = jnp.float32, jnp.bfloat16
MESH = pl.DeviceIdType.MESH
AXES = ("x", "y", "c")
N_DEV = 8

N_HEADS = 8
QK_NOPE = 64
QK_ROPE = 32
V_HEAD = 64
HEAD_PAD = 128
CONV_W = 31
HALO = 32
EPS = 1e-6
ROPE_THETA = 10000.0
NEG_INF = -1e30
ATTN_SCALE = (QK_NOPE + QK_ROPE) ** -0.5

ADAM_LR, ADAM_B1, ADAM_B2, ADAM_EPS, ADAM_WD, ADAM_STEP = 0.001, 0.9, 0.999, 1e-08, 0.01, 10

LANES = 128
VMEM_LIMIT = 60 * 1024 * 1024


def _params(sem=None):
    return pltpu.CompilerParams(dimension_semantics=sem, vmem_limit_bytes=VMEM_LIMIT)


def _pick(n, cap):
    if n <= cap:
        return n
    best = None
    for d in range(LANES, cap + 1, LANES):
        if n % d == 0:
            best = d
    assert best is not None, (n, cap)
    return best


def _div_tile(n, cap, mult):
    best = None
    for d in range(mult, min(n, cap) + 1, mult):
        if n % d == 0:
            best = d
    assert best is not None, (n, cap, mult)
    return best


def _row_tile(s, cap=512):
    return cap if s % cap == 0 and s >= 2 * cap else s // 2


def _sig(v):
    return 1.0 / (1.0 + jnp.exp(-v))


def _rstd(v):
    return lax.rsqrt(jnp.mean(v * v, axis=-1, keepdims=True) + EPS)


def _dot(a, b):
    return jnp.dot(a, b, preferred_element_type=F32)


def _dot_nt(a, b):
    return lax.dot_general(a, b, (((1,), (1,)), ((), ())), preferred_element_type=F32)


def _dot_tn(a, b):
    return lax.dot_general(a, b, (((0,), (0,)), ((), ())), preferred_element_type=F32)


def _rope(v, cos_t, sin_t):
    return v * cos_t + pltpu.roll(v, HEAD_PAD - QK_ROPE, 1) * sin_t


def _rope_bwd(dv, cos_t, sin_t):
    return dv * cos_t + pltpu.roll(dv * sin_t, QK_ROPE, 1)


def _const(shape):
    n = len(shape)
    return pl.BlockSpec(shape, lambda *_: (0,) * n, pipeline_mode=pl.Buffered(1))


def _rows(tm, w):
    return pl.BlockSpec((1, tm, w), lambda b, s: (b, s, 0))


def _per_b(r, w):
    return pl.BlockSpec((1, r, 1, w), lambda b, s: (b, 0, 0, 0))


def _all_gather(arrs, vmem, name):
    n = len(arrs)
    space = pltpu.VMEM if vmem else pl.ANY

    def body(*refs):
        x_refs, out_refs = refs[:n], refs[n:2 * n]
        send_sems, recv_sems, local_sems = refs[2 * n:]
        x_, y_, c_ = lax.axis_index("x"), lax.axis_index("y"), lax.axis_index("c")
        me, sibling = (x_, y_, c_), (x_, y_, 1 - c_)
        chips = [(1 - x_, y_), (x_, 1 - y_), (1 - x_, 1 - y_)]

        def copy(a, k, block, to, own=False):
            px, py, pc = block
            slot = out_refs[a].at[4 * px + 2 * py + pc]
            return pltpu.make_async_remote_copy(
                src_ref=x_refs[a] if own else slot, dst_ref=slot, send_sem=send_sems.at[k * n + a],
                recv_sem=recv_sems.at[k * n + a], device_id=to, device_id_type=MESH)

        mine = [pltpu.make_async_copy(x_refs[a], out_refs[a].at[4 * x_ + 2 * y_ + c_], local_sems.at[a]) for a in range(n)]
        sent = []
        for a in range(n):
            mine[a].start()
            sent.append(copy(a, 0, me, sibling, own=True))
            sent += [copy(a, 1 + j, me, (*chip, c_), own=True) for j, chip in enumerate(chips)]
        for cp in sent:
            cp.start()
        for j, chip in enumerate(chips):
            for a in range(n):
                copy(a, 1 + j, (*chip, c_), me).wait_recv()
                passed = copy(a, 4 + j, (*chip, c_), sibling)
                passed.start()
                sent.append(passed)
        for a in range(n):
            copy(a, 0, sibling, me).wait_recv()
            for j, chip in enumerate(chips):
                copy(a, 4 + j, (*chip, 1 - c_), me).wait_recv()
        for cp in sent:
            cp.wait_send()
        for cp in mine:
            cp.wait()

    return pl.pallas_call(
        body, name=name,
        out_shape=[jax.ShapeDtypeStruct((N_DEV,) + t.shape, t.dtype) for t in arrs],
        in_specs=[pl.BlockSpec(memory_space=space)] * n, out_specs=[pl.BlockSpec(memory_space=space)] * n,
        scratch_shapes=[pltpu.SemaphoreType.DMA((7 * n,)), pltpu.SemaphoreType.DMA((7 * n,)), pltpu.SemaphoreType.DMA((n,))],
        compiler_params=pltpu.CompilerParams(vmem_limit_bytes=VMEM_LIMIT),
    )(*arrs)


FLIPS = tuple((fx, fy, fc) for fx in (0, 1) for fy in (0, 1) for fc in (0, 1))[1:]


class _DirectExchange:
    def __init__(self, kind, in_refs, out_refs, send_sems, recv_sems, local_sems):
        n = len(in_refs)
        x_, y_, c_ = lax.axis_index("x"), lax.axis_index("y"), lax.axis_index("c")
        me = 4 * x_ + 2 * y_ + c_
        self.copies, self.local = [], []
        for r, (fx, fy, fc) in enumerate(FLIPS):
            px, py, pc = (1 - x_ if fx else x_), (1 - y_ if fy else y_), (1 - c_ if fc else c_)
            for a in range(n):
                src = in_refs[a] if kind == "gather" else in_refs[a].at[4 * px + 2 * py + pc]
                dst = out_refs[a].at[me] if kind == "gather" else out_refs[a].at[r]
                self.copies.append(pltpu.make_async_remote_copy(
                    src_ref=src, dst_ref=dst, send_sem=send_sems.at[r * n + a], recv_sem=recv_sems.at[r * n + a],
                    device_id=(px, py, pc), device_id_type=MESH))
        if kind == "gather":
            self.local = [pltpu.make_async_copy(in_refs[a], out_refs[a].at[me], local_sems.at[a]) for a in range(n)]

    def start(self):
        for cp in self.local + self.copies:
            cp.start()

    def wait(self):
        for cp in self.copies + self.local:
            cp.wait()

    @staticmethod
    def out_shapes(kind, arrs):
        if kind == "gather":
            return [jax.ShapeDtypeStruct((N_DEV,) + t.shape, t.dtype) for t in arrs]
        return [jax.ShapeDtypeStruct((N_DEV - 1,) + t.shape[1:], t.dtype) for t in arrs]

    @staticmethod
    def scratch(n):
        return [pltpu.SemaphoreType.DMA((7 * n,)), pltpu.SemaphoreType.DMA((7 * n,)), pltpu.SemaphoreType.DMA((n,))]


def _grad_scatter(gs):
    n = len(gs)

    def body(*refs):
        ex = _DirectExchange("scatter", refs[:n], refs[n:2 * n], *refs[2 * n:])
        ex.start()
        ex.wait()

    return pl.pallas_call(
        body, name="grad_scatter", out_shape=_DirectExchange.out_shapes("scatter", gs),
        in_specs=[pl.BlockSpec(memory_space=pl.ANY)] * n, out_specs=[pl.BlockSpec(memory_space=pl.ANY)] * n,
        scratch_shapes=_DirectExchange.scratch(n),
    )(*gs)


def _rows_2d(t, lead):
    return t.reshape((lead, -1, t.shape[-1]) if lead else (-1, t.shape[-1]))


def _grad_row_tile(rows, cols):
    return _div_tile(rows, max(16, (1 << 18) // cols), 16)


def _sum_devices(g):
    _, m, n = g.shape

    def body(g_ref, o_ref):
        s = g_ref[0]
        for j in range(1, N_DEV):
            s = s + g_ref[j]
        o_ref[...] = s

    return pl.pallas_call(body, name="small_grad_sum", out_shape=jax.ShapeDtypeStruct((m, n), F32),
                          compiler_params=pltpu.CompilerParams(vmem_limit_bytes=VMEM_LIMIT))(g)


def _ada_fwd(c_full, w_ada, b_cols):
    nl, d, n = w_ada.shape
    nb = c_full.shape[0]

    def body(c_ref, w_ref, b_ref, o_ref):
        cv = c_ref[...]
        act = cv * _sig(cv)
        o_ref[0] = jnp.dot(act, w_ref[0], preferred_element_type=F32, precision=lax.Precision.HIGHEST) + b_ref[0]

    return pl.pallas_call(
        body, name="ada_fwd", grid=(nl,), out_shape=jax.ShapeDtypeStruct((nl, nb, n), F32),
        in_specs=[pl.BlockSpec((nb, d), lambda l: (0, 0)), pl.BlockSpec((1, d, n), lambda l: (l, 0, 0)),
                  pl.BlockSpec((1, 1, n), lambda l: (l, 0, 0))],
        out_specs=pl.BlockSpec((1, nb, n), lambda l: (l, 0, 0)), compiler_params=_params(("arbitrary",)),
    )(c_full, w_ada, b_cols)


def _ada_bwd(c_full, dmod_cols, dmod_full):
    nl, nb, n = dmod_cols.shape
    d = c_full.shape[1]
    nfull = dmod_full.shape[2]

    def body(c_ref, dc_ref, df_ref, gw_ref, gb_ref):
        cv = c_ref[...]
        act = cv * _sig(cv)
        gw_ref[0] = lax.dot_general(act, dc_ref[0], (((0,), (0,)), ((), ())), preferred_element_type=F32,
                                    precision=lax.Precision.HIGHEST)
        gb_ref[0] = jnp.sum(df_ref[0], axis=0, keepdims=True)

    return pl.pallas_call(
        body, name="ada_bwd", grid=(nl,),
        out_shape=(jax.ShapeDtypeStruct((nl, d, n), F32), jax.ShapeDtypeStruct((nl, 1, nfull), F32)),
        in_specs=[pl.BlockSpec((nb, d), lambda l: (0, 0)), pl.BlockSpec((1, nb, n), lambda l: (l, 0, 0)),
                  pl.BlockSpec((1, nb, nfull), lambda l: (l, 0, 0))],
        out_specs=(pl.BlockSpec((1, d, n), lambda l: (l, 0, 0)), pl.BlockSpec((1, 1, nfull), lambda l: (l, 0, 0))),
        compiler_params=_params(("arbitrary",)),
    )(c_full, dmod_cols, dmod_full)


def _rope_tables(pos, freq_row, sign_row):
    b, s, _ = pos.shape
    tm = _row_tile(s)

    def body(p_ref, f_ref, g_ref, c_ref, s_ref):
        ang = p_ref[0] * f_ref[...]
        lane = lax.broadcasted_iota(jnp.int32, ang.shape, 1)
        c_ref[0] = jnp.where(lane < QK_NOPE, 1.0, jnp.where(lane < QK_NOPE + QK_ROPE, jnp.cos(ang), 0.0))
        s_ref[0] = g_ref[...] * jnp.sin(ang)

    return pl.pallas_call(
        body, name="rope_tables", grid=(b, s // tm),
        out_shape=(jax.ShapeDtypeStruct((b, s, HEAD_PAD), F32),) * 2,
        in_specs=[_rows(tm, 1), pl.BlockSpec((1, HEAD_PAD), lambda i, j: (0, 0)),
                  pl.BlockSpec((1, HEAD_PAD), lambda i, j: (0, 0))],
        out_specs=(_rows(tm, HEAD_PAD),) * 2, compiler_params=_params(("arbitrary", "arbitrary")),
    )(pos, freq_row, sign_row)


def _mix_pre_fwd(x, mod, cos_t, sin_t, g_mix, g_q, g_kv, w_lat, w_glu, w_gate, w_q, w_k, w_v):
    b, s, d = x.shape
    ql, kl = g_q.shape[1], g_kv.shape[1]
    wl, wg, wt = w_lat.shape[1], w_glu.shape[1], w_gate.shape[1]
    hw = N_HEADS * HEAD_PAD
    tm = _row_tile(s)

    def body(x_ref, mod_ref, cos_ref, sin_ref, gm_ref, gq_ref, gkv_ref, wlat_ref, wglu_ref, wgate_ref, wq_ref, wk_ref,
             wv_ref, lat_ref, glu_ref, gate_ref, q_ref, k_ref, v_ref):
        xf = x_ref[0]
        sh, sc = mod_ref[0, 0], mod_ref[0, 1]
        hb = ((xf * _rstd(xf) * gm_ref[...]) * (1.0 + sc) + sh).astype(BF16)
        glu_ref[0] = _dot(hb, wglu_ref[...]).astype(BF16)
        gate_ref[0] = _dot(hb, wgate_ref[...]).astype(BF16)
        lat = _dot(hb, wlat_ref[...]).astype(BF16)
        lat_ref[0] = lat
        latf = lat.astype(F32)
        q_lat, kv_lat, kr_sec = latf[:, :ql], latf[:, ql:ql + kl], latf[:, ql + kl:]
        qn = (q_lat * _rstd(q_lat) * gq_ref[...]).astype(BF16)
        kvn = (kv_lat * _rstd(kv_lat) * gkv_ref[...]).astype(BF16)
        cos_v, sin_v = cos_ref[0], sin_ref[0]
        lane = lax.broadcasted_iota(jnp.int32, kr_sec.shape, 1)
        kr = jnp.where(lane >= QK_NOPE, _rope(kr_sec, cos_v, sin_v), 0.0)
        q_all, k_all, v_all = _dot(qn, wq_ref[...]), _dot(kvn, wk_ref[...]), _dot(kvn, wv_ref[...])
        vlane = lax.broadcasted_iota(jnp.int32, v_all.shape, 1)
        v_ref[0] = jnp.where(vlane % HEAD_PAD == V_HEAD, 1.0, v_all).astype(BF16)
        for h in range(N_HEADS):
            cols = slice(h * HEAD_PAD, (h + 1) * HEAD_PAD)
            q_ref[0, :, cols] = (_rope(q_all[:, cols], cos_v, sin_v) * ATTN_SCALE).astype(BF16)
            k_ref[0, :, cols] = (k_all[:, cols] + kr).astype(BF16)

    hshape = jax.ShapeDtypeStruct((b, s, hw), BF16)
    return pl.pallas_call(
        body, name="mix_pre_fwd", grid=(b, s // tm),
        out_shape=(jax.ShapeDtypeStruct((b, s, wl), BF16), jax.ShapeDtypeStruct((b, s, wg), BF16),
                   jax.ShapeDtypeStruct((b, s, wt), BF16), hshape, hshape, hshape),
        in_specs=[_rows(tm, d), _per_b(6, d), _rows(tm, HEAD_PAD), _rows(tm, HEAD_PAD), _const((1, d)), _const((1, ql)),
                  _const((1, kl)), _const(w_lat.shape), _const(w_glu.shape), _const(w_gate.shape), _const(w_q.shape),
                  _const(w_k.shape), _const(w_v.shape)],
        out_specs=(_rows(tm, wl), _rows(tm, wg), _rows(tm, wt), _rows(tm, hw), _rows(tm, hw), _rows(tm, hw)),
        compiler_params=_params(("arbitrary", "arbitrary")),
    )(x, mod, cos_t, sin_t, g_mix, g_q, g_kv, w_lat, w_glu, w_gate, w_q, w_k, w_v)


def _causal_mask(tq, tk):
    return lax.broadcasted_iota(jnp.int32, (tq, tk), 0) >= lax.broadcasted_iota(jnp.int32, (tq, tk), 1)


def _riding_exchange(kind, n, refs, grid):
    if not n:
        return
    ex = _DirectExchange(kind, refs[:n], refs[n:2 * n], *refs[2 * n:])
    ids = [pl.program_id(a) for a in range(len(grid))]
    first, last = ids[0] == 0, ids[0] == grid[0] - 1
    for a in range(1, len(grid)):
        first, last = jnp.logical_and(first, ids[a] == 0), jnp.logical_and(last, ids[a] == grid[a] - 1)
    pl.when(first)(ex.start)
    return lambda: pl.when(last)(ex.wait)


def _flash_fwd(q, k, v, gather=()):
    b, s, hw = q.shape
    nh, hp = hw // HEAD_PAD, HEAD_PAD
    t = _row_tile(s)
    n = len(gather)
    grid = (b, nh, s // t)

    def body(q_ref, k_ref, v_ref, *rest):
        o_ref, lse_ref = rest[n], rest[n + 1]
        finish = _riding_exchange("gather", n, rest[:n] + rest[n + 2:], grid)
        i = pl.program_id(2)
        qv = q_ref[0]

        def step(j, carry, masked):
            m, acc = carry
            rows = pl.ds(pl.multiple_of(j * t, t), t)
            sc = _dot_nt(qv, k_ref[0, rows, :])
            if masked:
                sc = jnp.where(_causal_mask(t, t), sc, NEG_INF)
            m_new = jnp.maximum(m, jnp.max(sc, axis=-1, keepdims=True))
            p = jnp.exp((sc - m_new).astype(BF16))
            acc = jnp.exp(m - m_new) * acc + _dot(p, v_ref[0, rows, :])
            return m_new, acc

        init = (jnp.full((t, 1), NEG_INF, F32), jnp.zeros((t, hp), F32))
        carry = lax.fori_loop(0, i, lambda j, cr: step(j, cr, False), init)
        m, acc = step(i, carry, True)
        lane = lax.broadcasted_iota(jnp.int32, acc.shape, 1)
        l = jnp.sum(jnp.where(lane == V_HEAD, acc, 0.0), axis=-1, keepdims=True)
        o_ref[0] = (acc / l).astype(BF16)
        lse_ref[0, 0] = m + jnp.log(l)
        if finish:
            finish()

    tile = pl.BlockSpec((1, t, hp), lambda bb, hh, ii: (bb, ii, hh))
    full = pl.BlockSpec((1, s, hp), lambda bb, hh, ii: (bb, 0, hh))
    hbm = [pl.BlockSpec(memory_space=pl.ANY)] * n
    outs = pl.pallas_call(
        body, name="flash_fwd_gather" if n else "flash_fwd", grid=grid,
        out_shape=[jax.ShapeDtypeStruct((b, s, hw), BF16), jax.ShapeDtypeStruct((b, nh, s, 1), F32)]
        + _DirectExchange.out_shapes("gather", gather),
        in_specs=[tile, full, full] + hbm,
        out_specs=[tile, pl.BlockSpec((1, 1, t, 1), lambda bb, hh, ii: (bb, hh, ii, 0))] + hbm,
        scratch_shapes=_DirectExchange.scratch(n) if n else [],
        compiler_params=_params(("arbitrary", "arbitrary", "arbitrary")),
    )(q, k, v, *gather)
    return outs[0], outs[1], outs[2:]


def _halo_prev(tm, w):
    r = tm // HALO
    return pl.BlockSpec((1, HALO, w), lambda b, s: (b, jnp.maximum(s * r - 1, 0), 0))


def _halo_next(tm, w, n_tiles):
    r = tm // HALO
    return pl.BlockSpec((1, HALO, w), lambda b, s: (b, jnp.minimum((s + 1) * r, n_tiles * r - 1), 0))


def _conv_rows(cc):
    return max(8, 16 * 8 * LANES // cc)


def _shifted_copies(buf, rows):
    buf[0, pl.ds(rows, 8), :] = jnp.zeros((8, buf.shape[2]), buf.dtype)
    for s in range(1, 8):
        buf[s, pl.ds(0, rows), :] = buf[0, pl.ds(s, rows), :]


def _window(buf, start, rows):
    return buf[start % 8, pl.ds(start - start % 8, rows), :]


def _glu(v, cc):
    a, g = v[:, :cc].astype(F32), v[:, cc:].astype(F32)
    return a * _sig(g)


def _conv_fwd(glu, w_dw, b_dw, g_cn, b_cn, w_pw2):
    b, s, w2 = glu.shape
    cc = w2 // 2
    d = w_pw2.shape[1]
    tm = _row_tile(s)

    rc = _conv_rows(cc)
    te = tm + HALO

    def body(cur_ref, prev_ref, w_ref, bdw_ref, g_ref, bcn_ref, wp_ref, u_ref, y_ref, ext, u_all):
        first = pl.program_id(1) == 0
        ext[0, pl.ds(0, HALO), :] = jnp.where(first, 0.0, _glu(prev_ref[0], cc))
        ext[0, pl.ds(HALO, tm), :] = _glu(cur_ref[0], cc)
        _shifted_copies(ext, te)
        for c0 in range(0, tm, rc):
            acc = jnp.zeros((rc, cc), F32) + bdw_ref[...]
            for kk in range(CONV_W):
                acc = acc + w_ref[pl.ds(kk, 1), :] * _window(ext, c0 + HALO - CONV_W + 1 + kk, rc)
            u_all[pl.ds(c0, rc), :] = acc
        ub = u_all[...].astype(BF16)
        u_ref[0] = ub
        uf = ub.astype(F32)
        mu = jnp.mean(uf, axis=-1, keepdims=True)
        uc = uf - mu
        ln = uc * lax.rsqrt(jnp.mean(uc * uc, axis=-1, keepdims=True) + EPS) * g_ref[...] + bcn_ref[...]
        y_ref[0] = _dot((ln * _sig(ln)).astype(BF16), wp_ref[...]).astype(BF16)

    return pl.pallas_call(
        body, name="conv_fwd", grid=(b, s // tm),
        out_shape=(jax.ShapeDtypeStruct((b, s, cc), BF16), jax.ShapeDtypeStruct((b, s, d), BF16)),
        in_specs=[_rows(tm, w2), _halo_prev(tm, w2), _const(w_dw.shape), _const((1, cc)), _const((1, cc)), _const((1, cc)),
                  _const(w_pw2.shape)],
        out_specs=(_rows(tm, cc), _rows(tm, d)),
        scratch_shapes=[pltpu.VMEM((8, te + 8, cc), F32), pltpu.VMEM((tm, cc), F32)],
        compiler_params=_params(("arbitrary", "arbitrary")),
    )(glu, glu, w_dw, b_dw, g_cn, b_cn, w_pw2)


def _mix_post_fwd(x, mod, o, y_conv, gate, w_o, w_out):
    b, s, d = x.shape
    hw = o.shape[2]
    tm = _row_tile(s)

    def body(x_ref, mod_ref, o_ref, yc_ref, gate_ref, wo_ref, wout_ref, x2_ref, ya_ref, o1_ref):
        yab = _dot(o_ref[0], wo_ref[...]).astype(BF16)
        ya_ref[0] = yab
        gv = gate_ref[0]
        y = _sig(gv[:, :d].astype(F32)) * yab.astype(F32) + _sig(gv[:, d:].astype(F32)) * yc_ref[0].astype(F32)
        o1 = _dot(y.astype(BF16), wout_ref[...])
        o1_ref[0] = o1.astype(BF16)
        x2_ref[0] = x_ref[0] + mod_ref[0, 2] * o1

    return pl.pallas_call(
        body, name="mix_post_fwd", grid=(b, s // tm),
        out_shape=(jax.ShapeDtypeStruct((b, s, d), F32), jax.ShapeDtypeStruct((b, s, d), BF16),
                   jax.ShapeDtypeStruct((b, s, d), BF16)),
        in_specs=[_rows(tm, d), _per_b(6, d), _rows(tm, hw), _rows(tm, d), _rows(tm, 2 * d), _const(w_o.shape),
                  _const(w_out.shape)],
        out_specs=(_rows(tm, d), _rows(tm, d), _rows(tm, d)),
        compiler_params=_params(("arbitrary", "arbitrary")),
    )(x, mod, o, y_conv, gate, w_o, w_out)


def _ffn_fwd(x2, mod, g_ffn, w_gu, w_down):
    b, s, d = x2.shape
    f = w_down.shape[0]
    fc = _pick(f, 512)
    tm = _row_tile(s)

    def body(x_ref, mod_ref, g_ref, wgu_ref, wdn_ref, x3_ref, gu_ref, o2_ref):
        xf = x_ref[0]
        hb = ((xf * _rstd(xf) * g_ref[...]) * (1.0 + mod_ref[0, 4]) + mod_ref[0, 3]).astype(BF16)
        o2 = jnp.zeros((tm, d), F32)
        for c0 in range(0, f, fc):
            gb = _dot(hb, wgu_ref[:, c0:c0 + fc]).astype(BF16)
            ub = _dot(hb, wgu_ref[:, f + c0:f + c0 + fc]).astype(BF16)
            gu_ref[0, :, c0:c0 + fc] = gb
            gu_ref[0, :, f + c0:f + c0 + fc] = ub
            gf = gb.astype(F32)
            act = (gf * _sig(gf) * ub.astype(F32)).astype(BF16)
            o2 = o2 + _dot(act, wdn_ref[c0:c0 + fc, :])
        o2_ref[0] = o2.astype(BF16)
        x3_ref[0] = xf + mod_ref[0, 5] * o2

    return pl.pallas_call(
        body, name="ffn_fwd", grid=(b, s // tm),
        out_shape=(jax.ShapeDtypeStruct((b, s, d), F32), jax.ShapeDtypeStruct((b, s, 2 * f), BF16),
                   jax.ShapeDtypeStruct((b, s, d), BF16)),
        in_specs=[_rows(tm, d), _per_b(6, d), _const((1, d)), _const(w_gu.shape), _const(w_down.shape)],
        out_specs=(_rows(tm, d), _rows(tm, 2 * f), _rows(tm, d)),
        compiler_params=_params(("arbitrary", "arbitrary")),
    )(x2, mod, g_ffn, w_gu, w_down)


def _zero_at_first_tile(*refs):
    @pl.when(pl.program_id(1) == 0)
    def _():
        for ref in refs:
            ref[...] = jnp.zeros_like(ref)


def _accumulate(ref, idx, val):
    ref[idx] = ref[idx] + val


def _colsum(v):
    return jnp.sum(v, axis=0, keepdims=True)


def _loss_bwd(x, target, g_final):
    b, s, d = x.shape
    tm = _row_tile(s)

    def body(x_ref, t_ref, g_ref, dx_ref, loss_ref, dg_ref):
        _zero_at_first_tile(loss_ref, dg_ref)
        xf = x_ref[0]
        r = _rstd(xf)
        xh = xf * r
        diff = xh * g_ref[...] - t_ref[0]
        _accumulate(loss_ref, (0, 0), _colsum(diff * diff))
        dy = diff * (1.0 / d)
        _accumulate(dg_ref, (0, 0), _colsum(dy * xh))
        dyg = dy * g_ref[...]
        dx_ref[0] = r * (dyg - xh * jnp.mean(dyg * xh, axis=-1, keepdims=True))

    return pl.pallas_call(
        body, name="loss_bwd", grid=(b, s // tm),
        out_shape=(jax.ShapeDtypeStruct((b, s, d), F32), jax.ShapeDtypeStruct((b, 1, 1, d), F32),
                   jax.ShapeDtypeStruct((b, 1, 1, d), F32)),
        in_specs=[_rows(tm, d), _rows(tm, d), _const((1, d))],
        out_specs=(_rows(tm, d), _per_b(1, d), _per_b(1, d)),
        compiler_params=_params(("arbitrary", "arbitrary")),
    )(x, target, g_final)


def _ffn_bwd(dx3, x2, mod, g_ffn, gu, o2, w_gu_t, w_down_t):
    b, s, d = x2.shape
    f = w_down_t.shape[1]
    fc = _pick(f, 512)
    tm = _row_tile(s, 256)

    def body(dx3_ref, x_ref, mod_ref, g_ref, gu_ref, o2_ref, wgut_ref, wdnt_ref, dx2_ref, do2_ref, act_ref, dgu_ref, h_ref,
             dmod_ref, dg_ref):
        _zero_at_first_tile(dmod_ref, dg_ref)
        dx3 = dx3_ref[0]
        sh, sc, gt = mod_ref[0, 3], mod_ref[0, 4], mod_ref[0, 5]
        do2 = (dx3 * gt).astype(BF16)
        do2_ref[0] = do2
        _accumulate(dmod_ref, (0, 2), _colsum(dx3 * o2_ref[0].astype(F32)))
        dh = jnp.zeros((tm, d), F32)
        for c0 in range(0, f, fc):
            gf = gu_ref[0, :, c0:c0 + fc].astype(F32)
            uf = gu_ref[0, :, f + c0:f + c0 + fc].astype(F32)
            sg = _sig(gf)
            silu = gf * sg
            act_ref[0, :, c0:c0 + fc] = (silu * uf).astype(BF16)
            dact = _dot(do2, wdnt_ref[:, c0:c0 + fc])
            dg = (dact * uf * (sg * (1.0 + gf * (1.0 - sg)))).astype(BF16)
            du = (dact * silu).astype(BF16)
            dgu_ref[0, :, c0:c0 + fc] = dg
            dgu_ref[0, :, f + c0:f + c0 + fc] = du
            dh = dh + _dot(dg, wgut_ref[c0:c0 + fc, :]) + _dot(du, wgut_ref[f + c0:f + c0 + fc, :])
        xf = x_ref[0]
        r = _rstd(xf)
        xh = xf * r
        n = xh * g_ref[...]
        h_ref[0] = (n * (1.0 + sc) + sh).astype(BF16)
        _accumulate(dmod_ref, (0, 0), _colsum(dh))
        _accumulate(dmod_ref, (0, 1), _colsum(dh * n))
        dn = dh * (1.0 + sc)
        _accumulate(dg_ref, (0, 0), _colsum(dn * xh))
        dyg = dn * g_ref[...]
        dx2_ref[0] = dx3 + r * (dyg - xh * jnp.mean(dyg * xh, axis=-1, keepdims=True))

    return pl.pallas_call(
        body, name="ffn_bwd", grid=(b, s // tm),
        out_shape=(jax.ShapeDtypeStruct((b, s, d), F32), jax.ShapeDtypeStruct((b, s, d), BF16),
                   jax.ShapeDtypeStruct((b, s, f), BF16), jax.ShapeDtypeStruct((b, s, 2 * f), BF16),
                   jax.ShapeDtypeStruct((b, s, d), BF16), jax.ShapeDtypeStruct((b, 3, 1, d), F32),
                   jax.ShapeDtypeStruct((b, 1, 1, d), F32)),
        in_specs=[_rows(tm, d), _rows(tm, d), _per_b(6, d), _const((1, d)), _rows(tm, 2 * f), _rows(tm, d),
                  _const(w_gu_t.shape), _const(w_down_t.shape)],
        out_specs=(_rows(tm, d), _rows(tm, d), _rows(tm, f), _rows(tm, 2 * f), _rows(tm, d), _per_b(3, d), _per_b(1, d)),
        compiler_params=_params(("arbitrary", "arbitrary")),
    )(dx3, x2, mod, g_ffn, gu, o2, w_gu_t, w_down_t)


def _mix_post_bwd(dx2, mod, o1, y_attn, y_conv, gate, w_out_t, w_o_t):
    b, s, d = dx2.shape
    hw = w_o_t.shape[1]
    tm = _row_tile(s)

    def body(dx_ref, mod_ref, o1_ref, ya_ref, yc_ref, gate_ref, woutt_ref, wot_ref, do1_ref, dya_ref, dyc_ref, dgate_ref,
             y_ref, do_ref, dgt_ref):
        _zero_at_first_tile(dgt_ref)
        dx = dx_ref[0]
        do1 = (dx * mod_ref[0, 2]).astype(BF16)
        do1_ref[0] = do1
        _accumulate(dgt_ref, (0, 0), _colsum(dx * o1_ref[0].astype(F32)))
        dy = _dot(do1, woutt_ref[...])
        gv = gate_ref[0]
        sa, sb = _sig(gv[:, :d].astype(F32)), _sig(gv[:, d:].astype(F32))
        ya, yc = ya_ref[0].astype(F32), yc_ref[0].astype(F32)
        y_ref[0] = (sa * ya + sb * yc).astype(BF16)
        dya = (dy * sa).astype(BF16)
        dya_ref[0] = dya
        dyc_ref[0] = (dy * sb).astype(BF16)
        dgate_ref[0, :, :d] = (dy * ya * sa * (1.0 - sa)).astype(BF16)
        dgate_ref[0, :, d:] = (dy * yc * sb * (1.0 - sb)).astype(BF16)
        do_ref[0] = _dot(dya, wot_ref[...]).astype(BF16)

    row = jax.ShapeDtypeStruct((b, s, d), BF16)
    return pl.pallas_call(
        body, name="mix_post_bwd", grid=(b, s // tm),
        out_shape=(row, row, row, jax.ShapeDtypeStruct((b, s, 2 * d), BF16), row,
                   jax.ShapeDtypeStruct((b, s, hw), BF16), jax.ShapeDtypeStruct((b, 1, 1, d), F32)),
        in_specs=[_rows(tm, d), _per_b(6, d), _rows(tm, d), _rows(tm, d), _rows(tm, d), _rows(tm, 2 * d),
                  _const(w_out_t.shape), _const(w_o_t.shape)],
        out_specs=(_rows(tm, d), _rows(tm, d), _rows(tm, d), _rows(tm, 2 * d), _rows(tm, d), _rows(tm, hw), _per_b(1, d)),
        compiler_params=_params(("arbitrary", "arbitrary")),
    )(dx2, mod, o1, y_attn, y_conv, gate, w_out_t, w_o_t)


def _conv_bwd(dyc, u, glu, w_dw, g_cn, b_cn, w_pw2_t):
    b, s, cc = u.shape
    d = dyc.shape[2]
    tm = _row_tile(s)
    nt = s // tm
    te = tm + HALO
    rc = _conv_rows(cc)

    def body(dyc_ref, dycn_ref, u_ref, un_ref, glu_ref, glup_ref, w_ref, g_ref, bcn_ref, wpt_ref, dglu_ref, s_ref, dw_ref,
             small_ref, du_ext, uin_ext, duin_all):
        _zero_at_first_tile(dw_ref, small_ref)
        st = pl.program_id(1)
        dy_all = jnp.concatenate([dyc_ref[0], dycn_ref[0]], axis=0)
        u_all = jnp.concatenate([u_ref[0], un_ref[0]], axis=0).astype(F32)
        ds = _dot(dy_all, wpt_ref[...])
        mu = jnp.mean(u_all, axis=-1, keepdims=True)
        uc = u_all - mu
        rstd = lax.rsqrt(jnp.mean(uc * uc, axis=-1, keepdims=True) + EPS)
        uh = uc * rstd
        ln = uh * g_ref[...] + bcn_ref[...]
        sg = _sig(ln)
        s_ref[0] = (ln * sg)[:tm].astype(BF16)
        dln = ds * (sg * (1.0 + ln * (1.0 - sg)))
        duh = dln * g_ref[...]
        du = rstd * (duh - jnp.mean(duh, axis=-1, keepdims=True) - uh * jnp.mean(duh * uh, axis=-1, keepdims=True))
        row = lax.broadcasted_iota(jnp.int32, (te, 1), 0)
        du = jnp.where(jnp.logical_and(st == nt - 1, row >= tm), 0.0, du)
        du_ext[0, pl.ds(0, te), :] = du
        _shifted_copies(du_ext, te)
        du_cur = du[:tm]
        _accumulate(small_ref, (0, 0), _colsum((dln * uh)[:tm]))
        _accumulate(small_ref, (0, 1), _colsum(dln[:tm]))
        _accumulate(small_ref, (0, 2), _colsum(du_cur))
        uin_ext[0, pl.ds(0, HALO), :] = jnp.where(st == 0, 0.0, _glu(glup_ref[0], cc))
        gv = glu_ref[0]
        ga, gb = gv[:, :cc].astype(F32), gv[:, cc:].astype(F32)
        sgb = _sig(gb)
        uin_ext[0, pl.ds(HALO, tm), :] = ga * sgb
        _shifted_copies(uin_ext, te)
        for c0 in range(0, tm, rc):
            du_c = du_ext[0, pl.ds(c0, rc), :]
            acc = jnp.zeros((rc, cc), F32)
            for kk in range(CONV_W):
                acc = acc + w_ref[pl.ds(kk, 1), :] * _window(du_ext, c0 + CONV_W - 1 - kk, rc)
                prod = du_c * _window(uin_ext, c0 + HALO - CONV_W + 1 + kk, rc)
                taps = pl.ds(8 * kk, 8)
                dw_ref[0, taps, :] = dw_ref[0, taps, :] + jnp.sum(prod.reshape(rc // 8, 8, cc), axis=0)
            duin_all[pl.ds(c0, rc), :] = acc
        duin = duin_all[...]
        dglu_ref[0, :, :cc] = (duin * sgb).astype(BF16)
        dglu_ref[0, :, cc:] = (duin * ga * sgb * (1.0 - sgb)).astype(BF16)

    return pl.pallas_call(
        body, name="conv_bwd", grid=(b, nt),
        out_shape=(jax.ShapeDtypeStruct((b, s, 2 * cc), BF16), jax.ShapeDtypeStruct((b, s, cc), BF16),
                   jax.ShapeDtypeStruct((b, 8 * HALO, cc), F32), jax.ShapeDtypeStruct((b, 3, 1, cc), F32)),
        in_specs=[_rows(tm, d), _halo_next(tm, d, nt), _rows(tm, cc), _halo_next(tm, cc, nt), _rows(tm, 2 * cc),
                  _halo_prev(tm, 2 * cc), _const(w_dw.shape), _const((1, cc)), _const((1, cc)), _const(w_pw2_t.shape)],
        out_specs=(_rows(tm, 2 * cc), _rows(tm, cc), pl.BlockSpec((1, 8 * HALO, cc), lambda i, j: (i, 0, 0)),
                   _per_b(3, cc)),
        scratch_shapes=[pltpu.VMEM((8, te + 8, cc), F32), pltpu.VMEM((8, te + 8, cc), F32), pltpu.VMEM((tm, cc), F32)],
        compiler_params=_params(("arbitrary", "arbitrary")),
    )(dyc, dyc, u, u, glu, glu, w_dw, g_cn, b_cn, w_pw2_t)


def _flash_bwd(q, k, v, o, lse, do, scatter=()):
    b, s, hw = q.shape
    nh, hp = hw // HEAD_PAD, HEAD_PAD
    t = _row_tile(s)
    nt = s // t
    n = len(scatter)
    grid = (b, nh, nt)

    def to_row(sel, cols):
        return lax.dot_general(sel, cols, (((1,), (1,)), ((), ())), preferred_element_type=F32,
                               precision=lax.Precision.HIGHEST)

    def body(q_ref, k_ref, v_ref, o_ref, lse_ref, do_ref, *rest):
        dqt_ref, dk_ref, dv_ref = rest[n:n + 3]
        lse_row, delta_row = rest[2 * n + 3:2 * n + 5]
        finish = _riding_exchange("scatter", n, rest[:n] + rest[n + 3:2 * n + 3] + rest[2 * n + 5:], grid)
        j = pl.program_id(2)

        @pl.when(j == 0)
        def _():
            dqt_ref[...] = jnp.zeros_like(dqt_ref)
            first_lane = (lax.broadcasted_iota(jnp.int32, (8, hp), 1) == 0).astype(F32)
            for i in range(nt):
                rows = pl.ds(i * t, t)
                prod = do_ref[0, rows, :].astype(F32) * o_ref[0, rows, :].astype(F32)
                delta_row[i] = to_row(jnp.ones((8, hp), F32), prod)
                lse_row[i] = to_row(first_lane, jnp.broadcast_to(lse_ref[0, 0, rows, :], (t, hp)))

        kv, vv = k_ref[0], v_ref[0]
        kt = kv.T
        query_not_before_key = (lax.broadcasted_iota(jnp.int32, (t, t), 1) >= lax.broadcasted_iota(jnp.int32, (t, t), 0))

        def step(i, carry, masked):
            dk, dv = carry
            rows = pl.ds(pl.multiple_of(i * t, t), t)
            qv, dov = q_ref[0, rows, :], do_ref[0, rows, :]
            pt = jnp.exp((_dot_nt(kv, qv) - lse_row[i, 0:1, :]).astype(BF16))
            if masked:
                pt = jnp.where(query_not_before_key, pt, jnp.zeros((), BF16))
            dv = dv + _dot(pt, dov)
            dst = pt * (_dot_nt(vv, dov) - delta_row[i, 0:1, :]).astype(BF16)
            dk = dk + _dot(dst, qv)
            dqt_ref[0, i] = dqt_ref[0, i] + _dot(kt, dst)
            return dk, dv

        carry = step(j, (jnp.zeros((t, hp), F32), jnp.zeros((t, hp), F32)), True)
        dk, dv = lax.fori_loop(j + 1, nt, lambda i, cr: step(i, cr, False), carry)
        dk_ref[0] = dk.astype(BF16)
        dv_ref[0] = dv.astype(BF16)
        if finish:
            finish()

    tile = pl.BlockSpec((1, t, hp), lambda bb, hh, jj: (bb, jj, hh))
    full = pl.BlockSpec((1, s, hp), lambda bb, hh, jj: (bb, 0, hh))
    hbm = [pl.BlockSpec(memory_space=pl.ANY)] * n
    outs = pl.pallas_call(
        body, name="flash_bwd_scatter" if n else "flash_bwd", grid=grid,
        out_shape=[jax.ShapeDtypeStruct((b, nt, hw, t), F32), jax.ShapeDtypeStruct((b, s, hw), BF16),
                   jax.ShapeDtypeStruct((b, s, hw), BF16)] + _DirectExchange.out_shapes("scatter", scatter),
        in_specs=[full, tile, tile, full, pl.BlockSpec((1, 1, s, 1), lambda bb, hh, jj: (bb, hh, 0, 0)), full] + hbm,
        out_specs=[pl.BlockSpec((1, nt, hp, t), lambda bb, hh, jj: (bb, 0, hh, 0)), tile, tile] + hbm,
        scratch_shapes=[pltpu.VMEM((nt, 8, t), F32), pltpu.VMEM((nt, 8, t), F32)]
        + (_DirectExchange.scratch(n) if n else []),
        compiler_params=_params(("arbitrary", "arbitrary", "arbitrary")),
    )(q, k, v, o, lse, do, *scatter)
    return outs[0], outs[1], outs[2], outs[3:]


def _mix_pre_bwd(x, dx2, mod, cos_t, sin_t, g_mix, g_q, g_kv, lat, dq, dk, dv, dglu, dgate, w_lat_t, w_glu_t, w_gate_t,
                 w_q_t, w_k_t, w_v_t):
    b, s, d = x.shape
    ql, kl = g_q.shape[1], g_kv.shape[1]
    wl = lat.shape[2]
    hw = N_HEADS * HEAD_PAD
    tm = _row_tile(s)

    def body(x_ref, dx2_ref, mod_ref, cos_ref, sin_ref, gm_ref, gq_ref, gkv_ref, lat_ref, dq_ref, dk_ref, dv_ref, dglu_ref,
             dgate_ref, wlt_ref, wgt_ref, wtt_ref, wqt_ref, wkt_ref, wvt_ref, dx_ref, dlat_ref, dqr_ref, qn_ref, kvn_ref,
             h_ref, dmod_ref, dgm_ref, dgq_ref, dgkv_ref):
        _zero_at_first_tile(dmod_ref, dgm_ref, dgq_ref, dgkv_ref)
        cos_v, sin_v = cos_ref[0], sin_ref[0]
        latf = lat_ref[0].astype(F32)
        q_lat, kv_lat = latf[:, :ql], latf[:, ql:ql + kl]
        rq, rk = _rstd(q_lat), _rstd(kv_lat)
        qh, kh = q_lat * rq, kv_lat * rk
        qn_ref[0] = (qh * gq_ref[...]).astype(BF16)
        kvn_ref[0] = (kh * gkv_ref[...]).astype(BF16)
        dk_sum = jnp.zeros((tm, HEAD_PAD), F32)
        for h in range(N_HEADS):
            cols = slice(h * HEAD_PAD, (h + 1) * HEAD_PAD)
            dq_head = dq_ref[0, 0, cols, :].T
            dqr_ref[0, :, cols] = _rope_bwd(dq_head * ATTN_SCALE, cos_v, sin_v).astype(BF16)
            dk_sum = dk_sum + dk_ref[0, :, cols].astype(F32)
        dqn = _dot(dqr_ref[0], wqt_ref[...])
        dkvn = _dot(dk_ref[0], wkt_ref[...]) + _dot(dv_ref[0], wvt_ref[...])
        lane = lax.broadcasted_iota(jnp.int32, dk_sum.shape, 1)
        dkr = _rope_bwd(jnp.where(lane >= QK_NOPE, dk_sum, 0.0), cos_v, sin_v)
        _accumulate(dgq_ref, (0, 0), _colsum(dqn * qh))
        _accumulate(dgkv_ref, (0, 0), _colsum(dkvn * kh))
        dqg, dkg = dqn * gq_ref[...], dkvn * gkv_ref[...]
        dlat_ref[0, :, :ql] = (rq * (dqg - qh * jnp.mean(dqg * qh, axis=-1, keepdims=True))).astype(BF16)
        dlat_ref[0, :, ql:ql + kl] = (rk * (dkg - kh * jnp.mean(dkg * kh, axis=-1, keepdims=True))).astype(BF16)
        dlat_ref[0, :, ql + kl:] = dkr.astype(BF16)
        dh = _dot(dlat_ref[0], wlt_ref[...]) + _dot(dglu_ref[0], wgt_ref[...]) + _dot(dgate_ref[0], wtt_ref[...])
        sh, sc = mod_ref[0, 0], mod_ref[0, 1]
        xf = x_ref[0]
        r = _rstd(xf)
        xh = xf * r
        n = xh * gm_ref[...]
        h_ref[0] = (n * (1.0 + sc) + sh).astype(BF16)
        _accumulate(dmod_ref, (0, 0), _colsum(dh))
        _accumulate(dmod_ref, (0, 1), _colsum(dh * n))
        dn = dh * (1.0 + sc)
        _accumulate(dgm_ref, (0, 0), _colsum(dn * xh))
        dyg = dn * gm_ref[...]
        dx_ref[0] = dx2_ref[0] + r * (dyg - xh * jnp.mean(dyg * xh, axis=-1, keepdims=True))

    return pl.pallas_call(
        body, name="mix_pre_bwd", grid=(b, s // tm),
        out_shape=(jax.ShapeDtypeStruct((b, s, d), F32), jax.ShapeDtypeStruct((b, s, wl), BF16),
                   jax.ShapeDtypeStruct((b, s, hw), BF16), jax.ShapeDtypeStruct((b, s, ql), BF16),
                   jax.ShapeDtypeStruct((b, s, kl), BF16), jax.ShapeDtypeStruct((b, s, d), BF16),
                   jax.ShapeDtypeStruct((b, 2, 1, d), F32), jax.ShapeDtypeStruct((b, 1, 1, d), F32),
                   jax.ShapeDtypeStruct((b, 1, 1, ql), F32), jax.ShapeDtypeStruct((b, 1, 1, kl), F32)),
        in_specs=[_rows(tm, d), _rows(tm, d), _per_b(6, d), _rows(tm, HEAD_PAD), _rows(tm, HEAD_PAD), _const((1, d)),
                  _const((1, ql)), _const((1, kl)), _rows(tm, wl),
                  pl.BlockSpec((1, 1, hw, tm), lambda b, s: (b, s, 0, 0)), _rows(tm, hw), _rows(tm, hw),
                  _rows(tm, dglu.shape[2]), _rows(tm, 2 * d), _const(w_lat_t.shape), _const(w_glu_t.shape),
                  _const(w_gate_t.shape), _const(w_q_t.shape), _const(w_k_t.shape), _const(w_v_t.shape)],
        out_specs=(_rows(tm, d), _rows(tm, wl), _rows(tm, hw), _rows(tm, ql), _rows(tm, kl), _rows(tm, d), _per_b(2, d),
                   _per_b(1, d), _per_b(1, ql), _per_b(1, kl)),
        compiler_params=_params(("arbitrary", "arbitrary")),
    )(x, dx2, mod, cos_t, sin_t, g_mix, g_q, g_kv, lat, dq, dk, dv, dglu, dgate, w_lat_t, w_glu_t, w_gate_t, w_q_t, w_k_t,
      w_v_t)


def _matmul_tn(a, bm, name):
    b, s, kd = a.shape
    nd = bm.shape[2]
    tk, tn = _pick(kd, 1536), _pick(nd, 1536)
    ts = _row_tile(s, 2048)

    def body(a_ref, b_ref, o_ref):
        part = _dot_tn(a_ref[0], b_ref[0])
        first = jnp.logical_and(pl.program_id(2) == 0, pl.program_id(3) == 0)

        @pl.when(first)
        def _():
            o_ref[...] = part

        @pl.when(jnp.logical_not(first))
        def _():
            o_ref[...] = o_ref[...] + part

    return pl.pallas_call(
        body, name=name, grid=(kd // tk, nd // tn, b, s // ts),
        out_shape=jax.ShapeDtypeStruct((kd, nd), F32),
        in_specs=[pl.BlockSpec((1, ts, tk), lambda i, j, bb, ss: (bb, ss, i)),
                  pl.BlockSpec((1, ts, tn), lambda i, j, bb, ss: (bb, ss, j))],
        out_specs=pl.BlockSpec((tk, tn), lambda i, j, bb, ss: (i, j)),
        compiler_params=_params(("arbitrary",) * 4),
    )(a, bm)


def _adamw_update(w, g, m, v):
    nm = ADAM_B1 * m + (1.0 - ADAM_B1) * g
    nv = ADAM_B2 * v + (1.0 - ADAM_B2) * (g * g)
    delta = -ADAM_LR * ((nm / (1.0 - ADAM_B1 ** ADAM_STEP)) / (jnp.sqrt(nv / (1.0 - ADAM_B2 ** ADAM_STEP)) + ADAM_EPS)
                        + ADAM_WD * w)
    return delta, nm, nv


def _adamw(w, g, m, v, name):
    shape = w.shape
    cols = shape[-1]
    rows = w.size // cols
    w2, g2, m2, v2 = (t.reshape(rows, cols) for t in (w, g, m, v))
    tr = rows
    if rows * cols * 4 > (1 << 20):
        tr = _div_tile(rows, max(8, (1 << 18) // cols), 8)

    def body(w_ref, g_ref, m_ref, v_ref, d_ref, nm_ref, nv_ref):
        d_ref[...], nm_ref[...], nv_ref[...] = _adamw_update(w_ref[...], g_ref[...], m_ref[...], v_ref[...])

    spec = pl.BlockSpec((tr, cols), lambda i: (i, 0))
    outs = pl.pallas_call(
        body, name=name, grid=(rows // tr,), out_shape=(jax.ShapeDtypeStruct((rows, cols), F32),) * 3,
        in_specs=[spec] * 4, out_specs=(spec,) * 3, compiler_params=_params(("arbitrary",)),
    )(w2, g2, m2, v2)
    return tuple(t.reshape(shape) for t in outs)


def _adamw_reduce(w, m, v, own, got, name):
    shape = w.shape
    cols = shape[-1]
    w2, m2, v2, o2 = (_rows_2d(t, 0) for t in (w, m, v, own))
    g3 = _rows_2d(got, N_DEV - 1)
    rows = w2.shape[0]
    tr = _grad_row_tile(rows, cols)

    def body(w_ref, m_ref, v_ref, o_ref, r_ref, g_ref, d_ref, nm_ref, nv_ref):
        g = o_ref[...].astype(F32)
        for r in range(N_DEV - 1):
            g = g + r_ref[r].astype(F32)
        g_ref[...] = g
        d_ref[...], nm_ref[...], nv_ref[...] = _adamw_update(w_ref[...], g, m_ref[...], v_ref[...])

    spec = pl.BlockSpec((tr, cols), lambda i: (i, 0))
    outs = pl.pallas_call(
        body, name=name, grid=(rows // tr,),
        in_specs=[spec, spec, spec, spec, pl.BlockSpec((N_DEV - 1, tr, cols), lambda i: (0, i, 0))],
        out_specs=(spec,) * 4,
        out_shape=(jax.ShapeDtypeStruct((rows, cols), F32),) * 4, compiler_params=_params(("arbitrary",)),
    )(w2, m2, v2, o2, g3)
    return tuple(t.reshape(shape) for t in outs)


GATHERED = (("w_in", 2), ("w_uq", 2), ("w_ukv", 2), ("w_o_attn", 2), ("w_pw2", 2), ("w_out", 1), ("w_gu", 2), ("w_down", 1))
PRE, POST = (0, 1, 2), (3, 4, 5, 6, 7)


def _from_chunks(chunks, axis):
    _, a, bb = chunks.shape
    if axis == 2:
        return jnp.transpose(chunks, (1, 0, 2)).reshape(a, N_DEV * bb)
    return chunks.reshape(N_DEV * a, bb)


def _to_chunks(full, axis):
    a, bb = full.shape
    if axis == 2:
        return jnp.transpose(full.reshape(a, N_DEV, bb // N_DEV), (1, 0, 2)).astype(BF16)
    return full.reshape(N_DEV, a // N_DEV, bb).astype(BF16)


def _swap_halves(t):
    half = QK_ROPE // 2
    return jnp.concatenate([t[..., half:], t[..., :half]], axis=-1)


def _t(w):
    return jnp.swapaxes(w, -1, -2)


def _pad_rows(t, mult=8):
    return jnp.pad(t, ((0, -t.shape[0] % mult), (0, 0)))


def _pad_last(t, width):
    return jnp.pad(t, ((0, 0),) * (t.ndim - 1) + ((0, width - t.shape[-1]),))


def kernel(x, c, positions, w_ada, b_ada, g_mix, w_in, g_q, w_uq, g_kv, w_ukv, w_o_attn, w_dw, b_dw, g_cn, b_cn, w_pw2, w_out, g_ffn, w_gu, w_down, g_final, loss_target, m_w_ada, m_b_ada, m_g_mix, m_w_in, m_g_q, m_w_uq, m_g_kv, m_w_ukv, m_w_o_attn, m_w_dw, m_b_dw, m_g_cn, m_b_cn, m_w_pw2, m_w_out, m_g_ffn, m_w_gu, m_w_down, m_g_final, v_w_ada, v_b_ada, v_g_mix, v_w_in, v_g_q, v_w_uq, v_g_kv, v_w_ukv, v_w_o_attn, v_w_dw, v_b_dw, v_g_cn, v_b_cn, v_w_pw2, v_w_out, v_g_ffn, v_w_gu, v_w_down, v_g_final):
    weights = dict(w_ada=w_ada, b_ada=b_ada, g_mix=g_mix, w_in=w_in, g_q=g_q, w_uq=w_uq, g_kv=g_kv, w_ukv=w_ukv,
                   w_o_attn=w_o_attn, w_dw=w_dw, b_dw=b_dw, g_cn=g_cn, b_cn=b_cn, w_pw2=w_pw2, w_out=w_out, g_ffn=g_ffn,
                   w_gu=w_gu, w_down=w_down, g_final=g_final)
    mom_m = dict(w_ada=m_w_ada, b_ada=m_b_ada, g_mix=m_g_mix, w_in=m_w_in, g_q=m_g_q, w_uq=m_w_uq, g_kv=m_g_kv,
                 w_ukv=m_w_ukv, w_o_attn=m_w_o_attn, w_dw=m_w_dw, b_dw=m_b_dw, g_cn=m_g_cn, b_cn=m_b_cn, w_pw2=m_w_pw2,
                 w_out=m_w_out, g_ffn=m_g_ffn, w_gu=m_w_gu, w_down=m_w_down, g_final=m_g_final)
    mom_v = dict(w_ada=v_w_ada, b_ada=v_b_ada, g_mix=v_g_mix, w_in=v_w_in, g_q=v_g_q, w_uq=v_w_uq, g_kv=v_g_kv,
                 w_ukv=v_w_ukv, w_o_attn=v_w_o_attn, w_dw=v_w_dw, b_dw=v_b_dw, g_cn=v_g_cn, b_cn=v_b_cn, w_pw2=v_w_pw2,
                 w_out=v_w_out, g_ffn=v_g_ffn, w_gu=v_w_gu, w_down=v_w_down, g_final=v_g_final)
    order = list(weights)

    nb, s, d = x.shape
    nl = w_in.shape[0]
    ql, kl, cc = g_q.shape[1], g_kv.shape[1], g_cn.shape[1]
    h = N_HEADS
    qk = QK_NOPE + QK_ROPE
    xi, yi, ci = lax.axis_index("x"), lax.axis_index("y"), lax.axis_index("c")
    me = 4 * xi + 2 * yi + ci

    shards = [weights[n].astype(BF16) for n, _ in GATHERED]
    gathered_pre = _all_gather([shards[a][0] for a in PRE], vmem=False, name="weight_all_gather")
    n_dw = w_dw.shape[2]
    dw_rows = jnp.pad(w_dw, ((0, 0), (0, HALO - CONV_W), (0, LANES - n_dw))).reshape(nl * HALO, LANES)
    c_all, dw_all = _all_gather([_pad_rows(c), dw_rows], vmem=True, name="cond_all_gather")
    c_full = c_all[:, :nb].reshape(N_DEV * nb, d)
    w_dw_full = jnp.transpose(dw_all.reshape(N_DEV, nl, HALO, LANES)[..., :n_dw], (1, 2, 0, 3)).reshape(nl, HALO, cc)

    o_kr, o_glu, o_gate = ql + kl, ql + kl + QK_ROPE, ql + kl + QK_ROPE + 2 * cc

    def layouts_pre(chunks):
        wi, w_uq_l, w_ukv_l = (_from_chunks(t, GATHERED[a][1]) for a, t in zip(PRE, chunks))
        w_kr = wi[:, o_kr:o_glu]
        wq = w_uq_l.reshape(ql, h, qk)
        wkv = w_ukv_l.reshape(kl, h, QK_NOPE + V_HEAD)
        return dict(
            lat=jnp.concatenate([wi[:, :o_kr], jnp.zeros((d, QK_NOPE), BF16), w_kr, _swap_halves(w_kr)], axis=1),
            glu=wi[:, o_glu:o_gate], gate=wi[:, o_gate:],
            q=jnp.concatenate([wq, _swap_halves(wq[..., QK_NOPE:])], axis=-1).reshape(ql, h * HEAD_PAD),
            k=_pad_last(wkv[..., :QK_NOPE], HEAD_PAD).reshape(kl, h * HEAD_PAD),
            v=_pad_last(wkv[..., QK_NOPE:], HEAD_PAD).reshape(kl, h * HEAD_PAD))

    def layouts_post(chunks):
        w_o_l, w_pw_l, w_out_l, w_gu_l, w_down_l = (_from_chunks(t, GATHERED[a][1]) for a, t in zip(POST, chunks))
        w_o = jnp.pad(w_o_l.reshape(h, V_HEAD, d), ((0, 0), (0, HEAD_PAD - V_HEAD), (0, 0)))
        return dict(o=w_o.reshape(h * HEAD_PAD, d), pw=w_pw_l, out=w_out_l, gu=w_gu_l, down=w_down_l)

    n_ada = w_ada.shape[2]
    b_cols = lax.dynamic_slice_in_dim(b_ada, me * n_ada, n_ada, axis=1).reshape(nl, 1, n_ada)
    mod_part = _ada_fwd(c_full, w_ada, b_cols)
    (mod_all,) = _all_gather([mod_part.reshape(nl * N_DEV * nb, n_ada)], vmem=True, name="mod_all_gather")
    mod_all = jnp.transpose(mod_all.reshape(N_DEV, nl, N_DEV * nb, n_ada), (1, 2, 0, 3)).reshape(nl, N_DEV * nb, 6 * d)
    mod = lax.dynamic_slice_in_dim(mod_all, me * nb, nb, axis=1).reshape(nl, nb, 6, 1, d)

    inv_freq = ROPE_THETA ** (-jnp.arange(0, QK_ROPE, 2, dtype=F32) / QK_ROPE)
    zeros = lambda n: jnp.zeros((n,), F32)
    freq_row = jnp.concatenate([zeros(QK_NOPE), inv_freq, inv_freq, zeros(HEAD_PAD - qk)]).reshape(1, -1)
    ones = jnp.ones((QK_ROPE // 2,), F32)
    sign_row = jnp.concatenate([zeros(QK_NOPE), -ones, ones, zeros(HEAD_PAD - qk)]).reshape(1, -1)
    cos_t, sin_t = _rope_tables(positions.astype(F32).reshape(nb, s, 1), freq_row, sign_row)

    row = lambda t, l: t[l].reshape(1, -1)

    saved, wts = [], []
    xc = x
    for l in range(nl):
        wl = layouts_pre(gathered_pre)
        lat, glu, gate, qh, kh, vh = _mix_pre_fwd(xc, mod[l], cos_t, sin_t, row(g_mix, l), row(g_q, l), row(g_kv, l),
                                                  wl["lat"], wl["glu"], wl["gate"], wl["q"], wl["k"], wl["v"])
        riders = [shards[a][l] for a in POST] + ([shards[a][l + 1] for a in PRE] if l + 1 < nl else [])
        o, lse, arrived = _flash_fwd(qh, kh, vh, gather=riders)
        wl.update(layouts_post(arrived[:len(POST)]))
        gathered_pre = arrived[len(POST):]
        wts.append(wl)
        u, y_conv = _conv_fwd(glu, w_dw_full[l], row(b_dw, l), row(g_cn, l), row(b_cn, l), wl["pw"])
        x2, y_attn, o1 = _mix_post_fwd(xc, mod[l], o, y_conv, gate, wl["o"], wl["out"])
        x3, gu, o2 = _ffn_fwd(x2, mod[l], row(g_ffn, l), wl["gu"], wl["down"])
        saved.append(dict(x=xc, lat=lat, glu=glu, gate=gate, q=qh, k=kh, v=vh, o=o, lse=lse, u=u, y_conv=y_conv, x2=x2,
                          y_attn=y_attn, o1=o1, gu=gu, o2=o2))
        xc = x3

    dx, loss_part, dgf_part = _loss_bwd(xc, loss_target, g_final.reshape(1, d))
    loss = lax.psum(0.5 / d * jnp.sum(loss_part), AXES)

    small_rows, dw_taps, dmod = [None] * nl, [None] * nl, [None] * nl
    own, got = {}, {}
    chunk = lambda g, a: _to_chunks(g[GATHERED[a][0]], GATHERED[a][1])
    mine = lambda t: lax.dynamic_index_in_dim(t, me, axis=0, keepdims=False)
    pending = []
    for l in reversed(range(nl)):
        sv, wl, gw = saved[l], wts[l], {}
        dx2, do2, act, dgu, h2, dmod2, dgffn = _ffn_bwd(dx, sv["x2"], mod[l], row(g_ffn, l), sv["gu"], sv["o2"],
                                                       _t(wl["gu"]), _t(wl["down"]))
        gw["w_gu"] = _matmul_tn(h2, dgu, "grad_w_gu")
        gw["w_down"] = _matmul_tn(act, do2, "grad_w_down")
        do1, dya, dyc, dgate, yv, do_h, dgt1 = _mix_post_bwd(dx2, mod[l], sv["o1"], sv["y_attn"], sv["y_conv"], sv["gate"],
                                                           _t(wl["out"]), _t(wl["o"]))
        gw["w_out"] = _matmul_tn(yv, do1, "grad_w_out")
        dwo = _matmul_tn(sv["o"], dya, "grad_w_o")
        gw["w_o_attn"] = dwo.reshape(h, HEAD_PAD, d)[:, :V_HEAD].reshape(h * V_HEAD, d)
        dglu, s_act, ddw, csmall = _conv_bwd(dyc, sv["u"], sv["glu"], w_dw_full[l], row(g_cn, l), row(b_cn, l), _t(wl["pw"]))
        gw["w_pw2"] = _matmul_tn(s_act, dyc, "grad_w_pw2")
        ready = [chunk(gw, a) for a in POST]
        dq, dk, dv, arrived = _flash_bwd(sv["q"], sv["k"], sv["v"], sv["o"], sv["lse"], do_h, scatter=ready + pending)
        for i, a in enumerate(POST):
            own[l, a], got[l, a] = mine(ready[i]), arrived[i]
        for i, a in enumerate(PRE if pending else ()):
            got[l + 1, a] = arrived[len(POST) + i]
        dx, dlat, dqr, qn, kvn, h1, dmod1, dgm, dgq, dgkv = _mix_pre_bwd(
            sv["x"], dx2, mod[l], cos_t, sin_t, row(g_mix, l), row(g_q, l), row(g_kv, l), sv["lat"], dq, dk, dv, dglu,
            dgate, _t(wl["lat"]), _t(wl["glu"]), _t(wl["gate"]), _t(wl["q"]), _t(wl["k"]), _t(wl["v"]))
        dwl = _matmul_tn(h1, dlat, "grad_w_lat")
        dwg = _matmul_tn(h1, dglu, "grad_w_glu")
        dwt = _matmul_tn(h1, dgate, "grad_w_gate")
        kr0 = o_kr + QK_NOPE
        dkr = dwl[:, kr0:kr0 + QK_ROPE] + _swap_halves(dwl[:, kr0 + QK_ROPE:])
        gw["w_in"] = jnp.concatenate([dwl[:, :o_kr], dkr, dwg, dwt], axis=1)
        dwq = _matmul_tn(qn, dqr, "grad_w_q").reshape(ql, h, HEAD_PAD)
        dwq = jnp.concatenate([dwq[..., :QK_NOPE], dwq[..., QK_NOPE:qk] + _swap_halves(dwq[..., qk:])], axis=-1)
        gw["w_uq"] = dwq.reshape(ql, h * qk)
        dwk = _matmul_tn(kvn, dk, "grad_w_k").reshape(kl, h, HEAD_PAD)
        dwv = _matmul_tn(kvn, dv, "grad_w_v").reshape(kl, h, HEAD_PAD)
        gw["w_ukv"] = jnp.concatenate([dwk[..., :QK_NOPE], dwv[..., :V_HEAD]], axis=-1).reshape(kl, h * (QK_NOPE + V_HEAD))
        pending = [chunk(gw, a) for a in PRE]
        for i, a in enumerate(PRE):
            own[l, a] = mine(pending[i])
        dmod[l] = jnp.concatenate([dmod1[:, :, 0], dgt1[:, :, 0], dmod2[:, :, 0]], axis=1).reshape(nb, 6 * d)
        bsum = lambda t: jnp.sum(t, axis=0).reshape(1, -1)
        cs = jnp.sum(csmall, axis=0)[:, 0]
        small_rows[l] = jnp.concatenate([bsum(dgm), bsum(dgq), bsum(dgkv), cs[2:3], cs[0:1], cs[1:2], bsum(dgffn)], axis=1)
        dw_taps[l] = jnp.sum(ddw.reshape(nb, HALO, 8, cc), axis=(0, 2))
    grad_x = dx

    dmod_rows = _pad_rows(jnp.stack(dmod).reshape(nl * nb, 6 * d))
    (dmod_all,) = _all_gather([dmod_rows], vmem=True, name="dmod_all_gather")
    dmod_full = jnp.transpose(dmod_all[:, :nl * nb].reshape(N_DEV, nl, nb, 6 * d), (1, 0, 2, 3)).reshape(nl, N_DEV * nb, 6 * d)
    dmod_cols = lax.dynamic_slice_in_dim(dmod_full, me * n_ada, n_ada, axis=2)
    grad_w_ada, grad_b_ada = _ada_bwd(c_full, dmod_cols, dmod_full)
    grads = {"w_ada": grad_w_ada, "b_ada": grad_b_ada.reshape(nl, 6 * d)}

    widths = (d, ql, kl, cc, cc, cc, d)
    wsum = sum(widths)
    final_row = _pad_last(jnp.sum(dgf_part, axis=0).reshape(1, d), wsum)
    small2d = _pad_rows(jnp.concatenate(small_rows + [final_row], axis=0))
    taps2d = jnp.concatenate(dw_taps, axis=0)
    small_all, taps_all = _all_gather([small2d, taps2d], vmem=True, name="small_grad_all_gather")
    small_sum, taps_sum = _sum_devices(small_all), _sum_devices(taps_all)
    off = 0
    for n, wdt in zip(("g_mix", "g_q", "g_kv", "b_dw", "g_cn", "b_cn", "g_ffn"), widths):
        grads[n] = small_sum[:nl, off:off + wdt]
        off += wdt
    grads["g_final"] = small_sum[nl, :d]
    taps = taps_sum.reshape(nl, HALO, cc)[:, :CONV_W]
    grads["w_dw"] = lax.dynamic_slice_in_dim(taps, me * n_dw, n_dw, axis=2)

    delta, new_m, new_v = {}, {}, {}
    for n in order:
        if n in grads:
            delta[n], new_m[n], new_v[n] = _adamw(weights[n], grads[n], mom_m[n], mom_v[n], "adamw_" + n)

    for a, t in zip(PRE, _grad_scatter(pending)):
        got[0, a] = t
    for a, (n, _) in enumerate(GATHERED):
        own_n = jnp.stack([own[l, a] for l in range(nl)])
        got_n = jnp.stack([got[l, a] for l in range(nl)], axis=1)
        grads[n], delta[n], new_m[n], new_v[n] = _adamw_reduce(weights[n], mom_m[n], mom_v[n], own_n, got_n, "adamw_" + n)

    return (loss, grad_x, *[grads[n] for n in order], *[delta[n] for n in order], *[new_m[n] for n in order],
            *[new_v[n] for n in order])
```

```python
import jax
import jax.numpy as jnp
from jax import lax
from jax.experimental import pallas as pl
from jax.experimental.pallas import tpu as pltpu

F32, BF16 = jnp.float32, jnp.bfloat16
MESH = pl.DeviceIdType.MESH
AXES = ("x", "y", "c")
N_DEV = 8

N_HEADS = 8
QK_NOPE = 64
QK_ROPE = 32
V_HEAD = 64
HEAD_PAD = 128
CONV_W = 31
HALO = 32
EPS = 1e-6
ROPE_THETA = 10000.0
NEG_INF = -1e30
ATTN_SCALE = (QK_NOPE + QK_ROPE) ** -0.5

ADAM_LR, ADAM_B1, ADAM_B2, ADAM_EPS, ADAM_WD, ADAM_STEP = 0.001, 0.9, 0.999, 1e-08, 0.01, 10

LANES = 128
VMEM_LIMIT = 60 * 1024 * 1024


def _params(sem=None):
    return pltpu.CompilerParams(dimension_semantics=sem, vmem_limit_bytes=VMEM_LIMIT)


def _pick(n, cap):
    if n <= cap:
        return n
    best = None
    for d in range(LANES, cap + 1, LANES):
        if n % d == 0:
            best = d
    assert best is not None, (n, cap)
    return best


def _div_tile(n, cap, mult):
    best = None
    for d in range(mult, min(n, cap) + 1, mult):
        if n % d == 0:
            best = d
    assert best is not None, (n, cap, mult)
    return best


def _row_tile(s, cap=512):
    return cap if s % cap == 0 and s >= 2 * cap else s // 2


def _sig(v):
    return 1.0 / (1.0 + jnp.exp(-v))


def _rstd(v):
    return lax.rsqrt(jnp.mean(v * v, axis=-1, keepdims=True) + EPS)


def _dot(a, b):
    return jnp.dot(a, b, preferred_element_type=F32)


def _dot_nt(a, b):
    return lax.dot_general(a, b, (((1,), (1,)), ((), ())), preferred_element_type=F32)


def _dot_tn(a, b):
    return lax.dot_general(a, b, (((0,), (0,)), ((), ())), preferred_element_type=F32)


def _rope(v, cos_t, sin_t):
    return v * cos_t + pltpu.roll(v, HEAD_PAD - QK_ROPE, 1) * sin_t


def _rope_bwd(dv, cos_t, sin_t):
    return dv * cos_t + pltpu.roll(dv * sin_t, QK_ROPE, 1)


def _const(shape):
    n = len(shape)
    return pl.BlockSpec(shape, lambda *_: (0,) * n, pipeline_mode=pl.Buffered(1))


def _rows(tm, w):
    return pl.BlockSpec((1, tm, w), lambda b, s: (b, s, 0))


def _per_b(r, w):
    return pl.BlockSpec((1, r, 1, w), lambda b, s: (b, 0, 0, 0))


def _all_gather(arrs, vmem, name):
    n = len(arrs)
    space = pltpu.VMEM if vmem else pl.ANY

    def body(*refs):
        x_refs, out_refs = refs[:n], refs[n:2 * n]
        send_sems, recv_sems, local_sems = refs[2 * n:]
        x_, y_, c_ = lax.axis_index("x"), lax.axis_index("y"), lax.axis_index("c")
        me, sibling = (x_, y_, c_), (x_, y_, 1 - c_)
        chips = [(1 - x_, y_), (x_, 1 - y_), (1 - x_, 1 - y_)]

        def copy(a, k, block, to, own=False):
            px, py, pc = block
            slot = out_refs[a].at[4 * px + 2 * py + pc]
            return pltpu.make_async_remote_copy(
                src_ref=x_refs[a] if own else slot, dst_ref=slot, send_sem=send_sems.at[k * n + a],
                recv_sem=recv_sems.at[k * n + a], device_id=to, device_id_type=MESH)

        mine = [pltpu.make_async_copy(x_refs[a], out_refs[a].at[4 * x_ + 2 * y_ + c_], local_sems.at[a]) for a in range(n)]
        sent = []
        for a in range(n):
            mine[a].start()
            sent.append(copy(a, 0, me, sibling, own=True))
            sent += [copy(a, 1 + j, me, (*chip, c_), own=True) for j, chip in enumerate(chips)]
        for cp in sent:
            cp.start()
        for j, chip in enumerate(chips):
            for a in range(n):
                copy(a, 1 + j, (*chip, c_), me).wait_recv()
                passed = copy(a, 4 + j, (*chip, c_), sibling)
                passed.start()
                sent.append(passed)
        for a in range(n):
            copy(a, 0, sibling, me).wait_recv()
            for j, chip in enumerate(chips):
                copy(a, 4 + j, (*chip, 1 - c_), me).wait_recv()
        for cp in sent:
            cp.wait_send()
        for cp in mine:
            cp.wait()

    return pl.pallas_call(
        body, name=name,
        out_shape=[jax.ShapeDtypeStruct((N_DEV,) + t.shape, t.dtype) for t in arrs],
        in_specs=[pl.BlockSpec(memory_space=space)] * n, out_specs=[pl.BlockSpec(memory_space=space)] * n,
        scratch_shapes=[pltpu.SemaphoreType.DMA((7 * n,)), pltpu.SemaphoreType.DMA((7 * n,)), pltpu.SemaphoreType.DMA((n,))],
        compiler_params=pltpu.CompilerParams(vmem_limit_bytes=VMEM_LIMIT),
    )(*arrs)


FLIPS = tuple((fx, fy, fc) for fx in (0, 1) for fy in (0, 1) for fc in (0, 1))[1:]


class _DirectExchange:
    def __init__(self, kind, in_refs, out_refs, send_sems, recv_sems, local_sems):
        n = len(in_refs)
        x_, y_, c_ = lax.axis_index("x"), lax.axis_index("y"), lax.axis_index("c")
        me = 4 * x_ + 2 * y_ + c_
        self.copies, self.local = [], []
        for r, (fx, fy, fc) in enumerate(FLIPS):
            px, py, pc = (1 - x_ if fx else x_), (1 - y_ if fy else y_), (1 - c_ if fc else c_)
            for a in range(n):
                src = in_refs[a] if kind == "gather" else in_refs[a].at[4 * px + 2 * py + pc]
                dst = out_refs[a].at[me] if kind == "gather" else out_refs[a].at[r]
                self.copies.append(pltpu.make_async_remote_copy(
                    src_ref=src, dst_ref=dst, send_sem=send_sems.at[r * n + a], recv_sem=recv_sems.at[r * n + a],
                    device_id=(px, py, pc), device_id_type=MESH))
        if kind == "gather":
            self.local = [pltpu.make_async_copy(in_refs[a], out_refs[a].at[me], local_sems.at[a]) for a in range(n)]

    def start(self):
        for cp in self.local + self.copies:
            cp.start()

    def wait(self):
        for cp in self.copies + self.local:
            cp.wait()

    @staticmethod
    def out_shapes(kind, arrs):
        if kind == "gather":
            return [jax.ShapeDtypeStruct((N_DEV,) + t.shape, t.dtype) for t in arrs]
        return [jax.ShapeDtypeStruct((N_DEV - 1,) + t.shape[1:], t.dtype) for t in arrs]

    @staticmethod
    def scratch(n):
        return [pltpu.SemaphoreType.DMA((7 * n,)), pltpu.SemaphoreType.DMA((7 * n,)), pltpu.SemaphoreType.DMA((n,))]


def _grad_scatter(gs):
    n = len(gs)

    def body(*refs):
        ex = _DirectExchange("scatter", refs[:n], refs[n:2 * n], *refs[2 * n:])
        ex.start()
        ex.wait()

    return pl.pallas_call(
        body, name="grad_scatter", out_shape=_DirectExchange.out_shapes("scatter", gs),
        in_specs=[pl.BlockSpec(memory_space=pl.ANY)] * n, out_specs=[pl.BlockSpec(memory_space=pl.ANY)] * n,
        scratch_shapes=_DirectExchange.scratch(n),
    )(*gs)


def _rows_2d(t, lead):
    return t.reshape((lead, -1, t.shape[-1]) if lead else (-1, t.shape[-1]))


def _grad_row_tile(rows, cols):
    return _div_tile(rows, max(16, (1 << 18) // cols), 16)


def _sum_devices(g):
    _, m, n = g.shape

    def body(g_ref, o_ref):
        s = g_ref[0]
        for j in range(1, N_DEV):
            s = s + g_ref[j]
        o_ref[...] = s

    return pl.pallas_call(body, name="small_grad_sum", out_shape=jax.ShapeDtypeStruct((m, n), F32),
                          compiler_params=pltpu.CompilerParams(vmem_limit_bytes=VMEM_LIMIT))(g)


def _ada_fwd(c_full, w_ada, b_cols):
    nl, d, n = w_ada.shape
    nb = c_full.shape[0]

    def body(c_ref, w_ref, b_ref, o_ref):
        cv = c_ref[...]
        act = cv * _sig(cv)
        o_ref[0] = jnp.dot(act, w_ref[0], preferred_element_type=F32, precision=lax.Precision.HIGHEST) + b_ref[0]

    return pl.pallas_call(
        body, name="ada_fwd", grid=(nl,), out_shape=jax.ShapeDtypeStruct((nl, nb, n), F32),
        in_specs=[pl.BlockSpec((nb, d), lambda l: (0, 0)), pl.BlockSpec((1, d, n), lambda l: (l, 0, 0)),
                  pl.BlockSpec((1, 1, n), lambda l: (l, 0, 0))],
        out_specs=pl.BlockSpec((1, nb, n), lambda l: (l, 0, 0)), compiler_params=_params(("arbitrary",)),
    )(c_full, w_ada, b_cols)


def _ada_bwd(c_full, dmod_cols, dmod_full):
    nl, nb, n = dmod_cols.shape
    d = c_full.shape[1]
    nfull = dmod_full.shape[2]

    def body(c_ref, dc_ref, df_ref, gw_ref, gb_ref):
        cv = c_ref[...]
        act = cv * _sig(cv)
        gw_ref[0] = lax.dot_general(act, dc_ref[0], (((0,), (0,)), ((), ())), preferred_element_type=F32,
                                    precision=lax.Precision.HIGHEST)
        gb_ref[0] = jnp.sum(df_ref[0], axis=0, keepdims=True)

    return pl.pallas_call(
        body, name="ada_bwd", grid=(nl,),
        out_shape=(jax.ShapeDtypeStruct((nl, d, n), F32), jax.ShapeDtypeStruct((nl, 1, nfull), F32)),
        in_specs=[pl.BlockSpec((nb, d), lambda l: (0, 0)), pl.BlockSpec((1, nb, n), lambda l: (l, 0, 0)),
                  pl.BlockSpec((1, nb, nfull), lambda l: (l, 0, 0))],
        out_specs=(pl.BlockSpec((1, d, n), lambda l: (l, 0, 0)), pl.BlockSpec((1, 1, nfull), lambda l: (l, 0, 0))),
        compiler_params=_params(("arbitrary",)),
    )(c_full, dmod_cols, dmod_full)


def _rope_tables(pos, freq_row, sign_row):
    b, s, _ = pos.shape
    tm = _row_tile(s)

    def body(p_ref, f_ref, g_ref, c_ref, s_ref):
        ang = p_ref[0] * f_ref[...]
        lane = lax.broadcasted_iota(jnp.int32, ang.shape, 1)
        c_ref[0] = jnp.where(lane < QK_NOPE, 1.0, jnp.where(lane < QK_NOPE + QK_ROPE, jnp.cos(ang), 0.0))
        s_ref[0] = g_ref[...] * jnp.sin(ang)

    return pl.pallas_call(
        body, name="rope_tables", grid=(b, s // tm),
        out_shape=(jax.ShapeDtypeStruct((b, s, HEAD_PAD), F32),) * 2,
        in_specs=[_rows(tm, 1), pl.BlockSpec((1, HEAD_PAD), lambda i, j: (0, 0)),
                  pl.BlockSpec((1, HEAD_PAD), lambda i, j: (0, 0))],
        out_specs=(_rows(tm, HEAD_PAD),) * 2, compiler_params=_params(("arbitrary", "arbitrary")),
    )(pos, freq_row, sign_row)


def _mix_pre_fwd(x, mod, cos_t, sin_t, g_mix, g_q, g_kv, w_lat, w_glu, w_gate, w_q, w_k, w_v):
    b, s, d = x.shape
    ql, kl = g_q.shape[1], g_kv.shape[1]
    wl, wg, wt = w_lat.shape[1], w_glu.shape[1], w_gate.shape[1]
    hw = N_HEADS * HEAD_PAD
    tm = _row_tile(s)

    def body(x_ref, mod_ref, cos_ref, sin_ref, gm_ref, gq_ref, gkv_ref, wlat_ref, wglu_ref, wgate_ref, wq_ref, wk_ref,
             wv_ref, lat_ref, glu_ref, gate_ref, q_ref, k_ref, v_ref):
        xf = x_ref[0]
        sh, sc = mod_ref[0, 0], mod_ref[0, 1]
        hb = ((xf * _rstd(xf) * gm_ref[...]) * (1.0 + sc) + sh).astype(BF16)
        glu_ref[0] = _dot(hb, wglu_ref[...]).astype(BF16)
        gate_ref[0] = _dot(hb, wgate_ref[...]).astype(BF16)
        lat = _dot(hb, wlat_ref[...]).astype(BF16)
        lat_ref[0] = lat
        latf = lat.astype(F32)
        q_lat, kv_lat, kr_sec = latf[:, :ql], latf[:, ql:ql + kl], latf[:, ql + kl:]
        qn = (q_lat * _rstd(q_lat) * gq_ref[...]).astype(BF16)
        kvn = (kv_lat * _rstd(kv_lat) * gkv_ref[...]).astype(BF16)
        cos_v, sin_v = cos_ref[0], sin_ref[0]
        lane = lax.broadcasted_iota(jnp.int32, kr_sec.shape, 1)
        kr = jnp.where(lane >= QK_NOPE, _rope(kr_sec, cos_v, sin_v), 0.0)
        q_all, k_all, v_all = _dot(qn, wq_ref[...]), _dot(kvn, wk_ref[...]), _dot(kvn, wv_ref[...])
        vlane = lax.broadcasted_iota(jnp.int32, v_all.shape, 1)
        v_ref[0] = jnp.where(vlane % HEAD_PAD == V_HEAD, 1.0, v_all).astype(BF16)
        for h in range(N_HEADS):
            cols = slice(h * HEAD_PAD, (h + 1) * HEAD_PAD)
            q_ref[0, :, cols] = (_rope(q_all[:, cols], cos_v, sin_v) * ATTN_SCALE).astype(BF16)
            k_ref[0, :, cols] = (k_all[:, cols] + kr).astype(BF16)

    hshape = jax.ShapeDtypeStruct((b, s, hw), BF16)
    return pl.pallas_call(
        body, name="mix_pre_fwd", grid=(b, s // tm),
        out_shape=(jax.ShapeDtypeStruct((b, s, wl), BF16), jax.ShapeDtypeStruct((b, s, wg), BF16),
                   jax.ShapeDtypeStruct((b, s, wt), BF16), hshape, hshape, hshape),
        in_specs=[_rows(tm, d), _per_b(6, d), _rows(tm, HEAD_PAD), _rows(tm, HEAD_PAD), _const((1, d)), _const((1, ql)),
                  _const((1, kl)), _const(w_lat.shape), _const(w_glu.shape), _const(w_gate.shape), _const(w_q.shape),
                  _const(w_k.shape), _const(w_v.shape)],
        out_specs=(_rows(tm, wl), _rows(tm, wg), _rows(tm, wt), _rows(tm, hw), _rows(tm, hw), _rows(tm, hw)),
        compiler_params=_params(("arbitrary", "arbitrary")),
    )(x, mod, cos_t, sin_t, g_mix, g_q, g_kv, w_lat, w_glu, w_gate, w_q, w_k, w_v)


def _causal_mask(tq, tk):
    return lax.broadcasted_iota(jnp.int32, (tq, tk), 0) >= lax.broadcasted_iota(jnp.int32, (tq, tk), 1)


def _riding_exchange(kind, n, refs, grid):
    if not n:
        return
    ex = _DirectExchange(kind, refs[:n], refs[n:2 * n], *refs[2 * n:])
    ids = [pl.program_id(a) for a in range(len(grid))]
    first, last = ids[0] == 0, ids[0] == grid[0] - 1
    for a in range(1, len(grid)):
        first, last = jnp.logical_and(first, ids[a] == 0), jnp.logical_and(last, ids[a] == grid[a] - 1)
    pl.when(first)(ex.start)
    return lambda: pl.when(last)(ex.wait)


def _flash_fwd(q, k, v, gather=()):
    b, s, hw = q.shape
    nh, hp = hw // HEAD_PAD, HEAD_PAD
    t = _row_tile(s)
    n = len(gather)
    grid = (b, nh, s // t)

    def body(q_ref, k_ref, v_ref, *rest):
        o_ref, lse_ref = rest[n], rest[n + 1]
        finish = _riding_exchange("gather", n, rest[:n] + rest[n + 2:], grid)
        i = pl.program_id(2)
        qv = q_ref[0]

        def step(j, carry, masked):
            m, acc = carry
            rows = pl.ds(pl.multiple_of(j * t, t), t)
            sc = _dot_nt(qv, k_ref[0, rows, :])
            if masked:
                sc = jnp.where(_causal_mask(t, t), sc, NEG_INF)
            m_new = jnp.maximum(m, jnp.max(sc, axis=-1, keepdims=True))
            p = jnp.exp((sc - m_new).astype(BF16))
            acc = jnp.exp(m - m_new) * acc + _dot(p, v_ref[0, rows, :])
            return m_new, acc

        init = (jnp.full((t, 1), NEG_INF, F32), jnp.zeros((t, hp), F32))
        carry = lax.fori_loop(0, i, lambda j, cr: step(j, cr, False), init)
        m, acc = step(i, carry, True)
        lane = lax.broadcasted_iota(jnp.int32, acc.shape, 1)
        l = jnp.sum(jnp.where(lane == V_HEAD, acc, 0.0), axis=-1, keepdims=True)
        o_ref[0] = (acc / l).astype(BF16)
        lse_ref[0, 0] = m + jnp.log(l)
        if finish:
            finish()

    tile = pl.BlockSpec((1, t, hp), lambda bb, hh, ii: (bb, ii, hh))
    full = pl.BlockSpec((1, s, hp), lambda bb, hh, ii: (bb, 0, hh))
    hbm = [pl.BlockSpec(memory_space=pl.ANY)] * n
    outs = pl.pallas_call(
        body, name="flash_fwd_gather" if n else "flash_fwd", grid=grid,
        out_shape=[jax.ShapeDtypeStruct((b, s, hw), BF16), jax.ShapeDtypeStruct((b, nh, s, 1), F32)]
        + _DirectExchange.out_shapes("gather", gather),
        in_specs=[tile, full, full] + hbm,
        out_specs=[tile, pl.BlockSpec((1, 1, t, 1), lambda bb, hh, ii: (bb, hh, ii, 0))] + hbm,
        scratch_shapes=_DirectExchange.scratch(n) if n else [],
        compiler_params=_params(("arbitrary", "arbitrary", "arbitrary")),
    )(q, k, v, *gather)
    return outs[0], outs[1], outs[2:]


def _halo_prev(tm, w):
    r = tm // HALO
    return pl.BlockSpec((1, HALO, w), lambda b, s: (b, jnp.maximum(s * r - 1, 0), 0))


def _halo_next(tm, w, n_tiles):
    r = tm // HALO
    return pl.BlockSpec((1, HALO, w), lambda b, s: (b, jnp.minimum((s + 1) * r, n_tiles * r - 1), 0))


def _conv_rows(cc):
    return max(8, 16 * 8 * LANES // cc)


def _shifted_copies(buf, rows):
    buf[0, pl.ds(rows, 8), :] = jnp.zeros((8, buf.shape[2]), buf.dtype)
    for s in range(1, 8):
        buf[s, pl.ds(0, rows), :] = buf[0, pl.ds(s, rows), :]


def _window(buf, start, rows):
    return buf[start % 8, pl.ds(start - start % 8, rows), :]


def _glu(v, cc):
    a, g = v[:, :cc].astype(F32), v[:, cc:].astype(F32)
    return a * _sig(g)


def _conv_fwd(glu, w_dw, b_dw, g_cn, b_cn, w_pw2):
    b, s, w2 = glu.shape
    cc = w2 // 2
    d = w_pw2.shape[1]
    tm = _row_tile(s)

    rc = _conv_rows(cc)
    te = tm + HALO

    def body(cur_ref, prev_ref, w_ref, bdw_ref, g_ref, bcn_ref, wp_ref, u_ref, y_ref, ext, u_all):
        first = pl.program_id(1) == 0
        ext[0, pl.ds(0, HALO), :] = jnp.where(first, 0.0, _glu(prev_ref[0], cc))
        ext[0, pl.ds(HALO, tm), :] = _glu(cur_ref[0], cc)
        _shifted_copies(ext, te)
        for c0 in range(0, tm, rc):
            acc = jnp.zeros((rc, cc), F32) + bdw_ref[...]
            for kk in range(CONV_W):
                acc = acc + w_ref[pl.ds(kk, 1), :] * _window(ext, c0 + HALO - CONV_W + 1 + kk, rc)
            u_all[pl.ds(c0, rc), :] = acc
        ub = u_all[...].astype(BF16)
        u_ref[0] = ub
        uf = ub.astype(F32)
        mu = jnp.mean(uf, axis=-1, keepdims=True)
        uc = uf - mu
        ln = uc * lax.rsqrt(jnp.mean(uc * uc, axis=-1, keepdims=True) + EPS) * g_ref[...] + bcn_ref[...]
        y_ref[0] = _dot((ln * _sig(ln)).astype(BF16), wp_ref[...]).astype(BF16)

    return pl.pallas_call(
        body, name="conv_fwd", grid=(b, s // tm),
        out_shape=(jax.ShapeDtypeStruct((b, s, cc), BF16), jax.ShapeDtypeStruct((b, s, d), BF16)),
        in_specs=[_rows(tm, w2), _halo_prev(tm, w2), _const(w_dw.shape), _const((1, cc)), _const((1, cc)), _const((1, cc)),
                  _const(w_pw2.shape)],
        out_specs=(_rows(tm, cc), _rows(tm, d)),
        scratch_shapes=[pltpu.VMEM((8, te + 8, cc), F32), pltpu.VMEM((tm, cc), F32)],
        compiler_params=_params(("arbitrary", "arbitrary")),
    )(glu, glu, w_dw, b_dw, g_cn, b_cn, w_pw2)


def _mix_post_fwd(x, mod, o, y_conv, gate, w_o, w_out):
    b, s, d = x.shape
    hw = o.shape[2]
    tm = _row_tile(s)

    def body(x_ref, mod_ref, o_ref, yc_ref, gate_ref, wo_ref, wout_ref, x2_ref, ya_ref, o1_ref):
        yab = _dot(o_ref[0], wo_ref[...]).astype(BF16)
        ya_ref[0] = yab
        gv = gate_ref[0]
        y = _sig(gv[:, :d].astype(F32)) * yab.astype(F32) + _sig(gv[:, d:].astype(F32)) * yc_ref[0].astype(F32)
        o1 = _dot(y.astype(BF16), wout_ref[...])
        o1_ref[0] = o1.astype(BF16)
        x2_ref[0] = x_ref[0] + mod_ref[0, 2] * o1

    return pl.pallas_call(
        body, name="mix_post_fwd", grid=(b, s // tm),
        out_shape=(jax.ShapeDtypeStruct((b, s, d), F32), jax.ShapeDtypeStruct((b, s, d), BF16),
                   jax.ShapeDtypeStruct((b, s, d), BF16)),
        in_specs=[_rows(tm, d), _per_b(6, d), _rows(tm, hw), _rows(tm, d), _rows(tm, 2 * d), _const(w_o.shape),
                  _const(w_out.shape)],
        out_specs=(_rows(tm, d), _rows(tm, d), _rows(tm, d)),
        compiler_params=_params(("arbitrary", "arbitrary")),
    )(x, mod, o, y_conv, gate, w_o, w_out)


def _ffn_fwd(x2, mod, g_ffn, w_gu, w_down):
    b, s, d = x2.shape
    f = w_down.shape[0]
    fc = _pick(f, 512)
    tm = _row_tile(s)

    def body(x_ref, mod_ref, g_ref, wgu_ref, wdn_ref, x3_ref, gu_ref, o2_ref, h_ref):
        xf = x_ref[0]
        hb = ((xf * _rstd(xf) * g_ref[...]) * (1.0 + mod_ref[0, 4]) + mod_ref[0, 3]).astype(BF16)
        h_ref[0] = hb
        o2 = jnp.zeros((tm, d), F32)
        for c0 in range(0, f, fc):
            gb = _dot(hb, wgu_ref[:, c0:c0 + fc]).astype(BF16)
            ub = _dot(hb, wgu_ref[:, f + c0:f + c0 + fc]).astype(BF16)
            gu_ref[0, :, c0:c0 + fc] = gb
            gu_ref[0, :, f + c0:f + c0 + fc] = ub
            gf = gb.astype(F32)
            act = (gf * _sig(gf) * ub.astype(F32)).astype(BF16)
            o2 = o2 + _dot(act, wdn_ref[c0:c0 + fc, :])
        o2_ref[0] = o2.astype(BF16)
        x3_ref[0] = xf + mod_ref[0, 5] * o2

    return pl.pallas_call(
        body, name="ffn_fwd", grid=(b, s // tm),
        out_shape=(jax.ShapeDtypeStruct((b, s, d), F32), jax.ShapeDtypeStruct((b, s, 2 * f), BF16),
                   jax.ShapeDtypeStruct((b, s, d), BF16), jax.ShapeDtypeStruct((b, s, d), BF16)),
        in_specs=[_rows(tm, d), _per_b(6, d), _const((1, d)), _const(w_gu.shape), _const(w_down.shape)],
        out_specs=(_rows(tm, d), _rows(tm, 2 * f), _rows(tm, d), _rows(tm, d)),
        compiler_params=_params(("arbitrary", "arbitrary")),
    )(x2, mod, g_ffn, w_gu, w_down)


def _zero_at_first_tile(*refs):
    @pl.when(pl.program_id(1) == 0)
    def _():
        for ref in refs:
            ref[...] = jnp.zeros_like(ref)


def _accumulate(ref, idx, val):
    ref[idx] = ref[idx] + val


def _colsum(v):
    return jnp.sum(v, axis=0, keepdims=True)


def _loss_bwd(x, target, g_final):
    b, s, d = x.shape
    tm = _row_tile(s)

    def body(x_ref, t_ref, g_ref, dx_ref, loss_ref, dg_ref):
        _zero_at_first_tile(loss_ref, dg_ref)
        xf = x_ref[0]
        r = _rstd(xf)
        xh = xf * r
        diff = xh * g_ref[...] - t_ref[0]
        _accumulate(loss_ref, (0, 0), _colsum(diff * diff))
        dy = diff * (1.0 / d)
        _accumulate(dg_ref, (0, 0), _colsum(dy * xh))
        dyg = dy * g_ref[...]
        dx_ref[0] = r * (dyg - xh * jnp.mean(dyg * xh, axis=-1, keepdims=True))

    return pl.pallas_call(
        body, name="loss_bwd", grid=(b, s // tm),
        out_shape=(jax.ShapeDtypeStruct((b, s, d), F32), jax.ShapeDtypeStruct((b, 1, 1, d), F32),
                   jax.ShapeDtypeStruct((b, 1, 1, d), F32)),
        in_specs=[_rows(tm, d), _rows(tm, d), _const((1, d))],
        out_specs=(_rows(tm, d), _per_b(1, d), _per_b(1, d)),
        compiler_params=_params(("arbitrary", "arbitrary")),
    )(x, target, g_final)


def _ffn_bwd(dx3, x2, mod, g_ffn, gu, o2, w_gu_t, w_down_t):
    b, s, d = x2.shape
    f = w_down_t.shape[1]
    fc = _pick(f, 512)
    tm = _row_tile(s)

    def body(dx3_ref, x_ref, mod_ref, g_ref, gu_ref, o2_ref, wgut_ref, wdnt_ref, dx2_ref, do2_ref, act_ref, dgu_ref,
             dmod_ref, dg_ref):
        _zero_at_first_tile(dmod_ref, dg_ref)
        dx3 = dx3_ref[0]
        sh, sc, gt = mod_ref[0, 3], mod_ref[0, 4], mod_ref[0, 5]
        do2 = (dx3 * gt).astype(BF16)
        do2_ref[0] = do2
        _accumulate(dmod_ref, (0, 2), _colsum(dx3 * o2_ref[0].astype(F32)))
        dh = jnp.zeros((tm, d), F32)
        for c0 in range(0, f, fc):
            gf = gu_ref[0, :, c0:c0 + fc].astype(F32)
            uf = gu_ref[0, :, f + c0:f + c0 + fc].astype(F32)
            sg = _sig(gf)
            silu = gf * sg
            act_ref[0, :, c0:c0 + fc] = (silu * uf).astype(BF16)
            dact = _dot(do2, wdnt_ref[:, c0:c0 + fc])
            dg = (dact * uf * (sg * (1.0 + gf * (1.0 - sg)))).astype(BF16)
            du = (dact * silu).astype(BF16)
            dgu_ref[0, :, c0:c0 + fc] = dg
            dgu_ref[0, :, f + c0:f + c0 + fc] = du
            dh = dh + _dot(dg, wgut_ref[c0:c0 + fc, :]) + _dot(du, wgut_ref[f + c0:f + c0 + fc, :])
        xf = x_ref[0]
        r = _rstd(xf)
        xh = xf * r
        n = xh * g_ref[...]
        _accumulate(dmod_ref, (0, 0), _colsum(dh))
        _accumulate(dmod_ref, (0, 1), _colsum(dh * n))
        dn = dh * (1.0 + sc)
        _accumulate(dg_ref, (0, 0), _colsum(dn * xh))
        dyg = dn * g_ref[...]
        dx2_ref[0] = dx3 + r * (dyg - xh * jnp.mean(dyg * xh, axis=-1, keepdims=True))

    single = lambda w: pl.BlockSpec((1, tm, w), lambda i, j: (i, j, 0), pipeline_mode=pl.Buffered(1))
    gu_spec = single(2 * f)
    return pl.pallas_call(
        body, name="ffn_bwd", grid=(b, s // tm),
        out_shape=(jax.ShapeDtypeStruct((b, s, d), F32), jax.ShapeDtypeStruct((b, s, d), BF16),
                   jax.ShapeDtypeStruct((b, s, f), BF16), jax.ShapeDtypeStruct((b, s, 2 * f), BF16),
                   jax.ShapeDtypeStruct((b, 3, 1, d), F32), jax.ShapeDtypeStruct((b, 1, 1, d), F32)),
        in_specs=[_rows(tm, d), _rows(tm, d), _per_b(6, d), _const((1, d)), gu_spec, _rows(tm, d),
                  _const(w_gu_t.shape), _const(w_down_t.shape)],
        out_specs=(_rows(tm, d), _rows(tm, d), single(f), _rows(tm, 2 * f), _per_b(3, d), _per_b(1, d)),
        compiler_params=_params(("arbitrary", "arbitrary")),
    )(dx3, x2, mod, g_ffn, gu, o2, w_gu_t, w_down_t)


def _mix_post_bwd(dx2, mod, o1, y_attn, y_conv, gate, w_out_t, w_o_t):
    b, s, d = dx2.shape
    hw = w_o_t.shape[1]
    tm = _row_tile(s)

    def body(dx_ref, mod_ref, o1_ref, ya_ref, yc_ref, gate_ref, woutt_ref, wot_ref, do1_ref, dya_ref, dyc_ref, dgate_ref,
             y_ref, do_ref, dgt_ref):
        _zero_at_first_tile(dgt_ref)
        dx = dx_ref[0]
        do1 = (dx * mod_ref[0, 2]).astype(BF16)
        do1_ref[0] = do1
        _accumulate(dgt_ref, (0, 0), _colsum(dx * o1_ref[0].astype(F32)))
        dy = _dot(do1, woutt_ref[...])
        gv = gate_ref[0]
        sa, sb = _sig(gv[:, :d].astype(F32)), _sig(gv[:, d:].astype(F32))
        ya, yc = ya_ref[0].astype(F32), yc_ref[0].astype(F32)
        y_ref[0] = (sa * ya + sb * yc).astype(BF16)
        dya = (dy * sa).astype(BF16)
        dya_ref[0] = dya
        dyc_ref[0] = (dy * sb).astype(BF16)
        dgate_ref[0, :, :d] = (dy * ya * sa * (1.0 - sa)).astype(BF16)
        dgate_ref[0, :, d:] = (dy * yc * sb * (1.0 - sb)).astype(BF16)
        do_ref[0] = _dot(dya, wot_ref[...]).astype(BF16)

    row = jax.ShapeDtypeStruct((b, s, d), BF16)
    return pl.pallas_call(
        body, name="mix_post_bwd", grid=(b, s // tm),
        out_shape=(row, row, row, jax.ShapeDtypeStruct((b, s, 2 * d), BF16), row,
                   jax.ShapeDtypeStruct((b, s, hw), BF16), jax.ShapeDtypeStruct((b, 1, 1, d), F32)),
        in_specs=[_rows(tm, d), _per_b(6, d), _rows(tm, d), _rows(tm, d), _rows(tm, d), _rows(tm, 2 * d),
                  _const(w_out_t.shape), _const(w_o_t.shape)],
        out_specs=(_rows(tm, d), _rows(tm, d), _rows(tm, d), _rows(tm, 2 * d), _rows(tm, d), _rows(tm, hw), _per_b(1, d)),
        compiler_params=_params(("arbitrary", "arbitrary")),
    )(dx2, mod, o1, y_attn, y_conv, gate, w_out_t, w_o_t)


def _conv_bwd(dyc, u, glu, w_dw, g_cn, b_cn, w_pw2_t):
    b, s, cc = u.shape
    d = dyc.shape[2]
    tm = _row_tile(s)
    nt = s // tm
    te = tm + HALO
    rc = _conv_rows(cc)

    def body(dyc_ref, dycn_ref, u_ref, un_ref, glu_ref, glup_ref, w_ref, g_ref, bcn_ref, wpt_ref, dglu_ref, s_ref, dw_ref,
             small_ref, du_ext, uin_ext, duin_all):
        _zero_at_first_tile(dw_ref, small_ref)
        st = pl.program_id(1)
        dy_all = jnp.concatenate([dyc_ref[0], dycn_ref[0]], axis=0)
        u_all = jnp.concatenate([u_ref[0], un_ref[0]], axis=0).astype(F32)
        ds = _dot(dy_all, wpt_ref[...])
        mu = jnp.mean(u_all, axis=-1, keepdims=True)
        uc = u_all - mu
        rstd = lax.rsqrt(jnp.mean(uc * uc, axis=-1, keepdims=True) + EPS)
        uh = uc * rstd
        ln = uh * g_ref[...] + bcn_ref[...]
        sg = _sig(ln)
        s_ref[0] = (ln * sg)[:tm].astype(BF16)
        dln = ds * (sg * (1.0 + ln * (1.0 - sg)))
        duh = dln * g_ref[...]
        du = rstd * (duh - jnp.mean(duh, axis=-1, keepdims=True) - uh * jnp.mean(duh * uh, axis=-1, keepdims=True))
        row = lax.broadcasted_iota(jnp.int32, (te, 1), 0)
        du = jnp.where(jnp.logical_and(st == nt - 1, row >= tm), 0.0, du)
        du_ext[0, pl.ds(0, te), :] = du
        _shifted_copies(du_ext, te)
        du_cur = du[:tm]
        _accumulate(small_ref, (0, 0), _colsum((dln * uh)[:tm]))
        _accumulate(small_ref, (0, 1), _colsum(dln[:tm]))
        _accumulate(small_ref, (0, 2), _colsum(du_cur))
        uin_ext[0, pl.ds(0, HALO), :] = jnp.where(st == 0, 0.0, _glu(glup_ref[0], cc))
        gv = glu_ref[0]
        ga, gb = gv[:, :cc].astype(F32), gv[:, cc:].astype(F32)
        sgb = _sig(gb)
        uin_ext[0, pl.ds(HALO, tm), :] = ga * sgb
        _shifted_copies(uin_ext, te)
        for c0 in range(0, tm, rc):
            du_c = du_ext[0, pl.ds(c0, rc), :]
            acc = jnp.zeros((rc, cc), F32)
            for kk in range(CONV_W):
                acc = acc + w_ref[pl.ds(kk, 1), :] * _window(du_ext, c0 + CONV_W - 1 - kk, rc)
                prod = du_c * _window(uin_ext, c0 + HALO - CONV_W + 1 + kk, rc)
                taps = pl.ds(8 * kk, 8)
                dw_ref[0, taps, :] = dw_ref[0, taps, :] + jnp.sum(prod.reshape(rc // 8, 8, cc), axis=0)
            duin_all[pl.ds(c0, rc), :] = acc
        duin = duin_all[...]
        dglu_ref[0, :, :cc] = (duin * sgb).astype(BF16)
        dglu_ref[0, :, cc:] = (duin * ga * sgb * (1.0 - sgb)).astype(BF16)

    return pl.pallas_call(
        body, name="conv_bwd", grid=(b, nt),
        out_shape=(jax.ShapeDtypeStruct((b, s, 2 * cc), BF16), jax.ShapeDtypeStruct((b, s, cc), BF16),
                   jax.ShapeDtypeStruct((b, 8 * HALO, cc), F32), jax.ShapeDtypeStruct((b, 3, 1, cc), F32)),
        in_specs=[_rows(tm, d), _halo_next(tm, d, nt), _rows(tm, cc), _halo_next(tm, cc, nt), _rows(tm, 2 * cc),
                  _halo_prev(tm, 2 * cc), _const(w_dw.shape), _const((1, cc)), _const((1, cc)), _const(w_pw2_t.shape)],
        out_specs=(_rows(tm, 2 * cc), _rows(tm, cc), pl.BlockSpec((1, 8 * HALO, cc), lambda i, j: (i, 0, 0)),
                   _per_b(3, cc)),
        scratch_shapes=[pltpu.VMEM((8, te + 8, cc), F32), pltpu.VMEM((8, te + 8, cc), F32), pltpu.VMEM((tm, cc), F32)],
        compiler_params=_params(("arbitrary", "arbitrary")),
    )(dyc, dyc, u, u, glu, glu, w_dw, g_cn, b_cn, w_pw2_t)


def _flash_bwd(q, k, v, o, lse, do, scatter=()):
    b, s, hw = q.shape
    nh, hp = hw // HEAD_PAD, HEAD_PAD
    t = _row_tile(s)
    nt = s // t
    n = len(scatter)
    grid = (b, nh, nt)

    def to_row(sel, cols):
        return lax.dot_general(sel, cols, (((1,), (1,)), ((), ())), preferred_element_type=F32,
                               precision=lax.Precision.HIGHEST)

    def body(q_ref, k_ref, v_ref, o_ref, lse_ref, do_ref, *rest):
        dqt_ref, dk_ref, dv_ref = rest[n:n + 3]
        lse_row, delta_row = rest[2 * n + 3:2 * n + 5]
        finish = _riding_exchange("scatter", n, rest[:n] + rest[n + 3:2 * n + 3] + rest[2 * n + 5:], grid)
        j = pl.program_id(2)

        @pl.when(j == 0)
        def _():
            dqt_ref[...] = jnp.zeros_like(dqt_ref)
            first_lane = (lax.broadcasted_iota(jnp.int32, (8, hp), 1) == 0).astype(F32)
            for i in range(nt):
                rows = pl.ds(i * t, t)
                prod = do_ref[0, rows, :].astype(F32) * o_ref[0, rows, :].astype(F32)
                delta_row[i] = to_row(jnp.ones((8, hp), F32), prod)
                lse_row[i] = to_row(first_lane, jnp.broadcast_to(lse_ref[0, 0, rows, :], (t, hp)))

        kv, vv = k_ref[0], v_ref[0]
        kt = kv.T
        query_not_before_key = (lax.broadcasted_iota(jnp.int32, (t, t), 1) >= lax.broadcasted_iota(jnp.int32, (t, t), 0))

        def step(i, carry, masked):
            dk, dv = carry
            rows = pl.ds(pl.multiple_of(i * t, t), t)
            qv, dov = q_ref[0, rows, :], do_ref[0, rows, :]
            pt = jnp.exp((_dot_nt(kv, qv) - lse_row[i, 0:1, :]).astype(BF16))
            if masked:
                pt = jnp.where(query_not_before_key, pt, jnp.zeros((), BF16))
            dv = dv + _dot(pt, dov)
            dst = pt * (_dot_nt(vv, dov) - delta_row[i, 0:1, :]).astype(BF16)
            dk = dk + _dot(dst, qv)
            dqt_ref[0, i] = dqt_ref[0, i] + _dot(kt, dst)
            return dk, dv

        carry = step(j, (jnp.zeros((t, hp), F32), jnp.zeros((t, hp), F32)), True)
        dk, dv = lax.fori_loop(j + 1, nt, lambda i, cr: step(i, cr, False), carry)
        dk_ref[0] = dk.astype(BF16)
        dv_ref[0] = dv.astype(BF16)
        if finish:
            finish()

    tile = pl.BlockSpec((1, t, hp), lambda bb, hh, jj: (bb, jj, hh))
    full = pl.BlockSpec((1, s, hp), lambda bb, hh, jj: (bb, 0, hh))
    hbm = [pl.BlockSpec(memory_space=pl.ANY)] * n
    outs = pl.pallas_call(
        body, name="flash_bwd_scatter" if n else "flash_bwd", grid=grid,
        out_shape=[jax.ShapeDtypeStruct((b, nt, hw, t), F32), jax.ShapeDtypeStruct((b, s, hw), BF16),
                   jax.ShapeDtypeStruct((b, s, hw), BF16)] + _DirectExchange.out_shapes("scatter", scatter),
        in_specs=[full, tile, tile, full, pl.BlockSpec((1, 1, s, 1), lambda bb, hh, jj: (bb, hh, 0, 0)), full] + hbm,
        out_specs=[pl.BlockSpec((1, nt, hp, t), lambda bb, hh, jj: (bb, 0, hh, 0)), tile, tile] + hbm,
        scratch_shapes=[pltpu.VMEM((nt, 8, t), F32), pltpu.VMEM((nt, 8, t), F32)]
        + (_DirectExchange.scratch(n) if n else []),
        compiler_params=_params(("arbitrary", "arbitrary", "arbitrary")),
    )(q, k, v, o, lse, do, *scatter)
    return outs[0], outs[1], outs[2], outs[3:]


def _mix_pre_bwd(x, dx2, mod, cos_t, sin_t, g_mix, g_q, g_kv, lat, dq, dk, dv, dglu, dgate, w_lat_t, w_glu_t, w_gate_t,
                 w_q_t, w_k_t, w_v_t):
    b, s, d = x.shape
    ql, kl = g_q.shape[1], g_kv.shape[1]
    wl = lat.shape[2]
    hw = N_HEADS * HEAD_PAD
    tm = _row_tile(s)

    def body(x_ref, dx2_ref, mod_ref, cos_ref, sin_ref, gm_ref, gq_ref, gkv_ref, lat_ref, dq_ref, dk_ref, dv_ref, dglu_ref,
             dgate_ref, wlt_ref, wgt_ref, wtt_ref, wqt_ref, wkt_ref, wvt_ref, dx_ref, dlat_ref, dqr_ref, qn_ref, kvn_ref,
             h_ref, dmod_ref, dgm_ref, dgq_ref, dgkv_ref):
        _zero_at_first_tile(dmod_ref, dgm_ref, dgq_ref, dgkv_ref)
        cos_v, sin_v = cos_ref[0], sin_ref[0]
        latf = lat_ref[0].astype(F32)
        q_lat, kv_lat = latf[:, :ql], latf[:, ql:ql + kl]
        rq, rk = _rstd(q_lat), _rstd(kv_lat)
        qh, kh = q_lat * rq, kv_lat * rk
        qn_ref[0] = (qh * gq_ref[...]).astype(BF16)
        kvn_ref[0] = (kh * gkv_ref[...]).astype(BF16)
        dk_sum = jnp.zeros((tm, HEAD_PAD), F32)
        for h in range(N_HEADS):
            cols = slice(h * HEAD_PAD, (h + 1) * HEAD_PAD)
            dq_head = dq_ref[0, 0, cols, :].T
            dqr_ref[0, :, cols] = _rope_bwd(dq_head * ATTN_SCALE, cos_v, sin_v).astype(BF16)
            dk_sum = dk_sum + dk_ref[0, :, cols].astype(F32)
        dqn = _dot(dqr_ref[0], wqt_ref[...])
        dkvn = _dot(dk_ref[0], wkt_ref[...]) + _dot(dv_ref[0], wvt_ref[...])
        lane = lax.broadcasted_iota(jnp.int32, dk_sum.shape, 1)
        dkr = _rope_bwd(jnp.where(lane >= QK_NOPE, dk_sum, 0.0), cos_v, sin_v)
        _accumulate(dgq_ref, (0, 0), _colsum(dqn * qh))
        _accumulate(dgkv_ref, (0, 0), _colsum(dkvn * kh))
        dqg, dkg = dqn * gq_ref[...], dkvn * gkv_ref[...]
        dlat_ref[0, :, :ql] = (rq * (dqg - qh * jnp.mean(dqg * qh, axis=-1, keepdims=True))).astype(BF16)
        dlat_ref[0, :, ql:ql + kl] = (rk * (dkg - kh * jnp.mean(dkg * kh, axis=-1, keepdims=True))).astype(BF16)
        dlat_ref[0, :, ql + kl:] = dkr.astype(BF16)
        dh = _dot(dlat_ref[0], wlt_ref[...]) + _dot(dglu_ref[0], wgt_ref[...]) + _dot(dgate_ref[0], wtt_ref[...])
        sh, sc = mod_ref[0, 0], mod_ref[0, 1]
        xf = x_ref[0]
        r = _rstd(xf)
        xh = xf * r
        n = xh * gm_ref[...]
        h_ref[0] = (n * (1.0 + sc) + sh).astype(BF16)
        _accumulate(dmod_ref, (0, 0), _colsum(dh))
        _accumulate(dmod_ref, (0, 1), _colsum(dh * n))
        dn = dh * (1.0 + sc)
        _accumulate(dgm_ref, (0, 0), _colsum(dn * xh))
        dyg = dn * gm_ref[...]
        dx_ref[0] = dx2_ref[0] + r * (dyg - xh * jnp.mean(dyg * xh, axis=-1, keepdims=True))

    return pl.pallas_call(
        body, name="mix_pre_bwd", grid=(b, s // tm),
        out_shape=(jax.ShapeDtypeStruct((b, s, d), F32), jax.ShapeDtypeStruct((b, s, wl), BF16),
                   jax.ShapeDtypeStruct((b, s, hw), BF16), jax.ShapeDtypeStruct((b, s, ql), BF16),
                   jax.ShapeDtypeStruct((b, s, kl), BF16), jax.ShapeDtypeStruct((b, s, d), BF16),
                   jax.ShapeDtypeStruct((b, 2, 1, d), F32), jax.ShapeDtypeStruct((b, 1, 1, d), F32),
                   jax.ShapeDtypeStruct((b, 1, 1, ql), F32), jax.ShapeDtypeStruct((b, 1, 1, kl), F32)),
        in_specs=[_rows(tm, d), _rows(tm, d), _per_b(6, d), _rows(tm, HEAD_PAD), _rows(tm, HEAD_PAD), _const((1, d)),
                  _const((1, ql)), _const((1, kl)), _rows(tm, wl),
                  pl.BlockSpec((1, 1, hw, tm), lambda b, s: (b, s, 0, 0)), _rows(tm, hw), _rows(tm, hw),
                  _rows(tm, dglu.shape[2]), _rows(tm, 2 * d), _const(w_lat_t.shape), _const(w_glu_t.shape),
                  _const(w_gate_t.shape), _const(w_q_t.shape), _const(w_k_t.shape), _const(w_v_t.shape)],
        out_specs=(_rows(tm, d), _rows(tm, wl), _rows(tm, hw), _rows(tm, ql), _rows(tm, kl), _rows(tm, d), _per_b(2, d),
                   _per_b(1, d), _per_b(1, ql), _per_b(1, kl)),
        compiler_params=_params(("arbitrary", "arbitrary")),
    )(x, dx2, mod, cos_t, sin_t, g_mix, g_q, g_kv, lat, dq, dk, dv, dglu, dgate, w_lat_t, w_glu_t, w_gate_t, w_q_t, w_k_t,
      w_v_t)


def _matmul_tn(a, bm, name):
    b, s, kd = a.shape
    nd = bm.shape[2]
    tk, tn = _pick(kd, 1536), _pick(nd, 1536)
    ts = _row_tile(s, 2048)

    def body(a_ref, b_ref, o_ref):
        part = _dot_tn(a_ref[0], b_ref[0])
        first = jnp.logical_and(pl.program_id(2) == 0, pl.program_id(3) == 0)

        @pl.when(first)
        def _():
            o_ref[...] = part

        @pl.when(jnp.logical_not(first))
        def _():
            o_ref[...] = o_ref[...] + part

    return pl.pallas_call(
        body, name=name, grid=(kd // tk, nd // tn, b, s // ts),
        out_shape=jax.ShapeDtypeStruct((kd, nd), F32),
        in_specs=[pl.BlockSpec((1, ts, tk), lambda i, j, bb, ss: (bb, ss, i)),
                  pl.BlockSpec((1, ts, tn), lambda i, j, bb, ss: (bb, ss, j))],
        out_specs=pl.BlockSpec((tk, tn), lambda i, j, bb, ss: (i, j)),
        compiler_params=_params(("arbitrary",) * 4),
    )(a, bm)


def _adamw_update(w, g, m, v):
    nm = ADAM_B1 * m + (1.0 - ADAM_B1) * g
    nv = ADAM_B2 * v + (1.0 - ADAM_B2) * (g * g)
    delta = -ADAM_LR * ((nm / (1.0 - ADAM_B1 ** ADAM_STEP)) / (jnp.sqrt(nv / (1.0 - ADAM_B2 ** ADAM_STEP)) + ADAM_EPS)
                        + ADAM_WD * w)
    return delta, nm, nv


def _adamw(w, g, m, v, name):
    shape = w.shape
    cols = shape[-1]
    rows = w.size // cols
    w2, g2, m2, v2 = (t.reshape(rows, cols) for t in (w, g, m, v))
    tr = rows
    if rows * cols * 4 > (1 << 20):
        tr = _div_tile(rows, max(8, (1 << 18) // cols), 8)

    def body(w_ref, g_ref, m_ref, v_ref, d_ref, nm_ref, nv_ref):
        d_ref[...], nm_ref[...], nv_ref[...] = _adamw_update(w_ref[...], g_ref[...], m_ref[...], v_ref[...])

    spec = pl.BlockSpec((tr, cols), lambda i: (i, 0))
    outs = pl.pallas_call(
        body, name=name, grid=(rows // tr,), out_shape=(jax.ShapeDtypeStruct((rows, cols), F32),) * 3,
        in_specs=[spec] * 4, out_specs=(spec,) * 3, compiler_params=_params(("arbitrary",)),
    )(w2, g2, m2, v2)
    return tuple(t.reshape(shape) for t in outs)


def _adamw_reduce(w, m, v, own, got, name):
    shape = w.shape
    cols = shape[-1]
    w2, m2, v2, o2 = (_rows_2d(t, 0) for t in (w, m, v, own))
    g3 = _rows_2d(got, N_DEV - 1)
    rows = w2.shape[0]
    tr = _grad_row_tile(rows, cols)

    def body(w_ref, m_ref, v_ref, o_ref, r_ref, g_ref, d_ref, nm_ref, nv_ref):
        g = o_ref[...].astype(F32)
        for r in range(N_DEV - 1):
            g = g + r_ref[r].astype(F32)
        g_ref[...] = g
        d_ref[...], nm_ref[...], nv_ref[...] = _adamw_update(w_ref[...], g, m_ref[...], v_ref[...])

    spec = pl.BlockSpec((tr, cols), lambda i: (i, 0))
    outs = pl.pallas_call(
        body, name=name, grid=(rows // tr,),
        in_specs=[spec, spec, spec, spec, pl.BlockSpec((N_DEV - 1, tr, cols), lambda i: (0, i, 0))],
        out_specs=(spec,) * 4,
        out_shape=(jax.ShapeDtypeStruct((rows, cols), F32),) * 4, compiler_params=_params(("arbitrary",)),
    )(w2, m2, v2, o2, g3)
    return tuple(t.reshape(shape) for t in outs)


GATHERED = (("w_in", 2), ("w_uq", 2), ("w_ukv", 2), ("w_o_attn", 2), ("w_pw2", 2), ("w_out", 1), ("w_gu", 2), ("w_down", 1))
PRE, POST = (0, 1, 2), (3, 4, 5, 6, 7)


def _from_chunks(chunks, axis):
    _, a, bb = chunks.shape
    if axis == 2:
        return jnp.transpose(chunks, (1, 0, 2)).reshape(a, N_DEV * bb)
    return chunks.reshape(N_DEV * a, bb)


def _to_chunks(full, axis):
    a, bb = full.shape
    if axis == 2:
        return jnp.transpose(full.reshape(a, N_DEV, bb // N_DEV), (1, 0, 2)).astype(BF16)
    return full.reshape(N_DEV, a // N_DEV, bb).astype(BF16)


def _swap_halves(t):
    half = QK_ROPE // 2
    return jnp.concatenate([t[..., half:], t[..., :half]], axis=-1)


def _t(w):
    return jnp.swapaxes(w, -1, -2)


def _pad_rows(t, mult=8):
    return jnp.pad(t, ((0, -t.shape[0] % mult), (0, 0)))


def _pad_last(t, width):
    return jnp.pad(t, ((0, 0),) * (t.ndim - 1) + ((0, width - t.shape[-1]),))


def kernel(x, c, positions, w_ada, b_ada, g_mix, w_in, g_q, w_uq, g_kv, w_ukv, w_o_attn, w_dw, b_dw, g_cn, b_cn, w_pw2, w_out, g_ffn, w_gu, w_down, g_final, loss_target, m_w_ada, m_b_ada, m_g_mix, m_w_in, m_g_q, m_w_uq, m_g_kv, m_w_ukv, m_w_o_attn, m_w_dw, m_b_dw, m_g_cn, m_b_cn, m_w_pw2, m_w_out, m_g_ffn, m_w_gu, m_w_down, m_g_final, v_w_ada, v_b_ada, v_g_mix, v_w_in, v_g_q, v_w_uq, v_g_kv, v_w_ukv, v_w_o_attn, v_w_dw, v_b_dw, v_g_cn, v_b_cn, v_w_pw2, v_w_out, v_g_ffn, v_w_gu, v_w_down, v_g_final):
    weights = dict(w_ada=w_ada, b_ada=b_ada, g_mix=g_mix, w_in=w_in, g_q=g_q, w_uq=w_uq, g_kv=g_kv, w_ukv=w_ukv,
                   w_o_attn=w_o_attn, w_dw=w_dw, b_dw=b_dw, g_cn=g_cn, b_cn=b_cn, w_pw2=w_pw2, w_out=w_out, g_ffn=g_ffn,
                   w_gu=w_gu, w_down=w_down, g_final=g_final)
    mom_m = dict(w_ada=m_w_ada, b_ada=m_b_ada, g_mix=m_g_mix, w_in=m_w_in, g_q=m_g_q, w_uq=m_w_uq, g_kv=m_g_kv,
                 w_ukv=m_w_ukv, w_o_attn=m_w_o_attn, w_dw=m_w_dw, b_dw=m_b_dw, g_cn=m_g_cn, b_cn=m_b_cn, w_pw2=m_w_pw2,
                 w_out=m_w_out, g_ffn=m_g_ffn, w_gu=m_w_gu, w_down=m_w_down, g_final=m_g_final)
    mom_v = dict(w_ada=v_w_ada, b_ada=v_b_ada, g_mix=v_g_mix, w_in=v_w_in, g_q=v_g_q, w_uq=v_w_uq, g_kv=v_g_kv,
                 w_ukv=v_w_ukv, w_o_attn=v_w_o_attn, w_dw=v_w_dw, b_dw=v_b_dw, g_cn=v_g_cn, b_cn=v_b_cn, w_pw2=v_w_pw2,
                 w_out=v_w_out, g_ffn=v_g_ffn, w_gu=v_w_gu, w_down=v_w_down, g_final=v_g_final)
    order = list(weights)

    nb, s, d = x.shape
    nl = w_in.shape[0]
    ql, kl, cc = g_q.shape[1], g_kv.shape[1], g_cn.shape[1]
    h = N_HEADS
    qk = QK_NOPE + QK_ROPE
    xi, yi, ci = lax.axis_index("x"), lax.axis_index("y"), lax.axis_index("c")
    me = 4 * xi + 2 * yi + ci

    shards = [weights[n].astype(BF16) for n, _ in GATHERED]
    gathered_pre = _all_gather([shards[a][0] for a in PRE], vmem=False, name="weight_all_gather")
    n_dw = w_dw.shape[2]
    dw_rows = jnp.pad(w_dw, ((0, 0), (0, HALO - CONV_W), (0, LANES - n_dw))).reshape(nl * HALO, LANES)
    c_all, dw_all = _all_gather([_pad_rows(c), dw_rows], vmem=True, name="cond_all_gather")
    c_full = c_all[:, :nb].reshape(N_DEV * nb, d)
    w_dw_full = jnp.transpose(dw_all.reshape(N_DEV, nl, HALO, LANES)[..., :n_dw], (1, 2, 0, 3)).reshape(nl, HALO, cc)

    o_kr, o_glu, o_gate = ql + kl, ql + kl + QK_ROPE, ql + kl + QK_ROPE + 2 * cc

    def layouts_pre(chunks):
        wi, w_uq_l, w_ukv_l = (_from_chunks(t, GATHERED[a][1]) for a, t in zip(PRE, chunks))
        w_kr = wi[:, o_kr:o_glu]
        wq = w_uq_l.reshape(ql, h, qk)
        wkv = w_ukv_l.reshape(kl, h, QK_NOPE + V_HEAD)
        return dict(
            lat=jnp.concatenate([wi[:, :o_kr], jnp.zeros((d, QK_NOPE), BF16), w_kr, _swap_halves(w_kr)], axis=1),
            glu=wi[:, o_glu:o_gate], gate=wi[:, o_gate:],
            q=jnp.concatenate([wq, _swap_halves(wq[..., QK_NOPE:])], axis=-1).reshape(ql, h * HEAD_PAD),
            k=_pad_last(wkv[..., :QK_NOPE], HEAD_PAD).reshape(kl, h * HEAD_PAD),
            v=_pad_last(wkv[..., QK_NOPE:], HEAD_PAD).reshape(kl, h * HEAD_PAD))

    def layouts_post(chunks):
        w_o_l, w_pw_l, w_out_l, w_gu_l, w_down_l = (_from_chunks(t, GATHERED[a][1]) for a, t in zip(POST, chunks))
        w_o = jnp.pad(w_o_l.reshape(h, V_HEAD, d), ((0, 0), (0, HEAD_PAD - V_HEAD), (0, 0)))
        return dict(o=w_o.reshape(h * HEAD_PAD, d), pw=w_pw_l, out=w_out_l, gu=w_gu_l, down=w_down_l)

    n_ada = w_ada.shape[2]
    b_cols = lax.dynamic_slice_in_dim(b_ada, me * n_ada, n_ada, axis=1).reshape(nl, 1, n_ada)
    mod_part = _ada_fwd(c_full, w_ada, b_cols)
    (mod_all,) = _all_gather([mod_part.reshape(nl * N_DEV * nb, n_ada)], vmem=True, name="mod_all_gather")
    mod_all = jnp.transpose(mod_all.reshape(N_DEV, nl, N_DEV * nb, n_ada), (1, 2, 0, 3)).reshape(nl, N_DEV * nb, 6 * d)
    mod = lax.dynamic_slice_in_dim(mod_all, me * nb, nb, axis=1).reshape(nl, nb, 6, 1, d)

    inv_freq = ROPE_THETA ** (-jnp.arange(0, QK_ROPE, 2, dtype=F32) / QK_ROPE)
    zeros = lambda n: jnp.zeros((n,), F32)
    freq_row = jnp.concatenate([zeros(QK_NOPE), inv_freq, inv_freq, zeros(HEAD_PAD - qk)]).reshape(1, -1)
    ones = jnp.ones((QK_ROPE // 2,), F32)
    sign_row = jnp.concatenate([zeros(QK_NOPE), -ones, ones, zeros(HEAD_PAD - qk)]).reshape(1, -1)
    cos_t, sin_t = _rope_tables(positions.astype(F32).reshape(nb, s, 1), freq_row, sign_row)

    row = lambda t, l: t[l].reshape(1, -1)

    saved, wts = [], []
    xc = x
    for l in range(nl):
        wl = layouts_pre(gathered_pre)
        lat, glu, gate, qh, kh, vh = _mix_pre_fwd(xc, mod[l], cos_t, sin_t, row(g_mix, l), row(g_q, l), row(g_kv, l),
                                                  wl["lat"], wl["glu"], wl["gate"], wl["q"], wl["k"], wl["v"])
        riders = [shards[a][l] for a in POST] + ([shards[a][l + 1] for a in PRE] if l + 1 < nl else [])
        o, lse, arrived = _flash_fwd(qh, kh, vh, gather=riders)
        wl.update(layouts_post(arrived[:len(POST)]))
        gathered_pre = arrived[len(POST):]
        wts.append(wl)
        u, y_conv = _conv_fwd(glu, w_dw_full[l], row(b_dw, l), row(g_cn, l), row(b_cn, l), wl["pw"])
        x2, y_attn, o1 = _mix_post_fwd(xc, mod[l], o, y_conv, gate, wl["o"], wl["out"])
        x3, gu, o2, h2 = _ffn_fwd(x2, mod[l], row(g_ffn, l), wl["gu"], wl["down"])
        saved.append(dict(x=xc, lat=lat, glu=glu, gate=gate, q=qh, k=kh, v=vh, o=o, lse=lse, u=u, y_conv=y_conv, x2=x2,
                          y_attn=y_attn, o1=o1, gu=gu, o2=o2, h2=h2))
        xc = x3

    dx, loss_part, dgf_part = _loss_bwd(xc, loss_target, g_final.reshape(1, d))
    loss = lax.psum(0.5 / d * jnp.sum(loss_part), AXES)

    small_rows, dw_taps, dmod = [None] * nl, [None] * nl, [None] * nl
    own, got = {}, {}
    chunk = lambda g, a: _to_chunks(g[GATHERED[a][0]], GATHERED[a][1])
    mine = lambda t: lax.dynamic_index_in_dim(t, me, axis=0, keepdims=False)
    pending = []
    for l in reversed(range(nl)):
        sv, wl, gw = saved[l], wts[l], {}
        dx2, do2, act, dgu, dmod2, dgffn = _ffn_bwd(dx, sv["x2"], mod[l], row(g_ffn, l), sv["gu"], sv["o2"],
                                                   _t(wl["gu"]), _t(wl["down"]))
        gw["w_gu"] = _matmul_tn(sv["h2"], dgu, "grad_w_gu")
        gw["w_down"] = _matmul_tn(act, do2, "grad_w_down")
        do1, dya, dyc, dgate, yv, do_h, dgt1 = _mix_post_bwd(dx2, mod[l], sv["o1"], sv["y_attn"], sv["y_conv"], sv["gate"],
                                                           _t(wl["out"]), _t(wl["o"]))
        gw["w_out"] = _matmul_tn(yv, do1, "grad_w_out")
        dwo = _matmul_tn(sv["o"], dya, "grad_w_o")
        gw["w_o_attn"] = dwo.reshape(h, HEAD_PAD, d)[:, :V_HEAD].reshape(h * V_HEAD, d)
        dglu, s_act, ddw, csmall = _conv_bwd(dyc, sv["u"], sv["glu"], w_dw_full[l], row(g_cn, l), row(b_cn, l), _t(wl["pw"]))
        gw["w_pw2"] = _matmul_tn(s_act, dyc, "grad_w_pw2")
        ready = [chunk(gw, a) for a in POST]
        dq, dk, dv, arrived = _flash_bwd(sv["q"], sv["k"], sv["v"], sv["o"], sv["lse"], do_h, scatter=ready + pending)
        for i, a in enumerate(POST):
            own[l, a], got[l, a] = mine(ready[i]), arrived[i]
        for i, a in enumerate(PRE if pending else ()):
            got[l + 1, a] = arrived[len(POST) + i]
        dx, dlat, dqr, qn, kvn, h1, dmod1, dgm, dgq, dgkv = _mix_pre_bwd(
            sv["x"], dx2, mod[l], cos_t, sin_t, row(g_mix, l), row(g_q, l), row(g_kv, l), sv["lat"], dq, dk, dv, dglu,
            dgate, _t(wl["lat"]), _t(wl["glu"]), _t(wl["gate"]), _t(wl["q"]), _t(wl["k"]), _t(wl["v"]))
        dwl = _matmul_tn(h1, dlat, "grad_w_lat")
        dwg = _matmul_tn(h1, dglu, "grad_w_glu")
        dwt = _matmul_tn(h1, dgate, "grad_w_gate")
        kr0 = o_kr + QK_NOPE
        dkr = dwl[:, kr0:kr0 + QK_ROPE] + _swap_halves(dwl[:, kr0 + QK_ROPE:])
        gw["w_in"] = jnp.concatenate([dwl[:, :o_kr], dkr, dwg, dwt], axis=1)
        dwq = _matmul_tn(qn, dqr, "grad_w_q").reshape(ql, h, HEAD_PAD)
        dwq = jnp.concatenate([dwq[..., :QK_NOPE], dwq[..., QK_NOPE:qk] + _swap_halves(dwq[..., qk:])], axis=-1)
        gw["w_uq"] = dwq.reshape(ql, h * qk)
        dwk = _matmul_tn(kvn, dk, "grad_w_k").reshape(kl, h, HEAD_PAD)
        dwv = _matmul_tn(kvn, dv, "grad_w_v").reshape(kl, h, HEAD_PAD)
        gw["w_ukv"] = jnp.concatenate([dwk[..., :QK_NOPE], dwv[..., :V_HEAD]], axis=-1).reshape(kl, h * (QK_NOPE + V_HEAD))
        pending = [chunk(gw, a) for a in PRE]
        for i, a in enumerate(PRE):
            own[l, a] = mine(pending[i])
        dmod[l] = jnp.concatenate([dmod1[:, :, 0], dgt1[:, :, 0], dmod2[:, :, 0]], axis=1).reshape(nb, 6 * d)
        bsum = lambda t: jnp.sum(t, axis=0).reshape(1, -1)
        cs = jnp.sum(csmall, axis=0)[:, 0]
        small_rows[l] = jnp.concatenate([bsum(dgm), bsum(dgq), bsum(dgkv), cs[2:3], cs[0:1], cs[1:2], bsum(dgffn)], axis=1)
        dw_taps[l] = jnp.sum(ddw.reshape(nb, HALO, 8, cc), axis=(0, 2))
    grad_x = dx

    dmod_rows = _pad_rows(jnp.stack(dmod).reshape(nl * nb, 6 * d))
    (dmod_all,) = _all_gather([dmod_rows], vmem=True, name="dmod_all_gather")
    dmod_full = jnp.transpose(dmod_all[:, :nl * nb].reshape(N_DEV, nl, nb, 6 * d), (1, 0, 2, 3)).reshape(nl, N_DEV * nb, 6 * d)
    dmod_cols = lax.dynamic_slice_in_dim(dmod_full, me * n_ada, n_ada, axis=2)
    grad_w_ada, grad_b_ada = _ada_bwd(c_full, dmod_cols, dmod_full)
    grads = {"w_ada": grad_w_ada, "b_ada": grad_b_ada.reshape(nl, 6 * d)}

    widths = (d, ql, kl, cc, cc, cc, d)
    wsum = sum(widths)
    final_row = _pad_last(jnp.sum(dgf_part, axis=0).reshape(1, d), wsum)
    small2d = _pad_rows(jnp.concatenate(small_rows + [final_row], axis=0))
    taps2d = jnp.concatenate(dw_taps, axis=0)
    small_all, taps_all = _all_gather([small2d, taps2d], vmem=True, name="small_grad_all_gather")
    small_sum, taps_sum = _sum_devices(small_all), _sum_devices(taps_all)
    off = 0
    for n, wdt in zip(("g_mix", "g_q", "g_kv", "b_dw", "g_cn", "b_cn", "g_ffn"), widths):
        grads[n] = small_sum[:nl, off:off + wdt]
        off += wdt
    grads["g_final"] = small_sum[nl, :d]
    taps = taps_sum.reshape(nl, HALO, cc)[:, :CONV_W]
    grads["w_dw"] = lax.dynamic_slice_in_dim(taps, me * n_dw, n_dw, axis=2)

    delta, new_m, new_v = {}, {}, {}
    for n in order:
        if n in grads:
            delta[n], new_m[n], new_v[n] = _adamw(weights[n], grads[n], mom_m[n], mom_v[n], "adamw_" + n)

    for a, t in zip(PRE, _grad_scatter(pending)):
        got[0, a] = t
    for a, (n, _) in enumerate(GATHERED):
        own_n = jnp.stack([own[l, a] for l in range(nl)])
        got_n = jnp.stack([got[l, a] for l in range(nl)], axis=1)
        grads[n], delta[n], new_m[n], new_v[n] = _adamw_reduce(weights[n], mom_m[n], mom_v[n], own_n, got_n, "adamw_" + n)

    return (loss, grad_x, *[grads[n] for n in order], *[delta[n] for n in order], *[new_m[n] for n in order],
            *[new_v[n] for n in order])
```

```python
import jax
import jax.numpy as jnp
from jax import lax
from jax.experimental import pallas as pl
from jax.experimental.pallas import tpu as pltpu

F32, BF16 = jnp.float32, jnp.bfloat16
MESH = pl.DeviceIdType.MESH
AXES = ("x", "y", "c")
N_DEV = 8

N_HEADS = 8
QK_NOPE = 64
QK_ROPE = 32
V_HEAD = 64
HEAD_PAD = 128
CONV_W = 31
HALO = 32
EPS = 1e-6
ROPE_THETA = 10000.0
NEG_INF = -1e30
ATTN_SCALE = (QK_NOPE + QK_ROPE) ** -0.5

ADAM_LR, ADAM_B1, ADAM_B2, ADAM_EPS, ADAM_WD, ADAM_STEP = 0.001, 0.9, 0.999, 1e-08, 0.01, 10

LANES = 128
VMEM_LIMIT = 60 * 1024 * 1024


def _params(sem=None):
    return pltpu.CompilerParams(dimension_semantics=sem, vmem_limit_bytes=VMEM_LIMIT)


def _pick(n, cap):
    if n <= cap:
        return n
    best = None
    for d in range(LANES, cap + 1, LANES):
        if n % d == 0:
            best = d
    assert best is not None, (n, cap)
    return best


def _div_tile(n, cap, mult):
    best = None
    for d in range(mult, min(n, cap) + 1, mult):
        if n % d == 0:
            best = d
    assert best is not None, (n, cap, mult)
    return best


def _row_tile(s, cap=512):
    return cap if s % cap == 0 and s >= 2 * cap else s // 2


def _sig(v):
    return 1.0 / (1.0 + jnp.exp(-v))


def _rstd(v):
    return lax.rsqrt(jnp.mean(v * v, axis=-1, keepdims=True) + EPS)


def _dot(a, b):
    return jnp.dot(a, b, preferred_element_type=F32)


def _dot_nt(a, b):
    return lax.dot_general(a, b, (((1,), (1,)), ((), ())), preferred_element_type=F32)


def _dot_tn(a, b):
    return lax.dot_general(a, b, (((0,), (0,)), ((), ())), preferred_element_type=F32)


def _rope(v, cos_t, sin_t):
    return v * cos_t + pltpu.roll(v, HEAD_PAD - QK_ROPE, 1) * sin_t


def _rope_bwd(dv, cos_t, sin_t):
    return dv * cos_t + pltpu.roll(dv * sin_t, QK_ROPE, 1)


def _const(shape):
    n = len(shape)
    return pl.BlockSpec(shape, lambda *_: (0,) * n, pipeline_mode=pl.Buffered(1))


def _rows(tm, w):
    return pl.BlockSpec((1, tm, w), lambda b, s: (b, s, 0))


def _per_b(r, w):
    return pl.BlockSpec((1, r, 1, w), lambda b, s: (b, 0, 0, 0))


def _all_gather(arrs, vmem, name):
    n = len(arrs)
    space = pltpu.VMEM if vmem else pl.ANY

    def body(*refs):
        x_refs, out_refs = refs[:n], refs[n:2 * n]
        send_sems, recv_sems, local_sems = refs[2 * n:]
        x_, y_, c_ = lax.axis_index("x"), lax.axis_index("y"), lax.axis_index("c")
        me, sibling = (x_, y_, c_), (x_, y_, 1 - c_)
        chips = [(1 - x_, y_), (x_, 1 - y_), (1 - x_, 1 - y_)]

        def copy(a, k, block, to, own=False):
            px, py, pc = block
            slot = out_refs[a].at[4 * px + 2 * py + pc]
            return pltpu.make_async_remote_copy(
                src_ref=x_refs[a] if own else slot, dst_ref=slot, send_sem=send_sems.at[k * n + a],
                recv_sem=recv_sems.at[k * n + a], device_id=to, device_id_type=MESH)

        mine = [pltpu.make_async_copy(x_refs[a], out_refs[a].at[4 * x_ + 2 * y_ + c_], local_sems.at[a]) for a in range(n)]
        sent = []
        for a in range(n):
            mine[a].start()
            sent.append(copy(a, 0, me, sibling, own=True))
            sent += [copy(a, 1 + j, me, (*chip, c_), own=True) for j, chip in enumerate(chips)]
        for cp in sent:
            cp.start()
        for j, chip in enumerate(chips):
            for a in range(n):
                copy(a, 1 + j, (*chip, c_), me).wait_recv()
                passed = copy(a, 4 + j, (*chip, c_), sibling)
                passed.start()
                sent.append(passed)
        for a in range(n):
            copy(a, 0, sibling, me).wait_recv()
            for j, chip in enumerate(chips):
                copy(a, 4 + j, (*chip, 1 - c_), me).wait_recv()
        for cp in sent:
            cp.wait_send()
        for cp in mine:
            cp.wait()

    return pl.pallas_call(
        body, name=name,
        out_shape=[jax.ShapeDtypeStruct((N_DEV,) + t.shape, t.dtype) for t in arrs],
        in_specs=[pl.BlockSpec(memory_space=space)] * n, out_specs=[pl.BlockSpec(memory_space=space)] * n,
        scratch_shapes=[pltpu.SemaphoreType.DMA((7 * n,)), pltpu.SemaphoreType.DMA((7 * n,)), pltpu.SemaphoreType.DMA((n,))],
        compiler_params=pltpu.CompilerParams(vmem_limit_bytes=VMEM_LIMIT),
    )(*arrs)


FLIPS = tuple((fx, fy, fc) for fx in (0, 1) for fy in (0, 1) for fc in (0, 1))[1:]


class _DirectExchange:
    def __init__(self, kind, in_refs, out_refs, send_sems, recv_sems, local_sems):
        n = len(in_refs)
        x_, y_, c_ = lax.axis_index("x"), lax.axis_index("y"), lax.axis_index("c")
        me = 4 * x_ + 2 * y_ + c_
        self.copies, self.local = [], []
        for r, (fx, fy, fc) in enumerate(FLIPS):
            px, py, pc = (1 - x_ if fx else x_), (1 - y_ if fy else y_), (1 - c_ if fc else c_)
            for a in range(n):
                src = in_refs[a] if kind == "gather" else in_refs[a].at[4 * px + 2 * py + pc]
                dst = out_refs[a].at[me] if kind == "gather" else out_refs[a].at[r]
                self.copies.append(pltpu.make_async_remote_copy(
                    src_ref=src, dst_ref=dst, send_sem=send_sems.at[r * n + a], recv_sem=recv_sems.at[r * n + a],
                    device_id=(px, py, pc), device_id_type=MESH))
        if kind == "gather":
            self.local = [pltpu.make_async_copy(in_refs[a], out_refs[a].at[me], local_sems.at[a]) for a in range(n)]

    def start(self):
        for cp in self.local + self.copies:
            cp.start()

    def wait(self):
        for cp in self.copies + self.local:
            cp.wait()

    @staticmethod
    def out_shapes(kind, arrs):
        if kind == "gather":
            return [jax.ShapeDtypeStruct((N_DEV,) + t.shape, t.dtype) for t in arrs]
        return [jax.ShapeDtypeStruct((N_DEV - 1,) + t.shape[1:], t.dtype) for t in arrs]

    @staticmethod
    def scratch(n):
        return [pltpu.SemaphoreType.DMA((7 * n,)), pltpu.SemaphoreType.DMA((7 * n,)), pltpu.SemaphoreType.DMA((n,))]


def _grad_scatter(gs):
    n = len(gs)

    def body(*refs):
        ex = _DirectExchange("scatter", refs[:n], refs[n:2 * n], *refs[2 * n:])
        ex.start()
        ex.wait()

    return pl.pallas_call(
        body, name="grad_scatter", out_shape=_DirectExchange.out_shapes("scatter", gs),
        in_specs=[pl.BlockSpec(memory_space=pl.ANY)] * n, out_specs=[pl.BlockSpec(memory_space=pl.ANY)] * n,
        scratch_shapes=_DirectExchange.scratch(n),
    )(*gs)


def _rows_2d(t, lead):
    return t.reshape((lead, -1, t.shape[-1]) if lead else (-1, t.shape[-1]))


def _grad_row_tile(rows, cols):
    return _div_tile(rows, max(16, (1 << 18) // cols), 16)


def _sum_devices(g):
    _, m, n = g.shape

    def body(g_ref, o_ref):
        s = g_ref[0]
        for j in range(1, N_DEV):
            s = s + g_ref[j]
        o_ref[...] = s

    return pl.pallas_call(body, name="small_grad_sum", out_shape=jax.ShapeDtypeStruct((m, n), F32),
                          compiler_params=pltpu.CompilerParams(vmem_limit_bytes=VMEM_LIMIT))(g)


def _ada_fwd(c_full, w_ada, b_cols):
    nl, d, n = w_ada.shape
    nb = c_full.shape[0]

    def body(c_ref, w_ref, b_ref, o_ref):
        cv = c_ref[...]
        act = cv * _sig(cv)
        o_ref[0] = jnp.dot(act, w_ref[0], preferred_element_type=F32, precision=lax.Precision.HIGHEST) + b_ref[0]

    return pl.pallas_call(
        body, name="ada_fwd", grid=(nl,), out_shape=jax.ShapeDtypeStruct((nl, nb, n), F32),
        in_specs=[pl.BlockSpec((nb, d), lambda l: (0, 0)), pl.BlockSpec((1, d, n), lambda l: (l, 0, 0)),
                  pl.BlockSpec((1, 1, n), lambda l: (l, 0, 0))],
        out_specs=pl.BlockSpec((1, nb, n), lambda l: (l, 0, 0)), compiler_params=_params(("arbitrary",)),
    )(c_full, w_ada, b_cols)


def _ada_bwd(c_full, dmod_cols, dmod_full):
    nl, nb, n = dmod_cols.shape
    d = c_full.shape[1]
    nfull = dmod_full.shape[2]

    def body(c_ref, dc_ref, df_ref, gw_ref, gb_ref):
        cv = c_ref[...]
        act = cv * _sig(cv)
        gw_ref[0] = lax.dot_general(act, dc_ref[0], (((0,), (0,)), ((), ())), preferred_element_type=F32,
                                    precision=lax.Precision.HIGHEST)
        gb_ref[0] = jnp.sum(df_ref[0], axis=0, keepdims=True)

    return pl.pallas_call(
        body, name="ada_bwd", grid=(nl,),
        out_shape=(jax.ShapeDtypeStruct((nl, d, n), F32), jax.ShapeDtypeStruct((nl, 1, nfull), F32)),
        in_specs=[pl.BlockSpec((nb, d), lambda l: (0, 0)), pl.BlockSpec((1, nb, n), lambda l: (l, 0, 0)),
                  pl.BlockSpec((1, nb, nfull), lambda l: (l, 0, 0))],
        out_specs=(pl.BlockSpec((1, d, n), lambda l: (l, 0, 0)), pl.BlockSpec((1, 1, nfull), lambda l: (l, 0, 0))),
        compiler_params=_params(("arbitrary",)),
    )(c_full, dmod_cols, dmod_full)


def _rope_tables(pos, freq_row, sign_row):
    b, s, _ = pos.shape
    tm = _row_tile(s)

    def body(p_ref, f_ref, g_ref, c_ref, s_ref):
        ang = p_ref[0] * f_ref[...]
        lane = lax.broadcasted_iota(jnp.int32, ang.shape, 1)
        c_ref[0] = jnp.where(lane < QK_NOPE, 1.0, jnp.where(lane < QK_NOPE + QK_ROPE, jnp.cos(ang), 0.0))
        s_ref[0] = g_ref[...] * jnp.sin(ang)

    return pl.pallas_call(
        body, name="rope_tables", grid=(b, s // tm),
        out_shape=(jax.ShapeDtypeStruct((b, s, HEAD_PAD), F32),) * 2,
        in_specs=[_rows(tm, 1), pl.BlockSpec((1, HEAD_PAD), lambda i, j: (0, 0)),
                  pl.BlockSpec((1, HEAD_PAD), lambda i, j: (0, 0))],
        out_specs=(_rows(tm, HEAD_PAD),) * 2, compiler_params=_params(("arbitrary", "arbitrary")),
    )(pos, freq_row, sign_row)


def _mix_pre_fwd(x, mod, cos_t, sin_t, g_mix, g_q, g_kv, w_lat, w_glu, w_gate, w_q, w_k, w_v):
    b, s, d = x.shape
    ql, kl = g_q.shape[1], g_kv.shape[1]
    wl, wg, wt = w_lat.shape[1], w_glu.shape[1], w_gate.shape[1]
    hw = N_HEADS * HEAD_PAD
    tm = _row_tile(s)

    def body(x_ref, mod_ref, cos_ref, sin_ref, gm_ref, gq_ref, gkv_ref, wlat_ref, wglu_ref, wgate_ref, wq_ref, wk_ref,
             wv_ref, lat_ref, glu_ref, gate_ref, q_ref, k_ref, v_ref):
        xf = x_ref[0]
        sh, sc = mod_ref[0, 0], mod_ref[0, 1]
        hb = ((xf * _rstd(xf) * gm_ref[...]) * (1.0 + sc) + sh).astype(BF16)
        glu_ref[0] = _dot(hb, wglu_ref[...]).astype(BF16)
        gate_ref[0] = _dot(hb, wgate_ref[...]).astype(BF16)
        lat = _dot(hb, wlat_ref[...]).astype(BF16)
        lat_ref[0] = lat
        latf = lat.astype(F32)
        q_lat, kv_lat, kr_sec = latf[:, :ql], latf[:, ql:ql + kl], latf[:, ql + kl:]
        qn = (q_lat * _rstd(q_lat) * gq_ref[...]).astype(BF16)
        kvn = (kv_lat * _rstd(kv_lat) * gkv_ref[...]).astype(BF16)
        cos_v, sin_v = cos_ref[0], sin_ref[0]
        lane = lax.broadcasted_iota(jnp.int32, kr_sec.shape, 1)
        kr = jnp.where(lane >= QK_NOPE, _rope(kr_sec, cos_v, sin_v), 0.0)
        q_all, k_all, v_all = _dot(qn, wq_ref[...]), _dot(kvn, wk_ref[...]), _dot(kvn, wv_ref[...])
        vlane = lax.broadcasted_iota(jnp.int32, v_all.shape, 1)
        v_ref[0] = jnp.where(vlane % HEAD_PAD == V_HEAD, 1.0, v_all).astype(BF16)
        for h in range(N_HEADS):
            cols = slice(h * HEAD_PAD, (h + 1) * HEAD_PAD)
            q_ref[0, :, cols] = (_rope(q_all[:, cols], cos_v, sin_v) * ATTN_SCALE).astype(BF16)
            k_ref[0, :, cols] = (k_all[:, cols] + kr).astype(BF16)

    hshape = jax.ShapeDtypeStruct((b, s, hw), BF16)
    return pl.pallas_call(
        body, name="mix_pre_fwd", grid=(b, s // tm),
        out_shape=(jax.ShapeDtypeStruct((b, s, wl), BF16), jax.ShapeDtypeStruct((b, s, wg), BF16),
                   jax.ShapeDtypeStruct((b, s, wt), BF16), hshape, hshape, hshape),
        in_specs=[_rows(tm, d), _per_b(6, d), _rows(tm, HEAD_PAD), _rows(tm, HEAD_PAD), _const((1, d)), _const((1, ql)),
                  _const((1, kl)), _const(w_lat.shape), _const(w_glu.shape), _const(w_gate.shape), _const(w_q.shape),
                  _const(w_k.shape), _const(w_v.shape)],
        out_specs=(_rows(tm, wl), _rows(tm, wg), _rows(tm, wt), _rows(tm, hw), _rows(tm, hw), _rows(tm, hw)),
        compiler_params=_params(("arbitrary", "arbitrary")),
    )(x, mod, cos_t, sin_t, g_mix, g_q, g_kv, w_lat, w_glu, w_gate, w_q, w_k, w_v)


def _causal_mask(tq, tk):
    return lax.broadcasted_iota(jnp.int32, (tq, tk), 0) >= lax.broadcasted_iota(jnp.int32, (tq, tk), 1)


def _riding_exchange(kind, n, refs, grid):
    if not n:
        return
    ex = _DirectExchange(kind, refs[:n], refs[n:2 * n], *refs[2 * n:])
    ids = [pl.program_id(a) for a in range(len(grid))]
    first, last = ids[0] == 0, ids[0] == grid[0] - 1
    for a in range(1, len(grid)):
        first, last = jnp.logical_and(first, ids[a] == 0), jnp.logical_and(last, ids[a] == grid[a] - 1)
    pl.when(first)(ex.start)
    return lambda: pl.when(last)(ex.wait)


def _flash_fwd(q, k, v, gather=()):
    b, s, hw = q.shape
    nh, hp = hw // HEAD_PAD, HEAD_PAD
    t = _row_tile(s)
    n = len(gather)
    grid = (b, nh, s // t)

    def body(q_ref, k_ref, v_ref, *rest):
        o_ref, lse_ref = rest[n], rest[n + 1]
        finish = _riding_exchange("gather", n, rest[:n] + rest[n + 2:], grid)
        i = pl.program_id(2)
        qv = q_ref[0]

        def step(j, carry, masked):
            m, acc = carry
            rows = pl.ds(pl.multiple_of(j * t, t), t)
            sc = _dot_nt(qv, k_ref[0, rows, :])
            if masked:
                sc = jnp.where(_causal_mask(t, t), sc, NEG_INF)
            m_new = jnp.maximum(m, jnp.max(sc, axis=-1, keepdims=True))
            p = jnp.exp((sc - m_new).astype(BF16))
            acc = jnp.exp(m - m_new) * acc + _dot(p, v_ref[0, rows, :])
            return m_new, acc

        init = (jnp.full((t, 1), NEG_INF, F32), jnp.zeros((t, hp), F32))
        carry = lax.fori_loop(0, i // 2, lambda jj, cr: step(2 * jj + 1, step(2 * jj, cr, False), False), init)
        carry = lax.cond(i % 2 == 1, lambda cr: step(i - 1, cr, False), lambda cr: cr, carry)
        m, acc = step(i, carry, True)
        lane = lax.broadcasted_iota(jnp.int32, acc.shape, 1)
        l = jnp.sum(jnp.where(lane == V_HEAD, acc, 0.0), axis=-1, keepdims=True)
        o_ref[0] = (acc / l).astype(BF16)
        lse_ref[0, 0] = m + jnp.log(l)
        if finish:
            finish()

    tile = pl.BlockSpec((1, t, hp), lambda bb, hh, ii: (bb, ii, hh))
    full = pl.BlockSpec((1, s, hp), lambda bb, hh, ii: (bb, 0, hh))
    hbm = [pl.BlockSpec(memory_space=pl.ANY)] * n
    outs = pl.pallas_call(
        body, name="flash_fwd_gather" if n else "flash_fwd", grid=grid,
        out_shape=[jax.ShapeDtypeStruct((b, s, hw), BF16), jax.ShapeDtypeStruct((b, nh, s, 1), F32)]
        + _DirectExchange.out_shapes("gather", gather),
        in_specs=[tile, full, full] + hbm,
        out_specs=[tile, pl.BlockSpec((1, 1, t, 1), lambda bb, hh, ii: (bb, hh, ii, 0))] + hbm,
        scratch_shapes=_DirectExchange.scratch(n) if n else [],
        compiler_params=_params(("arbitrary", "arbitrary", "arbitrary")),
    )(q, k, v, *gather)
    return outs[0], outs[1], outs[2:]


def _halo_prev(tm, w):
    r = tm // HALO
    return pl.BlockSpec((1, HALO, w), lambda b, s: (b, jnp.maximum(s * r - 1, 0), 0))


def _halo_next(tm, w, n_tiles):
    r = tm // HALO
    return pl.BlockSpec((1, HALO, w), lambda b, s: (b, jnp.minimum((s + 1) * r, n_tiles * r - 1), 0))


def _conv_rows(cc):
    return max(8, 16 * 8 * LANES // cc)


def _shifted_copies(buf, rows):
    buf[0, pl.ds(rows, 8), :] = jnp.zeros((8, buf.shape[2]), buf.dtype)
    for s in range(1, 8):
        buf[s, pl.ds(0, rows), :] = buf[0, pl.ds(s, rows), :]


def _window(buf, start, rows):
    return buf[start % 8, pl.ds(start - start % 8, rows), :]


def _glu(v, cc):
    a, g = v[:, :cc].astype(F32), v[:, cc:].astype(F32)
    return a * _sig(g)


def _conv_fwd(glu, w_dw, b_dw, g_cn, b_cn, w_pw2):
    b, s, w2 = glu.shape
    cc = w2 // 2
    d = w_pw2.shape[1]
    tm = _row_tile(s)

    rc = _conv_rows(cc)
    te = tm + HALO

    def body(cur_ref, prev_ref, w_ref, bdw_ref, g_ref, bcn_ref, wp_ref, u_ref, y_ref, ext, u_all):
        first = pl.program_id(1) == 0
        ext[0, pl.ds(0, HALO), :] = jnp.where(first, 0.0, _glu(prev_ref[0], cc))
        ext[0, pl.ds(HALO, tm), :] = _glu(cur_ref[0], cc)
        _shifted_copies(ext, te)
        for c0 in range(0, tm, rc):
            acc = jnp.zeros((rc, cc), F32) + bdw_ref[...]
            for kk in range(CONV_W):
                acc = acc + w_ref[pl.ds(kk, 1), :] * _window(ext, c0 + HALO - CONV_W + 1 + kk, rc)
            u_all[pl.ds(c0, rc), :] = acc
        ub = u_all[...].astype(BF16)
        u_ref[0] = ub
        uf = ub.astype(F32)
        mu = jnp.mean(uf, axis=-1, keepdims=True)
        uc = uf - mu
        ln = uc * lax.rsqrt(jnp.mean(uc * uc, axis=-1, keepdims=True) + EPS) * g_ref[...] + bcn_ref[...]
        y_ref[0] = _dot((ln * _sig(ln)).astype(BF16), wp_ref[...]).astype(BF16)

    return pl.pallas_call(
        body, name="conv_fwd", grid=(b, s // tm),
        out_shape=(jax.ShapeDtypeStruct((b, s, cc), BF16), jax.ShapeDtypeStruct((b, s, d), BF16)),
        in_specs=[_rows(tm, w2), _halo_prev(tm, w2), _const(w_dw.shape), _const((1, cc)), _const((1, cc)), _const((1, cc)),
                  _const(w_pw2.shape)],
        out_specs=(_rows(tm, cc), _rows(tm, d)),
        scratch_shapes=[pltpu.VMEM((8, te + 8, cc), F32), pltpu.VMEM((tm, cc), F32)],
        compiler_params=_params(("arbitrary", "arbitrary")),
    )(glu, glu, w_dw, b_dw, g_cn, b_cn, w_pw2)


def _mix_post_fwd(x, mod, o, y_conv, gate, w_o, w_out):
    b, s, d = x.shape
    hw = o.shape[2]
    tm = _row_tile(s)

    def body(x_ref, mod_ref, o_ref, yc_ref, gate_ref, wo_ref, wout_ref, x2_ref, ya_ref, o1_ref):
        yab = _dot(o_ref[0], wo_ref[...]).astype(BF16)
        ya_ref[0] = yab
        gv = gate_ref[0]
        y = _sig(gv[:, :d].astype(F32)) * yab.astype(F32) + _sig(gv[:, d:].astype(F32)) * yc_ref[0].astype(F32)
        o1 = _dot(y.astype(BF16), wout_ref[...])
        o1_ref[0] = o1.astype(BF16)
        x2_ref[0] = x_ref[0] + mod_ref[0, 2] * o1

    return pl.pallas_call(
        body, name="mix_post_fwd", grid=(b, s // tm),
        out_shape=(jax.ShapeDtypeStruct((b, s, d), F32), jax.ShapeDtypeStruct((b, s, d), BF16),
                   jax.ShapeDtypeStruct((b, s, d), BF16)),
        in_specs=[_rows(tm, d), _per_b(6, d), _rows(tm, hw), _rows(tm, d), _rows(tm, 2 * d), _const(w_o.shape),
                  _const(w_out.shape)],
        out_specs=(_rows(tm, d), _rows(tm, d), _rows(tm, d)),
        compiler_params=_params(("arbitrary", "arbitrary")),
    )(x, mod, o, y_conv, gate, w_o, w_out)


def _ffn_fwd(x2, mod, g_ffn, w_gu, w_down):
    b, s, d = x2.shape
    f = w_down.shape[0]
    fc = _pick(f, 512)
    tm = _row_tile(s)

    def body(x_ref, mod_ref, g_ref, wgu_ref, wdn_ref, x3_ref, gu_ref, o2_ref, h_ref):
        xf = x_ref[0]
        hb = ((xf * _rstd(xf) * g_ref[...]) * (1.0 + mod_ref[0, 4]) + mod_ref[0, 3]).astype(BF16)
        h_ref[0] = hb
        o2 = jnp.zeros((tm, d), F32)
        for c0 in range(0, f, fc):
            gb = _dot(hb, wgu_ref[:, c0:c0 + fc]).astype(BF16)
            ub = _dot(hb, wgu_ref[:, f + c0:f + c0 + fc]).astype(BF16)
            gu_ref[0, :, c0:c0 + fc] = gb
            gu_ref[0, :, f + c0:f + c0 + fc] = ub
            gf = gb.astype(F32)
            act = (gf * _sig(gf) * ub.astype(F32)).astype(BF16)
            o2 = o2 + _dot(act, wdn_ref[c0:c0 + fc, :])
        o2_ref[0] = o2.astype(BF16)
        x3_ref[0] = xf + mod_ref[0, 5] * o2

    return pl.pallas_call(
        body, name="ffn_fwd", grid=(b, s // tm),
        out_shape=(jax.ShapeDtypeStruct((b, s, d), F32), jax.ShapeDtypeStruct((b, s, 2 * f), BF16),
                   jax.ShapeDtypeStruct((b, s, d), BF16), jax.ShapeDtypeStruct((b, s, d), BF16)),
        in_specs=[_rows(tm, d), _per_b(6, d), _const((1, d)), _const(w_gu.shape), _const(w_down.shape)],
        out_specs=(_rows(tm, d), _rows(tm, 2 * f), _rows(tm, d), _rows(tm, d)),
        compiler_params=_params(("arbitrary", "arbitrary")),
    )(x2, mod, g_ffn, w_gu, w_down)


def _zero_at_first_tile(*refs):
    @pl.when(pl.program_id(1) == 0)
    def _():
        for ref in refs:
            ref[...] = jnp.zeros_like(ref)


def _accumulate(ref, idx, val):
    ref[idx] = ref[idx] + val


def _colsum(v):
    return jnp.sum(v, axis=0, keepdims=True)


def _loss_bwd(x, target, g_final):
    b, s, d = x.shape
    tm = _row_tile(s)

    def body(x_ref, t_ref, g_ref, dx_ref, loss_ref, dg_ref):
        _zero_at_first_tile(loss_ref, dg_ref)
        xf = x_ref[0]
        r = _rstd(xf)
        xh = xf * r
        diff = xh * g_ref[...] - t_ref[0]
        _accumulate(loss_ref, (0, 0), _colsum(diff * diff))
        dy = diff * (1.0 / d)
        _accumulate(dg_ref, (0, 0), _colsum(dy * xh))
        dyg = dy * g_ref[...]
        dx_ref[0] = r * (dyg - xh * jnp.mean(dyg * xh, axis=-1, keepdims=True))

    return pl.pallas_call(
        body, name="loss_bwd", grid=(b, s // tm),
        out_shape=(jax.ShapeDtypeStruct((b, s, d), F32), jax.ShapeDtypeStruct((b, 1, 1, d), F32),
                   jax.ShapeDtypeStruct((b, 1, 1, d), F32)),
        in_specs=[_rows(tm, d), _rows(tm, d), _const((1, d))],
        out_specs=(_rows(tm, d), _per_b(1, d), _per_b(1, d)),
        compiler_params=_params(("arbitrary", "arbitrary")),
    )(x, target, g_final)


def _ffn_bwd(dx3, x2, mod, g_ffn, gu, o2, w_gu_t, w_down_t):
    b, s, d = x2.shape
    f = w_down_t.shape[1]
    fc = _pick(f, 512)
    tm = _row_tile(s, 256)

    def body(dx3_ref, x_ref, mod_ref, g_ref, gu_ref, o2_ref, wgut_ref, wdnt_ref, dx2_ref, do2_ref, act_ref, dgu_ref,
             dmod_ref, dg_ref):
        _zero_at_first_tile(dmod_ref, dg_ref)
        dx3 = dx3_ref[0]
        sh, sc, gt = mod_ref[0, 3], mod_ref[0, 4], mod_ref[0, 5]
        do2 = (dx3 * gt).astype(BF16)
        do2_ref[0] = do2
        _accumulate(dmod_ref, (0, 2), _colsum(dx3 * o2_ref[0].astype(F32)))
        dh = jnp.zeros((tm, d), F32)
        for c0 in range(0, f, fc):
            gf = gu_ref[0, :, c0:c0 + fc].astype(F32)
            uf = gu_ref[0, :, f + c0:f + c0 + fc].astype(F32)
            sg = _sig(gf)
            silu = gf * sg
            act_ref[0, :, c0:c0 + fc] = (silu * uf).astype(BF16)
            dact = _dot(do2, wdnt_ref[:, c0:c0 + fc])
            dg = (dact * uf * (sg * (1.0 + gf * (1.0 - sg)))).astype(BF16)
            du = (dact * silu).astype(BF16)
            dgu_ref[0, :, c0:c0 + fc] = dg
            dgu_ref[0, :, f + c0:f + c0 + fc] = du
            dh = dh + _dot(dg, wgut_ref[c0:c0 + fc, :]) + _dot(du, wgut_ref[f + c0:f + c0 + fc, :])
        xf = x_ref[0]
        r = _rstd(xf)
        xh = xf * r
        n = xh * g_ref[...]
        _accumulate(dmod_ref, (0, 0), _colsum(dh))
        _accumulate(dmod_ref, (0, 1), _colsum(dh * n))
        dn = dh * (1.0 + sc)
        _accumulate(dg_ref, (0, 0), _colsum(dn * xh))
        dyg = dn * g_ref[...]
        dx2_ref[0] = dx3 + r * (dyg - xh * jnp.mean(dyg * xh, axis=-1, keepdims=True))

    return pl.pallas_call(
        body, name="ffn_bwd", grid=(b, s // tm),
        out_shape=(jax.ShapeDtypeStruct((b, s, d), F32), jax.ShapeDtypeStruct((b, s, d), BF16),
                   jax.ShapeDtypeStruct((b, s, f), BF16), jax.ShapeDtypeStruct((b, s, 2 * f), BF16),
                   jax.ShapeDtypeStruct((b, 3, 1, d), F32), jax.ShapeDtypeStruct((b, 1, 1, d), F32)),
        in_specs=[_rows(tm, d), _rows(tm, d), _per_b(6, d), _const((1, d)), _rows(tm, 2 * f), _rows(tm, d),
                  _const(w_gu_t.shape), _const(w_down_t.shape)],
        out_specs=(_rows(tm, d), _rows(tm, d), _rows(tm, f), _rows(tm, 2 * f), _per_b(3, d), _per_b(1, d)),
        compiler_params=_params(("arbitrary", "arbitrary")),
    )(dx3, x2, mod, g_ffn, gu, o2, w_gu_t, w_down_t)


def _mix_post_bwd(dx2, mod, o1, y_attn, y_conv, gate, w_out_t, w_o_t):
    b, s, d = dx2.shape
    hw = w_o_t.shape[1]
    tm = _row_tile(s)

    def body(dx_ref, mod_ref, o1_ref, ya_ref, yc_ref, gate_ref, woutt_ref, wot_ref, do1_ref, dya_ref, dyc_ref, dgate_ref,
             y_ref, do_ref, dgt_ref):
        _zero_at_first_tile(dgt_ref)
        dx = dx_ref[0]
        do1 = (dx * mod_ref[0, 2]).astype(BF16)
        do1_ref[0] = do1
        _accumulate(dgt_ref, (0, 0), _colsum(dx * o1_ref[0].astype(F32)))
        dy = _dot(do1, woutt_ref[...])
        gv = gate_ref[0]
        sa, sb = _sig(gv[:, :d].astype(F32)), _sig(gv[:, d:].astype(F32))
        ya, yc = ya_ref[0].astype(F32), yc_ref[0].astype(F32)
        y_ref[0] = (sa * ya + sb * yc).astype(BF16)
        dya = (dy * sa).astype(BF16)
        dya_ref[0] = dya
        dyc_ref[0] = (dy * sb).astype(BF16)
        dgate_ref[0, :, :d] = (dy * ya * sa * (1.0 - sa)).astype(BF16)
        dgate_ref[0, :, d:] = (dy * yc * sb * (1.0 - sb)).astype(BF16)
        do_ref[0] = _dot(dya, wot_ref[...]).astype(BF16)

    row = jax.ShapeDtypeStruct((b, s, d), BF16)
    return pl.pallas_call(
        body, name="mix_post_bwd", grid=(b, s // tm),
        out_shape=(row, row, row, jax.ShapeDtypeStruct((b, s, 2 * d), BF16), row,
                   jax.ShapeDtypeStruct((b, s, hw), BF16), jax.ShapeDtypeStruct((b, 1, 1, d), F32)),
        in_specs=[_rows(tm, d), _per_b(6, d), _rows(tm, d), _rows(tm, d), _rows(tm, d), _rows(tm, 2 * d),
                  _const(w_out_t.shape), _const(w_o_t.shape)],
        out_specs=(_rows(tm, d), _rows(tm, d), _rows(tm, d), _rows(tm, 2 * d), _rows(tm, d), _rows(tm, hw), _per_b(1, d)),
        compiler_params=_params(("arbitrary", "arbitrary")),
    )(dx2, mod, o1, y_attn, y_conv, gate, w_out_t, w_o_t)


def _conv_bwd(dyc, u, glu, w_dw, g_cn, b_cn, w_pw2_t):
    b, s, cc = u.shape
    d = dyc.shape[2]
    tm = _row_tile(s)
    nt = s // tm
    te = tm + HALO
    rc = _conv_rows(cc)

    def body(dyc_ref, dycn_ref, u_ref, un_ref, glu_ref, glup_ref, w_ref, g_ref, bcn_ref, wpt_ref, dglu_ref, s_ref, dw_ref,
             small_ref, du_ext, uin_ext, duin_all):
        _zero_at_first_tile(dw_ref, small_ref)
        st = pl.program_id(1)
        dy_all = jnp.concatenate([dyc_ref[0], dycn_ref[0]], axis=0)
        u_all = jnp.concatenate([u_ref[0], un_ref[0]], axis=0).astype(F32)
        ds = _dot(dy_all, wpt_ref[...])
        mu = jnp.mean(u_all, axis=-1, keepdims=True)
        uc = u_all - mu
        rstd = lax.rsqrt(jnp.mean(uc * uc, axis=-1, keepdims=True) + EPS)
        uh = uc * rstd
        ln = uh * g_ref[...] + bcn_ref[...]
        sg = _sig(ln)
        s_ref[0] = (ln * sg)[:tm].astype(BF16)
        dln = ds * (sg * (1.0 + ln * (1.0 - sg)))
        duh = dln * g_ref[...]
        du = rstd * (duh - jnp.mean(duh, axis=-1, keepdims=True) - uh * jnp.mean(duh * uh, axis=-1, keepdims=True))
        row = lax.broadcasted_iota(jnp.int32, (te, 1), 0)
        du = jnp.where(jnp.logical_and(st == nt - 1, row >= tm), 0.0, du)
        du_ext[0, pl.ds(0, te), :] = du
        _shifted_copies(du_ext, te)
        du_cur = du[:tm]
        _accumulate(small_ref, (0, 0), _colsum((dln * uh)[:tm]))
        _accumulate(small_ref, (0, 1), _colsum(dln[:tm]))
        _accumulate(small_ref, (0, 2), _colsum(du_cur))
        uin_ext[0, pl.ds(0, HALO), :] = jnp.where(st == 0, 0.0, _glu(glup_ref[0], cc))
        gv = glu_ref[0]
        ga, gb = gv[:, :cc].astype(F32), gv[:, cc:].astype(F32)
        sgb = _sig(gb)
        uin_ext[0, pl.ds(HALO, tm), :] = ga * sgb
        _shifted_copies(uin_ext, te)
        for c0 in range(0, tm, rc):
            du_c = du_ext[0, pl.ds(c0, rc), :]
            acc = jnp.zeros((rc, cc), F32)
            for kk in range(CONV_W):
                acc = acc + w_ref[pl.ds(kk, 1), :] * _window(du_ext, c0 + CONV_W - 1 - kk, rc)
                prod = du_c * _window(uin_ext, c0 + HALO - CONV_W + 1 + kk, rc)
                taps = pl.ds(8 * kk, 8)
                dw_ref[0, taps, :] = dw_ref[0, taps, :] + jnp.sum(prod.reshape(rc // 8, 8, cc), axis=0)
            duin_all[pl.ds(c0, rc), :] = acc
        duin = duin_all[...]
        dglu_ref[0, :, :cc] = (duin * sgb).astype(BF16)
        dglu_ref[0, :, cc:] = (duin * ga * sgb * (1.0 - sgb)).astype(BF16)

    return pl.pallas_call(
        body, name="conv_bwd", grid=(b, nt),
        out_shape=(jax.ShapeDtypeStruct((b, s, 2 * cc), BF16), jax.ShapeDtypeStruct((b, s, cc), BF16),
                   jax.ShapeDtypeStruct((b, 8 * HALO, cc), F32), jax.ShapeDtypeStruct((b, 3, 1, cc), F32)),
        in_specs=[_rows(tm, d), _halo_next(tm, d, nt), _rows(tm, cc), _halo_next(tm, cc, nt), _rows(tm, 2 * cc),
                  _halo_prev(tm, 2 * cc), _const(w_dw.shape), _const((1, cc)), _const((1, cc)), _const(w_pw2_t.shape)],
        out_specs=(_rows(tm, 2 * cc), _rows(tm, cc), pl.BlockSpec((1, 8 * HALO, cc), lambda i, j: (i, 0, 0)),
                   _per_b(3, cc)),
        scratch_shapes=[pltpu.VMEM((8, te + 8, cc), F32), pltpu.VMEM((8, te + 8, cc), F32), pltpu.VMEM((tm, cc), F32)],
        compiler_params=_params(("arbitrary", "arbitrary")),
    )(dyc, dyc, u, u, glu, glu, w_dw, g_cn, b_cn, w_pw2_t)


def _flash_bwd(q, k, v, o, lse, do, scatter=()):
    b, s, hw = q.shape
    nh, hp = hw // HEAD_PAD, HEAD_PAD
    t = _row_tile(s)
    nt = s // t
    n = len(scatter)
    grid = (b, nh, nt)

    def to_row(sel, cols):
        hi = cols.astype(BF16)
        lo = (cols - hi.astype(F32)).astype(BF16)
        return _dot_nt(sel, hi) + _dot_nt(sel, lo)

    def body(q_ref, k_ref, v_ref, o_ref, lse_ref, do_ref, *rest):
        dqt_ref, dk_ref, dv_ref = rest[n:n + 3]
        lse_row, delta_row = rest[2 * n + 3:2 * n + 5]
        finish = _riding_exchange("scatter", n, rest[:n] + rest[n + 3:2 * n + 3] + rest[2 * n + 5:], grid)
        j = pl.program_id(2)

        @pl.when(j == 0)
        def _():
            dqt_ref[...] = jnp.zeros_like(dqt_ref)
            first_lane = (lax.broadcasted_iota(jnp.int32, (8, hp), 1) == 0).astype(BF16)
            for i in range(nt):
                rows = pl.ds(i * t, t)
                prod = do_ref[0, rows, :].astype(F32) * o_ref[0, rows, :].astype(F32)
                delta_row[i] = to_row(jnp.ones((8, hp), BF16), prod)
                lse_row[i] = to_row(first_lane, jnp.broadcast_to(lse_ref[0, 0, rows, :], (t, hp)))

        kv, vv = k_ref[0], v_ref[0]
        kt = kv.T
        query_not_before_key = (lax.broadcasted_iota(jnp.int32, (t, t), 1) >= lax.broadcasted_iota(jnp.int32, (t, t), 0))

        def step(i, carry, masked):
            dk, dv = carry
            rows = pl.ds(pl.multiple_of(i * t, t), t)
            qv, dov = q_ref[0, rows, :], do_ref[0, rows, :]
            pt = jnp.exp((_dot_nt(kv, qv) - lse_row[i, 0:1, :]).astype(BF16))
            if masked:
                pt = jnp.where(query_not_before_key, pt, jnp.zeros((), BF16))
            dv = dv + _dot(pt, dov)
            dst = pt * (_dot_nt(vv, dov) - delta_row[i, 0:1, :]).astype(BF16)
            dk = dk + _dot(dst, qv)
            dqt_ref[0, i] = dqt_ref[0, i] + _dot(kt, dst)
            return dk, dv

        carry = step(j, (jnp.zeros((t, hp), F32), jnp.zeros((t, hp), F32)), True)
        pairs = (nt - 1 - j) // 2
        carry = lax.fori_loop(0, pairs, lambda ii, cr: step(j + 2 + 2 * ii, step(j + 1 + 2 * ii, cr, False), False), carry)
        dk, dv = lax.cond((nt - 1 - j) % 2 == 1, lambda cr: step(nt - 1, cr, False), lambda cr: cr, carry)
        dk_ref[0] = dk.astype(BF16)
        dv_ref[0] = dv.astype(BF16)
        if finish:
            finish()

    tile = pl.BlockSpec((1, t, hp), lambda bb, hh, jj: (bb, jj, hh))
    full = pl.BlockSpec((1, s, hp), lambda bb, hh, jj: (bb, 0, hh))
    hbm = [pl.BlockSpec(memory_space=pl.ANY)] * n
    outs = pl.pallas_call(
        body, name="flash_bwd_scatter" if n else "flash_bwd", grid=grid,
        out_shape=[jax.ShapeDtypeStruct((b, nt, hw, t), F32), jax.ShapeDtypeStruct((b, s, hw), BF16),
                   jax.ShapeDtypeStruct((b, s, hw), BF16)] + _DirectExchange.out_shapes("scatter", scatter),
        in_specs=[full, tile, tile, full, pl.BlockSpec((1, 1, s, 1), lambda bb, hh, jj: (bb, hh, 0, 0)), full] + hbm,
        out_specs=[pl.BlockSpec((1, nt, hp, t), lambda bb, hh, jj: (bb, 0, hh, 0)), tile, tile] + hbm,
        scratch_shapes=[pltpu.VMEM((nt, 8, t), F32), pltpu.VMEM((nt, 8, t), F32)]
        + (_DirectExchange.scratch(n) if n else []),
        compiler_params=_params(("arbitrary", "arbitrary", "arbitrary")),
    )(q, k, v, o, lse, do, *scatter)
    return outs[0], outs[1], outs[2], outs[3:]


def _mix_pre_bwd(x, dx2, mod, cos_t, sin_t, g_mix, g_q, g_kv, lat, dq, dk, dv, dglu, dgate, w_lat_t, w_glu_t, w_gate_t,
                 w_q_t, w_k_t, w_v_t):
    b, s, d = x.shape
    ql, kl = g_q.shape[1], g_kv.shape[1]
    wl = lat.shape[2]
    hw = N_HEADS * HEAD_PAD
    tm = _row_tile(s)

    def body(x_ref, dx2_ref, mod_ref, cos_ref, sin_ref, gm_ref, gq_ref, gkv_ref, lat_ref, dq_ref, dk_ref, dv_ref, dglu_ref,
             dgate_ref, wlt_ref, wgt_ref, wtt_ref, wqt_ref, wkt_ref, wvt_ref, dx_ref, dlat_ref, dqr_ref, qn_ref, kvn_ref,
             h_ref, dmod_ref, dgm_ref, dgq_ref, dgkv_ref):
        _zero_at_first_tile(dmod_ref, dgm_ref, dgq_ref, dgkv_ref)
        cos_v, sin_v = cos_ref[0], sin_ref[0]
        latf = lat_ref[0].astype(F32)
        q_lat, kv_lat = latf[:, :ql], latf[:, ql:ql + kl]
        rq, rk = _rstd(q_lat), _rstd(kv_lat)
        qh, kh = q_lat * rq, kv_lat * rk
        qn_ref[0] = (qh * gq_ref[...]).astype(BF16)
        kvn_ref[0] = (kh * gkv_ref[...]).astype(BF16)
        dk_sum = jnp.zeros((tm, HEAD_PAD), F32)
        for h in range(N_HEADS):
            cols = slice(h * HEAD_PAD, (h + 1) * HEAD_PAD)
            dq_head = dq_ref[0, 0, cols, :].T
            dqr_ref[0, :, cols] = _rope_bwd(dq_head * ATTN_SCALE, cos_v, sin_v).astype(BF16)
            dk_sum = dk_sum + dk_ref[0, :, cols].astype(F32)
        dqn = _dot(dqr_ref[0], wqt_ref[...])
        dkvn = _dot(dk_ref[0], wkt_ref[...]) + _dot(dv_ref[0], wvt_ref[...])
        lane = lax.broadcasted_iota(jnp.int32, dk_sum.shape, 1)
        dkr = _rope_bwd(jnp.where(lane >= QK_NOPE, dk_sum, 0.0), cos_v, sin_v)
        _accumulate(dgq_ref, (0, 0), _colsum(dqn * qh))
        _accumulate(dgkv_ref, (0, 0), _colsum(dkvn * kh))
        dqg, dkg = dqn * gq_ref[...], dkvn * gkv_ref[...]
        dlat_ref[0, :, :ql] = (rq * (dqg - qh * jnp.mean(dqg * qh, axis=-1, keepdims=True))).astype(BF16)
        dlat_ref[0, :, ql:ql + kl] = (rk * (dkg - kh * jnp.mean(dkg * kh, axis=-1, keepdims=True))).astype(BF16)
        dlat_ref[0, :, ql + kl:] = dkr.astype(BF16)
        dh = _dot(dlat_ref[0], wlt_ref[...]) + _dot(dglu_ref[0], wgt_ref[...]) + _dot(dgate_ref[0], wtt_ref[...])
        sh, sc = mod_ref[0, 0], mod_ref[0, 1]
        xf = x_ref[0]
        r = _rstd(xf)
        xh = xf * r
        n = xh * gm_ref[...]
        h_ref[0] = (n * (1.0 + sc) + sh).astype(BF16)
        _accumulate(dmod_ref, (0, 0), _colsum(dh))
        _accumulate(dmod_ref, (0, 1), _colsum(dh * n))
        dn = dh * (1.0 + sc)
        _accumulate(dgm_ref, (0, 0), _colsum(dn * xh))
        dyg = dn * gm_ref[...]
        dx_ref[0] = dx2_ref[0] + r * (dyg - xh * jnp.mean(dyg * xh, axis=-1, keepdims=True))

    return pl.pallas_call(
        body, name="mix_pre_bwd", grid=(b, s // tm),
        out_shape=(jax.ShapeDtypeStruct((b, s, d), F32), jax.ShapeDtypeStruct((b, s, wl), BF16),
                   jax.ShapeDtypeStruct((b, s, hw), BF16), jax.ShapeDtypeStruct((b, s, ql), BF16),
                   jax.ShapeDtypeStruct((b, s, kl), BF16), jax.ShapeDtypeStruct((b, s, d), BF16),
                   jax.ShapeDtypeStruct((b, 2, 1, d), F32), jax.ShapeDtypeStruct((b, 1, 1, d), F32),
                   jax.ShapeDtypeStruct((b, 1, 1, ql), F32), jax.ShapeDtypeStruct((b, 1, 1, kl), F32)),
        in_specs=[_rows(tm, d), _rows(tm, d), _per_b(6, d), _rows(tm, HEAD_PAD), _rows(tm, HEAD_PAD), _const((1, d)),
                  _const((1, ql)), _const((1, kl)), _rows(tm, wl),
                  pl.BlockSpec((1, 1, hw, tm), lambda b, s: (b, s, 0, 0)), _rows(tm, hw), _rows(tm, hw),
                  _rows(tm, dglu.shape[2]), _rows(tm, 2 * d), _const(w_lat_t.shape), _const(w_glu_t.shape),
                  _const(w_gate_t.shape), _const(w_q_t.shape), _const(w_k_t.shape), _const(w_v_t.shape)],
        out_specs=(_rows(tm, d), _rows(tm, wl), _rows(tm, hw), _rows(tm, ql), _rows(tm, kl), _rows(tm, d), _per_b(2, d),
                   _per_b(1, d), _per_b(1, ql), _per_b(1, kl)),
        compiler_params=_params(("arbitrary", "arbitrary")),
    )(x, dx2, mod, cos_t, sin_t, g_mix, g_q, g_kv, lat, dq, dk, dv, dglu, dgate, w_lat_t, w_glu_t, w_gate_t, w_q_t, w_k_t,
      w_v_t)


def _matmul_tn(a, bm, name):
    b, s, kd = a.shape
    nd = bm.shape[2]
    tk, tn = _pick(kd, 1536), _pick(nd, 1536)
    ts = _row_tile(s, 2048)

    def body(a_ref, b_ref, o_ref):
        part = _dot_tn(a_ref[0], b_ref[0])
        first = jnp.logical_and(pl.program_id(2) == 0, pl.program_id(3) == 0)

        @pl.when(first)
        def _():
            o_ref[...] = part

        @pl.when(jnp.logical_not(first))
        def _():
            o_ref[...] = o_ref[...] + part

    return pl.pallas_call(
        body, name=name, grid=(kd // tk, nd // tn, b, s // ts),
        out_shape=jax.ShapeDtypeStruct((kd, nd), F32),
        in_specs=[pl.BlockSpec((1, ts, tk), lambda i, j, bb, ss: (bb, ss, i)),
                  pl.BlockSpec((1, ts, tn), lambda i, j, bb, ss: (bb, ss, j))],
        out_specs=pl.BlockSpec((tk, tn), lambda i, j, bb, ss: (i, j)),
        compiler_params=_params(("arbitrary",) * 4),
    )(a, bm)


def _adamw_update(w, g, m, v):
    nm = ADAM_B1 * m + (1.0 - ADAM_B1) * g
    nv = ADAM_B2 * v + (1.0 - ADAM_B2) * (g * g)
    delta = -ADAM_LR * ((nm / (1.0 - ADAM_B1 ** ADAM_STEP)) / (jnp.sqrt(nv / (1.0 - ADAM_B2 ** ADAM_STEP)) + ADAM_EPS)
                        + ADAM_WD * w)
    return delta, nm, nv


def _adamw(w, g, m, v, name):
    shape = w.shape
    cols = shape[-1]
    rows = w.size // cols
    w2, g2, m2, v2 = (t.reshape(rows, cols) for t in (w, g, m, v))
    tr = rows
    if rows * cols * 4 > (1 << 20):
        tr = _div_tile(rows, max(8, (1 << 18) // cols), 8)

    def body(w_ref, g_ref, m_ref, v_ref, d_ref, nm_ref, nv_ref):
        d_ref[...], nm_ref[...], nv_ref[...] = _adamw_update(w_ref[...], g_ref[...], m_ref[...], v_ref[...])

    spec = pl.BlockSpec((tr, cols), lambda i: (i, 0))
    outs = pl.pallas_call(
        body, name=name, grid=(rows // tr,), out_shape=(jax.ShapeDtypeStruct((rows, cols), F32),) * 3,
        in_specs=[spec] * 4, out_specs=(spec,) * 3, compiler_params=_params(("arbitrary",)),
    )(w2, g2, m2, v2)
    return tuple(t.reshape(shape) for t in outs)


def _adamw_reduce(w, m, v, own, got, name):
    shape = w.shape
    cols = shape[-1]
    w2, m2, v2, o2 = (_rows_2d(t, 0) for t in (w, m, v, own))
    g3 = _rows_2d(got, N_DEV - 1)
    rows = w2.shape[0]
    tr = _grad_row_tile(rows, cols)

    def body(w_ref, m_ref, v_ref, o_ref, r_ref, g_ref, d_ref, nm_ref, nv_ref):
        g = o_ref[...].astype(F32)
        for r in range(N_DEV - 1):
            g = g + r_ref[r].astype(F32)
        g_ref[...] = g
        d_ref[...], nm_ref[...], nv_ref[...] = _adamw_update(w_ref[...], g, m_ref[...], v_ref[...])

    spec = pl.BlockSpec((tr, cols), lambda i: (i, 0))
    outs = pl.pallas_call(
        body, name=name, grid=(rows // tr,),
        in_specs=[spec, spec, spec, spec, pl.BlockSpec((N_DEV - 1, tr, cols), lambda i: (0, i, 0))],
        out_specs=(spec,) * 4,
        out_shape=(jax.ShapeDtypeStruct((rows, cols), F32),) * 4, compiler_params=_params(("arbitrary",)),
    )(w2, m2, v2, o2, g3)
    return tuple(t.reshape(shape) for t in outs)


GATHERED = (("w_in", 2), ("w_uq", 2), ("w_ukv", 2), ("w_o_attn", 2), ("w_pw2", 2), ("w_out", 1), ("w_gu", 2), ("w_down", 1))
PRE, POST = (0, 1, 2), (3, 4, 5, 6, 7)


def _from_chunks(chunks, axis):
    _, a, bb = chunks.shape
    if axis == 2:
        return jnp.transpose(chunks, (1, 0, 2)).reshape(a, N_DEV * bb)
    return chunks.reshape(N_DEV * a, bb)


def _to_chunks(full, axis):
    a, bb = full.shape
    if axis == 2:
        return jnp.transpose(full.reshape(a, N_DEV, bb // N_DEV), (1, 0, 2)).astype(BF16)
    return full.reshape(N_DEV, a // N_DEV, bb).astype(BF16)


def _swap_halves(t):
    half = QK_ROPE // 2
    return jnp.concatenate([t[..., half:], t[..., :half]], axis=-1)


def _t(w):
    return jnp.swapaxes(w, -1, -2)


def _pad_rows(t, mult=8):
    return jnp.pad(t, ((0, -t.shape[0] % mult), (0, 0)))


def _pad_last(t, width):
    return jnp.pad(t, ((0, 0),) * (t.ndim - 1) + ((0, width - t.shape[-1]),))


def kernel(x, c, positions, w_ada, b_ada, g_mix, w_in, g_q, w_uq, g_kv, w_ukv, w_o_attn, w_dw, b_dw, g_cn, b_cn, w_pw2, w_out, g_ffn, w_gu, w_down, g_final, loss_target, m_w_ada, m_b_ada, m_g_mix, m_w_in, m_g_q, m_w_uq, m_g_kv, m_w_ukv, m_w_o_attn, m_w_dw, m_b_dw, m_g_cn, m_b_cn, m_w_pw2, m_w_out, m_g_ffn, m_w_gu, m_w_down, m_g_final, v_w_ada, v_b_ada, v_g_mix, v_w_in, v_g_q, v_w_uq, v_g_kv, v_w_ukv, v_w_o_attn, v_w_dw, v_b_dw, v_g_cn, v_b_cn, v_w_pw2, v_w_out, v_g_ffn, v_w_gu, v_w_down, v_g_final):
    weights = dict(w_ada=w_ada, b_ada=b_ada, g_mix=g_mix, w_in=w_in, g_q=g_q, w_uq=w_uq, g_kv=g_kv, w_ukv=w_ukv,
                   w_o_attn=w_o_attn, w_dw=w_dw, b_dw=b_dw, g_cn=g_cn, b_cn=b_cn, w_pw2=w_pw2, w_out=w_out, g_ffn=g_ffn,
                   w_gu=w_gu, w_down=w_down, g_final=g_final)
    mom_m = dict(w_ada=m_w_ada, b_ada=m_b_ada, g_mix=m_g_mix, w_in=m_w_in, g_q=m_g_q, w_uq=m_w_uq, g_kv=m_g_kv,
                 w_ukv=m_w_ukv, w_o_attn=m_w_o_attn, w_dw=m_w_dw, b_dw=m_b_dw, g_cn=m_g_cn, b_cn=m_b_cn, w_pw2=m_w_pw2,
                 w_out=m_w_out, g_ffn=m_g_ffn, w_gu=m_w_gu, w_down=m_w_down, g_final=m_g_final)
    mom_v = dict(w_ada=v_w_ada, b_ada=v_b_ada, g_mix=v_g_mix, w_in=v_w_in, g_q=v_g_q, w_uq=v_w_uq, g_kv=v_g_kv,
                 w_ukv=v_w_ukv, w_o_attn=v_w_o_attn, w_dw=v_w_dw, b_dw=v_b_dw, g_cn=v_g_cn, b_cn=v_b_cn, w_pw2=v_w_pw2,
                 w_out=v_w_out, g_ffn=v_g_ffn, w_gu=v_w_gu, w_down=v_w_down, g_final=v_g_final)
    order = list(weights)

    nb, s, d = x.shape
    nl = w_in.shape[0]
    ql, kl, cc = g_q.shape[1], g_kv.shape[1], g_cn.shape[1]
    h = N_HEADS
    qk = QK_NOPE + QK_ROPE
    xi, yi, ci = lax.axis_index("x"), lax.axis_index("y"), lax.axis_index("c")
    me = 4 * xi + 2 * yi + ci

    shards = [weights[n].astype(BF16) for n, _ in GATHERED]
    gathered_pre = _all_gather([shards[a][0] for a in PRE], vmem=False, name="weight_all_gather")
    n_dw = w_dw.shape[2]
    dw_rows = jnp.pad(w_dw, ((0, 0), (0, HALO - CONV_W), (0, LANES - n_dw))).reshape(nl * HALO, LANES)
    c_all, dw_all = _all_gather([_pad_rows(c), dw_rows], vmem=True, name="cond_all_gather")
    c_full = c_all[:, :nb].reshape(N_DEV * nb, d)
    w_dw_full = jnp.transpose(dw_all.reshape(N_DEV, nl, HALO, LANES)[..., :n_dw], (1, 2, 0, 3)).reshape(nl, HALO, cc)

    o_kr, o_glu, o_gate = ql + kl, ql + kl + QK_ROPE, ql + kl + QK_ROPE + 2 * cc

    def layouts_pre(chunks):
        wi, w_uq_l, w_ukv_l = (_from_chunks(t, GATHERED[a][1]) for a, t in zip(PRE, chunks))
        w_kr = wi[:, o_kr:o_glu]
        wq = w_uq_l.reshape(ql, h, qk)
        wkv = w_ukv_l.reshape(kl, h, QK_NOPE + V_HEAD)
        return dict(
            lat=jnp.concatenate([wi[:, :o_kr], jnp.zeros((d, QK_NOPE), BF16), w_kr, _swap_halves(w_kr)], axis=1),
            glu=wi[:, o_glu:o_gate], gate=wi[:, o_gate:],
            q=jnp.concatenate([wq, _swap_halves(wq[..., QK_NOPE:])], axis=-1).reshape(ql, h * HEAD_PAD),
            k=_pad_last(wkv[..., :QK_NOPE], HEAD_PAD).reshape(kl, h * HEAD_PAD),
            v=_pad_last(wkv[..., QK_NOPE:], HEAD_PAD).reshape(kl, h * HEAD_PAD))

    def layouts_post(chunks):
        w_o_l, w_pw_l, w_out_l, w_gu_l, w_down_l = (_from_chunks(t, GATHERED[a][1]) for a, t in zip(POST, chunks))
        w_o = jnp.pad(w_o_l.reshape(h, V_HEAD, d), ((0, 0), (0, HEAD_PAD - V_HEAD), (0, 0)))
        return dict(o=w_o.reshape(h * HEAD_PAD, d), pw=w_pw_l, out=w_out_l, gu=w_gu_l, down=w_down_l)

    n_ada = w_ada.shape[2]
    b_cols = lax.dynamic_slice_in_dim(b_ada, me * n_ada, n_ada, axis=1).reshape(nl, 1, n_ada)
    mod_part = _ada_fwd(c_full, w_ada, b_cols)
    (mod_all,) = _all_gather([mod_part.reshape(nl * N_DEV * nb, n_ada)], vmem=True, name="mod_all_gather")
    mod_all = jnp.transpose(mod_all.reshape(N_DEV, nl, N_DEV * nb, n_ada), (1, 2, 0, 3)).reshape(nl, N_DEV * nb, 6 * d)
    mod = lax.dynamic_slice_in_dim(mod_all, me * nb, nb, axis=1).reshape(nl, nb, 6, 1, d)

    inv_freq = ROPE_THETA ** (-jnp.arange(0, QK_ROPE, 2, dtype=F32) / QK_ROPE)
    zeros = lambda n: jnp.zeros((n,), F32)
    freq_row = jnp.concatenate([zeros(QK_NOPE), inv_freq, inv_freq, zeros(HEAD_PAD - qk)]).reshape(1, -1)
    ones = jnp.ones((QK_ROPE // 2,), F32)
    sign_row = jnp.concatenate([zeros(QK_NOPE), -ones, ones, zeros(HEAD_PAD - qk)]).reshape(1, -1)
    cos_t, sin_t = _rope_tables(positions.astype(F32).reshape(nb, s, 1), freq_row, sign_row)

    row = lambda t, l: t[l].reshape(1, -1)

    saved, wts = [], []
    xc = x
    for l in range(nl):
        wl = layouts_pre(gathered_pre)
        lat, glu, gate, qh, kh, vh = _mix_pre_fwd(xc, mod[l], cos_t, sin_t, row(g_mix, l), row(g_q, l), row(g_kv, l),
                                                  wl["lat"], wl["glu"], wl["gate"], wl["q"], wl["k"], wl["v"])
        riders = [shards[a][l] for a in POST] + ([shards[a][l + 1] for a in PRE] if l + 1 < nl else [])
        o, lse, arrived = _flash_fwd(qh, kh, vh, gather=riders)
        wl.update(layouts_post(arrived[:len(POST)]))
        gathered_pre = arrived[len(POST):]
        wts.append(wl)
        u, y_conv = _conv_fwd(glu, w_dw_full[l], row(b_dw, l), row(g_cn, l), row(b_cn, l), wl["pw"])
        x2, y_attn, o1 = _mix_post_fwd(xc, mod[l], o, y_conv, gate, wl["o"], wl["out"])
        x3, gu, o2, h2 = _ffn_fwd(x2, mod[l], row(g_ffn, l), wl["gu"], wl["down"])
        saved.append(dict(x=xc, lat=lat, glu=glu, gate=gate, q=qh, k=kh, v=vh, o=o, lse=lse, u=u, y_conv=y_conv, x2=x2,
                          y_attn=y_attn, o1=o1, gu=gu, o2=o2, h2=h2))
        xc = x3

    dx, loss_part, dgf_part = _loss_bwd(xc, loss_target, g_final.reshape(1, d))
    loss = lax.psum(0.5 / d * jnp.sum(loss_part), AXES)

    small_rows, dw_taps, dmod = [None] * nl, [None] * nl, [None] * nl
    own, got = {}, {}
    chunk = lambda g, a: _to_chunks(g[GATHERED[a][0]], GATHERED[a][1])
    mine = lambda t: lax.dynamic_index_in_dim(t, me, axis=0, keepdims=False)
    pending = []
    for l in reversed(range(nl)):
        sv, wl, gw = saved[l], wts[l], {}
        dx2, do2, act, dgu, dmod2, dgffn = _ffn_bwd(dx, sv["x2"], mod[l], row(g_ffn, l), sv["gu"], sv["o2"],
                                                   _t(wl["gu"]), _t(wl["down"]))
        gw["w_gu"] = _matmul_tn(sv["h2"], dgu, "grad_w_gu")
        gw["w_down"] = _matmul_tn(act, do2, "grad_w_down")
        do1, dya, dyc, dgate, yv, do_h, dgt1 = _mix_post_bwd(dx2, mod[l], sv["o1"], sv["y_attn"], sv["y_conv"], sv["gate"],
                                                           _t(wl["out"]), _t(wl["o"]))
        gw["w_out"] = _matmul_tn(yv, do1, "grad_w_out")
        dwo = _matmul_tn(sv["o"], dya, "grad_w_o")
        gw["w_o_attn"] = dwo.reshape(h, HEAD_PAD, d)[:, :V_HEAD].reshape(h * V_HEAD, d)
        dglu, s_act, ddw, csmall = _conv_bwd(dyc, sv["u"], sv["glu"], w_dw_full[l], row(g_cn, l), row(b_cn, l), _t(wl["pw"]))
        gw["w_pw2"] = _matmul_tn(s_act, dyc, "grad_w_pw2")
        ready = [chunk(gw, a) for a in POST]
        dq, dk, dv, arrived = _flash_bwd(sv["q"], sv["k"], sv["v"], sv["o"], sv["lse"], do_h, scatter=ready + pending)
        for i, a in enumerate(POST):
            own[l, a], got[l, a] = mine(ready[i]), arrived[i]
        for i, a in enumerate(PRE if pending else ()):
            got[l + 1, a] = arrived[len(POST) + i]
        dx, dlat, dqr, qn, kvn, h1, dmod1, dgm, dgq, dgkv = _mix_pre_bwd(
            sv["x"], dx2, mod[l], cos_t, sin_t, row(g_mix, l), row(g_q, l), row(g_kv, l), sv["lat"], dq, dk, dv, dglu,
            dgate, _t(wl["lat"]), _t(wl["glu"]), _t(wl["gate"]), _t(wl["q"]), _t(wl["k"]), _t(wl["v"]))
        dwl = _matmul_tn(h1, dlat, "grad_w_lat")
        dwg = _matmul_tn(h1, dglu, "grad_w_glu")
        dwt = _matmul_tn(h1, dgate, "grad_w_gate")
        kr0 = o_kr + QK_NOPE
        dkr = dwl[:, kr0:kr0 + QK_ROPE] + _swap_halves(dwl[:, kr0 + QK_ROPE:])
        gw["w_in"] = jnp.concatenate([dwl[:, :o_kr], dkr, dwg, dwt], axis=1)
        dwq = _matmul_tn(qn, dqr, "grad_w_q").reshape(ql, h, HEAD_PAD)
        dwq = jnp.concatenate([dwq[..., :QK_NOPE], dwq[..., QK_NOPE:qk] + _swap_halves(dwq[..., qk:])], axis=-1)
        gw["w_uq"] = dwq.reshape(ql, h * qk)
        dwk = _matmul_tn(kvn, dk, "grad_w_k").reshape(kl, h, HEAD_PAD)
        dwv = _matmul_tn(kvn, dv, "grad_w_v").reshape(kl, h, HEAD_PAD)
        gw["w_ukv"] = jnp.concatenate([dwk[..., :QK_NOPE], dwv[..., :V_HEAD]], axis=-1).reshape(kl, h * (QK_NOPE + V_HEAD))
        pending = [chunk(gw, a) for a in PRE]
        for i, a in enumerate(PRE):
            own[l, a] = mine(pending[i])
        dmod[l] = jnp.concatenate([dmod1[:, :, 0], dgt1[:, :, 0], dmod2[:, :, 0]], axis=1).reshape(nb, 6 * d)
        bsum = lambda t: jnp.sum(t, axis=0).reshape(1, -1)
        cs = jnp.sum(csmall, axis=0)[:, 0]
        small_rows[l] = jnp.concatenate([bsum(dgm), bsum(dgq), bsum(dgkv), cs[2:3], cs[0:1], cs[1:2], bsum(dgffn)], axis=1)
        dw_taps[l] = jnp.sum(ddw.reshape(nb, HALO, 8, cc), axis=(0, 2))
    grad_x = dx

    dmod_rows = _pad_rows(jnp.stack(dmod).reshape(nl * nb, 6 * d))
    (dmod_all,) = _all_gather([dmod_rows], vmem=True, name="dmod_all_gather")
    dmod_full = jnp.transpose(dmod_all[:, :nl * nb].reshape(N_DEV, nl, nb, 6 * d), (1, 0, 2, 3)).reshape(nl, N_DEV * nb, 6 * d)
    dmod_cols = lax.dynamic_slice_in_dim(dmod_full, me * n_ada, n_ada, axis=2)
    grad_w_ada, grad_b_ada = _ada_bwd(c_full, dmod_cols, dmod_full)
    grads = {"w_ada": grad_w_ada, "b_ada": grad_b_ada.reshape(nl, 6 * d)}

    widths = (d, ql, kl, cc, cc, cc, d)
    wsum = sum(widths)
    final_row = _pad_last(jnp.sum(dgf_part, axis=0).reshape(1, d), wsum)
    small2d = _pad_rows(jnp.concatenate(small_rows + [final_row], axis=0))
    taps2d = jnp.concatenate(dw_taps, axis=0)
    small_all, taps_all = _all_gather([small2d, taps2d], vmem=True, name="small_grad_all_gather")
    small_sum, taps_sum = _sum_devices(small_all), _sum_devices(taps_all)
    off = 0
    for n, wdt in zip(("g_mix", "g_q", "g_kv", "b_dw", "g_cn", "b_cn", "g_ffn"), widths):
        grads[n] = small_sum[:nl, off:off + wdt]
        off += wdt
    grads["g_final"] = small_sum[nl, :d]
    taps = taps_sum.reshape(nl, HALO, cc)[:, :CONV_W]
    grads["w_dw"] = lax.dynamic_slice_in_dim(taps, me * n_dw, n_dw, axis=2)

    delta, new_m, new_v = {}, {}, {}
    for n in order:
        if n in grads:
            delta[n], new_m[n], new_v[n] = _adamw(weights[n], grads[n], mom_m[n], mom_v[n], "adamw_" + n)

    for a, t in zip(PRE, _grad_scatter(pending)):
        got[0, a] = t
    for a, (n, _) in enumerate(GATHERED):
        own_n = jnp.stack([own[l, a] for l in range(nl)])
        got_n = jnp.stack([got[l, a] for l in range(nl)], axis=1)
        grads[n], delta[n], new_m[n], new_v[n] = _adamw_reduce(weights[n], mom_m[n], mom_v[n], own_n, got_n, "adamw_" + n)

    return (loss, grad_x, *[grads[n] for n in order], *[delta[n] for n in order], *[new_m[n] for n in order],
            *[new_v[n] for n in order])
```

```python
import jax
import jax.numpy as jnp
from jax import lax
from jax.experimental import pallas as pl
from jax.experimental.pallas import tpu as pltpu

F32, BF16 = jnp.float32, jnp.bfloat16
MESH = pl.DeviceIdType.MESH
AXES = ("x", "y", "c")
N_DEV = 8

N_HEADS = 8
QK_NOPE = 64
QK_ROPE = 32
V_HEAD = 64
HEAD_PAD = 128
CONV_W = 31
HALO = 32
EPS = 1e-6
ROPE_THETA = 10000.0
NEG_INF = -1e30
ATTN_SCALE = (QK_NOPE + QK_ROPE) ** -0.5

ADAM_LR, ADAM_B1, ADAM_B2, ADAM_EPS, ADAM_WD, ADAM_STEP = 0.001, 0.9, 0.999, 1e-08, 0.01, 10

LANES = 128
VMEM_LIMIT = 60 * 1024 * 1024


def _params(sem=None):
    return pltpu.CompilerParams(dimension_semantics=sem, vmem_limit_bytes=VMEM_LIMIT)


def _pick(n, cap):
    if n <= cap:
        return n
    best = None
    for d in range(LANES, cap + 1, LANES):
        if n % d == 0:
            best = d
    assert best is not None, (n, cap)
    return best


def _div_tile(n, cap, mult):
    best = None
    for d in range(mult, min(n, cap) + 1, mult):
        if n % d == 0:
            best = d
    assert best is not None, (n, cap, mult)
    return best


def _row_tile(s, cap=512):
    return cap if s % cap == 0 and s >= 2 * cap else s // 2


def _sig(v):
    return 1.0 / (1.0 + jnp.exp(-v))


def _rstd(v):
    return lax.rsqrt(jnp.mean(v * v, axis=-1, keepdims=True) + EPS)


def _dot(a, b):
    return jnp.dot(a, b, preferred_element_type=F32)


def _dot_nt(a, b):
    return lax.dot_general(a, b, (((1,), (1,)), ((), ())), preferred_element_type=F32)


def _dot_tn(a, b):
    return lax.dot_general(a, b, (((0,), (0,)), ((), ())), preferred_element_type=F32)


def _rope(v, cos_t, sin_t):
    return v * cos_t + pltpu.roll(v, HEAD_PAD - QK_ROPE, 1) * sin_t


def _rope_bwd(dv, cos_t, sin_t):
    return dv * cos_t + pltpu.roll(dv * sin_t, QK_ROPE, 1)


def _const(shape):
    n = len(shape)
    return pl.BlockSpec(shape, lambda *_: (0,) * n, pipeline_mode=pl.Buffered(1))


def _rows(tm, w):
    return pl.BlockSpec((1, tm, w), lambda b, s: (b, s, 0))


def _per_b(r, w):
    return pl.BlockSpec((1, r, 1, w), lambda b, s: (b, 0, 0, 0))


def _all_gather(arrs, vmem, name):
    n = len(arrs)
    space = pltpu.VMEM if vmem else pl.ANY

    def body(*refs):
        x_refs, out_refs = refs[:n], refs[n:2 * n]
        send_sems, recv_sems, local_sems = refs[2 * n:]
        x_, y_, c_ = lax.axis_index("x"), lax.axis_index("y"), lax.axis_index("c")
        me, sibling = (x_, y_, c_), (x_, y_, 1 - c_)
        chips = [(1 - x_, y_), (x_, 1 - y_), (1 - x_, 1 - y_)]

        def copy(a, k, block, to, own=False):
            px, py, pc = block
            slot = out_refs[a].at[4 * px + 2 * py + pc]
            return pltpu.make_async_remote_copy(
                src_ref=x_refs[a] if own else slot, dst_ref=slot, send_sem=send_sems.at[k * n + a],
                recv_sem=recv_sems.at[k * n + a], device_id=to, device_id_type=MESH)

        mine = [pltpu.make_async_copy(x_refs[a], out_refs[a].at[4 * x_ + 2 * y_ + c_], local_sems.at[a]) for a in range(n)]
        sent = []
        for a in range(n):
            mine[a].start()
            sent.append(copy(a, 0, me, sibling, own=True))
            sent += [copy(a, 1 + j, me, (*chip, c_), own=True) for j, chip in enumerate(chips)]
        for cp in sent:
            cp.start()
        for j, chip in enumerate(chips):
            for a in range(n):
                copy(a, 1 + j, (*chip, c_), me).wait_recv()
                passed = copy(a, 4 + j, (*chip, c_), sibling)
                passed.start()
                sent.append(passed)
        for a in range(n):
            copy(a, 0, sibling, me).wait_recv()
            for j, chip in enumerate(chips):
                copy(a, 4 + j, (*chip, 1 - c_), me).wait_recv()
        for cp in sent:
            cp.wait_send()
        for cp in mine:
            cp.wait()

    return pl.pallas_call(
        body, name=name,
        out_shape=[jax.ShapeDtypeStruct((N_DEV,) + t.shape, t.dtype) for t in arrs],
        in_specs=[pl.BlockSpec(memory_space=space)] * n, out_specs=[pl.BlockSpec(memory_space=space)] * n,
        scratch_shapes=[pltpu.SemaphoreType.DMA((7 * n,)), pltpu.SemaphoreType.DMA((7 * n,)), pltpu.SemaphoreType.DMA((n,))],
        compiler_params=pltpu.CompilerParams(vmem_limit_bytes=VMEM_LIMIT),
    )(*arrs)


FLIPS = tuple((fx, fy, fc) for fx in (0, 1) for fy in (0, 1) for fc in (0, 1))[1:]


class _DirectExchange:
    def __init__(self, kind, in_refs, out_refs, send_sems, recv_sems, local_sems):
        n = len(in_refs)
        x_, y_, c_ = lax.axis_index("x"), lax.axis_index("y"), lax.axis_index("c")
        me = 4 * x_ + 2 * y_ + c_
        self.copies, self.local = [], []
        for r, (fx, fy, fc) in enumerate(FLIPS):
            px, py, pc = (1 - x_ if fx else x_), (1 - y_ if fy else y_), (1 - c_ if fc else c_)
            for a in range(n):
                src = in_refs[a] if kind == "gather" else in_refs[a].at[4 * px + 2 * py + pc]
                dst = out_refs[a].at[me] if kind == "gather" else out_refs[a].at[r]
                self.copies.append(pltpu.make_async_remote_copy(
                    src_ref=src, dst_ref=dst, send_sem=send_sems.at[r * n + a], recv_sem=recv_sems.at[r * n + a],
                    device_id=(px, py, pc), device_id_type=MESH))
        if kind == "gather":
            self.local = [pltpu.make_async_copy(in_refs[a], out_refs[a].at[me], local_sems.at[a]) for a in range(n)]

    def start(self):
        for cp in self.local + self.copies:
            cp.start()

    def wait(self):
        for cp in self.copies + self.local:
            cp.wait()

    @staticmethod
    def out_shapes(kind, arrs):
        if kind == "gather":
            return [jax.ShapeDtypeStruct((N_DEV,) + t.shape, t.dtype) for t in arrs]
        return [jax.ShapeDtypeStruct((N_DEV - 1,) + t.shape[1:], t.dtype) for t in arrs]

    @staticmethod
    def scratch(n):
        return [pltpu.SemaphoreType.DMA((7 * n,)), pltpu.SemaphoreType.DMA((7 * n,)), pltpu.SemaphoreType.DMA((n,))]


def _grad_scatter(gs):
    n = len(gs)

    def body(*refs):
        ex = _DirectExchange("scatter", refs[:n], refs[n:2 * n], *refs[2 * n:])
        ex.start()
        ex.wait()

    return pl.pallas_call(
        body, name="grad_scatter", out_shape=_DirectExchange.out_shapes("scatter", gs),
        in_specs=[pl.BlockSpec(memory_space=pl.ANY)] * n, out_specs=[pl.BlockSpec(memory_space=pl.ANY)] * n,
        scratch_shapes=_DirectExchange.scratch(n),
    )(*gs)


def _rows_2d(t, lead):
    return t.reshape((lead, -1, t.shape[-1]) if lead else (-1, t.shape[-1]))


def _grad_row_tile(rows, cols):
    return _div_tile(rows, max(16, (1 << 18) // cols), 16)


def _sum_devices(g):
    _, m, n = g.shape

    def body(g_ref, o_ref):
        s = g_ref[0]
        for j in range(1, N_DEV):
            s = s + g_ref[j]
        o_ref[...] = s

    return pl.pallas_call(body, name="small_grad_sum", out_shape=jax.ShapeDtypeStruct((m, n), F32),
                          compiler_params=pltpu.CompilerParams(vmem_limit_bytes=VMEM_LIMIT))(g)


def _ada_fwd(c_full, w_ada, b_cols):
    nl, d, n = w_ada.shape
    nb = c_full.shape[0]

    def body(c_ref, w_ref, b_ref, o_ref):
        cv = c_ref[...]
        act = cv * _sig(cv)
        o_ref[0] = jnp.dot(act, w_ref[0], preferred_element_type=F32, precision=lax.Precision.HIGHEST) + b_ref[0]

    return pl.pallas_call(
        body, name="ada_fwd", grid=(nl,), out_shape=jax.ShapeDtypeStruct((nl, nb, n), F32),
        in_specs=[pl.BlockSpec((nb, d), lambda l: (0, 0)), pl.BlockSpec((1, d, n), lambda l: (l, 0, 0)),
                  pl.BlockSpec((1, 1, n), lambda l: (l, 0, 0))],
        out_specs=pl.BlockSpec((1, nb, n), lambda l: (l, 0, 0)), compiler_params=_params(("arbitrary",)),
    )(c_full, w_ada, b_cols)


def _ada_bwd(c_full, dmod_cols, dmod_full):
    nl, nb, n = dmod_cols.shape
    d = c_full.shape[1]
    nfull = dmod_full.shape[2]

    def body(c_ref, dc_ref, df_ref, gw_ref, gb_ref):
        cv = c_ref[...]
        act = cv * _sig(cv)
        gw_ref[0] = lax.dot_general(act, dc_ref[0], (((0,), (0,)), ((), ())), preferred_element_type=F32,
                                    precision=lax.Precision.HIGHEST)
        gb_ref[0] = jnp.sum(df_ref[0], axis=0, keepdims=True)

    return pl.pallas_call(
        body, name="ada_bwd", grid=(nl,),
        out_shape=(jax.ShapeDtypeStruct((nl, d, n), F32), jax.ShapeDtypeStruct((nl, 1, nfull), F32)),
        in_specs=[pl.BlockSpec((nb, d), lambda l: (0, 0)), pl.BlockSpec((1, nb, n), lambda l: (l, 0, 0)),
                  pl.BlockSpec((1, nb, nfull), lambda l: (l, 0, 0))],
        out_specs=(pl.BlockSpec((1, d, n), lambda l: (l, 0, 0)), pl.BlockSpec((1, 1, nfull), lambda l: (l, 0, 0))),
        compiler_params=_params(("arbitrary",)),
    )(c_full, dmod_cols, dmod_full)


def _rope_tables(pos, freq_row, sign_row):
    b, s, _ = pos.shape
    tm = _row_tile(s)

    def body(p_ref, f_ref, g_ref, c_ref, s_ref):
        ang = p_ref[0] * f_ref[...]
        lane = lax.broadcasted_iota(jnp.int32, ang.shape, 1)
        c_ref[0] = jnp.where(lane < QK_NOPE, 1.0, jnp.where(lane < QK_NOPE + QK_ROPE, jnp.cos(ang), 0.0))
        s_ref[0] = g_ref[...] * jnp.sin(ang)

    return pl.pallas_call(
        body, name="rope_tables", grid=(b, s // tm),
        out_shape=(jax.ShapeDtypeStruct((b, s, HEAD_PAD), F32),) * 2,
        in_specs=[_rows(tm, 1), pl.BlockSpec((1, HEAD_PAD), lambda i, j: (0, 0)),
                  pl.BlockSpec((1, HEAD_PAD), lambda i, j: (0, 0))],
        out_specs=(_rows(tm, HEAD_PAD),) * 2, compiler_params=_params(("arbitrary", "arbitrary")),
    )(pos, freq_row, sign_row)


def _mix_pre_fwd(x, mod, cos_t, sin_t, g_mix, g_q, g_kv, w_lat, w_glu, w_gate, w_q, w_k, w_v):
    b, s, d = x.shape
    ql, kl = g_q.shape[1], g_kv.shape[1]
    wl, wg, wt = w_lat.shape[1], w_glu.shape[1], w_gate.shape[1]
    hw = N_HEADS * HEAD_PAD
    tm = _row_tile(s)

    def body(x_ref, mod_ref, cos_ref, sin_ref, gm_ref, gq_ref, gkv_ref, wlat_ref, wglu_ref, wgate_ref, wq_ref, wk_ref,
             wv_ref, lat_ref, glu_ref, gate_ref, q_ref, k_ref, v_ref):
        xf = x_ref[0]
        sh, sc = mod_ref[0, 0], mod_ref[0, 1]
        hb = ((xf * _rstd(xf) * gm_ref[...]) * (1.0 + sc) + sh).astype(BF16)
        glu_ref[0] = _dot(hb, wglu_ref[...]).astype(BF16)
        gate_ref[0] = _dot(hb, wgate_ref[...]).astype(BF16)
        lat = _dot(hb, wlat_ref[...]).astype(BF16)
        lat_ref[0] = lat
        latf = lat.astype(F32)
        q_lat, kv_lat, kr_sec = latf[:, :ql], latf[:, ql:ql + kl], latf[:, ql + kl:]
        qn = (q_lat * _rstd(q_lat) * gq_ref[...]).astype(BF16)
        kvn = (kv_lat * _rstd(kv_lat) * gkv_ref[...]).astype(BF16)
        cos_v, sin_v = cos_ref[0], sin_ref[0]
        lane = lax.broadcasted_iota(jnp.int32, kr_sec.shape, 1)
        kr = jnp.where(lane >= QK_NOPE, _rope(kr_sec, cos_v, sin_v), 0.0)
        q_all, k_all, v_all = _dot(qn, wq_ref[...]), _dot(kvn, wk_ref[...]), _dot(kvn, wv_ref[...])
        vlane = lax.broadcasted_iota(jnp.int32, v_all.shape, 1)
        v_ref[0] = jnp.where(vlane % HEAD_PAD == V_HEAD, 1.0, v_all).astype(BF16)
        for h in range(N_HEADS):
            cols = slice(h * HEAD_PAD, (h + 1) * HEAD_PAD)
            q_ref[0, :, cols] = (_rope(q_all[:, cols], cos_v, sin_v) * ATTN_SCALE).astype(BF16)
            k_ref[0, :, cols] = (k_all[:, cols] + kr).astype(BF16)

    hshape = jax.ShapeDtypeStruct((b, s, hw), BF16)
    return pl.pallas_call(
        body, name="mix_pre_fwd", grid=(b, s // tm),
        out_shape=(jax.ShapeDtypeStruct((b, s, wl), BF16), jax.ShapeDtypeStruct((b, s, wg), BF16),
                   jax.ShapeDtypeStruct((b, s, wt), BF16), hshape, hshape, hshape),
        in_specs=[_rows(tm, d), _per_b(6, d), _rows(tm, HEAD_PAD), _rows(tm, HEAD_PAD), _const((1, d)), _const((1, ql)),
                  _const((1, kl)), _const(w_lat.shape), _const(w_glu.shape), _const(w_gate.shape), _const(w_q.shape),
                  _const(w_k.shape), _const(w_v.shape)],
        out_specs=(_rows(tm, wl), _rows(tm, wg), _rows(tm, wt), _rows(tm, hw), _rows(tm, hw), _rows(tm, hw)),
        compiler_params=_params(("arbitrary", "arbitrary")),
    )(x, mod, cos_t, sin_t, g_mix, g_q, g_kv, w_lat, w_glu, w_gate, w_q, w_k, w_v)


def _causal_mask(tq, tk):
    return lax.broadcasted_iota(jnp.int32, (tq, tk), 0) >= lax.broadcasted_iota(jnp.int32, (tq, tk), 1)


def _riding_exchange(kind, n, refs, grid):
    if not n:
        return
    ex = _DirectExchange(kind, refs[:n], refs[n:2 * n], *refs[2 * n:])
    ids = [pl.program_id(a) for a in range(len(grid))]
    first, last = ids[0] == 0, ids[0] == grid[0] - 1
    for a in range(1, len(grid)):
        first, last = jnp.logical_and(first, ids[a] == 0), jnp.logical_and(last, ids[a] == grid[a] - 1)
    pl.when(first)(ex.start)
    return lambda: pl.when(last)(ex.wait)


def _flash_fwd(q, k, v, gather=()):
    b, s, hw = q.shape
    nh, hp = hw // HEAD_PAD, HEAD_PAD
    t = _row_tile(s)
    n = len(gather)
    grid = (b, nh, s // t)

    def body(q_ref, k_ref, v_ref, *rest):
        o_ref, lse_ref = rest[n], rest[n + 1]
        finish = _riding_exchange("gather", n, rest[:n] + rest[n + 2:], grid)
        i = pl.program_id(2)
        qv = q_ref[0]

        def step(j, carry, masked):
            m, acc = carry
            rows = pl.ds(pl.multiple_of(j * t, t), t)
            sc = _dot_nt(qv, k_ref[0, rows, :])
            if masked:
                sc = jnp.where(_causal_mask(t, t), sc, NEG_INF)
            m_new = jnp.maximum(m, jnp.max(sc, axis=-1, keepdims=True))
            p = jnp.exp((sc - m_new).astype(BF16))
            acc = jnp.exp(m - m_new) * acc + _dot(p, v_ref[0, rows, :])
            return m_new, acc

        init = (jnp.full((t, 1), NEG_INF, F32), jnp.zeros((t, hp), F32))
        carry = lax.fori_loop(0, i // 2, lambda jj, cr: step(2 * jj + 1, step(2 * jj, cr, False), False), init)
        carry = lax.cond(i % 2 == 1, lambda cr: step(i - 1, cr, False), lambda cr: cr, carry)
        m, acc = step(i, carry, True)
        lane = lax.broadcasted_iota(jnp.int32, acc.shape, 1)
        l = jnp.sum(jnp.where(lane == V_HEAD, acc, 0.0), axis=-1, keepdims=True)
        o_ref[0] = (acc / l).astype(BF16)
        lse_ref[0, 0] = m + jnp.log(l)
        if finish:
            finish()

    tile = pl.BlockSpec((1, t, hp), lambda bb, hh, ii: (bb, ii, hh))
    full = pl.BlockSpec((1, s, hp), lambda bb, hh, ii: (bb, 0, hh))
    hbm = [pl.BlockSpec(memory_space=pl.ANY)] * n
    outs = pl.pallas_call(
        body, name="flash_fwd_gather" if n else "flash_fwd", grid=grid,
        out_shape=[jax.ShapeDtypeStruct((b, s, hw), BF16), jax.ShapeDtypeStruct((b, nh, s, 1), F32)]
        + _DirectExchange.out_shapes("gather", gather),
        in_specs=[tile, full, full] + hbm,
        out_specs=[tile, pl.BlockSpec((1, 1, t, 1), lambda bb, hh, ii: (bb, hh, ii, 0))] + hbm,
        scratch_shapes=_DirectExchange.scratch(n) if n else [],
        compiler_params=_params(("arbitrary", "arbitrary", "arbitrary")),
    )(q, k, v, *gather)
    return outs[0], outs[1], outs[2:]


def _halo_prev(tm, w):
    r = tm // HALO
    return pl.BlockSpec((1, HALO, w), lambda b, s: (b, jnp.maximum(s * r - 1, 0), 0))


def _halo_next(tm, w, n_tiles):
    r = tm // HALO
    return pl.BlockSpec((1, HALO, w), lambda b, s: (b, jnp.minimum((s + 1) * r, n_tiles * r - 1), 0))


def _conv_rows(cc):
    return max(8, 16 * 8 * LANES // cc)


def _shifted_copies(buf, rows):
    buf[0, pl.ds(rows, 8), :] = jnp.zeros((8, buf.shape[2]), buf.dtype)
    for s in range(1, 8):
        buf[s, pl.ds(0, rows), :] = buf[0, pl.ds(s, rows), :]


def _window(buf, start, rows):
    return buf[start % 8, pl.ds(start - start % 8, rows), :]


def _glu(v, cc):
    a, g = v[:, :cc].astype(F32), v[:, cc:].astype(F32)
    return a * _sig(g)


def _conv_fwd(glu, w_dw, b_dw, g_cn, b_cn, w_pw2):
    b, s, w2 = glu.shape
    cc = w2 // 2
    d = w_pw2.shape[1]
    tm = _row_tile(s)

    rc = _conv_rows(cc)
    te = tm + HALO

    def body(cur_ref, prev_ref, w_ref, bdw_ref, g_ref, bcn_ref, wp_ref, u_ref, y_ref, ext, u_all):
        first = pl.program_id(1) == 0
        ext[0, pl.ds(0, HALO), :] = jnp.where(first, 0.0, _glu(prev_ref[0], cc))
        ext[0, pl.ds(HALO, tm), :] = _glu(cur_ref[0], cc)
        _shifted_copies(ext, te)
        for c0 in range(0, tm, rc):
            acc = jnp.zeros((rc, cc), F32) + bdw_ref[...]
            for kk in range(CONV_W):
                acc = acc + w_ref[pl.ds(kk, 1), :] * _window(ext, c0 + HALO - CONV_W + 1 + kk, rc)
            u_all[pl.ds(c0, rc), :] = acc
        ub = u_all[...].astype(BF16)
        u_ref[0] = ub
        uf = ub.astype(F32)
        mu = jnp.mean(uf, axis=-1, keepdims=True)
        uc = uf - mu
        ln = uc * lax.rsqrt(jnp.mean(uc * uc, axis=-1, keepdims=True) + EPS) * g_ref[...] + bcn_ref[...]
        y_ref[0] = _dot((ln * _sig(ln)).astype(BF16), wp_ref[...]).astype(BF16)

    return pl.pallas_call(
        body, name="conv_fwd", grid=(b, s // tm),
        out_shape=(jax.ShapeDtypeStruct((b, s, cc), BF16), jax.ShapeDtypeStruct((b, s, d), BF16)),
        in_specs=[_rows(tm, w2), _halo_prev(tm, w2), _const(w_dw.shape), _const((1, cc)), _const((1, cc)), _const((1, cc)),
                  _const(w_pw2.shape)],
        out_specs=(_rows(tm, cc), _rows(tm, d)),
        scratch_shapes=[pltpu.VMEM((8, te + 8, cc), F32), pltpu.VMEM((tm, cc), F32)],
        compiler_params=_params(("arbitrary", "arbitrary")),
    )(glu, glu, w_dw, b_dw, g_cn, b_cn, w_pw2)


def _mix_post_fwd(x, mod, o, y_conv, gate, w_o, w_out):
    b, s, d = x.shape
    hw = o.shape[2]
    tm = _row_tile(s)

    def body(x_ref, mod_ref, o_ref, yc_ref, gate_ref, wo_ref, wout_ref, x2_ref, ya_ref, o1_ref):
        yab = _dot(o_ref[0], wo_ref[...]).astype(BF16)
        ya_ref[0] = yab
        gv = gate_ref[0]
        y = _sig(gv[:, :d].astype(F32)) * yab.astype(F32) + _sig(gv[:, d:].astype(F32)) * yc_ref[0].astype(F32)
        o1 = _dot(y.astype(BF16), wout_ref[...])
        o1_ref[0] = o1.astype(BF16)
        x2_ref[0] = x_ref[0] + mod_ref[0, 2] * o1

    return pl.pallas_call(
        body, name="mix_post_fwd", grid=(b, s // tm),
        out_shape=(jax.ShapeDtypeStruct((b, s, d), F32), jax.ShapeDtypeStruct((b, s, d), BF16),
                   jax.ShapeDtypeStruct((b, s, d), BF16)),
        in_specs=[_rows(tm, d), _per_b(6, d), _rows(tm, hw), _rows(tm, d), _rows(tm, 2 * d), _const(w_o.shape),
                  _const(w_out.shape)],
        out_specs=(_rows(tm, d), _rows(tm, d), _rows(tm, d)),
        compiler_params=_params(("arbitrary", "arbitrary")),
    )(x, mod, o, y_conv, gate, w_o, w_out)


def _ffn_fwd(x2, mod, g_ffn, w_gu, w_down):
    b, s, d = x2.shape
    f = w_down.shape[0]
    fc = _pick(f, 512)
    tm = _row_tile(s)

    def body(x_ref, mod_ref, g_ref, wgu_ref, wdn_ref, x3_ref, gu_ref, o2_ref, h_ref):
        xf = x_ref[0]
        hb = ((xf * _rstd(xf) * g_ref[...]) * (1.0 + mod_ref[0, 4]) + mod_ref[0, 3]).astype(BF16)
        h_ref[0] = hb
        o2 = jnp.zeros((tm, d), F32)
        for c0 in range(0, f, fc):
            gb = _dot(hb, wgu_ref[:, c0:c0 + fc]).astype(BF16)
            ub = _dot(hb, wgu_ref[:, f + c0:f + c0 + fc]).astype(BF16)
            gu_ref[0, :, c0:c0 + fc] = gb
            gu_ref[0, :, f + c0:f + c0 + fc] = ub
            gf = gb.astype(F32)
            act = (gf * _sig(gf) * ub.astype(F32)).astype(BF16)
            o2 = o2 + _dot(act, wdn_ref[c0:c0 + fc, :])
        o2_ref[0] = o2.astype(BF16)
        x3_ref[0] = xf + mod_ref[0, 5] * o2

    return pl.pallas_call(
        body, name="ffn_fwd", grid=(b, s // tm),
        out_shape=(jax.ShapeDtypeStruct((b, s, d), F32), jax.ShapeDtypeStruct((b, s, 2 * f), BF16),
                   jax.ShapeDtypeStruct((b, s, d), BF16), jax.ShapeDtypeStruct((b, s, d), BF16)),
        in_specs=[_rows(tm, d), _per_b(6, d), _const((1, d)), _const(w_gu.shape), _const(w_down.shape)],
        out_specs=(_rows(tm, d), _rows(tm, 2 * f), _rows(tm, d), _rows(tm, d)),
        compiler_params=_params(("arbitrary", "arbitrary")),
    )(x2, mod, g_ffn, w_gu, w_down)


def _zero_at_first_tile(*refs):
    @pl.when(pl.program_id(1) == 0)
    def _():
        for ref in refs:
            ref[...] = jnp.zeros_like(ref)


def _accumulate(ref, idx, val):
    ref[idx] = ref[idx] + val


def _colsum(v):
    return jnp.sum(v, axis=0, keepdims=True)


def _loss_bwd(x, target, g_final):
    b, s, d = x.shape
    tm = _row_tile(s)

    def body(x_ref, t_ref, g_ref, dx_ref, loss_ref, dg_ref):
        _zero_at_first_tile(loss_ref, dg_ref)
        xf = x_ref[0]
        r = _rstd(xf)
        xh = xf * r
        diff = xh * g_ref[...] - t_ref[0]
        _accumulate(loss_ref, (0, 0), _colsum(diff * diff))
        dy = diff * (1.0 / d)
        _accumulate(dg_ref, (0, 0), _colsum(dy * xh))
        dyg = dy * g_ref[...]
        dx_ref[0] = r * (dyg - xh * jnp.mean(dyg * xh, axis=-1, keepdims=True))

    return pl.pallas_call(
        body, name="loss_bwd", grid=(b, s // tm),
        out_shape=(jax.ShapeDtypeStruct((b, s, d), F32), jax.ShapeDtypeStruct((b, 1, 1, d), F32),
                   jax.ShapeDtypeStruct((b, 1, 1, d), F32)),
        in_specs=[_rows(tm, d), _rows(tm, d), _const((1, d))],
        out_specs=(_rows(tm, d), _per_b(1, d), _per_b(1, d)),
        compiler_params=_params(("arbitrary", "arbitrary")),
    )(x, target, g_final)


def _ffn_bwd(dx3, x2, mod, g_ffn, gu, o2, w_gu_t, w_down_t):
    b, s, d = x2.shape
    f = w_down_t.shape[1]
    fc = _pick(f, 512)
    tm = _row_tile(s)

    def hidden_body(dx3_ref, mod_ref, gu_ref, o2_ref, wdnt_ref, do2_ref, act_ref, dgu_ref, dgt_ref):
        _zero_at_first_tile(dgt_ref)
        dx3 = dx3_ref[0]
        do2 = (dx3 * mod_ref[0, 5]).astype(BF16)
        do2_ref[0] = do2
        _accumulate(dgt_ref, (0, 0), _colsum(dx3 * o2_ref[0].astype(F32)))
        for c0 in range(0, f, fc):
            gf = gu_ref[0, :, c0:c0 + fc].astype(F32)
            uf = gu_ref[0, :, f + c0:f + c0 + fc].astype(F32)
            sg = _sig(gf)
            silu = gf * sg
            act_ref[0, :, c0:c0 + fc] = (silu * uf).astype(BF16)
            dact = _dot(do2, wdnt_ref[:, c0:c0 + fc])
            dgu_ref[0, :, c0:c0 + fc] = (dact * uf * (sg * (1.0 + gf * (1.0 - sg)))).astype(BF16)
            dgu_ref[0, :, f + c0:f + c0 + fc] = (dact * silu).astype(BF16)

    do2, act, dgu, dgt = pl.pallas_call(
        hidden_body, name="ffn_bwd_hidden", grid=(b, s // tm),
        out_shape=(jax.ShapeDtypeStruct((b, s, d), BF16), jax.ShapeDtypeStruct((b, s, f), BF16),
                   jax.ShapeDtypeStruct((b, s, 2 * f), BF16), jax.ShapeDtypeStruct((b, 1, 1, d), F32)),
        in_specs=[_rows(tm, d), _per_b(6, d), _rows(tm, 2 * f), _rows(tm, d), _const(w_down_t.shape)],
        out_specs=(_rows(tm, d), _rows(tm, f), _rows(tm, 2 * f), _per_b(1, d)),
        compiler_params=_params(("arbitrary", "arbitrary")),
    )(dx3, mod, gu, o2, w_down_t)

    def input_body(dx3_ref, x_ref, mod_ref, g_ref, dgu_ref, wgut_ref, dx2_ref, dmod_ref, dg_ref):
        _zero_at_first_tile(dmod_ref, dg_ref)
        sc = mod_ref[0, 4]
        dh = _dot(dgu_ref[0], wgut_ref[...])
        xf = x_ref[0]
        r = _rstd(xf)
        xh = xf * r
        n = xh * g_ref[...]
        _accumulate(dmod_ref, (0, 0), _colsum(dh))
        _accumulate(dmod_ref, (0, 1), _colsum(dh * n))
        dn = dh * (1.0 + sc)
        _accumulate(dg_ref, (0, 0), _colsum(dn * xh))
        dyg = dn * g_ref[...]
        dx2_ref[0] = dx3_ref[0] + r * (dyg - xh * jnp.mean(dyg * xh, axis=-1, keepdims=True))

    dx2, dmod_sc, dg = pl.pallas_call(
        input_body, name="ffn_bwd_input", grid=(b, s // tm),
        out_shape=(jax.ShapeDtypeStruct((b, s, d), F32), jax.ShapeDtypeStruct((b, 2, 1, d), F32),
                   jax.ShapeDtypeStruct((b, 1, 1, d), F32)),
        in_specs=[_rows(tm, d), _rows(tm, d), _per_b(6, d), _const((1, d)), _rows(tm, 2 * f), _const(w_gu_t.shape)],
        out_specs=(_rows(tm, d), _per_b(2, d), _per_b(1, d)),
        compiler_params=_params(("arbitrary", "arbitrary")),
    )(dx3, x2, mod, g_ffn, dgu, w_gu_t)
    return dx2, do2, act, dgu, jnp.concatenate([dmod_sc, dgt], axis=1), dg


def _mix_post_bwd(dx2, mod, o1, y_attn, y_conv, gate, w_out_t, w_o_t):
    b, s, d = dx2.shape
    hw = w_o_t.shape[1]
    tm = _row_tile(s)

    def body(dx_ref, mod_ref, o1_ref, ya_ref, yc_ref, gate_ref, woutt_ref, wot_ref, do1_ref, dya_ref, dyc_ref, dgate_ref,
             y_ref, do_ref, dgt_ref):
        _zero_at_first_tile(dgt_ref)
        dx = dx_ref[0]
        do1 = (dx * mod_ref[0, 2]).astype(BF16)
        do1_ref[0] = do1
        _accumulate(dgt_ref, (0, 0), _colsum(dx * o1_ref[0].astype(F32)))
        dy = _dot(do1, woutt_ref[...])
        gv = gate_ref[0]
        sa, sb = _sig(gv[:, :d].astype(F32)), _sig(gv[:, d:].astype(F32))
        ya, yc = ya_ref[0].astype(F32), yc_ref[0].astype(F32)
        y_ref[0] = (sa * ya + sb * yc).astype(BF16)
        dya = (dy * sa).astype(BF16)
        dya_ref[0] = dya
        dyc_ref[0] = (dy * sb).astype(BF16)
        dgate_ref[0, :, :d] = (dy * ya * sa * (1.0 - sa)).astype(BF16)
        dgate_ref[0, :, d:] = (dy * yc * sb * (1.0 - sb)).astype(BF16)
        do_ref[0] = _dot(dya, wot_ref[...]).astype(BF16)

    row = jax.ShapeDtypeStruct((b, s, d), BF16)
    return pl.pallas_call(
        body, name="mix_post_bwd", grid=(b, s // tm),
        out_shape=(row, row, row, jax.ShapeDtypeStruct((b, s, 2 * d), BF16), row,
                   jax.ShapeDtypeStruct((b, s, hw), BF16), jax.ShapeDtypeStruct((b, 1, 1, d), F32)),
        in_specs=[_rows(tm, d), _per_b(6, d), _rows(tm, d), _rows(tm, d), _rows(tm, d), _rows(tm, 2 * d),
                  _const(w_out_t.shape), _const(w_o_t.shape)],
        out_specs=(_rows(tm, d), _rows(tm, d), _rows(tm, d), _rows(tm, 2 * d), _rows(tm, d), _rows(tm, hw), _per_b(1, d)),
        compiler_params=_params(("arbitrary", "arbitrary")),
    )(dx2, mod, o1, y_attn, y_conv, gate, w_out_t, w_o_t)


def _conv_bwd(dyc, u, glu, w_dw, g_cn, b_cn, w_pw2_t):
    b, s, cc = u.shape
    d = dyc.shape[2]
    tm = _row_tile(s)
    nt = s // tm
    te = tm + HALO
    rc = _conv_rows(cc)

    def body(dyc_ref, dycn_ref, u_ref, un_ref, glu_ref, glup_ref, w_ref, g_ref, bcn_ref, wpt_ref, dglu_ref, s_ref, dw_ref,
             small_ref, du_ext, uin_ext, duin_all):
        _zero_at_first_tile(dw_ref, small_ref)
        st = pl.program_id(1)
        dy_all = jnp.concatenate([dyc_ref[0], dycn_ref[0]], axis=0)
        u_all = jnp.concatenate([u_ref[0], un_ref[0]], axis=0).astype(F32)
        ds = _dot(dy_all, wpt_ref[...])
        mu = jnp.mean(u_all, axis=-1, keepdims=True)
        uc = u_all - mu
        rstd = lax.rsqrt(jnp.mean(uc * uc, axis=-1, keepdims=True) + EPS)
        uh = uc * rstd
        ln = uh * g_ref[...] + bcn_ref[...]
        sg = _sig(ln)
        s_ref[0] = (ln * sg)[:tm].astype(BF16)
        dln = ds * (sg * (1.0 + ln * (1.0 - sg)))
        duh = dln * g_ref[...]
        du = rstd * (duh - jnp.mean(duh, axis=-1, keepdims=True) - uh * jnp.mean(duh * uh, axis=-1, keepdims=True))
        row = lax.broadcasted_iota(jnp.int32, (te, 1), 0)
        du = jnp.where(jnp.logical_and(st == nt - 1, row >= tm), 0.0, du)
        du_ext[0, pl.ds(0, te), :] = du
        _shifted_copies(du_ext, te)
        du_cur = du[:tm]
        _accumulate(small_ref, (0, 0), _colsum((dln * uh)[:tm]))
        _accumulate(small_ref, (0, 1), _colsum(dln[:tm]))
        _accumulate(small_ref, (0, 2), _colsum(du_cur))
        uin_ext[0, pl.ds(0, HALO), :] = jnp.where(st == 0, 0.0, _glu(glup_ref[0], cc))
        gv = glu_ref[0]
        ga, gb = gv[:, :cc].astype(F32), gv[:, cc:].astype(F32)
        sgb = _sig(gb)
        uin_ext[0, pl.ds(HALO, tm), :] = ga * sgb
        _shifted_copies(uin_ext, te)
        for c0 in range(0, tm, rc):
            du_c = du_ext[0, pl.ds(c0, rc), :]
            acc = jnp.zeros((rc, cc), F32)
            for kk in range(CONV_W):
                acc = acc + w_ref[pl.ds(kk, 1), :] * _window(du_ext, c0 + CONV_W - 1 - kk, rc)
                prod = du_c * _window(uin_ext, c0 + HALO - CONV_W + 1 + kk, rc)
                taps = pl.ds(8 * kk, 8)
                dw_ref[0, taps, :] = dw_ref[0, taps, :] + jnp.sum(prod.reshape(rc // 8, 8, cc), axis=0)
            duin_all[pl.ds(c0, rc), :] = acc
        duin = duin_all[...]
        dglu_ref[0, :, :cc] = (duin * sgb).astype(BF16)
        dglu_ref[0, :, cc:] = (duin * ga * sgb * (1.0 - sgb)).astype(BF16)

    return pl.pallas_call(
        body, name="conv_bwd", grid=(b, nt),
        out_shape=(jax.ShapeDtypeStruct((b, s, 2 * cc), BF16), jax.ShapeDtypeStruct((b, s, cc), BF16),
                   jax.ShapeDtypeStruct((b, 8 * HALO, cc), F32), jax.ShapeDtypeStruct((b, 3, 1, cc), F32)),
        in_specs=[_rows(tm, d), _halo_next(tm, d, nt), _rows(tm, cc), _halo_next(tm, cc, nt), _rows(tm, 2 * cc),
                  _halo_prev(tm, 2 * cc), _const(w_dw.shape), _const((1, cc)), _const((1, cc)), _const(w_pw2_t.shape)],
        out_specs=(_rows(tm, 2 * cc), _rows(tm, cc), pl.BlockSpec((1, 8 * HALO, cc), lambda i, j: (i, 0, 0)),
                   _per_b(3, cc)),
        scratch_shapes=[pltpu.VMEM((8, te + 8, cc), F32), pltpu.VMEM((8, te + 8, cc), F32), pltpu.VMEM((tm, cc), F32)],
        compiler_params=_params(("arbitrary", "arbitrary")),
    )(dyc, dyc, u, u, glu, glu, w_dw, g_cn, b_cn, w_pw2_t)


def _flash_bwd(q, k, v, o, lse, do, scatter=()):
    b, s, hw = q.shape
    nh, hp = hw // HEAD_PAD, HEAD_PAD
    t = _row_tile(s)
    nt = s // t
    n = len(scatter)
    grid = (b, nh, nt)

    def to_row(sel, cols):
        hi = cols.astype(BF16)
        lo = (cols - hi.astype(F32)).astype(BF16)
        return _dot_nt(sel, hi) + _dot_nt(sel, lo)

    def body(q_ref, k_ref, v_ref, o_ref, lse_ref, do_ref, *rest):
        dqt_ref, dk_ref, dv_ref = rest[n:n + 3]
        lse_row, delta_row = rest[2 * n + 3:2 * n + 5]
        finish = _riding_exchange("scatter", n, rest[:n] + rest[n + 3:2 * n + 3] + rest[2 * n + 5:], grid)
        j = pl.program_id(2)

        @pl.when(j == 0)
        def _():
            dqt_ref[...] = jnp.zeros_like(dqt_ref)
            first_lane = (lax.broadcasted_iota(jnp.int32, (8, hp), 1) == 0).astype(BF16)
            for i in range(nt):
                rows = pl.ds(i * t, t)
                prod = do_ref[0, rows, :].astype(F32) * o_ref[0, rows, :].astype(F32)
                delta_row[i] = to_row(jnp.ones((8, hp), BF16), prod)
                lse_row[i] = to_row(first_lane, jnp.broadcast_to(lse_ref[0, 0, rows, :], (t, hp)))

        kv, vv = k_ref[0], v_ref[0]
        kt = kv.T
        query_not_before_key = (lax.broadcasted_iota(jnp.int32, (t, t), 1) >= lax.broadcasted_iota(jnp.int32, (t, t), 0))

        def step(i, carry, masked):
            dk, dv = carry
            rows = pl.ds(pl.multiple_of(i * t, t), t)
            qv, dov = q_ref[0, rows, :], do_ref[0, rows, :]
            pt = jnp.exp((_dot_nt(kv, qv) - lse_row[i, 0:1, :]).astype(BF16))
            if masked:
                pt = jnp.where(query_not_before_key, pt, jnp.zeros((), BF16))
            dv = dv + _dot(pt, dov)
            dst = pt * (_dot_nt(vv, dov) - delta_row[i, 0:1, :]).astype(BF16)
            dk = dk + _dot(dst, qv)
            dqt_ref[0, i] = dqt_ref[0, i] + _dot(kt, dst)
            return dk, dv

        carry = step(j, (jnp.zeros((t, hp), F32), jnp.zeros((t, hp), F32)), True)
        pairs = (nt - 1 - j) // 2
        carry = lax.fori_loop(0, pairs, lambda ii, cr: step(j + 2 + 2 * ii, step(j + 1 + 2 * ii, cr, False), False), carry)
        dk, dv = lax.cond((nt - 1 - j) % 2 == 1, lambda cr: step(nt - 1, cr, False), lambda cr: cr, carry)
        dk_ref[0] = dk.astype(BF16)
        dv_ref[0] = dv.astype(BF16)
        if finish:
            finish()

    tile = pl.BlockSpec((1, t, hp), lambda bb, hh, jj: (bb, jj, hh))
    full = pl.BlockSpec((1, s, hp), lambda bb, hh, jj: (bb, 0, hh))
    hbm = [pl.BlockSpec(memory_space=pl.ANY)] * n
    outs = pl.pallas_call(
        body, name="flash_bwd_scatter" if n else "flash_bwd", grid=grid,
        out_shape=[jax.ShapeDtypeStruct((b, nt, hw, t), F32), jax.ShapeDtypeStruct((b, s, hw), BF16),
                   jax.ShapeDtypeStruct((b, s, hw), BF16)] + _DirectExchange.out_shapes("scatter", scatter),
        in_specs=[full, tile, tile, full, pl.BlockSpec((1, 1, s, 1), lambda bb, hh, jj: (bb, hh, 0, 0)), full] + hbm,
        out_specs=[pl.BlockSpec((1, nt, hp, t), lambda bb, hh, jj: (bb, 0, hh, 0)), tile, tile] + hbm,
        scratch_shapes=[pltpu.VMEM((nt, 8, t), F32), pltpu.VMEM((nt, 8, t), F32)]
        + (_DirectExchange.scratch(n) if n else []),
        compiler_params=_params(("arbitrary", "arbitrary", "arbitrary")),
    )(q, k, v, o, lse, do, *scatter)
    return outs[0], outs[1], outs[2], outs[3:]


def _mix_pre_bwd(x, dx2, mod, cos_t, sin_t, g_mix, g_q, g_kv, lat, dq, dk, dv, dglu, dgate, w_lat_t, w_glu_t, w_gate_t,
                 w_q_t, w_k_t, w_v_t):
    b, s, d = x.shape
    ql, kl = g_q.shape[1], g_kv.shape[1]
    wl = lat.shape[2]
    hw = N_HEADS * HEAD_PAD
    tm = _row_tile(s)

    def body(x_ref, dx2_ref, mod_ref, cos_ref, sin_ref, gm_ref, gq_ref, gkv_ref, lat_ref, dq_ref, dk_ref, dv_ref, dglu_ref,
             dgate_ref, wlt_ref, wgt_ref, wtt_ref, wqt_ref, wkt_ref, wvt_ref, dx_ref, dlat_ref, dqr_ref, qn_ref, kvn_ref,
             h_ref, dmod_ref, dgm_ref, dgq_ref, dgkv_ref):
        _zero_at_first_tile(dmod_ref, dgm_ref, dgq_ref, dgkv_ref)
        cos_v, sin_v = cos_ref[0], sin_ref[0]
        latf = lat_ref[0].astype(F32)
        q_lat, kv_lat = latf[:, :ql], latf[:, ql:ql + kl]
        rq, rk = _rstd(q_lat), _rstd(kv_lat)
        qh, kh = q_lat * rq, kv_lat * rk
        qn_ref[0] = (qh * gq_ref[...]).astype(BF16)
        kvn_ref[0] = (kh * gkv_ref[...]).astype(BF16)
        dk_sum = jnp.zeros((tm, HEAD_PAD), F32)
        for h in range(N_HEADS):
            cols = slice(h * HEAD_PAD, (h + 1) * HEAD_PAD)
            dq_head = dq_ref[0, 0, cols, :].T
            dqr_ref[0, :, cols] = _rope_bwd(dq_head * ATTN_SCALE, cos_v, sin_v).astype(BF16)
            dk_sum = dk_sum + dk_ref[0, :, cols].astype(F32)
        dqn = _dot(dqr_ref[0], wqt_ref[...])
        dkvn = _dot(dk_ref[0], wkt_ref[...]) + _dot(dv_ref[0], wvt_ref[...])
        lane = lax.broadcasted_iota(jnp.int32, dk_sum.shape, 1)
        dkr = _rope_bwd(jnp.where(lane >= QK_NOPE, dk_sum, 0.0), cos_v, sin_v)
        _accumulate(dgq_ref, (0, 0), _colsum(dqn * qh))
        _accumulate(dgkv_ref, (0, 0), _colsum(dkvn * kh))
        dqg, dkg = dqn * gq_ref[...], dkvn * gkv_ref[...]
        dlat_ref[0, :, :ql] = (rq * (dqg - qh * jnp.mean(dqg * qh, axis=-1, keepdims=True))).astype(BF16)
        dlat_ref[0, :, ql:ql + kl] = (rk * (dkg - kh * jnp.mean(dkg * kh, axis=-1, keepdims=True))).astype(BF16)
        dlat_ref[0, :, ql + kl:] = dkr.astype(BF16)
        dh = _dot(dlat_ref[0], wlt_ref[...]) + _dot(dglu_ref[0], wgt_ref[...]) + _dot(dgate_ref[0], wtt_ref[...])
        sh, sc = mod_ref[0, 0], mod_ref[0, 1]
        xf = x_ref[0]
        r = _rstd(xf)
        xh = xf * r
        n = xh * gm_ref[...]
        h_ref[0] = (n * (1.0 + sc) + sh).astype(BF16)
        _accumulate(dmod_ref, (0, 0), _colsum(dh))
        _accumulate(dmod_ref, (0, 1), _colsum(dh * n))
        dn = dh * (1.0 + sc)
        _accumulate(dgm_ref, (0, 0), _colsum(dn * xh))
        dyg = dn * gm_ref[...]
        dx_ref[0] = dx2_ref[0] + r * (dyg - xh * jnp.mean(dyg * xh, axis=-1, keepdims=True))

    return pl.pallas_call(
        body, name="mix_pre_bwd", grid=(b, s // tm),
        out_shape=(jax.ShapeDtypeStruct((b, s, d), F32), jax.ShapeDtypeStruct((b, s, wl), BF16),
                   jax.ShapeDtypeStruct((b, s, hw), BF16), jax.ShapeDtypeStruct((b, s, ql), BF16),
                   jax.ShapeDtypeStruct((b, s, kl), BF16), jax.ShapeDtypeStruct((b, s, d), BF16),
                   jax.ShapeDtypeStruct((b, 2, 1, d), F32), jax.ShapeDtypeStruct((b, 1, 1, d), F32),
                   jax.ShapeDtypeStruct((b, 1, 1, ql), F32), jax.ShapeDtypeStruct((b, 1, 1, kl), F32)),
        in_specs=[_rows(tm, d), _rows(tm, d), _per_b(6, d), _rows(tm, HEAD_PAD), _rows(tm, HEAD_PAD), _const((1, d)),
                  _const((1, ql)), _const((1, kl)), _rows(tm, wl),
                  pl.BlockSpec((1, 1, hw, tm), lambda b, s: (b, s, 0, 0)), _rows(tm, hw), _rows(tm, hw),
                  _rows(tm, dglu.shape[2]), _rows(tm, 2 * d), _const(w_lat_t.shape), _const(w_glu_t.shape),
                  _const(w_gate_t.shape), _const(w_q_t.shape), _const(w_k_t.shape), _const(w_v_t.shape)],
        out_specs=(_rows(tm, d), _rows(tm, wl), _rows(tm, hw), _rows(tm, ql), _rows(tm, kl), _rows(tm, d), _per_b(2, d),
                   _per_b(1, d), _per_b(1, ql), _per_b(1, kl)),
        compiler_params=_params(("arbitrary", "arbitrary")),
    )(x, dx2, mod, cos_t, sin_t, g_mix, g_q, g_kv, lat, dq, dk, dv, dglu, dgate, w_lat_t, w_glu_t, w_gate_t, w_q_t, w_k_t,
      w_v_t)


def _matmul_tn(a, bm, name):
    b, s, kd = a.shape
    nd = bm.shape[2]
    tk, tn = _pick(kd, 1536), _pick(nd, 1536)
    ts = _row_tile(s, 2048)

    def body(a_ref, b_ref, o_ref):
        part = _dot_tn(a_ref[0], b_ref[0])
        first = jnp.logical_and(pl.program_id(2) == 0, pl.program_id(3) == 0)

        @pl.when(first)
        def _():
            o_ref[...] = part

        @pl.when(jnp.logical_not(first))
        def _():
            o_ref[...] = o_ref[...] + part

    return pl.pallas_call(
        body, name=name, grid=(kd // tk, nd // tn, b, s // ts),
        out_shape=jax.ShapeDtypeStruct((kd, nd), F32),
        in_specs=[pl.BlockSpec((1, ts, tk), lambda i, j, bb, ss: (bb, ss, i)),
                  pl.BlockSpec((1, ts, tn), lambda i, j, bb, ss: (bb, ss, j))],
        out_specs=pl.BlockSpec((tk, tn), lambda i, j, bb, ss: (i, j)),
        compiler_params=_params(("arbitrary",) * 4),
    )(a, bm)


def _adamw_update(w, g, m, v):
    nm = ADAM_B1 * m + (1.0 - ADAM_B1) * g
    nv = ADAM_B2 * v + (1.0 - ADAM_B2) * (g * g)
    delta = -ADAM_LR * ((nm / (1.0 - ADAM_B1 ** ADAM_STEP)) / (jnp.sqrt(nv / (1.0 - ADAM_B2 ** ADAM_STEP)) + ADAM_EPS)
                        + ADAM_WD * w)
    return delta, nm, nv


def _adamw(w, g, m, v, name):
    shape = w.shape
    cols = shape[-1]
    rows = w.size // cols
    w2, g2, m2, v2 = (t.reshape(rows, cols) for t in (w, g, m, v))
    tr = rows
    if rows * cols * 4 > (1 << 20):
        tr = _div_tile(rows, max(8, (1 << 18) // cols), 8)

    def body(w_ref, g_ref, m_ref, v_ref, d_ref, nm_ref, nv_ref):
        d_ref[...], nm_ref[...], nv_ref[...] = _adamw_update(w_ref[...], g_ref[...], m_ref[...], v_ref[...])

    spec = pl.BlockSpec((tr, cols), lambda i: (i, 0))
    outs = pl.pallas_call(
        body, name=name, grid=(rows // tr,), out_shape=(jax.ShapeDtypeStruct((rows, cols), F32),) * 3,
        in_specs=[spec] * 4, out_specs=(spec,) * 3, compiler_params=_params(("arbitrary",)),
    )(w2, g2, m2, v2)
    return tuple(t.reshape(shape) for t in outs)


def _adamw_reduce(w, m, v, own, got, name):
    shape = w.shape
    cols = shape[-1]
    w2, m2, v2, o2 = (_rows_2d(t, 0) for t in (w, m, v, own))
    g3 = _rows_2d(got, N_DEV - 1)
    rows = w2.shape[0]
    tr = _grad_row_tile(rows, cols)

    def body(w_ref, m_ref, v_ref, o_ref, r_ref, g_ref, d_ref, nm_ref, nv_ref):
        g = o_ref[...].astype(F32)
        for r in range(N_DEV - 1):
            g = g + r_ref[r].astype(F32)
        g_ref[...] = g
        d_ref[...], nm_ref[...], nv_ref[...] = _adamw_update(w_ref[...], g, m_ref[...], v_ref[...])

    spec = pl.BlockSpec((tr, cols), lambda i: (i, 0))
    outs = pl.pallas_call(
        body, name=name, grid=(rows // tr,),
        in_specs=[spec, spec, spec, spec, pl.BlockSpec((N_DEV - 1, tr, cols), lambda i: (0, i, 0))],
        out_specs=(spec,) * 4,
        out_shape=(jax.ShapeDtypeStruct((rows, cols), F32),) * 4, compiler_params=_params(("arbitrary",)),
    )(w2, m2, v2, o2, g3)
    return tuple(t.reshape(shape) for t in outs)


GATHERED = (("w_in", 2), ("w_uq", 2), ("w_ukv", 2), ("w_o_attn", 2), ("w_pw2", 2), ("w_out", 1), ("w_gu", 2), ("w_down", 1))
PRE, POST = (0, 1, 2), (3, 4, 5, 6, 7)


def _from_chunks(chunks, axis):
    _, a, bb = chunks.shape
    if axis == 2:
        return jnp.transpose(chunks, (1, 0, 2)).reshape(a, N_DEV * bb)
    return chunks.reshape(N_DEV * a, bb)


def _to_chunks(full, axis):
    a, bb = full.shape
    if axis == 2:
        return jnp.transpose(full.reshape(a, N_DEV, bb // N_DEV), (1, 0, 2)).astype(BF16)
    return full.reshape(N_DEV, a // N_DEV, bb).astype(BF16)


def _swap_halves(t):
    half = QK_ROPE // 2
    return jnp.concatenate([t[..., half:], t[..., :half]], axis=-1)


def _t(w):
    return jnp.swapaxes(w, -1, -2)


def _pad_rows(t, mult=8):
    return jnp.pad(t, ((0, -t.shape[0] % mult), (0, 0)))


def _pad_last(t, width):
    return jnp.pad(t, ((0, 0),) * (t.ndim - 1) + ((0, width - t.shape[-1]),))


def kernel(x, c, positions, w_ada, b_ada, g_mix, w_in, g_q, w_uq, g_kv, w_ukv, w_o_attn, w_dw, b_dw, g_cn, b_cn, w_pw2, w_out, g_ffn, w_gu, w_down, g_final, loss_target, m_w_ada, m_b_ada, m_g_mix, m_w_in, m_g_q, m_w_uq, m_g_kv, m_w_ukv, m_w_o_attn, m_w_dw, m_b_dw, m_g_cn, m_b_cn, m_w_pw2, m_w_out, m_g_ffn, m_w_gu, m_w_down, m_g_final, v_w_ada, v_b_ada, v_g_mix, v_w_in, v_g_q, v_w_uq, v_g_kv, v_w_ukv, v_w_o_attn, v_w_dw, v_b_dw, v_g_cn, v_b_cn, v_w_pw2, v_w_out, v_g_ffn, v_w_gu, v_w_down, v_g_final):
    weights = dict(w_ada=w_ada, b_ada=b_ada, g_mix=g_mix, w_in=w_in, g_q=g_q, w_uq=w_uq, g_kv=g_kv, w_ukv=w_ukv,
                   w_o_attn=w_o_attn, w_dw=w_dw, b_dw=b_dw, g_cn=g_cn, b_cn=b_cn, w_pw2=w_pw2, w_out=w_out, g_ffn=g_ffn,
                   w_gu=w_gu, w_down=w_down, g_final=g_final)
    mom_m = dict(w_ada=m_w_ada, b_ada=m_b_ada, g_mix=m_g_mix, w_in=m_w_in, g_q=m_g_q, w_uq=m_w_uq, g_kv=m_g_kv,
                 w_ukv=m_w_ukv, w_o_attn=m_w_o_attn, w_dw=m_w_dw, b_dw=m_b_dw, g_cn=m_g_cn, b_cn=m_b_cn, w_pw2=m_w_pw2,
                 w_out=m_w_out, g_ffn=m_g_ffn, w_gu=m_w_gu, w_down=m_w_down, g_final=m_g_final)
    mom_v = dict(w_ada=v_w_ada, b_ada=v_b_ada, g_mix=v_g_mix, w_in=v_w_in, g_q=v_g_q, w_uq=v_w_uq, g_kv=v_g_kv,
                 w_ukv=v_w_ukv, w_o_attn=v_w_o_attn, w_dw=v_w_dw, b_dw=v_b_dw, g_cn=v_g_cn, b_cn=v_b_cn, w_pw2=v_w_pw2,
                 w_out=v_w_out, g_ffn=v_g_ffn, w_gu=v_w_gu, w_down=v_w_down, g_final=v_g_final)
    order = list(weights)

    nb, s, d = x.shape
    nl = w_in.shape[0]
    ql, kl, cc = g_q.shape[1], g_kv.shape[1], g_cn.shape[1]
    h = N_HEADS
    qk = QK_NOPE + QK_ROPE
    xi, yi, ci = lax.axis_index("x"), lax.axis_index("y"), lax.axis_index("c")
    me = 4 * xi + 2 * yi + ci

    shards = [weights[n].astype(BF16) for n, _ in GATHERED]
    gathered_pre = _all_gather([shards[a][0] for a in PRE], vmem=False, name="weight_all_gather")
    n_dw = w_dw.shape[2]
    dw_rows = jnp.pad(w_dw, ((0, 0), (0, HALO - CONV_W), (0, LANES - n_dw))).reshape(nl * HALO, LANES)
    c_all, dw_all = _all_gather([_pad_rows(c), dw_rows], vmem=True, name="cond_all_gather")
    c_full = c_all[:, :nb].reshape(N_DEV * nb, d)
    w_dw_full = jnp.transpose(dw_all.reshape(N_DEV, nl, HALO, LANES)[..., :n_dw], (1, 2, 0, 3)).reshape(nl, HALO, cc)

    o_kr, o_glu, o_gate = ql + kl, ql + kl + QK_ROPE, ql + kl + QK_ROPE + 2 * cc

    def layouts_pre(chunks):
        wi, w_uq_l, w_ukv_l = (_from_chunks(t, GATHERED[a][1]) for a, t in zip(PRE, chunks))
        w_kr = wi[:, o_kr:o_glu]
        wq = w_uq_l.reshape(ql, h, qk)
        wkv = w_ukv_l.reshape(kl, h, QK_NOPE + V_HEAD)
        return dict(
            lat=jnp.concatenate([wi[:, :o_kr], jnp.zeros((d, QK_NOPE), BF16), w_kr, _swap_halves(w_kr)], axis=1),
            glu=wi[:, o_glu:o_gate], gate=wi[:, o_gate:],
            q=jnp.concatenate([wq, _swap_halves(wq[..., QK_NOPE:])], axis=-1).reshape(ql, h * HEAD_PAD),
            k=_pad_last(wkv[..., :QK_NOPE], HEAD_PAD).reshape(kl, h * HEAD_PAD),
            v=_pad_last(wkv[..., QK_NOPE:], HEAD_PAD).reshape(kl, h * HEAD_PAD))

    def layouts_post(chunks):
        w_o_l, w_pw_l, w_out_l, w_gu_l, w_down_l = (_from_chunks(t, GATHERED[a][1]) for a, t in zip(POST, chunks))
        w_o = jnp.pad(w_o_l.reshape(h, V_HEAD, d), ((0, 0), (0, HEAD_PAD - V_HEAD), (0, 0)))
        return dict(o=w_o.reshape(h * HEAD_PAD, d), pw=w_pw_l, out=w_out_l, gu=w_gu_l, down=w_down_l)

    n_ada = w_ada.shape[2]
    b_cols = lax.dynamic_slice_in_dim(b_ada, me * n_ada, n_ada, axis=1).reshape(nl, 1, n_ada)
    mod_part = _ada_fwd(c_full, w_ada, b_cols)
    (mod_all,) = _all_gather([mod_part.reshape(nl * N_DEV * nb, n_ada)], vmem=True, name="mod_all_gather")
    mod_all = jnp.transpose(mod_all.reshape(N_DEV, nl, N_DEV * nb, n_ada), (1, 2, 0, 3)).reshape(nl, N_DEV * nb, 6 * d)
    mod = lax.dynamic_slice_in_dim(mod_all, me * nb, nb, axis=1).reshape(nl, nb, 6, 1, d)

    inv_freq = ROPE_THETA ** (-jnp.arange(0, QK_ROPE, 2, dtype=F32) / QK_ROPE)
    zeros = lambda n: jnp.zeros((n,), F32)
    freq_row = jnp.concatenate([zeros(QK_NOPE), inv_freq, inv_freq, zeros(HEAD_PAD - qk)]).reshape(1, -1)
    ones = jnp.ones((QK_ROPE // 2,), F32)
    sign_row = jnp.concatenate([zeros(QK_NOPE), -ones, ones, zeros(HEAD_PAD - qk)]).reshape(1, -1)
    cos_t, sin_t = _rope_tables(positions.astype(F32).reshape(nb, s, 1), freq_row, sign_row)

    row = lambda t, l: t[l].reshape(1, -1)

    saved, wts = [], []
    xc = x
    for l in range(nl):
        wl = layouts_pre(gathered_pre)
        lat, glu, gate, qh, kh, vh = _mix_pre_fwd(xc, mod[l], cos_t, sin_t, row(g_mix, l), row(g_q, l), row(g_kv, l),
                                                  wl["lat"], wl["glu"], wl["gate"], wl["q"], wl["k"], wl["v"])
        riders = [shards[a][l] for a in POST] + ([shards[a][l + 1] for a in PRE] if l + 1 < nl else [])
        o, lse, arrived = _flash_fwd(qh, kh, vh, gather=riders)
        wl.update(layouts_post(arrived[:len(POST)]))
        gathered_pre = arrived[len(POST):]
        wts.append(wl)
        u, y_conv = _conv_fwd(glu, w_dw_full[l], row(b_dw, l), row(g_cn, l), row(b_cn, l), wl["pw"])
        x2, y_attn, o1 = _mix_post_fwd(xc, mod[l], o, y_conv, gate, wl["o"], wl["out"])
        x3, gu, o2, h2 = _ffn_fwd(x2, mod[l], row(g_ffn, l), wl["gu"], wl["down"])
        saved.append(dict(x=xc, lat=lat, glu=glu, gate=gate, q=qh, k=kh, v=vh, o=o, lse=lse, u=u, y_conv=y_conv, x2=x2,
                          y_attn=y_attn, o1=o1, gu=gu, o2=o2, h2=h2))
        xc = x3

    dx, loss_part, dgf_part = _loss_bwd(xc, loss_target, g_final.reshape(1, d))
    loss = lax.psum(0.5 / d * jnp.sum(loss_part), AXES)

    small_rows, dw_taps, dmod = [None] * nl, [None] * nl, [None] * nl
    own, got = {}, {}
    chunk = lambda g, a: _to_chunks(g[GATHERED[a][0]], GATHERED[a][1])
    mine = lambda t: lax.dynamic_index_in_dim(t, me, axis=0, keepdims=False)
    pending = []
    for l in reversed(range(nl)):
        sv, wl, gw = saved[l], wts[l], {}
        dx2, do2, act, dgu, dmod2, dgffn = _ffn_bwd(dx, sv["x2"], mod[l], row(g_ffn, l), sv["gu"], sv["o2"],
                                                   _t(wl["gu"]), _t(wl["down"]))
        gw["w_gu"] = _matmul_tn(sv["h2"], dgu, "grad_w_gu")
        gw["w_down"] = _matmul_tn(act, do2, "grad_w_down")
        do1, dya, dyc, dgate, yv, do_h, dgt1 = _mix_post_bwd(dx2, mod[l], sv["o1"], sv["y_attn"], sv["y_conv"], sv["gate"],
                                                           _t(wl["out"]), _t(wl["o"]))
        gw["w_out"] = _matmul_tn(yv, do1, "grad_w_out")
        dwo = _matmul_tn(sv["o"], dya, "grad_w_o")
        gw["w_o_attn"] = dwo.reshape(h, HEAD_PAD, d)[:, :V_HEAD].reshape(h * V_HEAD, d)
        dglu, s_act, ddw, csmall = _conv_bwd(dyc, sv["u"], sv["glu"], w_dw_full[l], row(g_cn, l), row(b_cn, l), _t(wl["pw"]))
        gw["w_pw2"] = _matmul_tn(s_act, dyc, "grad_w_pw2")
        ready = [chunk(gw, a) for a in POST]
        dq, dk, dv, arrived = _flash_bwd(sv["q"], sv["k"], sv["v"], sv["o"], sv["lse"], do_h, scatter=ready + pending)
        for i, a in enumerate(POST):
            own[l, a], got[l, a] = mine(ready[i]), arrived[i]
        for i, a in enumerate(PRE if pending else ()):
            got[l + 1, a] = arrived[len(POST) + i]
        dx, dlat, dqr, qn, kvn, h1, dmod1, dgm, dgq, dgkv = _mix_pre_bwd(
            sv["x"], dx2, mod[l], cos_t, sin_t, row(g_mix, l), row(g_q, l), row(g_kv, l), sv["lat"], dq, dk, dv, dglu,
            dgate, _t(wl["lat"]), _t(wl["glu"]), _t(wl["gate"]), _t(wl["q"]), _t(wl["k"]), _t(wl["v"]))
        dwl = _matmul_tn(h1, dlat, "grad_w_lat")
        dwg = _matmul_tn(h1, dglu, "grad_w_glu")
        dwt = _matmul_tn(h1, dgate, "grad_w_gate")
        kr0 = o_kr + QK_NOPE
        dkr = dwl[:, kr0:kr0 + QK_ROPE] + _swap_halves(dwl[:, kr0 + QK_ROPE:])
        gw["w_in"] = jnp.concatenate([dwl[:, :o_kr], dkr, dwg, dwt], axis=1)
        dwq = _matmul_tn(qn, dqr, "grad_w_q").reshape(ql, h, HEAD_PAD)
        dwq = jnp.concatenate([dwq[..., :QK_NOPE], dwq[..., QK_NOPE:qk] + _swap_halves(dwq[..., qk:])], axis=-1)
        gw["w_uq"] = dwq.reshape(ql, h * qk)
        dwk = _matmul_tn(kvn, dk, "grad_w_k").reshape(kl, h, HEAD_PAD)
        dwv = _matmul_tn(kvn, dv, "grad_w_v").reshape(kl, h, HEAD_PAD)
        gw["w_ukv"] = jnp.concatenate([dwk[..., :QK_NOPE], dwv[..., :V_HEAD]], axis=-1).reshape(kl, h * (QK_NOPE + V_HEAD))
        pending = [chunk(gw, a) for a in PRE]
        for i, a in enumerate(PRE):
            own[l, a] = mine(pending[i])
        dmod[l] = jnp.concatenate([dmod1[:, :, 0], dgt1[:, :, 0], dmod2[:, :, 0]], axis=1).reshape(nb, 6 * d)
        bsum = lambda t: jnp.sum(t, axis=0).reshape(1, -1)
        cs = jnp.sum(csmall, axis=0)[:, 0]
        small_rows[l] = jnp.concatenate([bsum(dgm), bsum(dgq), bsum(dgkv), cs[2:3], cs[0:1], cs[1:2], bsum(dgffn)], axis=1)
        dw_taps[l] = jnp.sum(ddw.reshape(nb, HALO, 8, cc), axis=(0, 2))
    grad_x = dx

    dmod_rows = _pad_rows(jnp.stack(dmod).reshape(nl * nb, 6 * d))
    (dmod_all,) = _all_gather([dmod_rows], vmem=True, name="dmod_all_gather")
    dmod_full = jnp.transpose(dmod_all[:, :nl * nb].reshape(N_DEV, nl, nb, 6 * d), (1, 0, 2, 3)).reshape(nl, N_DEV * nb, 6 * d)
    dmod_cols = lax.dynamic_slice_in_dim(dmod_full, me * n_ada, n_ada, axis=2)
    grad_w_ada, grad_b_ada = _ada_bwd(c_full, dmod_cols, dmod_full)
    grads = {"w_ada": grad_w_ada, "b_ada": grad_b_ada.reshape(nl, 6 * d)}

    widths = (d, ql, kl, cc, cc, cc, d)
    wsum = sum(widths)
    final_row = _pad_last(jnp.sum(dgf_part, axis=0).reshape(1, d), wsum)
    small2d = _pad_rows(jnp.concatenate(small_rows + [final_row], axis=0))
    taps2d = jnp.concatenate(dw_taps, axis=0)
    small_all, taps_all = _all_gather([small2d, taps2d], vmem=True, name="small_grad_all_gather")
    small_sum, taps_sum = _sum_devices(small_all), _sum_devices(taps_all)
    off = 0
    for n, wdt in zip(("g_mix", "g_q", "g_kv", "b_dw", "g_cn", "b_cn", "g_ffn"), widths):
        grads[n] = small_sum[:nl, off:off + wdt]
        off += wdt
    grads["g_final"] = small_sum[nl, :d]
    taps = taps_sum.reshape(nl, HALO, cc)[:, :CONV_W]
    grads["w_dw"] = lax.dynamic_slice_in_dim(taps, me * n_dw, n_dw, axis=2)

    delta, new_m, new_v = {}, {}, {}
    for n in order:
        if n in grads:
            delta[n], new_m[n], new_v[n] = _adamw(weights[n], grads[n], mom_m[n], mom_v[n], "adamw_" + n)

    for a, t in zip(PRE, _grad_scatter(pending)):
        got[0, a] = t
    for a, (n, _) in enumerate(GATHERED):
        own_n = jnp.stack([own[l, a] for l in range(nl)])
        got_n = jnp.stack([got[l, a] for l in range(nl)], axis=1)
        grads[n], delta[n], new_m[n], new_v[n] = _adamw_reduce(weights[n], mom_m[n], mom_v[n], own_n, got_n, "adamw_" + n)

    return (loss, grad_x, *[grads[n] for n in order], *[delta[n] for n in order], *[new_m[n] for n in order],
            *[new_v[n] for n in order])
```

```python
import jax
import jax.numpy as jnp
from jax import lax
from jax.experimental import pallas as pl
from jax.experimental.pallas import tpu as pltpu

F32, BF16 = jnp.float32, jnp.bfloat16
MESH = pl.DeviceIdType.MESH
AXES = ("x", "y", "c")
N_DEV = 8

N_HEADS = 8
QK_NOPE = 64
QK_ROPE = 32
V_HEAD = 64
HEAD_PAD = 128
CONV_W = 31
HALO = 32
EPS = 1e-6
ROPE_THETA = 10000.0
NEG_INF = -1e30
ATTN_SCALE = (QK_NOPE + QK_ROPE) ** -0.5

ADAM_LR, ADAM_B1, ADAM_B2, ADAM_EPS, ADAM_WD, ADAM_STEP = 0.001, 0.9, 0.999, 1e-08, 0.01, 10

LANES = 128
VMEM_LIMIT = 60 * 1024 * 1024


def _params(sem=None):
    return pltpu.CompilerParams(dimension_semantics=sem, vmem_limit_bytes=VMEM_LIMIT)


def _pick(n, cap):
    if n <= cap:
        return n
    best = None
    for d in range(LANES, cap + 1, LANES):
        if n % d == 0:
            best = d
    assert best is not None, (n, cap)
    return best


def _div_tile(n, cap, mult):
    best = None
    for d in range(mult, min(n, cap) + 1, mult):
        if n % d == 0:
            best = d
    assert best is not None, (n, cap, mult)
    return best


def _row_tile(s, cap=512):
    return cap if s % cap == 0 and s >= 2 * cap else s // 2


def _sig(v):
    return 1.0 / (1.0 + jnp.exp(-v))


def _rstd(v):
    return lax.rsqrt(jnp.mean(v * v, axis=-1, keepdims=True) + EPS)


def _dot(a, b):
    return jnp.dot(a, b, preferred_element_type=F32)


def _dot_nt(a, b):
    return lax.dot_general(a, b, (((1,), (1,)), ((), ())), preferred_element_type=F32)


def _dot_tn(a, b):
    return lax.dot_general(a, b, (((0,), (0,)), ((), ())), preferred_element_type=F32)


def _rope(v, cos_t, sin_t):
    return v * cos_t + pltpu.roll(v, HEAD_PAD - QK_ROPE, 1) * sin_t


def _rope_bwd(dv, cos_t, sin_t):
    return dv * cos_t + pltpu.roll(dv * sin_t, QK_ROPE, 1)


def _const(shape):
    n = len(shape)
    return pl.BlockSpec(shape, lambda *_: (0,) * n, pipeline_mode=pl.Buffered(1))


def _rows(tm, w):
    return pl.BlockSpec((1, tm, w), lambda b, s: (b, s, 0))


def _per_b(r, w):
    return pl.BlockSpec((1, r, 1, w), lambda b, s: (b, 0, 0, 0))


def _all_gather(arrs, vmem, name):
    n = len(arrs)
    space = pltpu.VMEM if vmem else pl.ANY

    def body(*refs):
        x_refs, out_refs = refs[:n], refs[n:2 * n]
        send_sems, recv_sems, local_sems = refs[2 * n:]
        x_, y_, c_ = lax.axis_index("x"), lax.axis_index("y"), lax.axis_index("c")
        me, sibling = (x_, y_, c_), (x_, y_, 1 - c_)
        chips = [(1 - x_, y_), (x_, 1 - y_), (1 - x_, 1 - y_)]

        def copy(a, k, block, to, own=False):
            px, py, pc = block
            slot = out_refs[a].at[4 * px + 2 * py + pc]
            return pltpu.make_async_remote_copy(
                src_ref=x_refs[a] if own else slot, dst_ref=slot, send_sem=send_sems.at[k * n + a],
                recv_sem=recv_sems.at[k * n + a], device_id=to, device_id_type=MESH)

        mine = [pltpu.make_async_copy(x_refs[a], out_refs[a].at[4 * x_ + 2 * y_ + c_], local_sems.at[a]) for a in range(n)]
        sent = []
        for a in range(n):
            mine[a].start()
            sent.append(copy(a, 0, me, sibling, own=True))
            sent += [copy(a, 1 + j, me, (*chip, c_), own=True) for j, chip in enumerate(chips)]
        for cp in sent:
            cp.start()
        for j, chip in enumerate(chips):
            for a in range(n):
                copy(a, 1 + j, (*chip, c_), me).wait_recv()
                passed = copy(a, 4 + j, (*chip, c_), sibling)
                passed.start()
                sent.append(passed)
        for a in range(n):
            copy(a, 0, sibling, me).wait_recv()
            for j, chip in enumerate(chips):
                copy(a, 4 + j, (*chip, 1 - c_), me).wait_recv()
        for cp in sent:
            cp.wait_send()
        for cp in mine:
            cp.wait()

    return pl.pallas_call(
        body, name=name,
        out_shape=[jax.ShapeDtypeStruct((N_DEV,) + t.shape, t.dtype) for t in arrs],
        in_specs=[pl.BlockSpec(memory_space=space)] * n, out_specs=[pl.BlockSpec(memory_space=space)] * n,
        scratch_shapes=[pltpu.SemaphoreType.DMA((7 * n,)), pltpu.SemaphoreType.DMA((7 * n,)), pltpu.SemaphoreType.DMA((n,))],
        compiler_params=pltpu.CompilerParams(vmem_limit_bytes=VMEM_LIMIT),
    )(*arrs)


FLIPS = tuple((fx, fy, fc) for fx in (0, 1) for fy in (0, 1) for fc in (0, 1))[1:]


class _DirectExchange:
    def __init__(self, kind, in_refs, out_refs, send_sems, recv_sems, local_sems):
        n = len(in_refs)
        x_, y_, c_ = lax.axis_index("x"), lax.axis_index("y"), lax.axis_index("c")
        me = 4 * x_ + 2 * y_ + c_
        self.copies, self.local = [], []
        for r, (fx, fy, fc) in enumerate(FLIPS):
            px, py, pc = (1 - x_ if fx else x_), (1 - y_ if fy else y_), (1 - c_ if fc else c_)
            for a in range(n):
                src = in_refs[a] if kind == "gather" else in_refs[a].at[4 * px + 2 * py + pc]
                dst = out_refs[a].at[me] if kind == "gather" else out_refs[a].at[r]
                self.copies.append(pltpu.make_async_remote_copy(
                    src_ref=src, dst_ref=dst, send_sem=send_sems.at[r * n + a], recv_sem=recv_sems.at[r * n + a],
                    device_id=(px, py, pc), device_id_type=MESH))
        if kind == "gather":
            self.local = [pltpu.make_async_copy(in_refs[a], out_refs[a].at[me], local_sems.at[a]) for a in range(n)]

    def start(self):
        for cp in self.local + self.copies:
            cp.start()

    def wait(self):
        for cp in self.copies + self.local:
            cp.wait()

    @staticmethod
    def out_shapes(kind, arrs):
        if kind == "gather":
            return [jax.ShapeDtypeStruct((N_DEV,) + t.shape, t.dtype) for t in arrs]
        return [jax.ShapeDtypeStruct((N_DEV - 1,) + t.shape[1:], t.dtype) for t in arrs]

    @staticmethod
    def scratch(n):
        return [pltpu.SemaphoreType.DMA((7 * n,)), pltpu.SemaphoreType.DMA((7 * n,)), pltpu.SemaphoreType.DMA((n,))]


def _grad_scatter(gs):
    n = len(gs)

    def body(*refs):
        ex = _DirectExchange("scatter", refs[:n], refs[n:2 * n], *refs[2 * n:])
        ex.start()
        ex.wait()

    return pl.pallas_call(
        body, name="grad_scatter", out_shape=_DirectExchange.out_shapes("scatter", gs),
        in_specs=[pl.BlockSpec(memory_space=pl.ANY)] * n, out_specs=[pl.BlockSpec(memory_space=pl.ANY)] * n,
        scratch_shapes=_DirectExchange.scratch(n),
    )(*gs)


def _rows_2d(t, lead):
    return t.reshape((lead, -1, t.shape[-1]) if lead else (-1, t.shape[-1]))


def _grad_row_tile(rows, cols):
    return _div_tile(rows, max(16, (1 << 18) // cols), 16)


def _sum_devices(g):
    _, m, n = g.shape

    def body(g_ref, o_ref):
        s = g_ref[0]
        for j in range(1, N_DEV):
            s = s + g_ref[j]
        o_ref[...] = s

    return pl.pallas_call(body, name="small_grad_sum", out_shape=jax.ShapeDtypeStruct((m, n), F32),
                          compiler_params=pltpu.CompilerParams(vmem_limit_bytes=VMEM_LIMIT))(g)


def _ada_fwd(c_full, w_ada, b_cols):
    nl, d, n = w_ada.shape
    nb = c_full.shape[0]

    def body(c_ref, w_ref, b_ref, o_ref):
        cv = c_ref[...]
        act = cv * _sig(cv)
        o_ref[0] = jnp.dot(act, w_ref[0], preferred_element_type=F32, precision=lax.Precision.HIGHEST) + b_ref[0]

    return pl.pallas_call(
        body, name="ada_fwd", grid=(nl,), out_shape=jax.ShapeDtypeStruct((nl, nb, n), F32),
        in_specs=[pl.BlockSpec((nb, d), lambda l: (0, 0)), pl.BlockSpec((1, d, n), lambda l: (l, 0, 0)),
                  pl.BlockSpec((1, 1, n), lambda l: (l, 0, 0))],
        out_specs=pl.BlockSpec((1, nb, n), lambda l: (l, 0, 0)), compiler_params=_params(("arbitrary",)),
    )(c_full, w_ada, b_cols)


def _ada_bwd(c_full, dmod_cols, dmod_full):
    nl, nb, n = dmod_cols.shape
    d = c_full.shape[1]
    nfull = dmod_full.shape[2]

    def body(c_ref, dc_ref, df_ref, gw_ref, gb_ref):
        cv = c_ref[...]
        act = cv * _sig(cv)
        gw_ref[0] = lax.dot_general(act, dc_ref[0], (((0,), (0,)), ((), ())), preferred_element_type=F32,
                                    precision=lax.Precision.HIGHEST)
        gb_ref[0] = jnp.sum(df_ref[0], axis=0, keepdims=True)

    return pl.pallas_call(
        body, name="ada_bwd", grid=(nl,),
        out_shape=(jax.ShapeDtypeStruct((nl, d, n), F32), jax.ShapeDtypeStruct((nl, 1, nfull), F32)),
        in_specs=[pl.BlockSpec((nb, d), lambda l: (0, 0)), pl.BlockSpec((1, nb, n), lambda l: (l, 0, 0)),
                  pl.BlockSpec((1, nb, nfull), lambda l: (l, 0, 0))],
        out_specs=(pl.BlockSpec((1, d, n), lambda l: (l, 0, 0)), pl.BlockSpec((1, 1, nfull), lambda l: (l, 0, 0))),
        compiler_params=_params(("arbitrary",)),
    )(c_full, dmod_cols, dmod_full)


def _rope_tables(pos, freq_row, sign_row):
    b, s, _ = pos.shape
    tm = _row_tile(s)

    def body(p_ref, f_ref, g_ref, c_ref, s_ref):
        ang = p_ref[0] * f_ref[...]
        lane = lax.broadcasted_iota(jnp.int32, ang.shape, 1)
        c_ref[0] = jnp.where(lane < QK_NOPE, 1.0, jnp.where(lane < QK_NOPE + QK_ROPE, jnp.cos(ang), 0.0))
        s_ref[0] = g_ref[...] * jnp.sin(ang)

    return pl.pallas_call(
        body, name="rope_tables", grid=(b, s // tm),
        out_shape=(jax.ShapeDtypeStruct((b, s, HEAD_PAD), F32),) * 2,
        in_specs=[_rows(tm, 1), pl.BlockSpec((1, HEAD_PAD), lambda i, j: (0, 0)),
                  pl.BlockSpec((1, HEAD_PAD), lambda i, j: (0, 0))],
        out_specs=(_rows(tm, HEAD_PAD),) * 2, compiler_params=_params(("arbitrary", "arbitrary")),
    )(pos, freq_row, sign_row)


def _mix_pre_fwd(x, mod, cos_t, sin_t, g_mix, g_q, g_kv, w_lat, w_glu, w_gate, w_q, w_k, w_v):
    b, s, d = x.shape
    ql, kl = g_q.shape[1], g_kv.shape[1]
    wl, wg, wt = w_lat.shape[1], w_glu.shape[1], w_gate.shape[1]
    hw = N_HEADS * HEAD_PAD
    tm = _row_tile(s)

    def body(x_ref, mod_ref, cos_ref, sin_ref, gm_ref, gq_ref, gkv_ref, wlat_ref, wglu_ref, wgate_ref, wq_ref, wk_ref,
             wv_ref, lat_ref, glu_ref, gate_ref, q_ref, k_ref, v_ref):
        xf = x_ref[0]
        sh, sc = mod_ref[0, 0], mod_ref[0, 1]
        hb = ((xf * _rstd(xf) * gm_ref[...]) * (1.0 + sc) + sh).astype(BF16)
        glu_ref[0] = _dot(hb, wglu_ref[...]).astype(BF16)
        gate_ref[0] = _dot(hb, wgate_ref[...]).astype(BF16)
        lat = _dot(hb, wlat_ref[...]).astype(BF16)
        lat_ref[0] = lat
        latf = lat.astype(F32)
        q_lat, kv_lat, kr_sec = latf[:, :ql], latf[:, ql:ql + kl], latf[:, ql + kl:]
        qn = (q_lat * _rstd(q_lat) * gq_ref[...]).astype(BF16)
        kvn = (kv_lat * _rstd(kv_lat) * gkv_ref[...]).astype(BF16)
        cos_v, sin_v = cos_ref[0], sin_ref[0]
        lane = lax.broadcasted_iota(jnp.int32, kr_sec.shape, 1)
        kr = jnp.where(lane >= QK_NOPE, _rope(kr_sec, cos_v, sin_v), 0.0)
        q_all, k_all, v_all = _dot(qn, wq_ref[...]), _dot(kvn, wk_ref[...]), _dot(kvn, wv_ref[...])
        vlane = lax.broadcasted_iota(jnp.int32, v_all.shape, 1)
        v_ref[0] = jnp.where(vlane % HEAD_PAD == V_HEAD, 1.0, v_all).astype(BF16)
        for h in range(N_HEADS):
            cols = slice(h * HEAD_PAD, (h + 1) * HEAD_PAD)
            q_ref[0, :, cols] = (_rope(q_all[:, cols], cos_v, sin_v) * ATTN_SCALE).astype(BF16)
            k_ref[0, :, cols] = (k_all[:, cols] + kr).astype(BF16)

    hshape = jax.ShapeDtypeStruct((b, s, hw), BF16)
    return pl.pallas_call(
        body, name="mix_pre_fwd", grid=(b, s // tm),
        out_shape=(jax.ShapeDtypeStruct((b, s, wl), BF16), jax.ShapeDtypeStruct((b, s, wg), BF16),
                   jax.ShapeDtypeStruct((b, s, wt), BF16), hshape, hshape, hshape),
        in_specs=[_rows(tm, d), _per_b(6, d), _rows(tm, HEAD_PAD), _rows(tm, HEAD_PAD), _const((1, d)), _const((1, ql)),
                  _const((1, kl)), _const(w_lat.shape), _const(w_glu.shape), _const(w_gate.shape), _const(w_q.shape),
                  _const(w_k.shape), _const(w_v.shape)],
        out_specs=(_rows(tm, wl), _rows(tm, wg), _rows(tm, wt), _rows(tm, hw), _rows(tm, hw), _rows(tm, hw)),
        compiler_params=_params(("arbitrary", "arbitrary")),
    )(x, mod, cos_t, sin_t, g_mix, g_q, g_kv, w_lat, w_glu, w_gate, w_q, w_k, w_v)


def _causal_mask(tq, tk):
    return lax.broadcasted_iota(jnp.int32, (tq, tk), 0) >= lax.broadcasted_iota(jnp.int32, (tq, tk), 1)


def _riding_exchange(kind, n, refs, grid):
    if not n:
        return
    ex = _DirectExchange(kind, refs[:n], refs[n:2 * n], *refs[2 * n:])
    ids = [pl.program_id(a) for a in range(len(grid))]
    first, last = ids[0] == 0, ids[0] == grid[0] - 1
    for a in range(1, len(grid)):
        first, last = jnp.logical_and(first, ids[a] == 0), jnp.logical_and(last, ids[a] == grid[a] - 1)
    pl.when(first)(ex.start)
    return lambda: pl.when(last)(ex.wait)


def _flash_fwd(q, k, v, gather=()):
    b, s, hw = q.shape
    nh, hp = hw // HEAD_PAD, HEAD_PAD
    t = _row_tile(s)
    n = len(gather)
    grid = (b, nh, s // t)

    def body(q_ref, k_ref, v_ref, *rest):
        o_ref, lse_ref = rest[n], rest[n + 1]
        finish = _riding_exchange("gather", n, rest[:n] + rest[n + 2:], grid)
        i = pl.program_id(2)
        qv = q_ref[0]

        def step(j, carry, masked):
            m, acc = carry
            rows = pl.ds(pl.multiple_of(j * t, t), t)
            sc = _dot_nt(qv, k_ref[0, rows, :])
            if masked:
                sc = jnp.where(_causal_mask(t, t), sc, NEG_INF)
            m_new = jnp.maximum(m, jnp.max(sc, axis=-1, keepdims=True))
            p = jnp.exp((sc - m_new).astype(BF16))
            acc = jnp.exp(m - m_new) * acc + _dot(p, v_ref[0, rows, :])
            return m_new, acc

        init = (jnp.full((t, 1), NEG_INF, F32), jnp.zeros((t, hp), F32))
        carry = lax.fori_loop(0, i // 2, lambda jj, cr: step(2 * jj + 1, step(2 * jj, cr, False), False), init)
        carry = lax.cond(i % 2 == 1, lambda cr: step(i - 1, cr, False), lambda cr: cr, carry)
        m, acc = step(i, carry, True)
        lane = lax.broadcasted_iota(jnp.int32, acc.shape, 1)
        l = jnp.sum(jnp.where(lane == V_HEAD, acc, 0.0), axis=-1, keepdims=True)
        o_ref[0] = (acc / l).astype(BF16)
        lse_ref[0, 0] = m + jnp.log(l)
        if finish:
            finish()

    tile = pl.BlockSpec((1, t, hp), lambda bb, hh, ii: (bb, ii, hh))
    full = pl.BlockSpec((1, s, hp), lambda bb, hh, ii: (bb, 0, hh))
    hbm = [pl.BlockSpec(memory_space=pl.ANY)] * n
    outs = pl.pallas_call(
        body, name="flash_fwd_gather" if n else "flash_fwd", grid=grid,
        out_shape=[jax.ShapeDtypeStruct((b, s, hw), BF16), jax.ShapeDtypeStruct((b, nh, s, 1), F32)]
        + _DirectExchange.out_shapes("gather", gather),
        in_specs=[tile, full, full] + hbm,
        out_specs=[tile, pl.BlockSpec((1, 1, t, 1), lambda bb, hh, ii: (bb, hh, ii, 0))] + hbm,
        scratch_shapes=_DirectExchange.scratch(n) if n else [],
        compiler_params=_params(("arbitrary", "arbitrary", "arbitrary")),
    )(q, k, v, *gather)
    return outs[0], outs[1], outs[2:]


def _halo_prev(tm, w):
    r = tm // HALO
    return pl.BlockSpec((1, HALO, w), lambda b, s: (b, jnp.maximum(s * r - 1, 0), 0))


def _halo_next(tm, w, n_tiles):
    r = tm // HALO
    return pl.BlockSpec((1, HALO, w), lambda b, s: (b, jnp.minimum((s + 1) * r, n_tiles * r - 1), 0))


def _conv_rows(cc):
    return max(8, 16 * 8 * LANES // cc)


def _shifted_copies(buf, rows):
    buf[0, pl.ds(rows, 8), :] = jnp.zeros((8, buf.shape[2]), buf.dtype)
    for s in range(1, 8):
        buf[s, pl.ds(0, rows), :] = buf[0, pl.ds(s, rows), :]


def _window(buf, start, rows):
    return buf[start % 8, pl.ds(start - start % 8, rows), :]


def _glu(v, cc):
    a, g = v[:, :cc].astype(F32), v[:, cc:].astype(F32)
    return a * _sig(g)


def _conv_fwd(glu, w_dw, b_dw, g_cn, b_cn, w_pw2):
    b, s, w2 = glu.shape
    cc = w2 // 2
    d = w_pw2.shape[1]
    tm = _row_tile(s)

    rc = _conv_rows(cc)
    te = tm + HALO

    def body(cur_ref, prev_ref, w_ref, bdw_ref, g_ref, bcn_ref, wp_ref, u_ref, y_ref, ext, u_all):
        first = pl.program_id(1) == 0
        ext[0, pl.ds(0, HALO), :] = jnp.where(first, 0.0, _glu(prev_ref[0], cc))
        ext[0, pl.ds(HALO, tm), :] = _glu(cur_ref[0], cc)
        _shifted_copies(ext, te)
        for c0 in range(0, tm, rc):
            acc = jnp.zeros((rc, cc), F32) + bdw_ref[...]
            for kk in range(CONV_W):
                acc = acc + w_ref[pl.ds(kk, 1), :] * _window(ext, c0 + HALO - CONV_W + 1 + kk, rc)
            u_all[pl.ds(c0, rc), :] = acc
        ub = u_all[...].astype(BF16)
        u_ref[0] = ub
        uf = ub.astype(F32)
        mu = jnp.mean(uf, axis=-1, keepdims=True)
        uc = uf - mu
        ln = uc * lax.rsqrt(jnp.mean(uc * uc, axis=-1, keepdims=True) + EPS) * g_ref[...] + bcn_ref[...]
        y_ref[0] = _dot((ln * _sig(ln)).astype(BF16), wp_ref[...]).astype(BF16)

    return pl.pallas_call(
        body, name="conv_fwd", grid=(b, s // tm),
        out_shape=(jax.ShapeDtypeStruct((b, s, cc), BF16), jax.ShapeDtypeStruct((b, s, d), BF16)),
        in_specs=[_rows(tm, w2), _halo_prev(tm, w2), _const(w_dw.shape), _const((1, cc)), _const((1, cc)), _const((1, cc)),
                  _const(w_pw2.shape)],
        out_specs=(_rows(tm, cc), _rows(tm, d)),
        scratch_shapes=[pltpu.VMEM((8, te + 8, cc), F32), pltpu.VMEM((tm, cc), F32)],
        compiler_params=_params(("arbitrary", "arbitrary")),
    )(glu, glu, w_dw, b_dw, g_cn, b_cn, w_pw2)


def _mix_post_fwd(x, mod, o, y_conv, gate, w_o, w_out):
    b, s, d = x.shape
    hw = o.shape[2]
    tm = _row_tile(s)

    def body(x_ref, mod_ref, o_ref, yc_ref, gate_ref, wo_ref, wout_ref, x2_ref, ya_ref, o1_ref):
        yab = _dot(o_ref[0], wo_ref[...]).astype(BF16)
        ya_ref[0] = yab
        gv = gate_ref[0]
        y = _sig(gv[:, :d].astype(F32)) * yab.astype(F32) + _sig(gv[:, d:].astype(F32)) * yc_ref[0].astype(F32)
        o1 = _dot(y.astype(BF16), wout_ref[...])
        o1_ref[0] = o1.astype(BF16)
        x2_ref[0] = x_ref[0] + mod_ref[0, 2] * o1

    return pl.pallas_call(
        body, name="mix_post_fwd", grid=(b, s // tm),
        out_shape=(jax.ShapeDtypeStruct((b, s, d), F32), jax.ShapeDtypeStruct((b, s, d), BF16),
                   jax.ShapeDtypeStruct((b, s, d), BF16)),
        in_specs=[_rows(tm, d), _per_b(6, d), _rows(tm, hw), _rows(tm, d), _rows(tm, 2 * d), _const(w_o.shape),
                  _const(w_out.shape)],
        out_specs=(_rows(tm, d), _rows(tm, d), _rows(tm, d)),
        compiler_params=_params(("arbitrary", "arbitrary")),
    )(x, mod, o, y_conv, gate, w_o, w_out)


def _ffn_fwd(x2, mod, g_ffn, w_gu, w_down):
    b, s, d = x2.shape
    f = w_down.shape[0]
    fc = _pick(f, 512)
    tm = _row_tile(s)

    def body(x_ref, mod_ref, g_ref, wgu_ref, wdn_ref, x3_ref, gu_ref, o2_ref, h_ref):
        xf = x_ref[0]
        hb = ((xf * _rstd(xf) * g_ref[...]) * (1.0 + mod_ref[0, 4]) + mod_ref[0, 3]).astype(BF16)
        h_ref[0] = hb
        o2 = jnp.zeros((tm, d), F32)
        for c0 in range(0, f, fc):
            gb = _dot(hb, wgu_ref[:, c0:c0 + fc]).astype(BF16)
            ub = _dot(hb, wgu_ref[:, f + c0:f + c0 + fc]).astype(BF16)
            gu_ref[0, :, c0:c0 + fc] = gb
            gu_ref[0, :, f + c0:f + c0 + fc] = ub
            gf = gb.astype(F32)
            act = (gf * _sig(gf) * ub.astype(F32)).astype(BF16)
            o2 = o2 + _dot(act, wdn_ref[c0:c0 + fc, :])
        o2_ref[0] = o2.astype(BF16)
        x3_ref[0] = xf + mod_ref[0, 5] * o2

    return pl.pallas_call(
        body, name="ffn_fwd", grid=(b, s // tm),
        out_shape=(jax.ShapeDtypeStruct((b, s, d), F32), jax.ShapeDtypeStruct((b, s, 2 * f), BF16),
                   jax.ShapeDtypeStruct((b, s, d), BF16), jax.ShapeDtypeStruct((b, s, d), BF16)),
        in_specs=[_rows(tm, d), _per_b(6, d), _const((1, d)), _const(w_gu.shape), _const(w_down.shape)],
        out_specs=(_rows(tm, d), _rows(tm, 2 * f), _rows(tm, d), _rows(tm, d)),
        compiler_params=_params(("arbitrary", "arbitrary")),
    )(x2, mod, g_ffn, w_gu, w_down)


def _zero_at_first_tile(*refs):
    @pl.when(pl.program_id(1) == 0)
    def _():
        for ref in refs:
            ref[...] = jnp.zeros_like(ref)


def _accumulate(ref, idx, val):
    ref[idx] = ref[idx] + val


def _colsum(v):
    return jnp.sum(v, axis=0, keepdims=True)


def _loss_bwd(x, target, g_final):
    b, s, d = x.shape
    tm = _row_tile(s)

    def body(x_ref, t_ref, g_ref, dx_ref, loss_ref, dg_ref):
        _zero_at_first_tile(loss_ref, dg_ref)
        xf = x_ref[0]
        r = _rstd(xf)
        xh = xf * r
        diff = xh * g_ref[...] - t_ref[0]
        _accumulate(loss_ref, (0, 0), _colsum(diff * diff))
        dy = diff * (1.0 / d)
        _accumulate(dg_ref, (0, 0), _colsum(dy * xh))
        dyg = dy * g_ref[...]
        dx_ref[0] = r * (dyg - xh * jnp.mean(dyg * xh, axis=-1, keepdims=True))

    return pl.pallas_call(
        body, name="loss_bwd", grid=(b, s // tm),
        out_shape=(jax.ShapeDtypeStruct((b, s, d), F32), jax.ShapeDtypeStruct((b, 1, 1, d), F32),
                   jax.ShapeDtypeStruct((b, 1, 1, d), F32)),
        in_specs=[_rows(tm, d), _rows(tm, d), _const((1, d))],
        out_specs=(_rows(tm, d), _per_b(1, d), _per_b(1, d)),
        compiler_params=_params(("arbitrary", "arbitrary")),
    )(x, target, g_final)


def _ffn_bwd(dx3, x2, mod, g_ffn, gu, o2, w_gu, w_down):
    b, s, d = x2.shape
    f = w_down.shape[0]
    fc = _pick(f, 512)
    tm = _row_tile(s)

    def hidden_body(dx3_ref, mod_ref, gu_ref, o2_ref, wdnt_ref, do2_ref, act_ref, dgu_ref, dgt_ref):
        _zero_at_first_tile(dgt_ref)
        dx3 = dx3_ref[0]
        do2 = (dx3 * mod_ref[0, 5]).astype(BF16)
        do2_ref[0] = do2
        _accumulate(dgt_ref, (0, 0), _colsum(dx3 * o2_ref[0].astype(F32)))
        for c0 in range(0, f, fc):
            gf = gu_ref[0, :, c0:c0 + fc].astype(F32)
            uf = gu_ref[0, :, f + c0:f + c0 + fc].astype(F32)
            sg = _sig(gf)
            silu = gf * sg
            act_ref[0, :, c0:c0 + fc] = (silu * uf).astype(BF16)
            dact = _dot_nt(do2, wdnt_ref[c0:c0 + fc, :])
            dgu_ref[0, :, c0:c0 + fc] = (dact * uf * (sg * (1.0 + gf * (1.0 - sg)))).astype(BF16)
            dgu_ref[0, :, f + c0:f + c0 + fc] = (dact * silu).astype(BF16)

    do2, act, dgu, dgt = pl.pallas_call(
        hidden_body, name="ffn_bwd_hidden", grid=(b, s // tm),
        out_shape=(jax.ShapeDtypeStruct((b, s, d), BF16), jax.ShapeDtypeStruct((b, s, f), BF16),
                   jax.ShapeDtypeStruct((b, s, 2 * f), BF16), jax.ShapeDtypeStruct((b, 1, 1, d), F32)),
        in_specs=[_rows(tm, d), _per_b(6, d), _rows(tm, 2 * f), _rows(tm, d), _const(w_down.shape)],
        out_specs=(_rows(tm, d), _rows(tm, f), _rows(tm, 2 * f), _per_b(1, d)),
        compiler_params=_params(("arbitrary", "arbitrary")),
    )(dx3, mod, gu, o2, w_down)

    def input_body(dx3_ref, x_ref, mod_ref, g_ref, dgu_ref, wgut_ref, dx2_ref, dmod_ref, dg_ref):
        _zero_at_first_tile(dmod_ref, dg_ref)
        sc = mod_ref[0, 4]
        dh = _dot_nt(dgu_ref[0], wgut_ref[...])
        xf = x_ref[0]
        r = _rstd(xf)
        xh = xf * r
        n = xh * g_ref[...]
        _accumulate(dmod_ref, (0, 0), _colsum(dh))
        _accumulate(dmod_ref, (0, 1), _colsum(dh * n))
        dn = dh * (1.0 + sc)
        _accumulate(dg_ref, (0, 0), _colsum(dn * xh))
        dyg = dn * g_ref[...]
        dx2_ref[0] = dx3_ref[0] + r * (dyg - xh * jnp.mean(dyg * xh, axis=-1, keepdims=True))

    dx2, dmod_sc, dg = pl.pallas_call(
        input_body, name="ffn_bwd_input", grid=(b, s // tm),
        out_shape=(jax.ShapeDtypeStruct((b, s, d), F32), jax.ShapeDtypeStruct((b, 2, 1, d), F32),
                   jax.ShapeDtypeStruct((b, 1, 1, d), F32)),
        in_specs=[_rows(tm, d), _rows(tm, d), _per_b(6, d), _const((1, d)), _rows(tm, 2 * f), _const(w_gu.shape)],
        out_specs=(_rows(tm, d), _per_b(2, d), _per_b(1, d)),
        compiler_params=_params(("arbitrary", "arbitrary")),
    )(dx3, x2, mod, g_ffn, dgu, w_gu)
    return dx2, do2, act, dgu, jnp.concatenate([dmod_sc, dgt], axis=1), dg


def _mix_post_bwd(dx2, mod, o1, y_attn, y_conv, gate, w_out_t, w_o_t):
    b, s, d = dx2.shape
    hw = w_o_t.shape[0]
    tm = _row_tile(s)

    def body(dx_ref, mod_ref, o1_ref, ya_ref, yc_ref, gate_ref, woutt_ref, wot_ref, do1_ref, dya_ref, dyc_ref, dgate_ref,
             y_ref, do_ref, dgt_ref):
        _zero_at_first_tile(dgt_ref)
        dx = dx_ref[0]
        do1 = (dx * mod_ref[0, 2]).astype(BF16)
        do1_ref[0] = do1
        _accumulate(dgt_ref, (0, 0), _colsum(dx * o1_ref[0].astype(F32)))
        dy = _dot_nt(do1, woutt_ref[...])
        gv = gate_ref[0]
        sa, sb = _sig(gv[:, :d].astype(F32)), _sig(gv[:, d:].astype(F32))
        ya, yc = ya_ref[0].astype(F32), yc_ref[0].astype(F32)
        y_ref[0] = (sa * ya + sb * yc).astype(BF16)
        dya = (dy * sa).astype(BF16)
        dya_ref[0] = dya
        dyc_ref[0] = (dy * sb).astype(BF16)
        dgate_ref[0, :, :d] = (dy * ya * sa * (1.0 - sa)).astype(BF16)
        dgate_ref[0, :, d:] = (dy * yc * sb * (1.0 - sb)).astype(BF16)
        do_ref[0] = _dot_nt(dya, wot_ref[...]).astype(BF16)

    row = jax.ShapeDtypeStruct((b, s, d), BF16)
    return pl.pallas_call(
        body, name="mix_post_bwd", grid=(b, s // tm),
        out_shape=(row, row, row, jax.ShapeDtypeStruct((b, s, 2 * d), BF16), row,
                   jax.ShapeDtypeStruct((b, s, hw), BF16), jax.ShapeDtypeStruct((b, 1, 1, d), F32)),
        in_specs=[_rows(tm, d), _per_b(6, d), _rows(tm, d), _rows(tm, d), _rows(tm, d), _rows(tm, 2 * d),
                  _const(w_out_t.shape), _const(w_o_t.shape)],
        out_specs=(_rows(tm, d), _rows(tm, d), _rows(tm, d), _rows(tm, 2 * d), _rows(tm, d), _rows(tm, hw), _per_b(1, d)),
        compiler_params=_params(("arbitrary", "arbitrary")),
    )(dx2, mod, o1, y_attn, y_conv, gate, w_out_t, w_o_t)


def _conv_bwd(dyc, u, glu, w_dw, g_cn, b_cn, w_pw2_t):
    b, s, cc = u.shape
    d = dyc.shape[2]
    tm = _row_tile(s)
    nt = s // tm
    te = tm + HALO
    rc = _conv_rows(cc)

    def body(dyc_ref, dycn_ref, u_ref, un_ref, glu_ref, glup_ref, w_ref, g_ref, bcn_ref, wpt_ref, dglu_ref, s_ref, dw_ref,
             small_ref, du_ext, uin_ext, duin_all):
        _zero_at_first_tile(dw_ref, small_ref)
        st = pl.program_id(1)
        dy_all = jnp.concatenate([dyc_ref[0], dycn_ref[0]], axis=0)
        u_all = jnp.concatenate([u_ref[0], un_ref[0]], axis=0).astype(F32)
        ds = _dot_nt(dy_all, wpt_ref[...])
        mu = jnp.mean(u_all, axis=-1, keepdims=True)
        uc = u_all - mu
        rstd = lax.rsqrt(jnp.mean(uc * uc, axis=-1, keepdims=True) + EPS)
        uh = uc * rstd
        ln = uh * g_ref[...] + bcn_ref[...]
        sg = _sig(ln)
        s_ref[0] = (ln * sg)[:tm].astype(BF16)
        dln = ds * (sg * (1.0 + ln * (1.0 - sg)))
        duh = dln * g_ref[...]
        du = rstd * (duh - jnp.mean(duh, axis=-1, keepdims=True) - uh * jnp.mean(duh * uh, axis=-1, keepdims=True))
        row = lax.broadcasted_iota(jnp.int32, (te, 1), 0)
        du = jnp.where(jnp.logical_and(st == nt - 1, row >= tm), 0.0, du)
        du_ext[0, pl.ds(0, te), :] = du
        _shifted_copies(du_ext, te)
        du_cur = du[:tm]
        _accumulate(small_ref, (0, 0), _colsum((dln * uh)[:tm]))
        _accumulate(small_ref, (0, 1), _colsum(dln[:tm]))
        _accumulate(small_ref, (0, 2), _colsum(du_cur))
        uin_ext[0, pl.ds(0, HALO), :] = jnp.where(st == 0, 0.0, _glu(glup_ref[0], cc))
        gv = glu_ref[0]
        ga, gb = gv[:, :cc].astype(F32), gv[:, cc:].astype(F32)
        sgb = _sig(gb)
        uin_ext[0, pl.ds(HALO, tm), :] = ga * sgb
        _shifted_copies(uin_ext, te)
        for c0 in range(0, tm, rc):
            du_c = du_ext[0, pl.ds(c0, rc), :]
            acc = jnp.zeros((rc, cc), F32)
            for kk in range(CONV_W):
                acc = acc + w_ref[pl.ds(kk, 1), :] * _window(du_ext, c0 + CONV_W - 1 - kk, rc)
                prod = du_c * _window(uin_ext, c0 + HALO - CONV_W + 1 + kk, rc)
                taps = pl.ds(8 * kk, 8)
                dw_ref[0, taps, :] = dw_ref[0, taps, :] + jnp.sum(prod.reshape(rc // 8, 8, cc), axis=0)
            duin_all[pl.ds(c0, rc), :] = acc
        duin = duin_all[...]
        dglu_ref[0, :, :cc] = (duin * sgb).astype(BF16)
        dglu_ref[0, :, cc:] = (duin * ga * sgb * (1.0 - sgb)).astype(BF16)

    return pl.pallas_call(
        body, name="conv_bwd", grid=(b, nt),
        out_shape=(jax.ShapeDtypeStruct((b, s, 2 * cc), BF16), jax.ShapeDtypeStruct((b, s, cc), BF16),
                   jax.ShapeDtypeStruct((b, 8 * HALO, cc), F32), jax.ShapeDtypeStruct((b, 3, 1, cc), F32)),
        in_specs=[_rows(tm, d), _halo_next(tm, d, nt), _rows(tm, cc), _halo_next(tm, cc, nt), _rows(tm, 2 * cc),
                  _halo_prev(tm, 2 * cc), _const(w_dw.shape), _const((1, cc)), _const((1, cc)), _const(w_pw2_t.shape)],
        out_specs=(_rows(tm, 2 * cc), _rows(tm, cc), pl.BlockSpec((1, 8 * HALO, cc), lambda i, j: (i, 0, 0)),
                   _per_b(3, cc)),
        scratch_shapes=[pltpu.VMEM((8, te + 8, cc), F32), pltpu.VMEM((8, te + 8, cc), F32), pltpu.VMEM((tm, cc), F32)],
        compiler_params=_params(("arbitrary", "arbitrary")),
    )(dyc, dyc, u, u, glu, glu, w_dw, g_cn, b_cn, w_pw2_t)


def _flash_bwd(q, k, v, o, lse, do, scatter=()):
    b, s, hw = q.shape
    nh, hp = hw // HEAD_PAD, HEAD_PAD
    t = _row_tile(s)
    nt = s // t
    n = len(scatter)
    grid = (b, nh, nt)

    def to_row(sel, cols):
        hi = cols.astype(BF16)
        lo = (cols - hi.astype(F32)).astype(BF16)
        return _dot_nt(sel, hi) + _dot_nt(sel, lo)

    def body(q_ref, k_ref, v_ref, o_ref, lse_ref, do_ref, *rest):
        dqt_ref, dk_ref, dv_ref = rest[n:n + 3]
        lse_row, delta_row = rest[2 * n + 3:2 * n + 5]
        finish = _riding_exchange("scatter", n, rest[:n] + rest[n + 3:2 * n + 3] + rest[2 * n + 5:], grid)
        j = pl.program_id(2)

        @pl.when(j == 0)
        def _():
            dqt_ref[...] = jnp.zeros_like(dqt_ref)
            first_lane = (lax.broadcasted_iota(jnp.int32, (8, hp), 1) == 0).astype(BF16)
            for i in range(nt):
                rows = pl.ds(i * t, t)
                prod = do_ref[0, rows, :].astype(F32) * o_ref[0, rows, :].astype(F32)
                delta_row[i] = to_row(jnp.ones((8, hp), BF16), prod)
                lse_row[i] = to_row(first_lane, jnp.broadcast_to(lse_ref[0, 0, rows, :], (t, hp)))

        kv, vv = k_ref[0], v_ref[0]
        kt = kv.T
        query_not_before_key = (lax.broadcasted_iota(jnp.int32, (t, t), 1) >= lax.broadcasted_iota(jnp.int32, (t, t), 0))

        def step(i, carry, masked):
            dk, dv = carry
            rows = pl.ds(pl.multiple_of(i * t, t), t)
            qv, dov = q_ref[0, rows, :], do_ref[0, rows, :]
            pt = jnp.exp((_dot_nt(kv, qv) - lse_row[i, 0:1, :]).astype(BF16))
            if masked:
                pt = jnp.where(query_not_before_key, pt, jnp.zeros((), BF16))
            dv = dv + _dot(pt, dov)
            dst = pt * (_dot_nt(vv, dov) - delta_row[i, 0:1, :]).astype(BF16)
            dk = dk + _dot(dst, qv)
            dqt_ref[0, i] = dqt_ref[0, i] + _dot(kt, dst)
            return dk, dv

        carry = step(j, (jnp.zeros((t, hp), F32), jnp.zeros((t, hp), F32)), True)
        pairs = (nt - 1 - j) // 2
        carry = lax.fori_loop(0, pairs, lambda ii, cr: step(j + 2 + 2 * ii, step(j + 1 + 2 * ii, cr, False), False), carry)
        dk, dv = lax.cond((nt - 1 - j) % 2 == 1, lambda cr: step(nt - 1, cr, False), lambda cr: cr, carry)
        dk_ref[0] = dk.astype(BF16)
        dv_ref[0] = dv.astype(BF16)
        if finish:
            finish()

    tile = pl.BlockSpec((1, t, hp), lambda bb, hh, jj: (bb, jj, hh))
    full = pl.BlockSpec((1, s, hp), lambda bb, hh, jj: (bb, 0, hh))
    hbm = [pl.BlockSpec(memory_space=pl.ANY)] * n
    outs = pl.pallas_call(
        body, name="flash_bwd_scatter" if n else "flash_bwd", grid=grid,
        out_shape=[jax.ShapeDtypeStruct((b, nt, hw, t), F32), jax.ShapeDtypeStruct((b, s, hw), BF16),
                   jax.ShapeDtypeStruct((b, s, hw), BF16)] + _DirectExchange.out_shapes("scatter", scatter),
        in_specs=[full, tile, tile, full, pl.BlockSpec((1, 1, s, 1), lambda bb, hh, jj: (bb, hh, 0, 0)), full] + hbm,
        out_specs=[pl.BlockSpec((1, nt, hp, t), lambda bb, hh, jj: (bb, 0, hh, 0)), tile, tile] + hbm,
        scratch_shapes=[pltpu.VMEM((nt, 8, t), F32), pltpu.VMEM((nt, 8, t), F32)]
        + (_DirectExchange.scratch(n) if n else []),
        compiler_params=_params(("arbitrary", "arbitrary", "arbitrary")),
    )(q, k, v, o, lse, do, *scatter)
    return outs[0], outs[1], outs[2], outs[3:]


def _mix_pre_bwd(x, dx2, mod, cos_t, sin_t, g_mix, g_q, g_kv, lat, dq, dk, dv, dglu, dgate, w_lat_t, w_glu_t, w_gate_t,
                 w_q_t, w_k_t, w_v_t):
    b, s, d = x.shape
    ql, kl = g_q.shape[1], g_kv.shape[1]
    wl = lat.shape[2]
    hw = N_HEADS * HEAD_PAD
    tm = _row_tile(s)

    def body(x_ref, dx2_ref, mod_ref, cos_ref, sin_ref, gm_ref, gq_ref, gkv_ref, lat_ref, dq_ref, dk_ref, dv_ref, dglu_ref,
             dgate_ref, wlt_ref, wgt_ref, wtt_ref, wqt_ref, wkt_ref, wvt_ref, dx_ref, dlat_ref, dqr_ref, qn_ref, kvn_ref,
             h_ref, dmod_ref, dgm_ref, dgq_ref, dgkv_ref):
        _zero_at_first_tile(dmod_ref, dgm_ref, dgq_ref, dgkv_ref)
        cos_v, sin_v = cos_ref[0], sin_ref[0]
        latf = lat_ref[0].astype(F32)
        q_lat, kv_lat = latf[:, :ql], latf[:, ql:ql + kl]
        rq, rk = _rstd(q_lat), _rstd(kv_lat)
        qh, kh = q_lat * rq, kv_lat * rk
        qn_ref[0] = (qh * gq_ref[...]).astype(BF16)
        kvn_ref[0] = (kh * gkv_ref[...]).astype(BF16)
        dk_sum = jnp.zeros((tm, HEAD_PAD), F32)
        for h in range(N_HEADS):
            cols = slice(h * HEAD_PAD, (h + 1) * HEAD_PAD)
            dq_head = dq_ref[0, 0, cols, :].T
            dqr_ref[0, :, cols] = _rope_bwd(dq_head * ATTN_SCALE, cos_v, sin_v).astype(BF16)
            dk_sum = dk_sum + dk_ref[0, :, cols].astype(F32)
        dqn = _dot_nt(dqr_ref[0], wqt_ref[...])
        dkvn = _dot_nt(dk_ref[0], wkt_ref[...]) + _dot_nt(dv_ref[0], wvt_ref[...])
        lane = lax.broadcasted_iota(jnp.int32, dk_sum.shape, 1)
        dkr = _rope_bwd(jnp.where(lane >= QK_NOPE, dk_sum, 0.0), cos_v, sin_v)
        _accumulate(dgq_ref, (0, 0), _colsum(dqn * qh))
        _accumulate(dgkv_ref, (0, 0), _colsum(dkvn * kh))
        dqg, dkg = dqn * gq_ref[...], dkvn * gkv_ref[...]
        dlat_ref[0, :, :ql] = (rq * (dqg - qh * jnp.mean(dqg * qh, axis=-1, keepdims=True))).astype(BF16)
        dlat_ref[0, :, ql:ql + kl] = (rk * (dkg - kh * jnp.mean(dkg * kh, axis=-1, keepdims=True))).astype(BF16)
        dlat_ref[0, :, ql + kl:] = dkr.astype(BF16)
        dh = (_dot_nt(dlat_ref[0], wlt_ref[...]) + _dot_nt(dglu_ref[0], wgt_ref[...])
              + _dot_nt(dgate_ref[0], wtt_ref[...]))
        sh, sc = mod_ref[0, 0], mod_ref[0, 1]
        xf = x_ref[0]
        r = _rstd(xf)
        xh = xf * r
        n = xh * gm_ref[...]
        h_ref[0] = (n * (1.0 + sc) + sh).astype(BF16)
        _accumulate(dmod_ref, (0, 0), _colsum(dh))
        _accumulate(dmod_ref, (0, 1), _colsum(dh * n))
        dn = dh * (1.0 + sc)
        _accumulate(dgm_ref, (0, 0), _colsum(dn * xh))
        dyg = dn * gm_ref[...]
        dx_ref[0] = dx2_ref[0] + r * (dyg - xh * jnp.mean(dyg * xh, axis=-1, keepdims=True))

    return pl.pallas_call(
        body, name="mix_pre_bwd", grid=(b, s // tm),
        out_shape=(jax.ShapeDtypeStruct((b, s, d), F32), jax.ShapeDtypeStruct((b, s, wl), BF16),
                   jax.ShapeDtypeStruct((b, s, hw), BF16), jax.ShapeDtypeStruct((b, s, ql), BF16),
                   jax.ShapeDtypeStruct((b, s, kl), BF16), jax.ShapeDtypeStruct((b, s, d), BF16),
                   jax.ShapeDtypeStruct((b, 2, 1, d), F32), jax.ShapeDtypeStruct((b, 1, 1, d), F32),
                   jax.ShapeDtypeStruct((b, 1, 1, ql), F32), jax.ShapeDtypeStruct((b, 1, 1, kl), F32)),
        in_specs=[_rows(tm, d), _rows(tm, d), _per_b(6, d), _rows(tm, HEAD_PAD), _rows(tm, HEAD_PAD), _const((1, d)),
                  _const((1, ql)), _const((1, kl)), _rows(tm, wl),
                  pl.BlockSpec((1, 1, hw, tm), lambda b, s: (b, s, 0, 0)), _rows(tm, hw), _rows(tm, hw),
                  _rows(tm, dglu.shape[2]), _rows(tm, 2 * d), _const(w_lat_t.shape), _const(w_glu_t.shape),
                  _const(w_gate_t.shape), _const(w_q_t.shape), _const(w_k_t.shape), _const(w_v_t.shape)],
        out_specs=(_rows(tm, d), _rows(tm, wl), _rows(tm, hw), _rows(tm, ql), _rows(tm, kl), _rows(tm, d), _per_b(2, d),
                   _per_b(1, d), _per_b(1, ql), _per_b(1, kl)),
        compiler_params=_params(("arbitrary", "arbitrary")),
    )(x, dx2, mod, cos_t, sin_t, g_mix, g_q, g_kv, lat, dq, dk, dv, dglu, dgate, w_lat_t, w_glu_t, w_gate_t, w_q_t, w_k_t,
      w_v_t)


def _matmul_tn(a, bm, name):
    b, s, kd = a.shape
    nd = bm.shape[2]
    tk, tn = _pick(kd, 1536), _pick(nd, 1536)
    ts = _row_tile(s, 2048)

    def body(a_ref, b_ref, o_ref):
        part = _dot_tn(a_ref[0], b_ref[0])
        first = jnp.logical_and(pl.program_id(2) == 0, pl.program_id(3) == 0)

        @pl.when(first)
        def _():
            o_ref[...] = part

        @pl.when(jnp.logical_not(first))
        def _():
            o_ref[...] = o_ref[...] + part

    return pl.pallas_call(
        body, name=name, grid=(kd // tk, nd // tn, b, s // ts),
        out_shape=jax.ShapeDtypeStruct((kd, nd), F32),
        in_specs=[pl.BlockSpec((1, ts, tk), lambda i, j, bb, ss: (bb, ss, i)),
                  pl.BlockSpec((1, ts, tn), lambda i, j, bb, ss: (bb, ss, j))],
        out_specs=pl.BlockSpec((tk, tn), lambda i, j, bb, ss: (i, j)),
        compiler_params=_params(("arbitrary",) * 4),
    )(a, bm)


def _adamw_update(w, g, m, v):
    nm = ADAM_B1 * m + (1.0 - ADAM_B1) * g
    nv = ADAM_B2 * v + (1.0 - ADAM_B2) * (g * g)
    delta = -ADAM_LR * ((nm / (1.0 - ADAM_B1 ** ADAM_STEP)) / (jnp.sqrt(nv / (1.0 - ADAM_B2 ** ADAM_STEP)) + ADAM_EPS)
                        + ADAM_WD * w)
    return delta, nm, nv


def _adamw(w, g, m, v, name):
    shape = w.shape
    cols = shape[-1]
    rows = w.size // cols
    w2, g2, m2, v2 = (t.reshape(rows, cols) for t in (w, g, m, v))
    tr = rows
    if rows * cols * 4 > (1 << 20):
        tr = _div_tile(rows, max(8, (1 << 18) // cols), 8)

    def body(w_ref, g_ref, m_ref, v_ref, d_ref, nm_ref, nv_ref):
        d_ref[...], nm_ref[...], nv_ref[...] = _adamw_update(w_ref[...], g_ref[...], m_ref[...], v_ref[...])

    spec = pl.BlockSpec((tr, cols), lambda i: (i, 0))
    outs = pl.pallas_call(
        body, name=name, grid=(rows // tr,), out_shape=(jax.ShapeDtypeStruct((rows, cols), F32),) * 3,
        in_specs=[spec] * 4, out_specs=(spec,) * 3, compiler_params=_params(("arbitrary",)),
    )(w2, g2, m2, v2)
    return tuple(t.reshape(shape) for t in outs)


def _adamw_reduce(w, m, v, own, got, name):
    shape = w.shape
    cols = shape[-1]
    w2, m2, v2, o2 = (_rows_2d(t, 0) for t in (w, m, v, own))
    g3 = _rows_2d(got, N_DEV - 1)
    rows = w2.shape[0]
    tr = _grad_row_tile(rows, cols)

    def body(w_ref, m_ref, v_ref, o_ref, r_ref, g_ref, d_ref, nm_ref, nv_ref):
        g = o_ref[...].astype(F32)
        for r in range(N_DEV - 1):
            g = g + r_ref[r].astype(F32)
        g_ref[...] = g
        d_ref[...], nm_ref[...], nv_ref[...] = _adamw_update(w_ref[...], g, m_ref[...], v_ref[...])

    spec = pl.BlockSpec((tr, cols), lambda i: (i, 0))
    outs = pl.pallas_call(
        body, name=name, grid=(rows // tr,),
        in_specs=[spec, spec, spec, spec, pl.BlockSpec((N_DEV - 1, tr, cols), lambda i: (0, i, 0))],
        out_specs=(spec,) * 4,
        out_shape=(jax.ShapeDtypeStruct((rows, cols), F32),) * 4, compiler_params=_params(("arbitrary",)),
    )(w2, m2, v2, o2, g3)
    return tuple(t.reshape(shape) for t in outs)


GATHERED = (("w_in", 2), ("w_uq", 2), ("w_ukv", 2), ("w_o_attn", 2), ("w_pw2", 2), ("w_out", 1), ("w_gu", 2), ("w_down", 1))
PRE, POST = (0, 1, 2), (3, 4, 5, 6, 7)


def _from_chunks(chunks, axis):
    _, a, bb = chunks.shape
    if axis == 2:
        return jnp.transpose(chunks, (1, 0, 2)).reshape(a, N_DEV * bb)
    return chunks.reshape(N_DEV * a, bb)


def _to_chunks(full, axis):
    a, bb = full.shape
    if axis == 2:
        return jnp.transpose(full.reshape(a, N_DEV, bb // N_DEV), (1, 0, 2)).astype(BF16)
    return full.reshape(N_DEV, a // N_DEV, bb).astype(BF16)


def _swap_halves(t):
    half = QK_ROPE // 2
    return jnp.concatenate([t[..., half:], t[..., :half]], axis=-1)


def _pad_rows(t, mult=8):
    return jnp.pad(t, ((0, -t.shape[0] % mult), (0, 0)))


def _pad_last(t, width):
    return jnp.pad(t, ((0, 0),) * (t.ndim - 1) + ((0, width - t.shape[-1]),))


def kernel(x, c, positions, w_ada, b_ada, g_mix, w_in, g_q, w_uq, g_kv, w_ukv, w_o_attn, w_dw, b_dw, g_cn, b_cn, w_pw2, w_out, g_ffn, w_gu, w_down, g_final, loss_target, m_w_ada, m_b_ada, m_g_mix, m_w_in, m_g_q, m_w_uq, m_g_kv, m_w_ukv, m_w_o_attn, m_w_dw, m_b_dw, m_g_cn, m_b_cn, m_w_pw2, m_w_out, m_g_ffn, m_w_gu, m_w_down, m_g_final, v_w_ada, v_b_ada, v_g_mix, v_w_in, v_g_q, v_w_uq, v_g_kv, v_w_ukv, v_w_o_attn, v_w_dw, v_b_dw, v_g_cn, v_b_cn, v_w_pw2, v_w_out, v_g_ffn, v_w_gu, v_w_down, v_g_final):
    weights = dict(w_ada=w_ada, b_ada=b_ada, g_mix=g_mix, w_in=w_in, g_q=g_q, w_uq=w_uq, g_kv=g_kv, w_ukv=w_ukv,
                   w_o_attn=w_o_attn, w_dw=w_dw, b_dw=b_dw, g_cn=g_cn, b_cn=b_cn, w_pw2=w_pw2, w_out=w_out, g_ffn=g_ffn,
                   w_gu=w_gu, w_down=w_down, g_final=g_final)
    mom_m = dict(w_ada=m_w_ada, b_ada=m_b_ada, g_mix=m_g_mix, w_in=m_w_in, g_q=m_g_q, w_uq=m_w_uq, g_kv=m_g_kv,
                 w_ukv=m_w_ukv, w_o_attn=m_w_o_attn, w_dw=m_w_dw, b_dw=m_b_dw, g_cn=m_g_cn, b_cn=m_b_cn, w_pw2=m_w_pw2,
                 w_out=m_w_out, g_ffn=m_g_ffn, w_gu=m_w_gu, w_down=m_w_down, g_final=m_g_final)
    mom_v = dict(w_ada=v_w_ada, b_ada=v_b_ada, g_mix=v_g_mix, w_in=v_w_in, g_q=v_g_q, w_uq=v_w_uq, g_kv=v_g_kv,
                 w_ukv=v_w_ukv, w_o_attn=v_w_o_attn, w_dw=v_w_dw, b_dw=v_b_dw, g_cn=v_g_cn, b_cn=v_b_cn, w_pw2=v_w_pw2,
                 w_out=v_w_out, g_ffn=v_g_ffn, w_gu=v_w_gu, w_down=v_w_down, g_final=v_g_final)
    order = list(weights)

    nb, s, d = x.shape
    nl = w_in.shape[0]
    ql, kl, cc = g_q.shape[1], g_kv.shape[1], g_cn.shape[1]
    h = N_HEADS
    qk = QK_NOPE + QK_ROPE
    xi, yi, ci = lax.axis_index("x"), lax.axis_index("y"), lax.axis_index("c")
    me = 4 * xi + 2 * yi + ci

    shards = [weights[n].astype(BF16) for n, _ in GATHERED]
    gathered_pre = _all_gather([shards[a][0] for a in PRE], vmem=False, name="weight_all_gather")
    n_dw = w_dw.shape[2]
    dw_rows = jnp.pad(w_dw, ((0, 0), (0, HALO - CONV_W), (0, LANES - n_dw))).reshape(nl * HALO, LANES)
    c_all, dw_all = _all_gather([_pad_rows(c), dw_rows], vmem=True, name="cond_all_gather")
    c_full = c_all[:, :nb].reshape(N_DEV * nb, d)
    w_dw_full = jnp.transpose(dw_all.reshape(N_DEV, nl, HALO, LANES)[..., :n_dw], (1, 2, 0, 3)).reshape(nl, HALO, cc)

    o_kr, o_glu, o_gate = ql + kl, ql + kl + QK_ROPE, ql + kl + QK_ROPE + 2 * cc

    def layouts_pre(chunks):
        wi, w_uq_l, w_ukv_l = (_from_chunks(t, GATHERED[a][1]) for a, t in zip(PRE, chunks))
        w_kr = wi[:, o_kr:o_glu]
        wq = w_uq_l.reshape(ql, h, qk)
        wkv = w_ukv_l.reshape(kl, h, QK_NOPE + V_HEAD)
        return dict(
            lat=jnp.concatenate([wi[:, :o_kr], jnp.zeros((d, QK_NOPE), BF16), w_kr, _swap_halves(w_kr)], axis=1),
            glu=wi[:, o_glu:o_gate], gate=wi[:, o_gate:],
            q=jnp.concatenate([wq, _swap_halves(wq[..., QK_NOPE:])], axis=-1).reshape(ql, h * HEAD_PAD),
            k=_pad_last(wkv[..., :QK_NOPE], HEAD_PAD).reshape(kl, h * HEAD_PAD),
            v=_pad_last(wkv[..., QK_NOPE:], HEAD_PAD).reshape(kl, h * HEAD_PAD))

    def layouts_post(chunks):
        w_o_l, w_pw_l, w_out_l, w_gu_l, w_down_l = (_from_chunks(t, GATHERED[a][1]) for a, t in zip(POST, chunks))
        w_o = jnp.pad(w_o_l.reshape(h, V_HEAD, d), ((0, 0), (0, HEAD_PAD - V_HEAD), (0, 0)))
        return dict(o=w_o.reshape(h * HEAD_PAD, d), pw=w_pw_l, out=w_out_l, gu=w_gu_l, down=w_down_l)

    n_ada = w_ada.shape[2]
    b_cols = lax.dynamic_slice_in_dim(b_ada, me * n_ada, n_ada, axis=1).reshape(nl, 1, n_ada)
    mod_part = _ada_fwd(c_full, w_ada, b_cols)
    (mod_all,) = _all_gather([mod_part.reshape(nl * N_DEV * nb, n_ada)], vmem=True, name="mod_all_gather")
    mod_all = jnp.transpose(mod_all.reshape(N_DEV, nl, N_DEV * nb, n_ada), (1, 2, 0, 3)).reshape(nl, N_DEV * nb, 6 * d)
    mod = lax.dynamic_slice_in_dim(mod_all, me * nb, nb, axis=1).reshape(nl, nb, 6, 1, d)

    inv_freq = ROPE_THETA ** (-jnp.arange(0, QK_ROPE, 2, dtype=F32) / QK_ROPE)
    zeros = lambda n: jnp.zeros((n,), F32)
    freq_row = jnp.concatenate([zeros(QK_NOPE), inv_freq, inv_freq, zeros(HEAD_PAD - qk)]).reshape(1, -1)
    ones = jnp.ones((QK_ROPE // 2,), F32)
    sign_row = jnp.concatenate([zeros(QK_NOPE), -ones, ones, zeros(HEAD_PAD - qk)]).reshape(1, -1)
    cos_t, sin_t = _rope_tables(positions.astype(F32).reshape(nb, s, 1), freq_row, sign_row)

    row = lambda t, l: t[l].reshape(1, -1)

    saved, wts = [], []
    xc = x
    for l in range(nl):
        wl = layouts_pre(gathered_pre)
        lat, glu, gate, qh, kh, vh = _mix_pre_fwd(xc, mod[l], cos_t, sin_t, row(g_mix, l), row(g_q, l), row(g_kv, l),
                                                  wl["lat"], wl["glu"], wl["gate"], wl["q"], wl["k"], wl["v"])
        riders = [shards[a][l] for a in POST] + ([shards[a][l + 1] for a in PRE] if l + 1 < nl else [])
        o, lse, arrived = _flash_fwd(qh, kh, vh, gather=riders)
        wl.update(layouts_post(arrived[:len(POST)]))
        gathered_pre = arrived[len(POST):]
        wts.append(wl)
        u, y_conv = _conv_fwd(glu, w_dw_full[l], row(b_dw, l), row(g_cn, l), row(b_cn, l), wl["pw"])
        x2, y_attn, o1 = _mix_post_fwd(xc, mod[l], o, y_conv, gate, wl["o"], wl["out"])
        x3, gu, o2, h2 = _ffn_fwd(x2, mod[l], row(g_ffn, l), wl["gu"], wl["down"])
        saved.append(dict(x=xc, lat=lat, glu=glu, gate=gate, q=qh, k=kh, v=vh, o=o, lse=lse, u=u, y_conv=y_conv, x2=x2,
                          y_attn=y_attn, o1=o1, gu=gu, o2=o2, h2=h2))
        xc = x3

    dx, loss_part, dgf_part = _loss_bwd(xc, loss_target, g_final.reshape(1, d))
    loss = lax.psum(0.5 / d * jnp.sum(loss_part), AXES)

    small_rows, dw_taps, dmod = [None] * nl, [None] * nl, [None] * nl
    own, got = {}, {}
    chunk = lambda g, a: _to_chunks(g[GATHERED[a][0]], GATHERED[a][1])
    mine = lambda t: lax.dynamic_index_in_dim(t, me, axis=0, keepdims=False)
    pending = []
    for l in reversed(range(nl)):
        sv, wl, gw = saved[l], wts[l], {}
        dx2, do2, act, dgu, dmod2, dgffn = _ffn_bwd(dx, sv["x2"], mod[l], row(g_ffn, l), sv["gu"], sv["o2"],
                                                   wl["gu"], wl["down"])
        gw["w_gu"] = _matmul_tn(sv["h2"], dgu, "grad_w_gu")
        gw["w_down"] = _matmul_tn(act, do2, "grad_w_down")
        do1, dya, dyc, dgate, yv, do_h, dgt1 = _mix_post_bwd(dx2, mod[l], sv["o1"], sv["y_attn"], sv["y_conv"], sv["gate"],
                                                           wl["out"], wl["o"])
        gw["w_out"] = _matmul_tn(yv, do1, "grad_w_out")
        dwo = _matmul_tn(sv["o"], dya, "grad_w_o")
        gw["w_o_attn"] = dwo.reshape(h, HEAD_PAD, d)[:, :V_HEAD].reshape(h * V_HEAD, d)
        dglu, s_act, ddw, csmall = _conv_bwd(dyc, sv["u"], sv["glu"], w_dw_full[l], row(g_cn, l), row(b_cn, l), wl["pw"])
        gw["w_pw2"] = _matmul_tn(s_act, dyc, "grad_w_pw2")
        ready = [chunk(gw, a) for a in POST]
        dq, dk, dv, arrived = _flash_bwd(sv["q"], sv["k"], sv["v"], sv["o"], sv["lse"], do_h, scatter=ready + pending)
        for i, a in enumerate(POST):
            own[l, a], got[l, a] = mine(ready[i]), arrived[i]
        for i, a in enumerate(PRE if pending else ()):
            got[l + 1, a] = arrived[len(POST) + i]
        dx, dlat, dqr, qn, kvn, h1, dmod1, dgm, dgq, dgkv = _mix_pre_bwd(
            sv["x"], dx2, mod[l], cos_t, sin_t, row(g_mix, l), row(g_q, l), row(g_kv, l), sv["lat"], dq, dk, dv, dglu,
            dgate, wl["lat"], wl["glu"], wl["gate"], wl["q"], wl["k"], wl["v"])
        dwl = _matmul_tn(h1, dlat, "grad_w_lat")
        dwg = _matmul_tn(h1, dglu, "grad_w_glu")
        dwt = _matmul_tn(h1, dgate, "grad_w_gate")
        kr0 = o_kr + QK_NOPE
        dkr = dwl[:, kr0:kr0 + QK_ROPE] + _swap_halves(dwl[:, kr0 + QK_ROPE:])
        gw["w_in"] = jnp.concatenate([dwl[:, :o_kr], dkr, dwg, dwt], axis=1)
        dwq = _matmul_tn(qn, dqr, "grad_w_q").reshape(ql, h, HEAD_PAD)
        dwq = jnp.concatenate([dwq[..., :QK_NOPE], dwq[..., QK_NOPE:qk] + _swap_halves(dwq[..., qk:])], axis=-1)
        gw["w_uq"] = dwq.reshape(ql, h * qk)
        dwk = _matmul_tn(kvn, dk, "grad_w_k").reshape(kl, h, HEAD_PAD)
        dwv = _matmul_tn(kvn, dv, "grad_w_v").reshape(kl, h, HEAD_PAD)
        gw["w_ukv"] = jnp.concatenate([dwk[..., :QK_NOPE], dwv[..., :V_HEAD]], axis=-1).reshape(kl, h * (QK_NOPE + V_HEAD))
        pending = [chunk(gw, a) for a in PRE]
        for i, a in enumerate(PRE):
            own[l, a] = mine(pending[i])
        dmod[l] = jnp.concatenate([dmod1[:, :, 0], dgt1[:, :, 0], dmod2[:, :, 0]], axis=1).reshape(nb, 6 * d)
        bsum = lambda t: jnp.sum(t, axis=0).reshape(1, -1)
        cs = jnp.sum(csmall, axis=0)[:, 0]
        small_rows[l] = jnp.concatenate([bsum(dgm), bsum(dgq), bsum(dgkv), cs[2:3], cs[0:1], cs[1:2], bsum(dgffn)], axis=1)
        dw_taps[l] = jnp.sum(ddw.reshape(nb, HALO, 8, cc), axis=(0, 2))
    grad_x = dx

    dmod_rows = _pad_rows(jnp.stack(dmod).reshape(nl * nb, 6 * d))
    (dmod_all,) = _all_gather([dmod_rows], vmem=True, name="dmod_all_gather")
    dmod_full = jnp.transpose(dmod_all[:, :nl * nb].reshape(N_DEV, nl, nb, 6 * d), (1, 0, 2, 3)).reshape(nl, N_DEV * nb, 6 * d)
    dmod_cols = lax.dynamic_slice_in_dim(dmod_full, me * n_ada, n_ada, axis=2)
    grad_w_ada, grad_b_ada = _ada_bwd(c_full, dmod_cols, dmod_full)
    grads = {"w_ada": grad_w_ada, "b_ada": grad_b_ada.reshape(nl, 6 * d)}

    widths = (d, ql, kl, cc, cc, cc, d)
    wsum = sum(widths)
    final_row = _pad_last(jnp.sum(dgf_part, axis=0).reshape(1, d), wsum)
    small2d = _pad_rows(jnp.concatenate(small_rows + [final_row], axis=0))
    taps2d = jnp.concatenate(dw_taps, axis=0)
    small_all, taps_all = _all_gather([small2d, taps2d], vmem=True, name="small_grad_all_gather")
    small_sum, taps_sum = _sum_devices(small_all), _sum_devices(taps_all)
    off = 0
    for n, wdt in zip(("g_mix", "g_q", "g_kv", "b_dw", "g_cn", "b_cn", "g_ffn"), widths):
        grads[n] = small_sum[:nl, off:off + wdt]
        off += wdt
    grads["g_final"] = small_sum[nl, :d]
    taps = taps_sum.reshape(nl, HALO, cc)[:, :CONV_W]
    grads["w_dw"] = lax.dynamic_slice_in_dim(taps, me * n_dw, n_dw, axis=2)

    delta, new_m, new_v = {}, {}, {}
    for n in order:
        if n in grads:
            delta[n], new_m[n], new_v[n] = _adamw(weights[n], grads[n], mom_m[n], mom_v[n], "adamw_" + n)

    for a, t in zip(PRE, _grad_scatter(pending)):
        got[0, a] = t
    for a, (n, _) in enumerate(GATHERED):
        own_n = jnp.stack([own[l, a] for l in range(nl)])
        got_n = jnp.stack([got[l, a] for l in range(nl)], axis=1)
        grads[n], delta[n], new_m[n], new_v[n] = _adamw_reduce(weights[n], mom_m[n], mom_v[n], own_n, got_n, "adamw_" + n)

    return (loss, grad_x, *[grads[n] for n in order], *[delta[n] for n in order], *[new_m[n] for n in order],
            *[new_v[n] for n in order])
```

```python
import jax
import jax.numpy as jnp
from jax import lax
from jax.experimental import pallas as pl
from jax.experimental.pallas import tpu as pltpu

F32, BF16 = jnp.float32, jnp.bfloat16
MESH = pl.DeviceIdType.MESH
AXES = ("x", "y", "c")
N_DEV = 8

N_HEADS = 8
QK_NOPE = 64
QK_ROPE = 32
V_HEAD = 64
HEAD_PAD = 128
CONV_W = 31
HALO = 32
EPS = 1e-6
ROPE_THETA = 10000.0
NEG_INF = -1e30
ATTN_SCALE = (QK_NOPE + QK_ROPE) ** -0.5

ADAM_LR, ADAM_B1, ADAM_B2, ADAM_EPS, ADAM_WD, ADAM_STEP = 0.001, 0.9, 0.999, 1e-08, 0.01, 10

LANES = 128
VMEM_LIMIT = 60 * 1024 * 1024


def _params(sem=None):
    return pltpu.CompilerParams(dimension_semantics=sem, vmem_limit_bytes=VMEM_LIMIT)


def _pick(n, cap):
    if n <= cap:
        return n
    best = None
    for d in range(LANES, cap + 1, LANES):
        if n % d == 0:
            best = d
    assert best is not None, (n, cap)
    return best


def _div_tile(n, cap, mult):
    best = None
    for d in range(mult, min(n, cap) + 1, mult):
        if n % d == 0:
            best = d
    assert best is not None, (n, cap, mult)
    return best


def _row_tile(s, cap=512):
    return cap if s % cap == 0 and s >= 2 * cap else s // 2


def _sig(v):
    return 1.0 / (1.0 + jnp.exp(-v))


def _rstd(v):
    return lax.rsqrt(jnp.mean(v * v, axis=-1, keepdims=True) + EPS)


def _dot(a, b):
    return jnp.dot(a, b, preferred_element_type=F32)


def _dot_nt(a, b):
    return lax.dot_general(a, b, (((1,), (1,)), ((), ())), preferred_element_type=F32)


def _dot_tn(a, b):
    return lax.dot_general(a, b, (((0,), (0,)), ((), ())), preferred_element_type=F32)


def _rope(v, cos_t, sin_t):
    return v * cos_t + pltpu.roll(v, HEAD_PAD - QK_ROPE, 1) * sin_t


def _rope_bwd(dv, cos_t, sin_t):
    return dv * cos_t + pltpu.roll(dv * sin_t, QK_ROPE, 1)


def _const(shape):
    n = len(shape)
    return pl.BlockSpec(shape, lambda *_: (0,) * n, pipeline_mode=pl.Buffered(1))


def _rows(tm, w):
    return pl.BlockSpec((1, tm, w), lambda b, s: (b, s, 0))


def _per_b(r, w):
    return pl.BlockSpec((1, r, 1, w), lambda b, s: (b, 0, 0, 0))


def _all_gather(arrs, vmem, name):
    n = len(arrs)
    space = pltpu.VMEM if vmem else pl.ANY

    def body(*refs):
        x_refs, out_refs = refs[:n], refs[n:2 * n]
        send_sems, recv_sems, local_sems = refs[2 * n:]
        x_, y_, c_ = lax.axis_index("x"), lax.axis_index("y"), lax.axis_index("c")
        me, sibling = (x_, y_, c_), (x_, y_, 1 - c_)
        chips = [(1 - x_, y_), (x_, 1 - y_), (1 - x_, 1 - y_)]

        def copy(a, k, block, to, own=False):
            px, py, pc = block
            slot = out_refs[a].at[4 * px + 2 * py + pc]
            return pltpu.make_async_remote_copy(
                src_ref=x_refs[a] if own else slot, dst_ref=slot, send_sem=send_sems.at[k * n + a],
                recv_sem=recv_sems.at[k * n + a], device_id=to, device_id_type=MESH)

        mine = [pltpu.make_async_copy(x_refs[a], out_refs[a].at[4 * x_ + 2 * y_ + c_], local_sems.at[a]) for a in range(n)]
        sent = []
        for a in range(n):
            mine[a].start()
            sent.append(copy(a, 0, me, sibling, own=True))
            sent += [copy(a, 1 + j, me, (*chip, c_), own=True) for j, chip in enumerate(chips)]
        for cp in sent:
            cp.start()
        for j, chip in enumerate(chips):
            for a in range(n):
                copy(a, 1 + j, (*chip, c_), me).wait_recv()
                passed = copy(a, 4 + j, (*chip, c_), sibling)
                passed.start()
                sent.append(passed)
        for a in range(n):
            copy(a, 0, sibling, me).wait_recv()
            for j, chip in enumerate(chips):
                copy(a, 4 + j, (*chip, 1 - c_), me).wait_recv()
        for cp in sent:
            cp.wait_send()
        for cp in mine:
            cp.wait()

    return pl.pallas_call(
        body, name=name,
        out_shape=[jax.ShapeDtypeStruct((N_DEV,) + t.shape, t.dtype) for t in arrs],
        in_specs=[pl.BlockSpec(memory_space=space)] * n, out_specs=[pl.BlockSpec(memory_space=space)] * n,
        scratch_shapes=[pltpu.SemaphoreType.DMA((7 * n,)), pltpu.SemaphoreType.DMA((7 * n,)), pltpu.SemaphoreType.DMA((n,))],
        compiler_params=pltpu.CompilerParams(vmem_limit_bytes=VMEM_LIMIT),
    )(*arrs)


FLIPS = tuple((fx, fy, fc) for fx in (0, 1) for fy in (0, 1) for fc in (0, 1))[1:]


class _DirectExchange:
    def __init__(self, kind, in_refs, out_refs, send_sems, recv_sems, local_sems):
        n = len(in_refs)
        x_, y_, c_ = lax.axis_index("x"), lax.axis_index("y"), lax.axis_index("c")
        me = 4 * x_ + 2 * y_ + c_
        self.copies, self.local = [], []
        for r, (fx, fy, fc) in enumerate(FLIPS):
            px, py, pc = (1 - x_ if fx else x_), (1 - y_ if fy else y_), (1 - c_ if fc else c_)
            for a in range(n):
                src = in_refs[a] if kind == "gather" else in_refs[a].at[4 * px + 2 * py + pc]
                dst = out_refs[a].at[me] if kind == "gather" else out_refs[a].at[r]
                self.copies.append(pltpu.make_async_remote_copy(
                    src_ref=src, dst_ref=dst, send_sem=send_sems.at[r * n + a], recv_sem=recv_sems.at[r * n + a],
                    device_id=(px, py, pc), device_id_type=MESH))
        if kind == "gather":
            self.local = [pltpu.make_async_copy(in_refs[a], out_refs[a].at[me], local_sems.at[a]) for a in range(n)]

    def start(self):
        for cp in self.local + self.copies:
            cp.start()

    def wait(self):
        for cp in self.copies + self.local:
            cp.wait()

    @staticmethod
    def out_shapes(kind, arrs):
        if kind == "gather":
            return [jax.ShapeDtypeStruct((N_DEV,) + t.shape, t.dtype) for t in arrs]
        return [jax.ShapeDtypeStruct((N_DEV - 1,) + t.shape[1:], t.dtype) for t in arrs]

    @staticmethod
    def scratch(n):
        return [pltpu.SemaphoreType.DMA((7 * n,)), pltpu.SemaphoreType.DMA((7 * n,)), pltpu.SemaphoreType.DMA((n,))]


def _grad_scatter(gs):
    n = len(gs)

    def body(*refs):
        ex = _DirectExchange("scatter", refs[:n], refs[n:2 * n], *refs[2 * n:])
        ex.start()
        ex.wait()

    return pl.pallas_call(
        body, name="grad_scatter", out_shape=_DirectExchange.out_shapes("scatter", gs),
        in_specs=[pl.BlockSpec(memory_space=pl.ANY)] * n, out_specs=[pl.BlockSpec(memory_space=pl.ANY)] * n,
        scratch_shapes=_DirectExchange.scratch(n),
    )(*gs)


def _rows_2d(t, lead):
    return t.reshape((lead, -1, t.shape[-1]) if lead else (-1, t.shape[-1]))


def _grad_row_tile(rows, cols):
    return _div_tile(rows, max(16, (1 << 18) // cols), 16)


def _sum_devices(g):
    _, m, n = g.shape

    def body(g_ref, o_ref):
        s = g_ref[0]
        for j in range(1, N_DEV):
            s = s + g_ref[j]
        o_ref[...] = s

    return pl.pallas_call(body, name="small_grad_sum", out_shape=jax.ShapeDtypeStruct((m, n), F32),
                          compiler_params=pltpu.CompilerParams(vmem_limit_bytes=VMEM_LIMIT))(g)


def _ada_fwd(c_full, w_ada, b_cols):
    nl, d, n = w_ada.shape
    nb = c_full.shape[0]

    def body(c_ref, w_ref, b_ref, o_ref):
        cv = c_ref[...]
        act = cv * _sig(cv)
        o_ref[0] = jnp.dot(act, w_ref[0], preferred_element_type=F32, precision=lax.Precision.HIGHEST) + b_ref[0]

    return pl.pallas_call(
        body, name="ada_fwd", grid=(nl,), out_shape=jax.ShapeDtypeStruct((nl, nb, n), F32),
        in_specs=[pl.BlockSpec((nb, d), lambda l: (0, 0)), pl.BlockSpec((1, d, n), lambda l: (l, 0, 0)),
                  pl.BlockSpec((1, 1, n), lambda l: (l, 0, 0))],
        out_specs=pl.BlockSpec((1, nb, n), lambda l: (l, 0, 0)), compiler_params=_params(("arbitrary",)),
    )(c_full, w_ada, b_cols)


def _ada_bwd(c_full, dmod_cols, dmod_full):
    nl, nb, n = dmod_cols.shape
    d = c_full.shape[1]
    nfull = dmod_full.shape[2]

    def body(c_ref, dc_ref, df_ref, gw_ref, gb_ref):
        cv = c_ref[...]
        act = cv * _sig(cv)
        gw_ref[0] = lax.dot_general(act, dc_ref[0], (((0,), (0,)), ((), ())), preferred_element_type=F32,
                                    precision=lax.Precision.HIGHEST)
        gb_ref[0] = jnp.sum(df_ref[0], axis=0, keepdims=True)

    return pl.pallas_call(
        body, name="ada_bwd", grid=(nl,),
        out_shape=(jax.ShapeDtypeStruct((nl, d, n), F32), jax.ShapeDtypeStruct((nl, 1, nfull), F32)),
        in_specs=[pl.BlockSpec((nb, d), lambda l: (0, 0)), pl.BlockSpec((1, nb, n), lambda l: (l, 0, 0)),
                  pl.BlockSpec((1, nb, nfull), lambda l: (l, 0, 0))],
        out_specs=(pl.BlockSpec((1, d, n), lambda l: (l, 0, 0)), pl.BlockSpec((1, 1, nfull), lambda l: (l, 0, 0))),
        compiler_params=_params(("arbitrary",)),
    )(c_full, dmod_cols, dmod_full)


def _rope_tables(pos, freq_row, sign_row):
    b, s, _ = pos.shape
    tm = _row_tile(s)

    def body(p_ref, f_ref, g_ref, c_ref, s_ref):
        ang = p_ref[0] * f_ref[...]
        lane = lax.broadcasted_iota(jnp.int32, ang.shape, 1)
        c_ref[0] = jnp.where(lane < QK_NOPE, 1.0, jnp.where(lane < QK_NOPE + QK_ROPE, jnp.cos(ang), 0.0))
        s_ref[0] = g_ref[...] * jnp.sin(ang)

    return pl.pallas_call(
        body, name="rope_tables", grid=(b, s // tm),
        out_shape=(jax.ShapeDtypeStruct((b, s, HEAD_PAD), F32),) * 2,
        in_specs=[_rows(tm, 1), pl.BlockSpec((1, HEAD_PAD), lambda i, j: (0, 0)),
                  pl.BlockSpec((1, HEAD_PAD), lambda i, j: (0, 0))],
        out_specs=(_rows(tm, HEAD_PAD),) * 2, compiler_params=_params(("arbitrary", "arbitrary")),
    )(pos, freq_row, sign_row)


def _mix_pre_fwd(x, mod, cos_t, sin_t, g_mix, g_q, g_kv, w_lat, w_glu, w_gate, w_q, w_k, w_v):
    b, s, d = x.shape
    ql, kl = g_q.shape[1], g_kv.shape[1]
    wl, wg, wt = w_lat.shape[1], w_glu.shape[1], w_gate.shape[1]
    hw = N_HEADS * HEAD_PAD
    tm = _row_tile(s)

    def body(x_ref, mod_ref, cos_ref, sin_ref, gm_ref, gq_ref, gkv_ref, wlat_ref, wglu_ref, wgate_ref, wq_ref, wk_ref,
             wv_ref, lat_ref, glu_ref, gate_ref, q_ref, k_ref, v_ref):
        xf = x_ref[0]
        sh, sc = mod_ref[0, 0], mod_ref[0, 1]
        hb = ((xf * _rstd(xf) * gm_ref[...]) * (1.0 + sc) + sh).astype(BF16)
        glu_ref[0] = _dot(hb, wglu_ref[...]).astype(BF16)
        gate_ref[0] = _dot(hb, wgate_ref[...]).astype(BF16)
        lat = _dot(hb, wlat_ref[...]).astype(BF16)
        lat_ref[0] = lat
        latf = lat.astype(F32)
        q_lat, kv_lat, kr_sec = latf[:, :ql], latf[:, ql:ql + kl], latf[:, ql + kl:]
        qn = (q_lat * _rstd(q_lat) * gq_ref[...]).astype(BF16)
        kvn = (kv_lat * _rstd(kv_lat) * gkv_ref[...]).astype(BF16)
        cos_v, sin_v = cos_ref[0], sin_ref[0]
        lane = lax.broadcasted_iota(jnp.int32, kr_sec.shape, 1)
        kr = jnp.where(lane >= QK_NOPE, _rope(kr_sec, cos_v, sin_v), 0.0)
        q_all, k_all, v_all = _dot(qn, wq_ref[...]), _dot(kvn, wk_ref[...]), _dot(kvn, wv_ref[...])
        vlane = lax.broadcasted_iota(jnp.int32, v_all.shape, 1)
        v_ref[0] = jnp.where(vlane % HEAD_PAD == V_HEAD, 1.0, v_all).astype(BF16)
        for h in range(N_HEADS):
            cols = slice(h * HEAD_PAD, (h + 1) * HEAD_PAD)
            q_ref[0, :, cols] = (_rope(q_all[:, cols], cos_v, sin_v) * ATTN_SCALE).astype(BF16)
            k_ref[0, :, cols] = (k_all[:, cols] + kr).astype(BF16)

    hshape = jax.ShapeDtypeStruct((b, s, hw), BF16)
    return pl.pallas_call(
        body, name="mix_pre_fwd", grid=(b, s // tm),
        out_shape=(jax.ShapeDtypeStruct((b, s, wl), BF16), jax.ShapeDtypeStruct((b, s, wg), BF16),
                   jax.ShapeDtypeStruct((b, s, wt), BF16), hshape, hshape, hshape),
        in_specs=[_rows(tm, d), _per_b(6, d), _rows(tm, HEAD_PAD), _rows(tm, HEAD_PAD), _const((1, d)), _const((1, ql)),
                  _const((1, kl)), _const(w_lat.shape), _const(w_glu.shape), _const(w_gate.shape), _const(w_q.shape),
                  _const(w_k.shape), _const(w_v.shape)],
        out_specs=(_rows(tm, wl), _rows(tm, wg), _rows(tm, wt), _rows(tm, hw), _rows(tm, hw), _rows(tm, hw)),
        compiler_params=_params(("arbitrary", "arbitrary")),
    )(x, mod, cos_t, sin_t, g_mix, g_q, g_kv, w_lat, w_glu, w_gate, w_q, w_k, w_v)


def _causal_mask(tq, tk):
    return lax.broadcasted_iota(jnp.int32, (tq, tk), 0) >= lax.broadcasted_iota(jnp.int32, (tq, tk), 1)


def _riding_exchange(kind, n, refs, grid):
    if not n:
        return
    ex = _DirectExchange(kind, refs[:n], refs[n:2 * n], *refs[2 * n:])
    ids = [pl.program_id(a) for a in range(len(grid))]
    first, last = ids[0] == 0, ids[0] == grid[0] - 1
    for a in range(1, len(grid)):
        first, last = jnp.logical_and(first, ids[a] == 0), jnp.logical_and(last, ids[a] == grid[a] - 1)
    pl.when(first)(ex.start)
    return lambda: pl.when(last)(ex.wait)


def _flash_fwd(q, k, v, gather=()):
    b, s, hw = q.shape
    nh, hp = hw // HEAD_PAD, HEAD_PAD
    t = _row_tile(s)
    n = len(gather)
    grid = (b, nh, s // t)

    def body(q_ref, k_ref, v_ref, *rest):
        o_ref, lse_ref = rest[n], rest[n + 1]
        finish = _riding_exchange("gather", n, rest[:n] + rest[n + 2:], grid)
        i = pl.program_id(2)
        qv = q_ref[0]

        def step(j, carry, masked):
            m, acc = carry
            rows = pl.ds(pl.multiple_of(j * t, t), t)
            sc = _dot_nt(qv, k_ref[0, rows, :])
            if masked:
                sc = jnp.where(_causal_mask(t, t), sc, NEG_INF)
            m_new = jnp.maximum(m, jnp.max(sc, axis=-1, keepdims=True))
            p = jnp.exp((sc - m_new).astype(BF16))
            acc = jnp.exp(m - m_new) * acc + _dot(p, v_ref[0, rows, :])
            return m_new, acc

        init = (jnp.full((t, 1), NEG_INF, F32), jnp.zeros((t, hp), F32))
        carry = lax.fori_loop(0, i // 2, lambda jj, cr: step(2 * jj + 1, step(2 * jj, cr, False), False), init)
        carry = lax.cond(i % 2 == 1, lambda cr: step(i - 1, cr, False), lambda cr: cr, carry)
        m, acc = step(i, carry, True)
        lane = lax.broadcasted_iota(jnp.int32, acc.shape, 1)
        l = jnp.sum(jnp.where(lane == V_HEAD, acc, 0.0), axis=-1, keepdims=True)
        o_ref[0] = (acc / l).astype(BF16)
        lse_ref[0, 0] = m + jnp.log(l)
        if finish:
            finish()

    tile = pl.BlockSpec((1, t, hp), lambda bb, hh, ii: (bb, ii, hh))
    full = pl.BlockSpec((1, s, hp), lambda bb, hh, ii: (bb, 0, hh))
    hbm = [pl.BlockSpec(memory_space=pl.ANY)] * n
    outs = pl.pallas_call(
        body, name="flash_fwd_gather" if n else "flash_fwd", grid=grid,
        out_shape=[jax.ShapeDtypeStruct((b, s, hw), BF16), jax.ShapeDtypeStruct((b, nh, s, 1), F32)]
        + _DirectExchange.out_shapes("gather", gather),
        in_specs=[tile, full, full] + hbm,
        out_specs=[tile, pl.BlockSpec((1, 1, t, 1), lambda bb, hh, ii: (bb, hh, ii, 0))] + hbm,
        scratch_shapes=_DirectExchange.scratch(n) if n else [],
        compiler_params=_params(("arbitrary", "arbitrary", "arbitrary")),
    )(q, k, v, *gather)
    return outs[0], outs[1], outs[2:]


def _halo_prev(tm, w):
    r = tm // HALO
    return pl.BlockSpec((1, HALO, w), lambda b, s: (b, jnp.maximum(s * r - 1, 0), 0))


def _halo_next(tm, w, n_tiles):
    r = tm // HALO
    return pl.BlockSpec((1, HALO, w), lambda b, s: (b, jnp.minimum((s + 1) * r, n_tiles * r - 1), 0))


def _conv_rows(cc):
    return max(8, 16 * 8 * LANES // cc)


def _shifted_copies(buf, rows):
    buf[0, pl.ds(rows, 8), :] = jnp.zeros((8, buf.shape[2]), buf.dtype)
    for s in range(1, 8):
        buf[s, pl.ds(0, rows), :] = buf[0, pl.ds(s, rows), :]


def _window(buf, start, rows):
    return buf[start % 8, pl.ds(start - start % 8, rows), :]


def _glu(v, cc):
    a, g = v[:, :cc].astype(F32), v[:, cc:].astype(F32)
    return a * _sig(g)


def _conv_fwd(glu, w_dw, b_dw, g_cn, b_cn, w_pw2):
    b, s, w2 = glu.shape
    cc = w2 // 2
    d = w_pw2.shape[1]
    tm = _row_tile(s)

    rc = _conv_rows(cc)
    te = tm + HALO

    def body(cur_ref, prev_ref, w_ref, bdw_ref, g_ref, bcn_ref, wp_ref, u_ref, y_ref, ext, u_all):
        first = pl.program_id(1) == 0
        ext[0, pl.ds(0, HALO), :] = jnp.where(first, 0.0, _glu(prev_ref[0], cc))
        ext[0, pl.ds(HALO, tm), :] = _glu(cur_ref[0], cc)
        _shifted_copies(ext, te)
        for c0 in range(0, tm, rc):
            acc = jnp.zeros((rc, cc), F32) + bdw_ref[...]
            for kk in range(CONV_W):
                acc = acc + w_ref[pl.ds(kk, 1), :] * _window(ext, c0 + HALO - CONV_W + 1 + kk, rc)
            u_all[pl.ds(c0, rc), :] = acc
        ub = u_all[...].astype(BF16)
        u_ref[0] = ub
        uf = ub.astype(F32)
        mu = jnp.mean(uf, axis=-1, keepdims=True)
        uc = uf - mu
        ln = uc * lax.rsqrt(jnp.mean(uc * uc, axis=-1, keepdims=True) + EPS) * g_ref[...] + bcn_ref[...]
        y_ref[0] = _dot((ln * _sig(ln)).astype(BF16), wp_ref[...]).astype(BF16)

    return pl.pallas_call(
        body, name="conv_fwd", grid=(b, s // tm),
        out_shape=(jax.ShapeDtypeStruct((b, s, cc), BF16), jax.ShapeDtypeStruct((b, s, d), BF16)),
        in_specs=[_rows(tm, w2), _halo_prev(tm, w2), _const(w_dw.shape), _const((1, cc)), _const((1, cc)), _const((1, cc)),
                  _const(w_pw2.shape)],
        out_specs=(_rows(tm, cc), _rows(tm, d)),
        scratch_shapes=[pltpu.VMEM((8, te + 8, cc), F32), pltpu.VMEM((tm, cc), F32)],
        compiler_params=_params(("arbitrary", "arbitrary")),
    )(glu, glu, w_dw, b_dw, g_cn, b_cn, w_pw2)


def _mix_post_fwd(x, mod, o, y_conv, gate, w_o, w_out):
    b, s, d = x.shape
    hw = o.shape[2]
    tm = _row_tile(s)

    def body(x_ref, mod_ref, o_ref, yc_ref, gate_ref, wo_ref, wout_ref, x2_ref, ya_ref, o1_ref):
        yab = _dot(o_ref[0], wo_ref[...]).astype(BF16)
        ya_ref[0] = yab
        gv = gate_ref[0]
        y = _sig(gv[:, :d].astype(F32)) * yab.astype(F32) + _sig(gv[:, d:].astype(F32)) * yc_ref[0].astype(F32)
        o1 = _dot(y.astype(BF16), wout_ref[...])
        o1_ref[0] = o1.astype(BF16)
        x2_ref[0] = x_ref[0] + mod_ref[0, 2] * o1

    return pl.pallas_call(
        body, name="mix_post_fwd", grid=(b, s // tm),
        out_shape=(jax.ShapeDtypeStruct((b, s, d), F32), jax.ShapeDtypeStruct((b, s, d), BF16),
                   jax.ShapeDtypeStruct((b, s, d), BF16)),
        in_specs=[_rows(tm, d), _per_b(6, d), _rows(tm, hw), _rows(tm, d), _rows(tm, 2 * d), _const(w_o.shape),
                  _const(w_out.shape)],
        out_specs=(_rows(tm, d), _rows(tm, d), _rows(tm, d)),
        compiler_params=_params(("arbitrary", "arbitrary")),
    )(x, mod, o, y_conv, gate, w_o, w_out)


def _ffn_fwd(x2, mod, g_ffn, w_gu, w_down):
    b, s, d = x2.shape
    f = w_down.shape[0]
    fc = _pick(f, 512)
    tm = _row_tile(s)

    def body(x_ref, mod_ref, g_ref, wgu_ref, wdn_ref, x3_ref, gu_ref, o2_ref, h_ref):
        xf = x_ref[0]
        hb = ((xf * _rstd(xf) * g_ref[...]) * (1.0 + mod_ref[0, 4]) + mod_ref[0, 3]).astype(BF16)
        h_ref[0] = hb
        o2 = jnp.zeros((tm, d), F32)
        for c0 in range(0, f, fc):
            gb = _dot(hb, wgu_ref[:, c0:c0 + fc]).astype(BF16)
            ub = _dot(hb, wgu_ref[:, f + c0:f + c0 + fc]).astype(BF16)
            gu_ref[0, :, c0:c0 + fc] = gb
            gu_ref[0, :, f + c0:f + c0 + fc] = ub
            gf = gb.astype(F32)
            act = (gf * _sig(gf) * ub.astype(F32)).astype(BF16)
            o2 = o2 + _dot(act, wdn_ref[c0:c0 + fc, :])
        o2_ref[0] = o2.astype(BF16)
        x3_ref[0] = xf + mod_ref[0, 5] * o2

    return pl.pallas_call(
        body, name="ffn_fwd", grid=(b, s // tm),
        out_shape=(jax.ShapeDtypeStruct((b, s, d), F32), jax.ShapeDtypeStruct((b, s, 2 * f), BF16),
                   jax.ShapeDtypeStruct((b, s, d), BF16), jax.ShapeDtypeStruct((b, s, d), BF16)),
        in_specs=[_rows(tm, d), _per_b(6, d), _const((1, d)), _const(w_gu.shape), _const(w_down.shape)],
        out_specs=(_rows(tm, d), _rows(tm, 2 * f), _rows(tm, d), _rows(tm, d)),
        compiler_params=_params(("arbitrary", "arbitrary")),
    )(x2, mod, g_ffn, w_gu, w_down)


def _zero_at_first_tile(*refs):
    @pl.when(pl.program_id(1) == 0)
    def _():
        for ref in refs:
            ref[...] = jnp.zeros_like(ref)


def _accumulate(ref, idx, val):
    ref[idx] = ref[idx] + val


def _colsum(v):
    return jnp.sum(v, axis=0, keepdims=True)


def _loss_bwd(x, target, g_final):
    b, s, d = x.shape
    tm = _row_tile(s)

    def body(x_ref, t_ref, g_ref, dx_ref, loss_ref, dg_ref):
        _zero_at_first_tile(loss_ref, dg_ref)
        xf = x_ref[0]
        r = _rstd(xf)
        xh = xf * r
        diff = xh * g_ref[...] - t_ref[0]
        _accumulate(loss_ref, (0, 0), _colsum(diff * diff))
        dy = diff * (1.0 / d)
        _accumulate(dg_ref, (0, 0), _colsum(dy * xh))
        dyg = dy * g_ref[...]
        dx_ref[0] = r * (dyg - xh * jnp.mean(dyg * xh, axis=-1, keepdims=True))

    return pl.pallas_call(
        body, name="loss_bwd", grid=(b, s // tm),
        out_shape=(jax.ShapeDtypeStruct((b, s, d), F32), jax.ShapeDtypeStruct((b, 1, 1, d), F32),
                   jax.ShapeDtypeStruct((b, 1, 1, d), F32)),
        in_specs=[_rows(tm, d), _rows(tm, d), _const((1, d))],
        out_specs=(_rows(tm, d), _per_b(1, d), _per_b(1, d)),
        compiler_params=_params(("arbitrary", "arbitrary")),
    )(x, target, g_final)


def _ffn_bwd(dx3, x2, mod, g_ffn, gu, o2, w_gu, w_down):
    b, s, d = x2.shape
    f = w_down.shape[0]
    fc = _pick(f, 512)
    tm = _row_tile(s)

    def hidden_body(dx3_ref, mod_ref, gu_ref, o2_ref, wdnt_ref, do2_ref, act_ref, dgu_ref, dgt_ref):
        _zero_at_first_tile(dgt_ref)
        dx3 = dx3_ref[0]
        do2 = (dx3 * mod_ref[0, 5]).astype(BF16)
        do2_ref[0] = do2
        _accumulate(dgt_ref, (0, 0), _colsum(dx3 * o2_ref[0].astype(F32)))
        for c0 in range(0, f, fc):
            gf = gu_ref[0, :, c0:c0 + fc].astype(F32)
            uf = gu_ref[0, :, f + c0:f + c0 + fc].astype(F32)
            sg = _sig(gf)
            silu = gf * sg
            act_ref[0, :, c0:c0 + fc] = (silu * uf).astype(BF16)
            dact = _dot_nt(do2, wdnt_ref[c0:c0 + fc, :])
            dgu_ref[0, :, c0:c0 + fc] = (dact * uf * (sg * (1.0 + gf * (1.0 - sg)))).astype(BF16)
            dgu_ref[0, :, f + c0:f + c0 + fc] = (dact * silu).astype(BF16)

    do2, act, dgu, dgt = pl.pallas_call(
        hidden_body, name="ffn_bwd_hidden", grid=(b, s // tm),
        out_shape=(jax.ShapeDtypeStruct((b, s, d), BF16), jax.ShapeDtypeStruct((b, s, f), BF16),
                   jax.ShapeDtypeStruct((b, s, 2 * f), BF16), jax.ShapeDtypeStruct((b, 1, 1, d), F32)),
        in_specs=[_rows(tm, d), _per_b(6, d), _rows(tm, 2 * f), _rows(tm, d), _const(w_down.shape)],
        out_specs=(_rows(tm, d), _rows(tm, f), _rows(tm, 2 * f), _per_b(1, d)),
        compiler_params=_params(("arbitrary", "arbitrary")),
    )(dx3, mod, gu, o2, w_down)

    def input_body(dx3_ref, x_ref, mod_ref, g_ref, dgu_ref, wgut_ref, dx2_ref, dmod_ref, dg_ref):
        _zero_at_first_tile(dmod_ref, dg_ref)
        sc = mod_ref[0, 4]
        dh = _dot_nt(dgu_ref[0], wgut_ref[...])
        xf = x_ref[0]
        r = _rstd(xf)
        xh = xf * r
        n = xh * g_ref[...]
        _accumulate(dmod_ref, (0, 0), _colsum(dh))
        _accumulate(dmod_ref, (0, 1), _colsum(dh * n))
        dn = dh * (1.0 + sc)
        _accumulate(dg_ref, (0, 0), _colsum(dn * xh))
        dyg = dn * g_ref[...]
        dx2_ref[0] = dx3_ref[0] + r * (dyg - xh * jnp.mean(dyg * xh, axis=-1, keepdims=True))

    dx2, dmod_sc, dg = pl.pallas_call(
        input_body, name="ffn_bwd_input", grid=(b, s // tm),
        out_shape=(jax.ShapeDtypeStruct((b, s, d), F32), jax.ShapeDtypeStruct((b, 2, 1, d), F32),
                   jax.ShapeDtypeStruct((b, 1, 1, d), F32)),
        in_specs=[_rows(tm, d), _rows(tm, d), _per_b(6, d), _const((1, d)), _rows(tm, 2 * f), _const(w_gu.shape)],
        out_specs=(_rows(tm, d), _per_b(2, d), _per_b(1, d)),
        compiler_params=_params(("arbitrary", "arbitrary")),
    )(dx3, x2, mod, g_ffn, dgu, w_gu)
    return dx2, do2, act, dgu, jnp.concatenate([dmod_sc, dgt], axis=1), dg


def _mix_post_bwd(dx2, mod, o1, y_attn, y_conv, gate, w_out_t, w_o_t):
    b, s, d = dx2.shape
    hw = w_o_t.shape[0]
    tm = _row_tile(s)

    def body(dx_ref, mod_ref, o1_ref, ya_ref, yc_ref, gate_ref, woutt_ref, wot_ref, do1_ref, dya_ref, dyc_ref, dgate_ref,
             y_ref, do_ref, dgt_ref):
        _zero_at_first_tile(dgt_ref)
        dx = dx_ref[0]
        do1 = (dx * mod_ref[0, 2]).astype(BF16)
        do1_ref[0] = do1
        _accumulate(dgt_ref, (0, 0), _colsum(dx * o1_ref[0].astype(F32)))
        dy = _dot_nt(do1, woutt_ref[...])
        gv = gate_ref[0]
        sa, sb = _sig(gv[:, :d].astype(F32)), _sig(gv[:, d:].astype(F32))
        ya, yc = ya_ref[0].astype(F32), yc_ref[0].astype(F32)
        y_ref[0] = (sa * ya + sb * yc).astype(BF16)
        dya = (dy * sa).astype(BF16)
        dya_ref[0] = dya
        dyc_ref[0] = (dy * sb).astype(BF16)
        dgate_ref[0, :, :d] = (dy * ya * sa * (1.0 - sa)).astype(BF16)
        dgate_ref[0, :, d:] = (dy * yc * sb * (1.0 - sb)).astype(BF16)
        do_ref[0] = _dot_nt(dya, wot_ref[...]).astype(BF16)

    row = jax.ShapeDtypeStruct((b, s, d), BF16)
    return pl.pallas_call(
        body, name="mix_post_bwd", grid=(b, s // tm),
        out_shape=(row, row, row, jax.ShapeDtypeStruct((b, s, 2 * d), BF16), row,
                   jax.ShapeDtypeStruct((b, s, hw), BF16), jax.ShapeDtypeStruct((b, 1, 1, d), F32)),
        in_specs=[_rows(tm, d), _per_b(6, d), _rows(tm, d), _rows(tm, d), _rows(tm, d), _rows(tm, 2 * d),
                  _const(w_out_t.shape), _const(w_o_t.shape)],
        out_specs=(_rows(tm, d), _rows(tm, d), _rows(tm, d), _rows(tm, 2 * d), _rows(tm, d), _rows(tm, hw), _per_b(1, d)),
        compiler_params=_params(("arbitrary", "arbitrary")),
    )(dx2, mod, o1, y_attn, y_conv, gate, w_out_t, w_o_t)


def _conv_bwd(dyc, u, glu, w_dw, g_cn, b_cn, w_pw2_t):
    b, s, cc = u.shape
    d = dyc.shape[2]
    tm = _row_tile(s)
    nt = s // tm
    te = tm + HALO
    rc = _conv_rows(cc)

    def body(dyc_ref, dycn_ref, u_ref, un_ref, glu_ref, glup_ref, w_ref, g_ref, bcn_ref, wpt_ref, dglu_ref, s_ref, dw_ref,
             small_ref, du_ext, uin_ext, duin_all):
        _zero_at_first_tile(dw_ref, small_ref)
        st = pl.program_id(1)
        dy_all = jnp.concatenate([dyc_ref[0], dycn_ref[0]], axis=0)
        u_all = jnp.concatenate([u_ref[0], un_ref[0]], axis=0).astype(F32)
        ds = _dot_nt(dy_all, wpt_ref[...])
        mu = jnp.mean(u_all, axis=-1, keepdims=True)
        uc = u_all - mu
        rstd = lax.rsqrt(jnp.mean(uc * uc, axis=-1, keepdims=True) + EPS)
        uh = uc * rstd
        ln = uh * g_ref[...] + bcn_ref[...]
        sg = _sig(ln)
        s_ref[0] = (ln * sg)[:tm].astype(BF16)
        dln = ds * (sg * (1.0 + ln * (1.0 - sg)))
        duh = dln * g_ref[...]
        du = rstd * (duh - jnp.mean(duh, axis=-1, keepdims=True) - uh * jnp.mean(duh * uh, axis=-1, keepdims=True))
        row = lax.broadcasted_iota(jnp.int32, (te, 1), 0)
        du = jnp.where(jnp.logical_and(st == nt - 1, row >= tm), 0.0, du)
        du_ext[0, pl.ds(0, te), :] = du
        _shifted_copies(du_ext, te)
        du_cur = du[:tm]
        _accumulate(small_ref, (0, 0), _colsum((dln * uh)[:tm]))
        _accumulate(small_ref, (0, 1), _colsum(dln[:tm]))
        _accumulate(small_ref, (0, 2), _colsum(du_cur))
        uin_ext[0, pl.ds(0, HALO), :] = jnp.where(st == 0, 0.0, _glu(glup_ref[0], cc))
        gv = glu_ref[0]
        ga, gb = gv[:, :cc].astype(F32), gv[:, cc:].astype(F32)
        sgb = _sig(gb)
        uin_ext[0, pl.ds(HALO, tm), :] = ga * sgb
        _shifted_copies(uin_ext, te)
        for c0 in range(0, tm, rc):
            du_c = du_ext[0, pl.ds(c0, rc), :]
            acc = jnp.zeros((rc, cc), F32)
            for kk in range(CONV_W):
                acc = acc + w_ref[pl.ds(kk, 1), :] * _window(du_ext, c0 + CONV_W - 1 - kk, rc)
                prod = du_c * _window(uin_ext, c0 + HALO - CONV_W + 1 + kk, rc)
                taps = pl.ds(8 * kk, 8)
                dw_ref[0, taps, :] = dw_ref[0, taps, :] + jnp.sum(prod.reshape(rc // 8, 8, cc), axis=0)
            duin_all[pl.ds(c0, rc), :] = acc
        duin = duin_all[...]
        dglu_ref[0, :, :cc] = (duin * sgb).astype(BF16)
        dglu_ref[0, :, cc:] = (duin * ga * sgb * (1.0 - sgb)).astype(BF16)

    return pl.pallas_call(
        body, name="conv_bwd", grid=(b, nt),
        out_shape=(jax.ShapeDtypeStruct((b, s, 2 * cc), BF16), jax.ShapeDtypeStruct((b, s, cc), BF16),
                   jax.ShapeDtypeStruct((b, 8 * HALO, cc), F32), jax.ShapeDtypeStruct((b, 3, 1, cc), F32)),
        in_specs=[_rows(tm, d), _halo_next(tm, d, nt), _rows(tm, cc), _halo_next(tm, cc, nt), _rows(tm, 2 * cc),
                  _halo_prev(tm, 2 * cc), _const(w_dw.shape), _const((1, cc)), _const((1, cc)), _const(w_pw2_t.shape)],
        out_specs=(_rows(tm, 2 * cc), _rows(tm, cc), pl.BlockSpec((1, 8 * HALO, cc), lambda i, j: (i, 0, 0)),
                   _per_b(3, cc)),
        scratch_shapes=[pltpu.VMEM((8, te + 8, cc), F32), pltpu.VMEM((8, te + 8, cc), F32), pltpu.VMEM((tm, cc), F32)],
        compiler_params=_params(("arbitrary", "arbitrary")),
    )(dyc, dyc, u, u, glu, glu, w_dw, g_cn, b_cn, w_pw2_t)


def _flash_bwd(q, k, v, o, lse, do, scatter=()):
    b, s, hw = q.shape
    nh, hp = hw // HEAD_PAD, HEAD_PAD
    t = _row_tile(s)
    nt = s // t
    n = len(scatter)
    grid = (b, nh, nt)

    def to_row(sel, cols):
        hi = cols.astype(BF16)
        lo = (cols - hi.astype(F32)).astype(BF16)
        return _dot_nt(sel, hi) + _dot_nt(sel, lo)

    def body(q_ref, k_ref, v_ref, o_ref, lse_ref, do_ref, *rest):
        dqt_ref, dk_ref, dv_ref = rest[n:n + 3]
        lse_row, delta_row = rest[2 * n + 3:2 * n + 5]
        finish = _riding_exchange("scatter", n, rest[:n] + rest[n + 3:2 * n + 3] + rest[2 * n + 5:], grid)
        j = pl.program_id(2)

        @pl.when(j == 0)
        def _():
            dqt_ref[...] = jnp.zeros_like(dqt_ref)
            first_lane = (lax.broadcasted_iota(jnp.int32, (8, hp), 1) == 0).astype(BF16)
            for i in range(nt):
                rows = pl.ds(i * t, t)
                prod = do_ref[0, rows, :].astype(F32) * o_ref[0, rows, :].astype(F32)
                delta_row[i] = to_row(jnp.ones((8, hp), BF16), prod)
                lse_row[i] = to_row(first_lane, jnp.broadcast_to(lse_ref[0, 0, rows, :], (t, hp)))

        kv, vv = k_ref[0], v_ref[0]
        kt = kv.T
        query_not_before_key = (lax.broadcasted_iota(jnp.int32, (t, t), 1) >= lax.broadcasted_iota(jnp.int32, (t, t), 0))

        def step(i, carry, masked):
            dk, dv = carry
            rows = pl.ds(pl.multiple_of(i * t, t), t)
            qv, dov = q_ref[0, rows, :], do_ref[0, rows, :]
            pt = jnp.exp((_dot_nt(kv, qv) - lse_row[i, 0:1, :]).astype(BF16))
            if masked:
                pt = jnp.where(query_not_before_key, pt, jnp.zeros((), BF16))
            dv = dv + _dot(pt, dov)
            dst = pt * (_dot_nt(vv, dov) - delta_row[i, 0:1, :]).astype(BF16)
            dk = dk + _dot(dst, qv)
            dqt_ref[0, i] = dqt_ref[0, i] + _dot(kt, dst)
            return dk, dv

        carry = step(j, (jnp.zeros((t, hp), F32), jnp.zeros((t, hp), F32)), True)
        pairs = (nt - 1 - j) // 2
        carry = lax.fori_loop(0, pairs, lambda ii, cr: step(j + 2 + 2 * ii, step(j + 1 + 2 * ii, cr, False), False), carry)
        dk, dv = lax.cond((nt - 1 - j) % 2 == 1, lambda cr: step(nt - 1, cr, False), lambda cr: cr, carry)
        dk_ref[0] = dk.astype(BF16)
        dv_ref[0] = dv.astype(BF16)
        if finish:
            finish()

    tile = pl.BlockSpec((1, t, hp), lambda bb, hh, jj: (bb, jj, hh))
    full = pl.BlockSpec((1, s, hp), lambda bb, hh, jj: (bb, 0, hh))
    hbm = [pl.BlockSpec(memory_space=pl.ANY)] * n
    outs = pl.pallas_call(
        body, name="flash_bwd_scatter" if n else "flash_bwd", grid=grid,
        out_shape=[jax.ShapeDtypeStruct((b, nt, hw, t), F32), jax.ShapeDtypeStruct((b, s, hw), BF16),
                   jax.ShapeDtypeStruct((b, s, hw), BF16)] + _DirectExchange.out_shapes("scatter", scatter),
        in_specs=[full, tile, tile, full, pl.BlockSpec((1, 1, s, 1), lambda bb, hh, jj: (bb, hh, 0, 0)), full] + hbm,
        out_specs=[pl.BlockSpec((1, nt, hp, t), lambda bb, hh, jj: (bb, 0, hh, 0)), tile, tile] + hbm,
        scratch_shapes=[pltpu.VMEM((nt, 8, t), F32), pltpu.VMEM((nt, 8, t), F32)]
        + (_DirectExchange.scratch(n) if n else []),
        compiler_params=_params(("arbitrary", "arbitrary", "arbitrary")),
    )(q, k, v, o, lse, do, *scatter)
    return outs[0], outs[1], outs[2], outs[3:]


def _mix_pre_bwd(x, dx2, mod, cos_t, sin_t, g_mix, g_q, g_kv, lat, dq, dk, dv, dglu, dgate, w_lat_t, w_glu_t, w_gate_t,
                 w_q_t, w_k_t, w_v_t):
    b, s, d = x.shape
    ql, kl = g_q.shape[1], g_kv.shape[1]
    wl = lat.shape[2]
    hw = N_HEADS * HEAD_PAD
    tm = _row_tile(s)

    def body(x_ref, dx2_ref, mod_ref, cos_ref, sin_ref, gm_ref, gq_ref, gkv_ref, lat_ref, dq_ref, dk_ref, dv_ref, dglu_ref,
             dgate_ref, wlt_ref, wgt_ref, wtt_ref, wqt_ref, wkt_ref, wvt_ref, dx_ref, dlat_ref, dqr_ref, qn_ref, kvn_ref,
             h_ref, dmod_ref, dgm_ref, dgq_ref, dgkv_ref):
        _zero_at_first_tile(dmod_ref, dgm_ref, dgq_ref, dgkv_ref)
        cos_v, sin_v = cos_ref[0], sin_ref[0]
        latf = lat_ref[0].astype(F32)
        q_lat, kv_lat = latf[:, :ql], latf[:, ql:ql + kl]
        rq, rk = _rstd(q_lat), _rstd(kv_lat)
        qh, kh = q_lat * rq, kv_lat * rk
        qn_ref[0] = (qh * gq_ref[...]).astype(BF16)
        kvn_ref[0] = (kh * gkv_ref[...]).astype(BF16)
        dk_sum = jnp.zeros((tm, HEAD_PAD), F32)
        for h in range(N_HEADS):
            cols = slice(h * HEAD_PAD, (h + 1) * HEAD_PAD)
            dq_head = dq_ref[0, 0, cols, :].T
            dqr_ref[0, :, cols] = _rope_bwd(dq_head * ATTN_SCALE, cos_v, sin_v).astype(BF16)
            dk_sum = dk_sum + dk_ref[0, :, cols].astype(F32)
        dqn = _dot_nt(dqr_ref[0], wqt_ref[...])
        dkvn = _dot_nt(dk_ref[0], wkt_ref[...]) + _dot_nt(dv_ref[0], wvt_ref[...])
        lane = lax.broadcasted_iota(jnp.int32, dk_sum.shape, 1)
        dkr = _rope_bwd(jnp.where(lane >= QK_NOPE, dk_sum, 0.0), cos_v, sin_v)
        _accumulate(dgq_ref, (0, 0), _colsum(dqn * qh))
        _accumulate(dgkv_ref, (0, 0), _colsum(dkvn * kh))
        dqg, dkg = dqn * gq_ref[...], dkvn * gkv_ref[...]
        dlat_ref[0, :, :ql] = (rq * (dqg - qh * jnp.mean(dqg * qh, axis=-1, keepdims=True))).astype(BF16)
        dlat_ref[0, :, ql:ql + kl] = (rk * (dkg - kh * jnp.mean(dkg * kh, axis=-1, keepdims=True))).astype(BF16)
        dlat_ref[0, :, ql + kl:] = dkr.astype(BF16)
        dh = (_dot_nt(dlat_ref[0], wlt_ref[...]) + _dot_nt(dglu_ref[0], wgt_ref[...])
              + _dot_nt(dgate_ref[0], wtt_ref[...]))
        sh, sc = mod_ref[0, 0], mod_ref[0, 1]
        xf = x_ref[0]
        r = _rstd(xf)
        xh = xf * r
        n = xh * gm_ref[...]
        h_ref[0] = (n * (1.0 + sc) + sh).astype(BF16)
        _accumulate(dmod_ref, (0, 0), _colsum(dh))
        _accumulate(dmod_ref, (0, 1), _colsum(dh * n))
        dn = dh * (1.0 + sc)
        _accumulate(dgm_ref, (0, 0), _colsum(dn * xh))
        dyg = dn * gm_ref[...]
        dx_ref[0] = dx2_ref[0] + r * (dyg - xh * jnp.mean(dyg * xh, axis=-1, keepdims=True))

    return pl.pallas_call(
        body, name="mix_pre_bwd", grid=(b, s // tm),
        out_shape=(jax.ShapeDtypeStruct((b, s, d), F32), jax.ShapeDtypeStruct((b, s, wl), BF16),
                   jax.ShapeDtypeStruct((b, s, hw), BF16), jax.ShapeDtypeStruct((b, s, ql), BF16),
                   jax.ShapeDtypeStruct((b, s, kl), BF16), jax.ShapeDtypeStruct((b, s, d), BF16),
                   jax.ShapeDtypeStruct((b, 2, 1, d), F32), jax.ShapeDtypeStruct((b, 1, 1, d), F32),
                   jax.ShapeDtypeStruct((b, 1, 1, ql), F32), jax.ShapeDtypeStruct((b, 1, 1, kl), F32)),
        in_specs=[_rows(tm, d), _rows(tm, d), _per_b(6, d), _rows(tm, HEAD_PAD), _rows(tm, HEAD_PAD), _const((1, d)),
                  _const((1, ql)), _const((1, kl)), _rows(tm, wl),
                  pl.BlockSpec((1, 1, hw, tm), lambda b, s: (b, s, 0, 0)), _rows(tm, hw), _rows(tm, hw),
                  _rows(tm, dglu.shape[2]), _rows(tm, 2 * d), _const(w_lat_t.shape), _const(w_glu_t.shape),
                  _const(w_gate_t.shape), _const(w_q_t.shape), _const(w_k_t.shape), _const(w_v_t.shape)],
        out_specs=(_rows(tm, d), _rows(tm, wl), _rows(tm, hw), _rows(tm, ql), _rows(tm, kl), _rows(tm, d), _per_b(2, d),
                   _per_b(1, d), _per_b(1, ql), _per_b(1, kl)),
        compiler_params=_params(("arbitrary", "arbitrary")),
    )(x, dx2, mod, cos_t, sin_t, g_mix, g_q, g_kv, lat, dq, dk, dv, dglu, dgate, w_lat_t, w_glu_t, w_gate_t, w_q_t, w_k_t,
      w_v_t)


def _matmul_tn(a, bm, name):
    b, s, kd = a.shape
    nd = bm.shape[2]
    tk, tn = _pick(kd, 1536), _pick(nd, 1536)
    ts = _row_tile(s, 2048)

    def body(a_ref, b_ref, o_ref):
        part = _dot_tn(a_ref[0], b_ref[0])
        first = jnp.logical_and(pl.program_id(2) == 0, pl.program_id(3) == 0)

        @pl.when(first)
        def _():
            o_ref[...] = part

        @pl.when(jnp.logical_not(first))
        def _():
            o_ref[...] = o_ref[...] + part

    return pl.pallas_call(
        body, name=name, grid=(kd // tk, nd // tn, b, s // ts),
        out_shape=jax.ShapeDtypeStruct((kd, nd), F32),
        in_specs=[pl.BlockSpec((1, ts, tk), lambda i, j, bb, ss: (bb, ss, i)),
                  pl.BlockSpec((1, ts, tn), lambda i, j, bb, ss: (bb, ss, j))],
        out_specs=pl.BlockSpec((tk, tn), lambda i, j, bb, ss: (i, j)),
        compiler_params=_params(("arbitrary",) * 4),
    )(a, bm)


def _adamw_update(w, g, m, v):
    nm = ADAM_B1 * m + (1.0 - ADAM_B1) * g
    nv = ADAM_B2 * v + (1.0 - ADAM_B2) * (g * g)
    delta = -ADAM_LR * ((nm / (1.0 - ADAM_B1 ** ADAM_STEP)) / (jnp.sqrt(nv / (1.0 - ADAM_B2 ** ADAM_STEP)) + ADAM_EPS)
                        + ADAM_WD * w)
    return delta, nm, nv


def _adamw(w, g, m, v, name):
    shape = w.shape
    cols = shape[-1]
    rows = w.size // cols
    w2, g2, m2, v2 = (t.reshape(rows, cols) for t in (w, g, m, v))
    tr = rows
    if rows * cols * 4 > (1 << 20):
        tr = _div_tile(rows, max(8, (1 << 18) // cols), 8)

    def body(w_ref, g_ref, m_ref, v_ref, d_ref, nm_ref, nv_ref):
        d_ref[...], nm_ref[...], nv_ref[...] = _adamw_update(w_ref[...], g_ref[...], m_ref[...], v_ref[...])

    spec = pl.BlockSpec((tr, cols), lambda i: (i, 0))
    outs = pl.pallas_call(
        body, name=name, grid=(rows // tr,), out_shape=(jax.ShapeDtypeStruct((rows, cols), F32),) * 3,
        in_specs=[spec] * 4, out_specs=(spec,) * 3, compiler_params=_params(("arbitrary",)),
    )(w2, g2, m2, v2)
    return tuple(t.reshape(shape) for t in outs)


def _adamw_reduce(w, m, v, own, got, name, scatter=()):
    shape = w.shape
    cols = shape[-1]
    w2, m2, v2, o2 = (_rows_2d(t, 0) for t in (w, m, v, own))
    g3 = _rows_2d(got, N_DEV - 1)
    rows = w2.shape[0]
    tr = _grad_row_tile(rows, cols)
    n = len(scatter)
    grid = (rows // tr,)

    def body(w_ref, m_ref, v_ref, o_ref, r_ref, *rest):
        g_ref, d_ref, nm_ref, nv_ref = rest[n:n + 4]
        finish = _riding_exchange("scatter", n, rest[:n] + rest[n + 4:], grid)
        g = o_ref[...].astype(F32)
        for r in range(N_DEV - 1):
            g = g + r_ref[r].astype(F32)
        g_ref[...] = g
        d_ref[...], nm_ref[...], nv_ref[...] = _adamw_update(w_ref[...], g, m_ref[...], v_ref[...])
        if finish:
            finish()

    spec = pl.BlockSpec((tr, cols), lambda i: (i, 0))
    hbm = [pl.BlockSpec(memory_space=pl.ANY)] * n
    outs = pl.pallas_call(
        body, name=name, grid=grid,
        in_specs=[spec, spec, spec, spec, pl.BlockSpec((N_DEV - 1, tr, cols), lambda i: (0, i, 0))] + hbm,
        out_specs=[spec] * 4 + hbm,
        out_shape=[jax.ShapeDtypeStruct((rows, cols), F32)] * 4 + _DirectExchange.out_shapes("scatter", scatter),
        scratch_shapes=_DirectExchange.scratch(n) if n else [],
        compiler_params=_params(("arbitrary",)),
    )(w2, m2, v2, o2, g3, *scatter)
    return tuple(t.reshape(shape) for t in outs[:4]), outs[4:]


GATHERED = (("w_in", 2), ("w_uq", 2), ("w_ukv", 2), ("w_o_attn", 2), ("w_pw2", 2), ("w_out", 1), ("w_gu", 2), ("w_down", 1))
PRE, POST = (0, 1, 2), (3, 4, 5, 6, 7)


def _from_chunks(chunks, axis):
    _, a, bb = chunks.shape
    if axis == 2:
        return jnp.transpose(chunks, (1, 0, 2)).reshape(a, N_DEV * bb)
    return chunks.reshape(N_DEV * a, bb)


def _to_chunks(full, axis):
    a, bb = full.shape
    if axis == 2:
        return jnp.transpose(full.reshape(a, N_DEV, bb // N_DEV), (1, 0, 2)).astype(BF16)
    return full.reshape(N_DEV, a // N_DEV, bb).astype(BF16)


def _swap_halves(t):
    half = QK_ROPE // 2
    return jnp.concatenate([t[..., half:], t[..., :half]], axis=-1)


def _pad_rows(t, mult=8):
    return jnp.pad(t, ((0, -t.shape[0] % mult), (0, 0)))


def _pad_last(t, width):
    return jnp.pad(t, ((0, 0),) * (t.ndim - 1) + ((0, width - t.shape[-1]),))


def kernel(x, c, positions, w_ada, b_ada, g_mix, w_in, g_q, w_uq, g_kv, w_ukv, w_o_attn, w_dw, b_dw, g_cn, b_cn, w_pw2, w_out, g_ffn, w_gu, w_down, g_final, loss_target, m_w_ada, m_b_ada, m_g_mix, m_w_in, m_g_q, m_w_uq, m_g_kv, m_w_ukv, m_w_o_attn, m_w_dw, m_b_dw, m_g_cn, m_b_cn, m_w_pw2, m_w_out, m_g_ffn, m_w_gu, m_w_down, m_g_final, v_w_ada, v_b_ada, v_g_mix, v_w_in, v_g_q, v_w_uq, v_g_kv, v_w_ukv, v_w_o_attn, v_w_dw, v_b_dw, v_g_cn, v_b_cn, v_w_pw2, v_w_out, v_g_ffn, v_w_gu, v_w_down, v_g_final):
    weights = dict(w_ada=w_ada, b_ada=b_ada, g_mix=g_mix, w_in=w_in, g_q=g_q, w_uq=w_uq, g_kv=g_kv, w_ukv=w_ukv,
                   w_o_attn=w_o_attn, w_dw=w_dw, b_dw=b_dw, g_cn=g_cn, b_cn=b_cn, w_pw2=w_pw2, w_out=w_out, g_ffn=g_ffn,
                   w_gu=w_gu, w_down=w_down, g_final=g_final)
    mom_m = dict(w_ada=m_w_ada, b_ada=m_b_ada, g_mix=m_g_mix, w_in=m_w_in, g_q=m_g_q, w_uq=m_w_uq, g_kv=m_g_kv,
                 w_ukv=m_w_ukv, w_o_attn=m_w_o_attn, w_dw=m_w_dw, b_dw=m_b_dw, g_cn=m_g_cn, b_cn=m_b_cn, w_pw2=m_w_pw2,
                 w_out=m_w_out, g_ffn=m_g_ffn, w_gu=m_w_gu, w_down=m_w_down, g_final=m_g_final)
    mom_v = dict(w_ada=v_w_ada, b_ada=v_b_ada, g_mix=v_g_mix, w_in=v_w_in, g_q=v_g_q, w_uq=v_w_uq, g_kv=v_g_kv,
                 w_ukv=v_w_ukv, w_o_attn=v_w_o_attn, w_dw=v_w_dw, b_dw=v_b_dw, g_cn=v_g_cn, b_cn=v_b_cn, w_pw2=v_w_pw2,
                 w_out=v_w_out, g_ffn=v_g_ffn, w_gu=v_w_gu, w_down=v_w_down, g_final=v_g_final)
    order = list(weights)

    nb, s, d = x.shape
    nl = w_in.shape[0]
    ql, kl, cc = g_q.shape[1], g_kv.shape[1], g_cn.shape[1]
    h = N_HEADS
    qk = QK_NOPE + QK_ROPE
    xi, yi, ci = lax.axis_index("x"), lax.axis_index("y"), lax.axis_index("c")
    me = 4 * xi + 2 * yi + ci

    shards = [weights[n].astype(BF16) for n, _ in GATHERED]
    gathered_pre = _all_gather([shards[a][0] for a in PRE], vmem=False, name="weight_all_gather")
    n_dw = w_dw.shape[2]
    dw_rows = jnp.pad(w_dw, ((0, 0), (0, HALO - CONV_W), (0, LANES - n_dw))).reshape(nl * HALO, LANES)
    c_all, dw_all = _all_gather([_pad_rows(c), dw_rows], vmem=True, name="cond_all_gather")
    c_full = c_all[:, :nb].reshape(N_DEV * nb, d)
    w_dw_full = jnp.transpose(dw_all.reshape(N_DEV, nl, HALO, LANES)[..., :n_dw], (1, 2, 0, 3)).reshape(nl, HALO, cc)

    o_kr, o_glu, o_gate = ql + kl, ql + kl + QK_ROPE, ql + kl + QK_ROPE + 2 * cc

    def layouts_pre(chunks):
        wi, w_uq_l, w_ukv_l = (_from_chunks(t, GATHERED[a][1]) for a, t in zip(PRE, chunks))
        w_kr = wi[:, o_kr:o_glu]
        wq = w_uq_l.reshape(ql, h, qk)
        wkv = w_ukv_l.reshape(kl, h, QK_NOPE + V_HEAD)
        return dict(
            lat=jnp.concatenate([wi[:, :o_kr], jnp.zeros((d, QK_NOPE), BF16), w_kr, _swap_halves(w_kr)], axis=1),
            glu=wi[:, o_glu:o_gate], gate=wi[:, o_gate:],
            q=jnp.concatenate([wq, _swap_halves(wq[..., QK_NOPE:])], axis=-1).reshape(ql, h * HEAD_PAD),
            k=_pad_last(wkv[..., :QK_NOPE], HEAD_PAD).reshape(kl, h * HEAD_PAD),
            v=_pad_last(wkv[..., QK_NOPE:], HEAD_PAD).reshape(kl, h * HEAD_PAD))

    def layouts_post(chunks):
        w_o_l, w_pw_l, w_out_l, w_gu_l, w_down_l = (_from_chunks(t, GATHERED[a][1]) for a, t in zip(POST, chunks))
        w_o = jnp.pad(w_o_l.reshape(h, V_HEAD, d), ((0, 0), (0, HEAD_PAD - V_HEAD), (0, 0)))
        return dict(o=w_o.reshape(h * HEAD_PAD, d), pw=w_pw_l, out=w_out_l, gu=w_gu_l, down=w_down_l)

    n_ada = w_ada.shape[2]
    b_cols = lax.dynamic_slice_in_dim(b_ada, me * n_ada, n_ada, axis=1).reshape(nl, 1, n_ada)
    mod_part = _ada_fwd(c_full, w_ada, b_cols)
    (mod_all,) = _all_gather([mod_part.reshape(nl * N_DEV * nb, n_ada)], vmem=True, name="mod_all_gather")
    mod_all = jnp.transpose(mod_all.reshape(N_DEV, nl, N_DEV * nb, n_ada), (1, 2, 0, 3)).reshape(nl, N_DEV * nb, 6 * d)
    mod = lax.dynamic_slice_in_dim(mod_all, me * nb, nb, axis=1).reshape(nl, nb, 6, 1, d)

    inv_freq = ROPE_THETA ** (-jnp.arange(0, QK_ROPE, 2, dtype=F32) / QK_ROPE)
    zeros = lambda n: jnp.zeros((n,), F32)
    freq_row = jnp.concatenate([zeros(QK_NOPE), inv_freq, inv_freq, zeros(HEAD_PAD - qk)]).reshape(1, -1)
    ones = jnp.ones((QK_ROPE // 2,), F32)
    sign_row = jnp.concatenate([zeros(QK_NOPE), -ones, ones, zeros(HEAD_PAD - qk)]).reshape(1, -1)
    cos_t, sin_t = _rope_tables(positions.astype(F32).reshape(nb, s, 1), freq_row, sign_row)

    row = lambda t, l: t[l].reshape(1, -1)

    saved, wts = [], []
    xc = x
    for l in range(nl):
        wl = layouts_pre(gathered_pre)
        lat, glu, gate, qh, kh, vh = _mix_pre_fwd(xc, mod[l], cos_t, sin_t, row(g_mix, l), row(g_q, l), row(g_kv, l),
                                                  wl["lat"], wl["glu"], wl["gate"], wl["q"], wl["k"], wl["v"])
        riders = [shards[a][l] for a in POST] + ([shards[a][l + 1] for a in PRE] if l + 1 < nl else [])
        o, lse, arrived = _flash_fwd(qh, kh, vh, gather=riders)
        wl.update(layouts_post(arrived[:len(POST)]))
        gathered_pre = arrived[len(POST):]
        wts.append(wl)
        u, y_conv = _conv_fwd(glu, w_dw_full[l], row(b_dw, l), row(g_cn, l), row(b_cn, l), wl["pw"])
        x2, y_attn, o1 = _mix_post_fwd(xc, mod[l], o, y_conv, gate, wl["o"], wl["out"])
        x3, gu, o2, h2 = _ffn_fwd(x2, mod[l], row(g_ffn, l), wl["gu"], wl["down"])
        saved.append(dict(x=xc, lat=lat, glu=glu, gate=gate, q=qh, k=kh, v=vh, o=o, lse=lse, u=u, y_conv=y_conv, x2=x2,
                          y_attn=y_attn, o1=o1, gu=gu, o2=o2, h2=h2))
        xc = x3

    dx, loss_part, dgf_part = _loss_bwd(xc, loss_target, g_final.reshape(1, d))
    loss = lax.psum(0.5 / d * jnp.sum(loss_part), AXES)

    small_rows, dw_taps, dmod = [None] * nl, [None] * nl, [None] * nl
    own, got = {}, {}
    chunk = lambda g, a: _to_chunks(g[GATHERED[a][0]], GATHERED[a][1])
    mine = lambda t: lax.dynamic_index_in_dim(t, me, axis=0, keepdims=False)
    pending = []
    for l in reversed(range(nl)):
        sv, wl, gw = saved[l], wts[l], {}
        dx2, do2, act, dgu, dmod2, dgffn = _ffn_bwd(dx, sv["x2"], mod[l], row(g_ffn, l), sv["gu"], sv["o2"],
                                                   wl["gu"], wl["down"])
        gw["w_gu"] = _matmul_tn(sv["h2"], dgu, "grad_w_gu")
        gw["w_down"] = _matmul_tn(act, do2, "grad_w_down")
        do1, dya, dyc, dgate, yv, do_h, dgt1 = _mix_post_bwd(dx2, mod[l], sv["o1"], sv["y_attn"], sv["y_conv"], sv["gate"],
                                                           wl["out"], wl["o"])
        gw["w_out"] = _matmul_tn(yv, do1, "grad_w_out")
        dwo = _matmul_tn(sv["o"], dya, "grad_w_o")
        gw["w_o_attn"] = dwo.reshape(h, HEAD_PAD, d)[:, :V_HEAD].reshape(h * V_HEAD, d)
        dglu, s_act, ddw, csmall = _conv_bwd(dyc, sv["u"], sv["glu"], w_dw_full[l], row(g_cn, l), row(b_cn, l), wl["pw"])
        gw["w_pw2"] = _matmul_tn(s_act, dyc, "grad_w_pw2")
        ready = [chunk(gw, a) for a in POST]
        dq, dk, dv, arrived = _flash_bwd(sv["q"], sv["k"], sv["v"], sv["o"], sv["lse"], do_h, scatter=ready + pending)
        for i, a in enumerate(POST):
            own[l, a], got[l, a] = mine(ready[i]), arrived[i]
        for i, a in enumerate(PRE if pending else ()):
            got[l + 1, a] = arrived[len(POST) + i]
        dx, dlat, dqr, qn, kvn, h1, dmod1, dgm, dgq, dgkv = _mix_pre_bwd(
            sv["x"], dx2, mod[l], cos_t, sin_t, row(g_mix, l), row(g_q, l), row(g_kv, l), sv["lat"], dq, dk, dv, dglu,
            dgate, wl["lat"], wl["glu"], wl["gate"], wl["q"], wl["k"], wl["v"])
        dwl = _matmul_tn(h1, dlat, "grad_w_lat")
        dwg = _matmul_tn(h1, dglu, "grad_w_glu")
        dwt = _matmul_tn(h1, dgate, "grad_w_gate")
        kr0 = o_kr + QK_NOPE
        dkr = dwl[:, kr0:kr0 + QK_ROPE] + _swap_halves(dwl[:, kr0 + QK_ROPE:])
        gw["w_in"] = jnp.concatenate([dwl[:, :o_kr], dkr, dwg, dwt], axis=1)
        dwq = _matmul_tn(qn, dqr, "grad_w_q").reshape(ql, h, HEAD_PAD)
        dwq = jnp.concatenate([dwq[..., :QK_NOPE], dwq[..., QK_NOPE:qk] + _swap_halves(dwq[..., qk:])], axis=-1)
        gw["w_uq"] = dwq.reshape(ql, h * qk)
        dwk = _matmul_tn(kvn, dk, "grad_w_k").reshape(kl, h, HEAD_PAD)
        dwv = _matmul_tn(kvn, dv, "grad_w_v").reshape(kl, h, HEAD_PAD)
        gw["w_ukv"] = jnp.concatenate([dwk[..., :QK_NOPE], dwv[..., :V_HEAD]], axis=-1).reshape(kl, h * (QK_NOPE + V_HEAD))
        pending = [chunk(gw, a) for a in PRE]
        for i, a in enumerate(PRE):
            own[l, a] = mine(pending[i])
        dmod[l] = jnp.concatenate([dmod1[:, :, 0], dgt1[:, :, 0], dmod2[:, :, 0]], axis=1).reshape(nb, 6 * d)
        bsum = lambda t: jnp.sum(t, axis=0).reshape(1, -1)
        cs = jnp.sum(csmall, axis=0)[:, 0]
        small_rows[l] = jnp.concatenate([bsum(dgm), bsum(dgq), bsum(dgkv), cs[2:3], cs[0:1], cs[1:2], bsum(dgffn)], axis=1)
        dw_taps[l] = jnp.sum(ddw.reshape(nb, HALO, 8, cc), axis=(0, 2))
    grad_x = dx

    dmod_rows = _pad_rows(jnp.stack(dmod).reshape(nl * nb, 6 * d))
    (dmod_all,) = _all_gather([dmod_rows], vmem=True, name="dmod_all_gather")
    dmod_full = jnp.transpose(dmod_all[:, :nl * nb].reshape(N_DEV, nl, nb, 6 * d), (1, 0, 2, 3)).reshape(nl, N_DEV * nb, 6 * d)
    dmod_cols = lax.dynamic_slice_in_dim(dmod_full, me * n_ada, n_ada, axis=2)
    grad_w_ada, grad_b_ada = _ada_bwd(c_full, dmod_cols, dmod_full)
    grads = {"w_ada": grad_w_ada, "b_ada": grad_b_ada.reshape(nl, 6 * d)}

    widths = (d, ql, kl, cc, cc, cc, d)
    wsum = sum(widths)
    final_row = _pad_last(jnp.sum(dgf_part, axis=0).reshape(1, d), wsum)
    small2d = _pad_rows(jnp.concatenate(small_rows + [final_row], axis=0))
    taps2d = jnp.concatenate(dw_taps, axis=0)
    small_all, taps_all = _all_gather([small2d, taps2d], vmem=True, name="small_grad_all_gather")
    small_sum, taps_sum = _sum_devices(small_all), _sum_devices(taps_all)
    off = 0
    for n, wdt in zip(("g_mix", "g_q", "g_kv", "b_dw", "g_cn", "b_cn", "g_ffn"), widths):
        grads[n] = small_sum[:nl, off:off + wdt]
        off += wdt
    grads["g_final"] = small_sum[nl, :d]
    taps = taps_sum.reshape(nl, HALO, cc)[:, :CONV_W]
    grads["w_dw"] = lax.dynamic_slice_in_dim(taps, me * n_dw, n_dw, axis=2)

    delta, new_m, new_v = {}, {}, {}
    for n in order:
        if n in grads:
            delta[n], new_m[n], new_v[n] = _adamw(weights[n], grads[n], mom_m[n], mom_v[n], "adamw_" + n)

    first = max(POST, key=lambda a: weights[GATHERED[a][0]].size)
    for a in (first,) + tuple(a for a in range(len(GATHERED)) if a != first):
        n = GATHERED[a][0]
        own_n = jnp.stack([own[l, a] for l in range(nl)])
        got_n = jnp.stack([got[l, a] for l in range(nl)], axis=1)
        (grads[n], delta[n], new_m[n], new_v[n]), arrived = _adamw_reduce(
            weights[n], mom_m[n], mom_v[n], own_n, got_n, "adamw_" + n, scatter=pending if a == first else ())
        for b_, t in zip(PRE, arrived):
            got[0, b_] = t

    return (loss, grad_x, *[grads[n] for n in order], *[delta[n] for n in order], *[new_m[n] for n in order],
            *[new_v[n] for n in order])
```
